```python
import math
import jax
import jax.numpy as jnp
from jax import lax
import numpy as np

D_MODEL = 1024
BATCH = 2
SEQ = 8192
DEPTH = 2

GRID_W = 64
CTX_LEN = 256

HEAD_DIM = 64
N_Q_HEADS = 8
N_KV_HEADS = 2
Q_GROUP = N_Q_HEADS // N_KV_HEADS
ATTN_W = N_Q_HEADS * HEAD_DIM
KV_W = N_KV_HEADS * HEAD_DIM
HYENA_W = 256
S5_W = 256
MIX_W = ATTN_W + HYENA_W + S5_W
Q_OFF = 0
K_OFF = ATTN_W
V_OFF = K_OFF + KV_W
HY_OFF = V_OFF + KV_W
S5_OFF = HY_OFF + 3 * HYENA_W
IN_W = S5_OFF + S5_W

WINDOW = 128
BLOCK = 128
NEG_INF = -1e30
ROPE_BASE = 10000.0
AXIS_DIM = HEAD_DIM // 2

SHORT_K = 3
FILTER_EMB = 33
FILTER_HIDDEN = 64
DECAY_FAST = 0.3
DECAY_SLOW = 1.5
DECAY_TARGET = 1e-2

S5_GROUP = 16
S5_GROUPS = S5_W // S5_GROUP
S5_STATE = 64
DT_MIN = 1e-3
DT_MAX = 1e-1

N_EXPERTS = 256
TOP_K = 8
N_EXPERT_GROUPS = 8
TOPK_GROUPS = 4
EXPERT_FF = 256
SHARED_FF = 256
ROUTED_SCALE = 2.5
MOE_BLOCK = 128

DEEPNORM_ALPHA = (2 * DEPTH) ** 0.25
DEEPNORM_BETA = (8 * DEPTH) ** -0.25
LN_EPS = 1e-5

kernel_name = 'hybrid_hymba_hyena_s5_moe_dit'


def layer_norm(x, g, b):
    xf = x.astype(jnp.float32)
    mu = jnp.mean(xf, axis=-1, keepdims=True)
    var = jnp.mean(jnp.square(xf - mu), axis=-1, keepdims=True)
    y = (xf - mu) * lax.rsqrt(var + LN_EPS)
    return (y * g.astype(jnp.float32) + b.astype(jnp.float32)).astype(x.dtype)


def rms_normalise(x):
    xf = x.astype(jnp.float32)
    return (xf * lax.rsqrt(jnp.mean(jnp.square(xf), axis=-1, keepdims=True) + LN_EPS)).astype(x.dtype)


def modulate(x, shift, scale):
    return x * (1.0 + scale) + shift


def rope_2d(x, row, col):
    inv_freq = ROPE_BASE ** (-jnp.arange(0, AXIS_DIM, 2, dtype=jnp.float32) / AXIS_DIM)
    half = AXIS_DIM // 2

    def rotate(xh, pos):
        ang = pos.astype(jnp.float32)[:, None] * inv_freq[None, :]
        cos = jnp.cos(ang)[None, :, None, :]
        sin = jnp.sin(ang)[None, :, None, :]
        x1, x2 = xh[..., :half], xh[..., half:]
        return jnp.concatenate([x1 * cos - x2 * sin, x1 * sin + x2 * cos], axis=-1)

    xf = x.astype(jnp.float32)
    out = jnp.concatenate([rotate(xf[..., :AXIS_DIM], row), rotate(xf[..., AXIS_DIM:], col)], axis=-1)
    return out.astype(x.dtype)


def sink_softmax(s, sink):
    sk = sink.astype(jnp.float32).reshape(N_KV_HEADS, Q_GROUP, 1, 1)
    m = jnp.maximum(jnp.max(s, axis=-1, keepdims=True), sk)
    e = jnp.exp(s - m)
    return e / (jnp.sum(e, axis=-1, keepdims=True) + jnp.exp(sk - m))


def windowed_attention(q, k, v, k_ctx, v_ctx, sink):
    B, L = q.shape[:2]
    nb = L // BLOCK
    scale = HEAD_DIM ** -0.5
    qb = q.reshape(B, nb, BLOCK, N_KV_HEADS, Q_GROUP, HEAD_DIM)

    def band(t):
        tp = jnp.pad(t, ((0, 0), (BLOCK, BLOCK), (0, 0), (0, 0))).reshape(B, nb + 2, BLOCK, N_KV_HEADS, HEAD_DIM)
        return jnp.concatenate([tp[:, :-2], tp[:, 1:-1], tp[:, 2:]], axis=2)

    kw, vw = band(k), band(v)
    s_win = jnp.einsum('bnqkgd,bnskd->bnkgqs', qb, kw, preferred_element_type=jnp.float32) * scale
    s_ctx = jnp.einsum('bnqkgd,bskd->bnkgqs', qb, k_ctx, preferred_element_type=jnp.float32) * scale
    q_abs = jnp.arange(nb)[:, None, None] * BLOCK + jnp.arange(BLOCK)[None, :, None]
    k_abs = jnp.arange(nb)[:, None, None] * BLOCK + jnp.arange(3 * BLOCK)[None, None, :] - BLOCK
    valid = (jnp.abs(k_abs - q_abs) <= WINDOW) & (k_abs >= 0) & (k_abs < L)
    s_win = jnp.where(valid[None, :, None, None], s_win, NEG_INF)
    p = sink_softmax(jnp.concatenate([s_win, s_ctx], axis=-1), sink)
    p_win, p_ctx = p[..., :3 * BLOCK], p[..., 3 * BLOCK:]
    out = (jnp.einsum('bnkgqs,bnskd->bnqkgd', p_win.astype(v.dtype), vw)
           + jnp.einsum('bnkgqs,bskd->bnqkgd', p_ctx.astype(v.dtype), v_ctx))
    return out.reshape(B, L, ATTN_W)


def context_attention(q_c, k_c, v_c, sink):
    B, C = q_c.shape[:2]
    qg = q_c.reshape(B, C, N_KV_HEADS, Q_GROUP, HEAD_DIM)
    s = jnp.einsum('bqkgd,bskd->bkgqs', qg, k_c, preferred_element_type=jnp.float32) * HEAD_DIM ** -0.5
    p = sink_softmax(s, sink)
    out = jnp.einsum('bkgqs,bskd->bqkgd', p.astype(v_c.dtype), v_c)
    return out.reshape(B, C, ATTN_W)


def short_conv(z, w, b):
    L = z.shape[1]
    pad = SHORT_K // 2
    zp = jnp.pad(z, ((0, 0), (pad, pad), (0, 0)))
    out = b
    for j in range(SHORT_K):
        out = out + zp[:, j:j + L] * w[j]
    return out


def hyena_filter(L, w1, b1, freq, w2, b2, w3):
    t = jnp.linspace(0.0, 1.0, L, dtype=jnp.float32)[:, None]
    bands = (FILTER_EMB - 1) // 2
    w = 2.0 * math.pi * jnp.arange(L, dtype=jnp.float32)[:, None] / L
    f = jnp.linspace(1e-4, bands - 1, bands, dtype=jnp.float32)[None, :]
    z = jnp.concatenate([t, jnp.cos(f * w), -jnp.sin(f * w)], axis=-1)
    fr = freq.astype(jnp.float32)
    h = jnp.sin(fr * (z @ w1.astype(jnp.float32) + b1.astype(jnp.float32)))
    h = jnp.sin(fr * (h @ w2.astype(jnp.float32) + b2.astype(jnp.float32)))
    h = h @ w3.astype(jnp.float32)
    deltas = jnp.abs(jnp.linspace(math.log(DECAY_TARGET) / DECAY_FAST, math.log(DECAY_TARGET) / DECAY_SLOW,
                                  HYENA_W, dtype=jnp.float32))
    decay = jnp.exp(-t * deltas[None, :])
    h_fwd = h[:, :HYENA_W] * decay
    h_bwd = h[:, HYENA_W:] * decay
    k = jnp.concatenate([h_fwd, jnp.zeros((1, HYENA_W), jnp.float32), h_bwd[:0:-1]], axis=0)
    return k / jnp.sum(jnp.abs(k), axis=0, keepdims=True)


def hyena_mix(z, short_w, short_b, w1, b1, freq, w2, b2, w3, d_skip):
    L = z.shape[1]
    zc = short_conv(z, short_w, short_b)
    v, x1, x0 = jnp.split(zc, 3, axis=-1)
    u = (v * x1).astype(jnp.float32)
    k = hyena_filter(L, w1, b1, freq, w2, b2, w3)
    n = 2 * L
    y = jnp.fft.irfft(jnp.fft.rfft(u, n=n, axis=1) * jnp.fft.rfft(k, n=n, axis=0)[None], n=n, axis=1)[:, :L]
    y = y + u * d_skip.astype(jnp.float32)
    return (y * x0.astype(jnp.float32)).astype(z.dtype)


def ssm_combine(e1, e2):
    a1, b1 = e1
    a2, b2 = e2
    return a2 * a1, a2 * b1 + b2


def s5_discretise(a_re, a_im, log_dt, b_re, b_im):
    lam = lax.complex(jnp.minimum(a_re.astype(jnp.float32), -1e-4), a_im.astype(jnp.float32))
    dt = jnp.exp(log_dt.astype(jnp.float32))[:, None]
    lam_bar = jnp.exp(lam * dt)
    b = lax.complex(b_re.astype(jnp.float32), b_im.astype(jnp.float32))
    return lam_bar, ((lam_bar - 1.0) / lam)[..., None] * b


def s5_states(u_g, lam_bar, b_bar, h0, reverse):
    bu = jnp.einsum('blgi,gpi->blgp', u_g.astype(jnp.complex64), b_bar)
    if h0 is not None:
        first = -1 if reverse else 0
        bu = bu.at[:, first].add(lam_bar * h0)
    a = jnp.broadcast_to(lam_bar, bu.shape)
    _, states = lax.associative_scan(ssm_combine, (a, bu), reverse=reverse, axis=1)
    return states


def s5_glu(y, w_glu):
    g = jax.nn.gelu(y)
    return g * jax.nn.sigmoid(g @ w_glu.astype(jnp.float32))


def s5_mix(u_lat, u_ctx, a_re, a_im, log_dt, b_re, b_im, c_re, c_im, d_skip, w_glu, ctx_out):
    B, L = u_lat.shape[:2]
    C = u_ctx.shape[1]
    ul = u_lat.astype(jnp.float32).reshape(B, L, S5_GROUPS, S5_GROUP)
    uc = u_ctx.astype(jnp.float32).reshape(B, C, S5_GROUPS, S5_GROUP)
    dg = d_skip.astype(jnp.float32).reshape(S5_GROUPS, S5_GROUP)
    y_lat = ul * dg
    y_ctx = uc * dg if ctx_out else None
    for direction, reverse in ((0, False), (1, True)):
        lam_bar, b_bar = s5_discretise(a_re[direction], a_im[direction], log_dt[direction],
                                       b_re[direction], b_im[direction])
        c_mat = lax.complex(c_re[direction].astype(jnp.float32), c_im[direction].astype(jnp.float32))
        st_c = s5_states(uc, lam_bar, b_bar, None, reverse)
        h0 = st_c[:, 0] if reverse else st_c[:, -1]
        st_l = s5_states(ul, lam_bar, b_bar, h0, reverse)
        y_lat = y_lat + jnp.einsum('blgp,gip->blgi', st_l, c_mat).real
        if ctx_out:
            y_ctx = y_ctx + jnp.einsum('blgp,gip->blgi', st_c, c_mat).real
    out_lat = s5_glu(y_lat.reshape(B, L, S5_W), w_glu).astype(u_lat.dtype)
    out_ctx = s5_glu(y_ctx.reshape(B, C, S5_W), w_glu).astype(u_ctx.dtype) if ctx_out else None
    return out_lat, out_ctx


def merge_groups(attn, hyena, s5, g):
    return jnp.concatenate([rms_normalise(attn), rms_normalise(hyena), rms_normalise(s5)], axis=-1) * g


def swiglu(h, w_gate, w_up, w_down):
    return (jax.nn.silu(h @ w_gate) * (h @ w_up)) @ w_down


def moe_ffn(xt, w_router, router_bias, w_exp_gate, w_exp_up, w_exp_down, w_sh_gate, w_sh_up, w_sh_down):
    T = xt.shape[0]
    scores = jax.nn.sigmoid((xt @ w_router).astype(jnp.float32))
    biased = scores + router_bias.astype(jnp.float32)
    grp = biased.reshape(T, N_EXPERT_GROUPS, N_EXPERTS // N_EXPERT_GROUPS)
    grp_score = jnp.sum(lax.top_k(grp, 2)[0], axis=-1)
    _, top_g = lax.top_k(grp_score, TOPK_GROUPS)
    keep = jnp.any(top_g[:, :, None] == jnp.arange(N_EXPERT_GROUPS)[None, None, :], axis=1)
    masked = jnp.where(keep[:, :, None], grp, -jnp.inf).reshape(T, N_EXPERTS)
    _, top_e = lax.top_k(masked, TOP_K)
    gate = jnp.take_along_axis(scores, top_e, axis=1)
    gate = gate / jnp.sum(gate, axis=-1, keepdims=True) * ROUTED_SCALE

    tk = T * TOP_K
    flat_e = top_e.reshape(tk)
    flat_tok = jnp.repeat(jnp.arange(T, dtype=jnp.int32), TOP_K)
    flat_w = gate.reshape(tk)
    order = jnp.argsort(flat_e)
    se, stok, sw = flat_e[order], flat_tok[order], flat_w[order]
    counts = jnp.bincount(flat_e, length=N_EXPERTS)
    starts = jnp.cumsum(counts) - counts
    padded = (counts + MOE_BLOCK - 1) // MOE_BLOCK * MOE_BLOCK
    pends = jnp.cumsum(padded)
    pstarts = pends - padded
    dest = pstarts[se] + jnp.arange(tk, dtype=jnp.int32) - starts[se]
    n_blocks = -(-tk // MOE_BLOCK) + N_EXPERTS
    n_rows = n_blocks * MOE_BLOCK
    tok_buf = jnp.zeros((n_rows,), jnp.int32).at[dest].set(stok)
    w_buf = jnp.zeros((n_rows,), jnp.float32).at[dest].set(sw)
    blk_e = jnp.minimum(jnp.searchsorted(pends, jnp.arange(n_blocks, dtype=jnp.int32) * MOE_BLOCK, side='right'),
                        N_EXPERTS - 1)

    def expert_block(y, blk):
        e, toks, wts = blk
        xb = xt[toks]
        h = jax.nn.silu(xb @ w_exp_gate[e]) * (xb @ w_exp_up[e])
        out = (h @ w_exp_down[e]) * wts[:, None].astype(xt.dtype)
        return y.at[toks].add(out.astype(y.dtype)), None

    y, _ = lax.scan(expert_block, jnp.zeros_like(xt),
                    (blk_e, tok_buf.reshape(n_blocks, MOE_BLOCK), w_buf.reshape(n_blocks, MOE_BLOCK)))
    return y + swiglu(xt, w_sh_gate, w_sh_up, w_sh_down)


def hybrid_layer(x, ctx, c, c_ctx, row, col, p, last):
    B, L, D = x.shape
    C = ctx.shape[1]
    mod = jax.nn.silu(c) @ p['w_ada'] + p['b_ada']
    mod_c = jax.nn.silu(c_ctx) @ p['w_ada'] + p['b_ada']
    sh1, sc1, g1, sh2, sc2, g2 = jnp.split(mod[:, None, :], 6, axis=-1)
    sh1c, sc1c, g1c, sh2c, sc2c, g2c = jnp.split(mod_c, 6, axis=-1)
    w_in = p['w_in']
    hy_args = (p['hy_short_w'], p['hy_short_b'], p['hy_w1'], p['hy_b1'], p['hy_freq'],
               p['hy_w2'], p['hy_b2'], p['hy_w3'], p['hy_d'])
    s5_args = (p['s5_a_re'], p['s5_a_im'], p['s5_log_dt'], p['s5_b_re'], p['s5_b_im'],
               p['s5_c_re'], p['s5_c_im'], p['s5_d'], p['s5_w_glu'])
    moe_args = (p['w_router'], p['router_bias'], p['w_exp_gate'], p['w_exp_up'], p['w_exp_down'],
                p['w_sh_gate'], p['w_sh_up'], p['w_sh_down'])

    u = modulate(x, sh1, sc1)
    uc = modulate(ctx, sh1c, sc1c)
    proj = u @ w_in
    q = rope_2d(proj[..., Q_OFF:K_OFF].reshape(B, L, N_Q_HEADS, HEAD_DIM), row, col)
    k = rope_2d(proj[..., K_OFF:V_OFF].reshape(B, L, N_KV_HEADS, HEAD_DIM), row, col)
    v = proj[..., V_OFF:HY_OFF].reshape(B, L, N_KV_HEADS, HEAD_DIM)
    kv_c = uc @ w_in[:, K_OFF:HY_OFF]
    k_c = kv_c[..., :KV_W].reshape(B, C, N_KV_HEADS, HEAD_DIM)
    v_c = kv_c[..., KV_W:].reshape(B, C, N_KV_HEADS, HEAD_DIM)
    s5_in_c = uc @ w_in[:, S5_OFF:]
    attn = windowed_attention(q, k, v, k_c, v_c, p['sink'])
    hyena = hyena_mix(proj[..., HY_OFF:S5_OFF], *hy_args)
    s5_lat, s5_ctx = s5_mix(proj[..., S5_OFF:], s5_in_c, *s5_args, ctx_out=not last)
    mix = merge_groups(attn, hyena, s5_lat, p['mix_g'])
    x = layer_norm(DEEPNORM_ALPHA * x + g1 * (mix @ p['w_out']), p['ln1_g'], p['ln1_b'])
    if not last:
        q_c = (uc @ w_in[:, Q_OFF:K_OFF]).reshape(B, C, N_Q_HEADS, HEAD_DIM)
        attn_c = context_attention(q_c, k_c, v_c, p['sink'])
        hyena_c = hyena_mix(uc @ w_in[:, HY_OFF:S5_OFF], *hy_args)
        mix_c = merge_groups(attn_c, hyena_c, s5_ctx, p['mix_g'])
        ctx = layer_norm(DEEPNORM_ALPHA * ctx + g1c * (mix_c @ p['w_out']), p['ln1_g'], p['ln1_b'])

    u2 = modulate(x, sh2, sc2).reshape(B * L, D)
    if last:
        f = moe_ffn(u2, *moe_args)
    else:
        u2c = modulate(ctx, sh2c, sc2c).reshape(B * C, D)
        f_all = moe_ffn(jnp.concatenate([u2, u2c], axis=0), *moe_args)
        f = f_all[:B * L]
        ctx = layer_norm(DEEPNORM_ALPHA * ctx + g2c * f_all[B * L:].reshape(B, C, D), p['ln2_g'], p['ln2_b'])
    x = layer_norm(DEEPNORM_ALPHA * x + g2 * f.reshape(B, L, D), p['ln2_g'], p['ln2_b'])
    return x, ctx


def setup_inputs(seed: int = 0) -> dict:
    key = jax.random.key(seed)
    keys = iter(jax.random.split(key, 48))

    def normal(shape, std):
        return jax.random.normal(next(keys), shape, jnp.float32) * std

    D = D_MODEL
    x = normal((BATCH, SEQ, D), 1.0)
    c = normal((BATCH, D), 1.0)
    ctx = normal((BATCH, CTX_LEN, D), 1.0)
    c_ctx = normal((D,), 1.0)
    w_ada = normal((DEPTH, D, 6 * D), 0.5 * D ** -0.5)
    b_ada = normal((DEPTH, 6 * D), 0.02)
    w_in = normal((DEPTH, D, IN_W), D ** -0.5)
    w_in = w_in.at[:, :, V_OFF:HY_OFF].multiply(DEEPNORM_BETA)
    w_out = normal((DEPTH, MIX_W, D), DEEPNORM_BETA * MIX_W ** -0.5)
    sink = normal((DEPTH, N_Q_HEADS), 0.5)
    mix_g = 1.0 + normal((DEPTH, MIX_W), 0.02)
    hy_short_w = normal((DEPTH, SHORT_K, 3 * HYENA_W), SHORT_K ** -0.5)
    hy_short_b = normal((DEPTH, 3 * HYENA_W), 0.02)
    hy_w1 = normal((DEPTH, FILTER_EMB, FILTER_HIDDEN), FILTER_EMB ** -0.5)
    hy_b1 = normal((DEPTH, FILTER_HIDDEN), 0.5)
    hy_freq = 1.0 + normal((DEPTH, FILTER_HIDDEN), 0.1)
    hy_w2 = normal((DEPTH, FILTER_HIDDEN, FILTER_HIDDEN), FILTER_HIDDEN ** -0.5)
    hy_b2 = normal((DEPTH, FILTER_HIDDEN), 0.5)
    hy_w3 = normal((DEPTH, FILTER_HIDDEN, 2 * HYENA_W), FILTER_HIDDEN ** -0.5)
    hy_d = normal((DEPTH, HYENA_W), 0.5)
    s5_a_re = -0.5 + normal((DEPTH, 2, S5_GROUPS, S5_STATE), 0.01)
    s5_a_im = math.pi * jnp.arange(S5_STATE, dtype=jnp.float32) + normal((DEPTH, 2, S5_GROUPS, S5_STATE), 0.01)
    s5_log_dt = jax.random.uniform(next(keys), (DEPTH, 2, S5_GROUPS), jnp.float32,
                                   math.log(DT_MIN), math.log(DT_MAX))
    s5_b_re = normal((DEPTH, 2, S5_GROUPS, S5_STATE, S5_GROUP), (2 * S5_GROUP) ** -0.5)
    s5_b_im = normal((DEPTH, 2, S5_GROUPS, S5_STATE, S5_GROUP), (2 * S5_GROUP) ** -0.5)
    s5_c_re = normal((DEPTH, 2, S5_GROUPS, S5_GROUP, S5_STATE), S5_STATE ** -0.5)
    s5_c_im = normal((DEPTH, 2, S5_GROUPS, S5_GROUP, S5_STATE), S5_STATE ** -0.5)
    s5_d = normal((DEPTH, S5_W), 1.0)
    s5_w_glu = normal((DEPTH, S5_W, S5_W), S5_W ** -0.5)
    ln1_g = 1.0 + normal((DEPTH, D), 0.02)
    ln1_b = normal((DEPTH, D), 0.02)
    ln2_g = 1.0 + normal((DEPTH, D), 0.02)
    ln2_b = normal((DEPTH, D), 0.02)
    w_router = normal((DEPTH, D, N_EXPERTS), D ** -0.5)
    router_bias = normal((DEPTH, N_EXPERTS), 0.01)
    w_exp_gate = normal((DEPTH, N_EXPERTS, D, EXPERT_FF), D ** -0.5)
    w_exp_up = normal((DEPTH, N_EXPERTS, D, EXPERT_FF), D ** -0.5)
    w_exp_down = normal((DEPTH, N_EXPERTS, EXPERT_FF, D), DEEPNORM_BETA * EXPERT_FF ** -0.5)
    w_sh_gate = normal((DEPTH, D, SHARED_FF), D ** -0.5)
    w_sh_up = normal((DEPTH, D, SHARED_FF), D ** -0.5)
    w_sh_down = normal((DEPTH, SHARED_FF, D), DEEPNORM_BETA * SHARED_FF ** -0.5)
    return dict(x=x, c=c, ctx=ctx, c_ctx=c_ctx, w_ada=w_ada, b_ada=b_ada, w_in=w_in, w_out=w_out,
                sink=sink, mix_g=mix_g, hy_short_w=hy_short_w, hy_short_b=hy_short_b, hy_w1=hy_w1,
                hy_b1=hy_b1, hy_freq=hy_freq, hy_w2=hy_w2, hy_b2=hy_b2, hy_w3=hy_w3, hy_d=hy_d,
                s5_a_re=s5_a_re, s5_a_im=s5_a_im, s5_log_dt=s5_log_dt, s5_b_re=s5_b_re, s5_b_im=s5_b_im,
                s5_c_re=s5_c_re, s5_c_im=s5_c_im, s5_d=s5_d, s5_w_glu=s5_w_glu, ln1_g=ln1_g, ln1_b=ln1_b,
                ln2_g=ln2_g, ln2_b=ln2_b, w_router=w_router, router_bias=router_bias,
                w_exp_gate=w_exp_gate, w_exp_up=w_exp_up, w_exp_down=w_exp_down,
                w_sh_gate=w_sh_gate, w_sh_up=w_sh_up, w_sh_down=w_sh_down)


def reference(x, c, ctx, c_ctx, w_ada, b_ada, w_in, w_out, sink, mix_g, hy_short_w, hy_short_b, hy_w1,
              hy_b1, hy_freq, hy_w2, hy_b2, hy_w3, hy_d, s5_a_re, s5_a_im, s5_log_dt, s5_b_re, s5_b_im,
              s5_c_re, s5_c_im, s5_d, s5_w_glu, ln1_g, ln1_b, ln2_g, ln2_b, w_router, router_bias,
              w_exp_gate, w_exp_up, w_exp_down, w_sh_gate, w_sh_up, w_sh_down):
    L = x.shape[1]
    ROWS = L // GRID_W
    row = jnp.repeat(jnp.arange(ROWS, dtype=jnp.int32), GRID_W)
    col = jnp.tile(jnp.arange(GRID_W, dtype=jnp.int32), ROWS)
    for l in range(DEPTH):
        p = dict(w_ada=w_ada[l], b_ada=b_ada[l], w_in=w_in[l], w_out=w_out[l], sink=sink[l], mix_g=mix_g[l],
                 hy_short_w=hy_short_w[l], hy_short_b=hy_short_b[l], hy_w1=hy_w1[l], hy_b1=hy_b1[l],
                 hy_freq=hy_freq[l], hy_w2=hy_w2[l], hy_b2=hy_b2[l], hy_w3=hy_w3[l], hy_d=hy_d[l],
                 s5_a_re=s5_a_re[l], s5_a_im=s5_a_im[l], s5_log_dt=s5_log_dt[l], s5_b_re=s5_b_re[l],
                 s5_b_im=s5_b_im[l], s5_c_re=s5_c_re[l], s5_c_im=s5_c_im[l], s5_d=s5_d[l],
                 s5_w_glu=s5_w_glu[l], ln1_g=ln1_g[l], ln1_b=ln1_b[l], ln2_g=ln2_g[l], ln2_b=ln2_b[l],
                 w_router=w_router[l], router_bias=router_bias[l], w_exp_gate=w_exp_gate[l],
                 w_exp_up=w_exp_up[l], w_exp_down=w_exp_down[l], w_sh_gate=w_sh_gate[l],
                 w_sh_up=w_sh_up[l], w_sh_down=w_sh_down[l])
        x, ctx = hybrid_layer(x, ctx, c, c_ctx, row, col, p, l == DEPTH - 1)
    return x
```

```python
import functools
import math

import jax
import jax.numpy as jnp
from jax import lax
from jax.experimental import pallas as pl
from jax.experimental.pallas import tpu as pltpu

F32 = jnp.float32
BF16 = jnp.bfloat16
HIGHEST = lax.Precision.HIGHEST

D = 1024
B = 2
L = 8192
DEPTH = 2
GRID_W = 64
C = 256
T_LAT = B * L
T_CTX = B * C
T_ALL = T_LAT + T_CTX

HEAD_DIM = 64
N_Q = 8
N_KV = 2
Q_GROUP = N_Q // N_KV
ATTN_W = N_Q * HEAD_DIM
KV_W = N_KV * HEAD_DIM
HY_W = 256
S5_W = 256
MIX_W = ATTN_W + HY_W + S5_W
K_OFF = ATTN_W
V_OFF = K_OFF + KV_W
HY_OFF = V_OFF + KV_W
S5_OFF = HY_OFF + 3 * HY_W
IN_W = S5_OFF + S5_W
WINDOW = 128
BLK = 128
NEG_INF = -1e30
ROPE_BASE = 10000.0
AXIS_DIM = HEAD_DIM // 2

SHORT_K = 3
FILTER_EMB = 33
DECAY_FAST = 0.3
DECAY_SLOW = 1.5
DECAY_TARGET = 1e-2

S5_GROUP = 16
S5_GROUPS = S5_W // S5_GROUP
S5_STATE = 64
S5_NSTATE = S5_GROUPS * S5_STATE
S5_CH = 16
S5_ROWW = S5_CH * S5_W
N_CHUNK = T_ALL // S5_CH
LAT_CHUNKS = L // S5_CH
CTX_CHUNKS = C // S5_CH

N_EXPERTS = 256
TOP_K = 8
N_EGROUPS = 8
EGROUP = N_EXPERTS // N_EGROUPS
TOPK_GROUPS = 4
EXPERT_FF = 256
ROUTED_SCALE = 2.5
MOE_BLOCK = 128

ALPHA = (2 * DEPTH) ** 0.25
LN_EPS = 1e-5

N_FFT = 2 * L
FFT_R = 128
FFT_T1 = L // FFT_R
K1_USED = FFT_R // 2 + 1
K1P = 80
K1H = K1P // 2

TM = 256
LANE = 128
VMEM_CAP = 60000 * 1024


def _cp(vmem_bytes, n_axes):
    return pltpu.CompilerParams(
        dimension_semantics=("arbitrary",) * n_axes,
        vmem_limit_bytes=min(int(vmem_bytes), VMEM_CAP),
    )


def _mod_sel(rows_per_tile):
    per_batch = L // rows_per_tile
    return lambda i: jnp.minimum(i // per_batch, 2)


def _vec_spec(rows_per_tile):
    sel = _mod_sel(rows_per_tile)
    return pl.BlockSpec((None, 1, D), lambda i: (sel(i), 0, 0))


ADA_TN = 1536


def _ada_kernel(c_ref, w_ref, b_ref, o_ref):
    c = c_ref[...]
    s = c * jax.nn.sigmoid(c)
    o_ref[...] = jnp.dot(s, w_ref[...], precision=HIGHEST, preferred_element_type=F32) + b_ref[...]


def _ada(cvec, w_ada, b_ada):
    return pl.pallas_call(
        _ada_kernel,
        grid=(DEPTH, 6 * D // ADA_TN),
        in_specs=[
            pl.BlockSpec((8, D), lambda l, j: (0, 0)),
            pl.BlockSpec((None, D, ADA_TN), lambda l, j: (l, 0, j)),
            pl.BlockSpec((None, 1, ADA_TN), lambda l, j: (l, 0, j)),
        ],
        out_specs=pl.BlockSpec((None, 8, ADA_TN), lambda l, j: (l, 0, j)),
        out_shape=jax.ShapeDtypeStruct((DEPTH, 8, 6 * D), F32),
        compiler_params=_cp(40 << 20, 2),
        name="ada",
    )(cvec, w_ada, b_ada.reshape(DEPTH, 1, 6 * D))


def _inproj_kernel(x_ref, sh_ref, sc_ref, w_ref, cos_ref, sin_ref, q_ref, k_ref, v_ref, hy_ref, s5_ref):
    u = x_ref[...] * (1.0 + sc_ref[...]) + sh_ref[...]
    proj = jnp.dot(u.astype(BF16), w_ref[...], preferred_element_type=F32)
    cos = cos_ref[...]
    sin = sin_ref[...]
    lane = lax.broadcasted_iota(jnp.int32, (TM, LANE), 1)
    first_half = (lane % AXIS_DIM) < (AXIS_DIM // 2)

    def rope(xc):
        partner = jnp.where(first_half, pltpu.roll(xc, LANE - AXIS_DIM // 2, 1), pltpu.roll(xc, AXIS_DIM // 2, 1))
        return xc * cos + partner * sin

    for j in range(ATTN_W // LANE):
        q_ref[:, j * LANE:(j + 1) * LANE] = rope(proj[:, j * LANE:(j + 1) * LANE]).astype(BF16)
    k_ref[...] = rope(proj[:, K_OFF:V_OFF]).astype(BF16)
    v_ref[...] = proj[:, V_OFF:HY_OFF].astype(BF16)
    hy_ref[...] = proj[:, HY_OFF:S5_OFF]
    s5_ref[...] = proj[:, S5_OFF:IN_W].astype(BF16)


def _inproj(xall, sh, sc, w_in_bf, cos_t, sin_t):
    nt = T_ALL // TM
    row = lambda i: (i, 0)
    return pl.pallas_call(
        _inproj_kernel,
        grid=(nt,),
        in_specs=[
            pl.BlockSpec((TM, D), row),
            _vec_spec(TM),
            _vec_spec(TM),
            pl.BlockSpec((D, IN_W), lambda i: (0, 0)),
            pl.BlockSpec((TM, LANE), row),
            pl.BlockSpec((TM, LANE), row),
        ],
        out_specs=[
            pl.BlockSpec((TM, ATTN_W), row),
            pl.BlockSpec((TM, KV_W), row),
            pl.BlockSpec((TM, KV_W), row),
            pl.BlockSpec((TM, 3 * HY_W), row),
            pl.BlockSpec((TM, S5_W), row),
        ],
        out_shape=[
            jax.ShapeDtypeStruct((T_ALL, ATTN_W), BF16),
            jax.ShapeDtypeStruct((T_ALL, KV_W), BF16),
            jax.ShapeDtypeStruct((T_ALL, KV_W), BF16),
            jax.ShapeDtypeStruct((T_ALL, 3 * HY_W), F32),
            jax.ShapeDtypeStruct((T_ALL, S5_W), BF16),
        ],
        compiler_params=_cp(40 << 20, 1),
        name="inproj",
    )(xall, sh, sc, w_in_bf, cos_t, sin_t)


NB_LAT = L // BLK
NB_CTX = C // BLK


def _nt_dot(a, b):
    return lax.dot_general(a, b, (((1,), (1,)), ((), ())), preferred_element_type=F32)


def _attn_kernel(sink_ref, q_ref, kp_ref, kc_ref, kn_ref, kx_ref, vp_ref, vc_ref, vn_ref, vx_ref, o_ref):
    n = pl.program_id(1)
    is_lat = n < NB_LAT
    r = lax.broadcasted_iota(jnp.int32, (BLK, 3 * BLK), 0)
    j = lax.broadcasted_iota(jnp.int32, (BLK, 3 * BLK), 1)
    k_abs = n * BLK + j - BLK
    valid = (jnp.abs(j - BLK - r) <= WINDOW) & (k_abs >= 0) & (k_abs < L) & is_lat
    scale = HEAD_DIM ** -0.5
    q = q_ref[...]
    for h in range(N_Q):
        kh = h // Q_GROUP
        hs = slice(kh * HEAD_DIM, (kh + 1) * HEAD_DIM)
        qh = q[:, h * HEAD_DIM:(h + 1) * HEAD_DIM]
        s_w = jnp.concatenate(
            [_nt_dot(qh, kp_ref[:, hs]), _nt_dot(qh, kc_ref[:, hs]), _nt_dot(qh, kn_ref[:, hs])], axis=1) * scale
        s_w = jnp.where(valid, s_w, NEG_INF)
        s_x = _nt_dot(qh, kx_ref[:, hs]) * scale
        sk = sink_ref[h]
        m = jnp.maximum(jnp.maximum(jnp.max(s_w, axis=1, keepdims=True), jnp.max(s_x, axis=1, keepdims=True)), sk)
        e_w = jnp.exp(s_w - m)
        e_x = jnp.exp(s_x - m)
        den = jnp.sum(e_w, axis=1, keepdims=True) + jnp.sum(e_x, axis=1, keepdims=True) + jnp.exp(sk - m)
        p_w = (e_w / den).astype(BF16)
        p_x = (e_x / den).astype(BF16)
        o = (jnp.dot(p_w[:, 0:BLK], vp_ref[:, hs], preferred_element_type=F32)
             + jnp.dot(p_w[:, BLK:2 * BLK], vc_ref[:, hs], preferred_element_type=F32)
             + jnp.dot(p_w[:, 2 * BLK:3 * BLK], vn_ref[:, hs], preferred_element_type=F32)
             + jnp.dot(p_x, vx_ref[:, hs], preferred_element_type=F32))
        o_ref[:, h * HEAD_DIM:(h + 1) * HEAD_DIM] = o


def _attention(sink, q, k, v, with_ctx):
    nblk = NB_LAT + (NB_CTX if with_ctx else 0)

    def q_idx(b, n):
        return (jnp.where(n < NB_LAT, b * NB_LAT + n, B * NB_LAT + b * NB_CTX + (n - NB_LAT)), 0)

    def kv_idx(off):
        def idx(b, n):
            nn = jnp.clip(jnp.minimum(n, NB_LAT - 1) + off, 0, NB_LAT - 1)
            return (b * NB_LAT + nn, 0)
        return idx

    ctx_idx = lambda b, n: (T_LAT // C + b, 0)
    kv_specs = lambda: [pl.BlockSpec((BLK, KV_W), kv_idx(-1)), pl.BlockSpec((BLK, KV_W), kv_idx(0)),
                        pl.BlockSpec((BLK, KV_W), kv_idx(1)), pl.BlockSpec((C, KV_W), ctx_idx)]
    return pl.pallas_call(
        _attn_kernel,
        grid=(B, nblk),
        in_specs=[pl.BlockSpec(memory_space=pltpu.SMEM), pl.BlockSpec((BLK, ATTN_W), q_idx)] + kv_specs() + kv_specs(),
        out_specs=pl.BlockSpec((BLK, ATTN_W), q_idx),
        out_shape=jax.ShapeDtypeStruct((T_ALL if with_ctx else T_LAT, ATTN_W), F32),
        compiler_params=_cp(32 << 20, 2),
        name="attention",
    )(sink, q, k, k, k, k, v, v, v, v)


def _hyena_pre_kernel(z_ref, zp_ref, zn_ref, w_ref, b_ref, u_ref, x0_ref):
    i = pl.program_id(0)
    tiles_per_seq = L // TM
    is_ctx = i >= B * tiles_per_seq
    first = is_ctx | (i % tiles_per_seq == 0)
    last = is_ctx | (i % tiles_per_seq == tiles_per_seq - 1)
    z = z_ref[...]
    prev_row = jnp.where(first, 0.0, zp_ref[7:8, :])
    next_row = jnp.where(last, 0.0, zn_ref[0:1, :])
    row = lax.broadcasted_iota(jnp.int32, z.shape, 0)
    z_m1 = jnp.where(row == 0, prev_row, pltpu.roll(z, 1, 0))
    z_p1 = jnp.where(row == TM - 1, next_row, pltpu.roll(z, TM - 1, 0))
    zc = b_ref[...] + z_m1 * w_ref[0:1, :] + z * w_ref[1:2, :] + z_p1 * w_ref[2:3, :]
    u_ref[...] = zc[:, 0:HY_W] * zc[:, HY_W:2 * HY_W]
    x0_ref[...] = zc[:, 2 * HY_W:3 * HY_W]


def _hyena_pre(z, short_w, short_b):
    nt = T_ALL // TM
    sub = TM // 8
    n8 = T_ALL // 8
    return pl.pallas_call(
        _hyena_pre_kernel,
        grid=(nt,),
        in_specs=[
            pl.BlockSpec((TM, 3 * HY_W), lambda i: (i, 0)),
            pl.BlockSpec((8, 3 * HY_W), lambda i: (jnp.maximum(i * sub - 1, 0), 0)),
            pl.BlockSpec((8, 3 * HY_W), lambda i: (jnp.minimum((i + 1) * sub, n8 - 1), 0)),
            pl.BlockSpec((SHORT_K, 3 * HY_W), lambda i: (0, 0)),
            pl.BlockSpec((1, 3 * HY_W), lambda i: (0, 0)),
        ],
        out_specs=[pl.BlockSpec((TM, HY_W), lambda i: (i, 0)), pl.BlockSpec((TM, HY_W), lambda i: (i, 0))],
        out_shape=[jax.ShapeDtypeStruct((T_ALL, HY_W), F32), jax.ShapeDtypeStruct((T_ALL, HY_W), F32)],
        compiler_params=_cp(32 << 20, 1),
        name="hyena_pre",
    )(z, z, z, short_w, short_b.reshape(1, 3 * HY_W))


def _dft_tables():
    t0 = jnp.arange(FFT_R, dtype=jnp.int32)[:, None, None]
    k1 = jnp.arange(K1P, dtype=jnp.int32)[None, :, None]
    t1 = jnp.arange(FFT_T1, dtype=jnp.int32)[None, None, :]
    m = (k1 * (FFT_R * t1 + t0)) % N_FFT
    ang = m.astype(F32) * (2.0 * math.pi / N_FFT)
    used = (k1 < K1_USED).astype(F32)
    g_cos = jnp.cos(ang) * used
    g_sin = jnp.sin(ang) * used
    a = jnp.arange(FFT_R, dtype=jnp.int32)
    ang2 = ((a[:, None] * a[None, :]) % FFT_R).astype(F32) * (2.0 * math.pi / FFT_R)
    return g_cos, g_sin, jnp.cos(ang2), jnp.sin(ang2)


def _hyena_spec_kernel(k_ref, gc_ref, gs_ref, fc_ref, fs_ref, wt_ref, kr_ref, ki_ref, ar_ref, ai_ref):
    half = pl.program_id(1)
    kk = lax.broadcasted_iota(jnp.int32, (K1H, 1), 0) + half * K1H
    sign = jnp.where(kk % 2 == 0, 1.0, -1.0).astype(F32)

    def stage1(t0, carry):
        x_lo = k_ref[pl.ds(t0, FFT_T1, stride=FFT_R), :]
        x_hi = k_ref[pl.ds(L + t0, FFT_T1, stride=FFT_R), :]
        gc = gc_ref[t0]
        gs = gs_ref[t0]
        dot = lambda a, b: jnp.dot(a, b, precision=HIGHEST, preferred_element_type=F32)
        rows = pl.ds(pl.multiple_of(t0 * K1H, 8), K1H)
        ar_ref[rows, :] = dot(gc, x_lo) + sign * dot(gc, x_hi)
        ai_ref[rows, :] = -(dot(gs, x_lo) + sign * dot(gs, x_hi))
        return carry

    lax.fori_loop(0, FFT_R, stage1, 0)
    fc = fc_ref[...]
    fs = fs_ref[...]

    def stage2(kl, carry):
        a_r = ar_ref[pl.ds(kl, FFT_R, stride=K1H), :]
        a_i = ai_ref[pl.ds(kl, FFT_R, stride=K1H), :]
        dot = lambda a, b: jnp.dot(a, b, precision=HIGHEST, preferred_element_type=F32)
        w = wt_ref[half * K1H + kl]
        kr_ref[kl] = (dot(fc, a_r) + dot(fs, a_i)) * w
        ki_ref[kl] = (dot(fc, a_i) - dot(fs, a_r)) * w
        return carry

    lax.fori_loop(0, K1H, stage2, 0)


def _hyena_spectrum(kfilt, g_cos, g_sin, f_cos, f_sin, wts):
    nct = HY_W // LANE
    gspec = pl.BlockSpec((FFT_R, K1H, FFT_T1), lambda c, h: (0, h, 0))
    fspec = pl.BlockSpec((FFT_R, FFT_R), lambda c, h: (0, 0))
    ospec = pl.BlockSpec((K1H, FFT_R, LANE), lambda c, h: (h, 0, c))
    return pl.pallas_call(
        _hyena_spec_kernel,
        grid=(nct, 2),
        in_specs=[pl.BlockSpec((N_FFT, LANE), lambda c, h: (0, c)), gspec, gspec, fspec, fspec,
                  pl.BlockSpec(memory_space=pltpu.SMEM)],
        out_specs=[ospec, ospec],
        out_shape=[jax.ShapeDtypeStruct((K1P, FFT_R, HY_W), F32)] * 2,
        scratch_shapes=[pltpu.VMEM((FFT_R * K1H, LANE), F32)] * 2,
        compiler_params=_cp(56 << 20, 2),
        name="hyena_spectrum",
    )(kfilt, g_cos, g_sin, f_cos, f_sin, wts)


def _hyena_fft_kernel(u_ref, gc_ref, gs_ref, ic_ref, is_ref, fc_ref, fs_ref, kr_ref, ki_ref, o_ref, ar_ref, ai_ref):
    bdot = lambda a, b: jnp.dot(a, b.astype(BF16), preferred_element_type=F32)

    def stage1(t0, carry):
        x = u_ref[pl.ds(t0, FFT_T1, stride=FFT_R), :]
        rows = pl.ds(pl.multiple_of(t0 * K1P, 8), K1P)
        ar_ref[rows, :] = bdot(gc_ref[t0], x)
        ai_ref[rows, :] = -bdot(gs_ref[t0], x)
        return carry

    lax.fori_loop(0, FFT_R, stage1, 0)
    fc = fc_ref[...]
    fs = fs_ref[...]

    def stage23(k1, carry):
        rows = pl.ds(k1, FFT_R, stride=K1P)
        a_r = ar_ref[rows, :]
        a_i = ai_ref[rows, :]
        z_r = bdot(fc, a_r) + bdot(fs, a_i)
        z_i = bdot(fc, a_i) - bdot(fs, a_r)
        k_r = kr_ref[k1]
        k_i = ki_ref[k1]
        y_r = z_r * k_r - z_i * k_i
        y_i = z_r * k_i + z_i * k_r
        ar_ref[rows, :] = bdot(fc, y_r) - bdot(fs, y_i)
        ai_ref[rows, :] = bdot(fc, y_i) + bdot(fs, y_r)
        return carry

    lax.fori_loop(0, K1_USED, stage23, 0)

    def stage4(t0, carry):
        rows = pl.ds(pl.multiple_of(t0 * K1P, 8), K1P)
        o_ref[pl.ds(t0, FFT_T1, stride=FFT_R), :] = bdot(ic_ref[t0], ar_ref[rows, :]) - bdot(is_ref[t0], ai_ref[rows, :])
        return carry

    lax.fori_loop(0, FFT_R, stage4, 0)


def _hyena_fft(u, g_cos_bf, g_sin_bf, i_cos_bf, i_sin_bf, f_cos_bf, f_sin_bf, k_r, k_i):
    nct = HY_W // LANE
    one = pl.Buffered(1)
    gspec = pl.BlockSpec((FFT_R, K1P, FFT_T1), lambda c, b: (0, 0, 0), pipeline_mode=one)
    ispec = pl.BlockSpec((FFT_R, FFT_T1, K1P), lambda c, b: (0, 0, 0), pipeline_mode=one)
    fspec = pl.BlockSpec((FFT_R, FFT_R), lambda c, b: (0, 0), pipeline_mode=one)
    kspec = pl.BlockSpec((K1P, FFT_R, LANE), lambda c, b: (0, 0, c), pipeline_mode=one)
    return pl.pallas_call(
        _hyena_fft_kernel,
        grid=(nct, B),
        in_specs=[pl.BlockSpec((L, LANE), lambda c, b: (b, c)), gspec, gspec, ispec, ispec, fspec, fspec, kspec, kspec],
        out_specs=pl.BlockSpec((L, LANE), lambda c, b: (b, c)),
        out_shape=jax.ShapeDtypeStruct((T_LAT, HY_W), F32),
        scratch_shapes=[pltpu.VMEM((FFT_R * K1P, LANE), F32)] * 2,
        compiler_params=_cp(56 << 20, 2),
        name="hyena_fft",
    )(u, g_cos_bf, g_sin_bf, i_cos_bf, i_sin_bf, f_cos_bf, f_sin_bf, k_r, k_i)


def _hyena_ctx_kernel(u_ref, k_ref, dc_ref, ds_ref, o_ref):
    dot = lambda a, b: jnp.dot(a, b, precision=HIGHEST, preferred_element_type=F32)
    dc = dc_ref[...]
    ds = ds_ref[...]
    u = u_ref[...]
    kf = k_ref[...]
    u_r = dot(dc[:, 0:C], u)
    u_i = -dot(ds[:, 0:C], u)
    k_r = dot(dc, kf)
    k_i = -dot(ds, kf)
    y_r = u_r * k_r - u_i * k_i
    y_i = u_r * k_i + u_i * k_r
    o_ref[...] = (dot(dc[0:C, :], y_r) - dot(ds[0:C, :], y_i)) * (1.0 / (2 * C))


def _hyena_ctx(u, kfilt_ctx, d_cos, d_sin):
    full = lambda b: (0, 0)
    return pl.pallas_call(
        _hyena_ctx_kernel,
        grid=(B,),
        in_specs=[pl.BlockSpec((C, HY_W), lambda b: (T_LAT // C + b, 0)),
                  pl.BlockSpec((2 * C, HY_W), full), pl.BlockSpec((2 * C, 2 * C), full), pl.BlockSpec((2 * C, 2 * C), full)],
        out_specs=pl.BlockSpec((C, HY_W), lambda b: (b, 0)),
        out_shape=jax.ShapeDtypeStruct((T_CTX, HY_W), F32),
        compiler_params=_cp(32 << 20, 1),
        name="hyena_ctx",
    )(u, kfilt_ctx, d_cos, d_sin)


def _hyena_filter(n, w1, b1, freq, w2, b2, w3):
    t = jnp.linspace(0.0, 1.0, n, dtype=F32)[:, None]
    bands = (FILTER_EMB - 1) // 2
    w = 2.0 * math.pi * jnp.arange(n, dtype=F32)[:, None] / n
    f = jnp.linspace(1e-4, bands - 1, bands, dtype=F32)[None, :]
    z = jnp.concatenate([t, jnp.cos(f * w), -jnp.sin(f * w)], axis=-1)
    mm = functools.partial(jnp.matmul, precision=HIGHEST)
    h = jnp.sin(freq * (mm(z, w1) + b1))
    h = jnp.sin(freq * (mm(h, w2) + b2))
    h = mm(h, w3)
    deltas = jnp.abs(jnp.linspace(math.log(DECAY_TARGET) / DECAY_FAST, math.log(DECAY_TARGET) / DECAY_SLOW,
                                  HY_W, dtype=F32))
    decay = jnp.exp(-t * deltas[None, :])
    h_fwd = h[:, :HY_W] * decay
    h_bwd = h[:, HY_W:] * decay
    k = jnp.concatenate([h_fwd, jnp.zeros((1, HY_W), F32), h_bwd[:0:-1]], axis=0)
    return k / jnp.sum(jnp.abs(k), axis=0, keepdims=True)


def _s5_matrices(a_re, a_im, log_dt, b_re, b_im, c_re, c_im, d_skip):
    dt = jnp.exp(log_dt)[:, :, None]
    lam_re = jnp.minimum(a_re, -1e-4)
    mag1 = jnp.exp(lam_re * dt)
    lbr = mag1 * jnp.cos(a_im * dt)
    lbi = mag1 * jnp.sin(a_im * dt)
    den = lam_re * lam_re + a_im * a_im
    fr = ((lbr - 1.0) * lam_re + lbi * a_im) / den
    fi = (lbi * lam_re - (lbr - 1.0) * a_im) / den
    bbr = fr[..., None] * b_re - fi[..., None] * b_im
    bbi = fr[..., None] * b_im + fi[..., None] * b_re
    j = jnp.arange(S5_CH + 1, dtype=F32)[:, None, None, None]
    magj = jnp.exp(j * (lam_re * dt)[None])
    pr = magj * jnp.cos(j * (a_im * dt)[None])
    pi = magj * jnp.sin(j * (a_im * dt)[None])
    hi = functools.partial(jnp.einsum, precision=HIGHEST)
    lbr_j = pr[..., None] * bbr[None] - pi[..., None] * bbi[None]
    lbi_j = pr[..., None] * bbi[None] + pi[..., None] * bbr[None]
    m = hi('dgop,jdgpi->jdgoi', c_re, lbr_j) - hi('dgop,jdgpi->jdgoi', c_im, lbi_j)
    eye_g = jnp.eye(S5_GROUPS, dtype=F32)
    s = jnp.arange(S5_CH)
    lag = s[None, :] - s[:, None]
    mf = jnp.where((lag >= 0)[:, :, None, None, None], m[jnp.clip(lag, 0, S5_CH), 0], 0.0)
    mb = jnp.where((lag <= 0)[:, :, None, None, None], m[jnp.clip(-lag, 0, S5_CH), 1], 0.0)
    dmat = (jnp.eye(S5_CH, dtype=F32)[:, :, None, None, None]
            * (d_skip.reshape(S5_GROUPS, S5_GROUP)[None, None, :, :, None] * jnp.eye(S5_GROUP, dtype=F32)[None, None, None]))
    kin = mf + mb + dmat
    wk = jnp.einsum('stgoi,gh->sgitho', kin, eye_g).reshape(S5_ROWW, S5_ROWW)
    sf_r = lbr_j[S5_CH - 1 - s, 0]
    sf_i = lbi_j[S5_CH - 1 - s, 0]
    sb_r = lbr_j[s, 1]
    sb_i = lbi_j[s, 1]
    st = jnp.stack([sf_r, sf_i, sb_r, sb_i], axis=0)
    ws = jnp.einsum('qsgpi,gh->sgiqhp', st, eye_g).reshape(S5_ROWW, 4 * S5_NSTATE)
    tt = jnp.arange(S5_CH)
    cf_r = c_re[0][None] * pr[tt + 1, 0][:, :, None, :] - c_im[0][None] * pi[tt + 1, 0][:, :, None, :]
    cf_i = c_re[0][None] * pi[tt + 1, 0][:, :, None, :] + c_im[0][None] * pr[tt + 1, 0][:, :, None, :]
    cb_r = c_re[1][None] * pr[S5_CH - tt, 1][:, :, None, :] - c_im[1][None] * pi[S5_CH - tt, 1][:, :, None, :]
    cb_i = c_re[1][None] * pi[S5_CH - tt, 1][:, :, None, :] + c_im[1][None] * pr[S5_CH - tt, 1][:, :, None, :]
    ct = jnp.stack([cf_r, -cf_i, cb_r, -cb_i], axis=0)
    wc = jnp.einsum('qtgop,gh->qgptho', ct, eye_g).reshape(4 * S5_NSTATE, S5_ROWW)
    w1 = jnp.concatenate([ws, wk], axis=1).astype(BF16)
    lam_p = jnp.stack([jnp.stack([pr[S5_CH, 0], pi[S5_CH, 0]]), jnp.stack([pr[S5_CH, 1], pi[S5_CH, 1]])])
    return w1, wc.astype(BF16), lam_p.reshape(2, 2, 8, LANE)


S5_TN = 512


def _s5_in_kernel(u_ref, w_ref, o_ref):
    o_ref[...] = jnp.dot(u_ref[...], w_ref[...], preferred_element_type=F32)


def _s5_in(u_rows, w1):
    return pl.pallas_call(
        _s5_in_kernel,
        grid=(2 * S5_ROWW // S5_TN,),
        in_specs=[pl.BlockSpec((N_CHUNK, S5_ROWW), lambda j: (0, 0)), pl.BlockSpec((S5_ROWW, S5_TN), lambda j: (0, j))],
        out_specs=pl.BlockSpec((N_CHUNK, S5_TN), lambda j: (0, j)),
        out_shape=jax.ShapeDtypeStruct((N_CHUNK, 2 * S5_ROWW), F32),
        compiler_params=_cp(48 << 20, 1),
        name="s5_in",
    )(u_rows, w1)


SLAB = 8
CH_ROWS = 4 * SLAB


def _s5_scan_kernel(s_ref, lam_ref, h_ref):
    lam = [[lam_ref[d, p] for p in range(2)] for d in range(2)]

    def step(b, d, chunk, h):
        base = pl.multiple_of(chunk * CH_ROWS + d * 2 * SLAB, SLAB)
        h_ref[pl.ds(base, SLAB), :] = h[0]
        h_ref[pl.ds(base + SLAB, SLAB), :] = h[1]
        s_r = s_ref[pl.ds(base, SLAB), :]
        s_i = s_ref[pl.ds(base + SLAB, SLAB), :]
        lr, li = lam[d]
        return (lr * h[0] - li * h[1] + s_r, lr * h[1] + li * h[0] + s_i)

    def chain_order(b, d, n_ctx_done):
        ctx0 = B * LAT_CHUNKS + b * CTX_CHUNKS
        lat0 = b * LAT_CHUNKS
        if d == 0:
            return (lambda i: ctx0 + i), (lambda i: lat0 + i)
        return (lambda i: ctx0 + CTX_CHUNKS - 1 - i), (lambda i: lat0 + LAT_CHUNKS - 1 - i)

    chains = [(b, d) for b in range(B) for d in range(2)]
    zero = jnp.zeros((SLAB, LANE), F32)
    init = tuple((zero, zero) for _ in chains)

    def phase(n_steps, which, carry):
        def body(i, hs):
            out = []
            for (b, d), h in zip(chains, hs):
                order = chain_order(b, d, 0)[which]
                out.append(step(b, d, order(i), h))
            return tuple(out)
        return lax.fori_loop(0, n_steps, body, carry)

    carry = phase(CTX_CHUNKS, 0, init)
    phase(LAT_CHUNKS, 1, carry)


def _s5_scan(s_slabs, lam_p):
    return pl.pallas_call(
        _s5_scan_kernel,
        out_shape=jax.ShapeDtypeStruct((N_CHUNK * CH_ROWS, LANE), F32),
        compiler_params=pltpu.CompilerParams(vmem_limit_bytes=48 << 20),
        name="s5_scan",
    )(s_slabs, lam_p)


S5_TM = N_CHUNK // 2


def _s5_out_kernel(h_ref, w_ref, y_ref, o_ref, hb_ref):
    @pl.when(pl.program_id(1) == 0)
    def _():
        hb_ref[...] = h_ref[...].astype(BF16)

    o_ref[...] = jnp.dot(hb_ref[...], w_ref[...], preferred_element_type=F32) + y_ref[...]


def _s5_out(h_rows, wc, sy):
    nn = S5_ROWW // S5_TN
    return pl.pallas_call(
        _s5_out_kernel,
        grid=(N_CHUNK // S5_TM, nn),
        in_specs=[pl.BlockSpec((S5_TM, 4 * S5_NSTATE), lambda i, j: (i, 0)),
                  pl.BlockSpec((4 * S5_NSTATE, S5_TN), lambda i, j: (0, j)),
                  pl.BlockSpec((S5_TM, S5_TN), lambda i, j: (i, nn + j))],
        out_specs=pl.BlockSpec((S5_TM, S5_TN), lambda i, j: (i, j)),
        out_shape=jax.ShapeDtypeStruct((N_CHUNK, S5_ROWW), F32),
        scratch_shapes=[pltpu.VMEM((S5_TM, 4 * S5_NSTATE), BF16)],
        compiler_params=_cp(48 << 20, 2),
        name="s5_out",
    )(h_rows, wc, sy)


def _rms(x):
    return x * lax.rsqrt(jnp.mean(x * x, axis=-1, keepdims=True) + LN_EPS)


def _layer_norm(x, g, b):
    mu = jnp.mean(x, axis=-1, keepdims=True)
    xc = x - mu
    var = jnp.mean(xc * xc, axis=-1, keepdims=True)
    return xc * lax.rsqrt(var + LN_EPS) * g + b


def _merge_kernel(x_ref, attn_ref, conv_ref, hu_ref, x0_ref, s5_ref, g1_ref, mixg_ref, hyd_ref, wglu_ref, wout_ref,
                  lng_ref, lnb_ref, o_ref):
    hy = (conv_ref[...] + hu_ref[...] * hyd_ref[...]) * x0_ref[...]
    g = jax.nn.gelu(s5_ref[...])
    s5 = g * jax.nn.sigmoid(jnp.dot(g.astype(BF16), wglu_ref[...], preferred_element_type=F32))
    mixg = mixg_ref[...]
    parts = [_rms(attn_ref[...]) * mixg[:, 0:ATTN_W],
             _rms(hy) * mixg[:, ATTN_W:ATTN_W + HY_W],
             _rms(s5) * mixg[:, ATTN_W + HY_W:MIX_W]]
    mix = jnp.concatenate(parts, axis=-1).astype(BF16)
    o = jnp.dot(mix, wout_ref[...], preferred_element_type=F32)
    o_ref[...] = _layer_norm(ALPHA * x_ref[...] + g1_ref[...] * o, lng_ref[...], lnb_ref[...])


def _merge(n_rows, xall, attn, conv, hu, x0c, s5y, g1, mix_g, hy_d, wglu_bf, wout_bf, ln_g, ln_b):
    nt = n_rows // TM
    row = lambda i: (i, 0)
    full = lambda i: (0, 0)
    return pl.pallas_call(
        _merge_kernel,
        grid=(nt,),
        in_specs=[pl.BlockSpec((TM, D), row), pl.BlockSpec((TM, ATTN_W), row), pl.BlockSpec((TM, HY_W), row),
                  pl.BlockSpec((TM, HY_W), row), pl.BlockSpec((TM, HY_W), row), pl.BlockSpec((TM, S5_W), row),
                  _vec_spec(TM), pl.BlockSpec((1, MIX_W), full), pl.BlockSpec((1, HY_W), full),
                  pl.BlockSpec((S5_W, S5_W), full), pl.BlockSpec((MIX_W, D), full),
                  pl.BlockSpec((1, D), full), pl.BlockSpec((1, D), full)],
        out_specs=pl.BlockSpec((TM, D), row),
        out_shape=jax.ShapeDtypeStruct((n_rows, D), F32),
        compiler_params=_cp(40 << 20, 1),
        name="merge",
    )(xall, attn, conv, hu, x0c, s5y, g1, mix_g.reshape(1, MIX_W), hy_d.reshape(1, HY_W), wglu_bf, wout_bf,
      ln_g.reshape(1, D), ln_b.reshape(1, D))


def _router_kernel(x_ref, sh_ref, sc_ref, w_ref, b_ref, e_ref, g_ref):
    u = x_ref[...] * (1.0 + sc_ref[...]) + sh_ref[...]
    logits = jnp.dot(u, w_ref[...], precision=HIGHEST, preferred_element_type=F32)
    scores = jax.nn.sigmoid(logits)
    biased = scores + b_ref[...]
    lane = lax.broadcasted_iota(jnp.int32, (TM, N_EXPERTS), 1)
    grp = lane // EGROUP
    ninf = -jnp.inf
    big = N_EXPERTS

    def first_argmax(vals):
        m = jnp.max(vals, axis=1, keepdims=True)
        idx = jnp.min(jnp.where(vals == m, lane, big), axis=1, keepdims=True)
        return m, idx

    gscore = []
    for gi in range(N_EGROUPS):
        vals = jnp.where(grp == gi, biased, ninf)
        m1, i1 = first_argmax(vals)
        m2 = jnp.max(jnp.where(lane == i1, ninf, vals), axis=1, keepdims=True)
        gscore.append(m1 + m2)
    keep = jnp.zeros((TM, N_EXPERTS), jnp.bool_)
    for gi in range(N_EGROUPS):
        rank = jnp.zeros((TM, 1), jnp.int32)
        for gj in range(N_EGROUPS):
            if gj == gi:
                continue
            ahead = (gscore[gj] > gscore[gi]) | ((gscore[gj] == gscore[gi]) & (gj < gi))
            rank = rank + ahead.astype(jnp.int32)
        keep = keep | ((grp == gi) & (rank < TOPK_GROUPS))
    masked = jnp.where(keep, biased, ninf)
    out_lane = lax.broadcasted_iota(jnp.int32, (TM, LANE), 1)
    e_out = jnp.zeros((TM, LANE), jnp.int32)
    g_out = jnp.zeros((TM, LANE), F32)
    gsum = jnp.zeros((TM, 1), F32)
    for kk in range(TOP_K):
        _, idx = first_argmax(masked)
        hit = lane == idx
        gate = jnp.sum(jnp.where(hit, scores, 0.0), axis=1, keepdims=True)
        masked = jnp.where(hit, ninf, masked)
        e_out = jnp.where(out_lane == kk, idx, e_out)
        g_out = jnp.where(out_lane == kk, gate, g_out)
        gsum = gsum + gate
    e_ref[...] = e_out
    g_ref[...] = g_out / gsum * ROUTED_SCALE


def _router(n_rows, x1, sh2, sc2, w_router, router_bias):
    nt = n_rows // TM
    row = lambda i: (i, 0)
    return pl.pallas_call(
        _router_kernel,
        grid=(nt,),
        in_specs=[pl.BlockSpec((TM, D), row), _vec_spec(TM), _vec_spec(TM),
                  pl.BlockSpec((D, N_EXPERTS), lambda i: (0, 0)), pl.BlockSpec((1, N_EXPERTS), lambda i: (0, 0))],
        out_specs=[pl.BlockSpec((TM, LANE), row), pl.BlockSpec((TM, LANE), row)],
        out_shape=[jax.ShapeDtypeStruct((n_rows, LANE), jnp.int32), jax.ShapeDtypeStruct((n_rows, LANE), F32)],
        compiler_params=_cp(32 << 20, 1),
        name="router",
    )(x1, sh2, sc2, w_router, router_bias.reshape(1, N_EXPERTS))


def _dispatch(top_e, gate):
    t = top_e.shape[0]
    tk = t * TOP_K
    flat_e = top_e.reshape(tk)
    order = jnp.argsort(flat_e).astype(jnp.int32)
    se = flat_e[order]
    counts = jnp.zeros((N_EXPERTS,), jnp.int32).at[flat_e].add(1)
    starts = jnp.cumsum(counts) - counts
    padded = (counts + MOE_BLOCK - 1) // MOE_BLOCK * MOE_BLOCK
    pends = jnp.cumsum(padded)
    pstarts = pends - padded
    dest = pstarts[se] + jnp.arange(tk, dtype=jnp.int32) - starts[se]
    n_blocks = -(-tk // MOE_BLOCK) + N_EXPERTS
    n_rows = n_blocks * MOE_BLOCK
    tok_buf = jnp.zeros((n_rows,), jnp.int32).at[dest].set(order // TOP_K)
    dst_buf = jnp.zeros((n_rows,), jnp.int32).at[dest].set(order)
    w_buf = jnp.zeros((n_rows,), F32).at[dest].set(gate.reshape(tk)[order])
    blk_start = jnp.arange(n_blocks, dtype=jnp.int32) * MOE_BLOCK
    blk_e = jnp.minimum(jnp.searchsorted(pends, blk_start, side='right'), N_EXPERTS - 1).astype(jnp.int32)
    nvalid = jnp.clip(counts[blk_e] - (blk_start - pstarts[blk_e]), 0, MOE_BLOCK)
    nvalid = jnp.where(blk_start < pends[-1], nvalid, 0).astype(jnp.int32)
    return (blk_e, nvalid, tok_buf.reshape(n_blocks, 1, MOE_BLOCK), dst_buf.reshape(n_blocks, 1, MOE_BLOCK),
            w_buf.reshape(n_blocks, MOE_BLOCK, 1))


def _expert_kernel(blk_e, nvalid, tok_c, tok_n, dst_c, w_ref, x_hbm, wg_ref, wu_ref, wd_ref, out_hbm,
                   xbuf, obuf, gsem, ssem, wgb, wub, wdb):
    i = pl.program_id(0)
    nb = pl.num_programs(0)
    slot = i % 2

    def gather_copy(tok_ref, r, s):
        return pltpu.make_async_copy(x_hbm.at[pl.ds(tok_ref[0, 0, r], 1)], xbuf.at[s, pl.ds(r, 1)], gsem.at[s])

    def scatter_copy(r, s):
        return pltpu.make_async_copy(obuf.at[s, pl.ds(r, 1)], out_hbm.at[pl.ds(dst_c[0, 0, r], 1)], ssem.at[s])

    def start_gather(tok_ref, n, s):
        def body(r, c):
            gather_copy(tok_ref, r, s).start()
            return c
        lax.fori_loop(0, n, body, 0)

    def wait_scatter(n, s):
        def body(r, c):
            scatter_copy(r, s).wait()
            return c
        lax.fori_loop(0, n, body, 0)

    @pl.when(i == 0)
    def _():
        xbuf[...] = jnp.zeros(xbuf.shape, xbuf.dtype)
        start_gather(tok_c, nvalid[0], 0)

    @pl.when(i + 1 < nb)
    def _():
        start_gather(tok_n, nvalid[jnp.minimum(i + 1, nb - 1)], 1 - slot)

    n_cur = nvalid[i]

    def wait_g(r, c):
        gather_copy(tok_c, r, slot).wait()
        return c
    lax.fori_loop(0, n_cur, wait_g, 0)

    e_prev = blk_e[jnp.maximum(i - 1, 0)]

    @pl.when((i == 0) | (blk_e[i] != e_prev))
    def _():
        wgb[...] = wg_ref[...].astype(BF16)
        wub[...] = wu_ref[...].astype(BF16)
        wdb[...] = wd_ref[...].astype(BF16)

    @pl.when(i >= 2)
    def _():
        wait_scatter(nvalid[jnp.maximum(i - 2, 0)], slot)

    @pl.when(n_cur > 0)
    def _():
        xb = xbuf[slot].astype(BF16)
        hg = jnp.dot(xb, wgb[...], preferred_element_type=F32)
        hu = jnp.dot(xb, wub[...], preferred_element_type=F32)
        h = (hg * jax.nn.sigmoid(hg)) * hu
        o = jnp.dot(h.astype(BF16), wdb[...], preferred_element_type=F32)
        obuf[slot] = o * w_ref[0]

        def body(r, c):
            scatter_copy(r, slot).start()
            return c
        lax.fori_loop(0, n_cur, body, 0)

    @pl.when(i == nb - 1)
    def _():
        @pl.when(i >= 1)
        def _():
            wait_scatter(nvalid[jnp.maximum(i - 1, 0)], 1 - slot)
        wait_scatter(n_cur, slot)


def _experts(u2, disp, wg, wu, wd, layer):
    blk_e, nvalid, tok, dst, wts = disp
    n_blocks = tok.shape[0]
    t = u2.shape[0]
    cur = lambda i, e, n: (i, 0, 0)
    nxt = lambda i, e, n: (jnp.minimum(i + 1, n_blocks - 1), 0, 0)
    wspec = lambda shape: pl.BlockSpec((None, None) + shape, lambda i, e, n: (layer, e[i], 0, 0))
    smem = lambda idx: pl.BlockSpec((1, 1, MOE_BLOCK), idx, memory_space=pltpu.SMEM)
    grid_spec = pltpu.PrefetchScalarGridSpec(
        num_scalar_prefetch=2,
        grid=(n_blocks,),
        in_specs=[smem(cur), smem(nxt), smem(cur),
                  pl.BlockSpec((1, MOE_BLOCK, 1), cur),
                  pl.BlockSpec(memory_space=pl.ANY),
                  wspec((D, EXPERT_FF)), wspec((D, EXPERT_FF)), wspec((EXPERT_FF, D))],
        out_specs=pl.BlockSpec(memory_space=pl.ANY),
        scratch_shapes=[pltpu.VMEM((2, MOE_BLOCK, D), F32), pltpu.VMEM((2, MOE_BLOCK, D), F32),
                        pltpu.SemaphoreType.DMA((2,)), pltpu.SemaphoreType.DMA((2,)),
                        pltpu.VMEM((D, EXPERT_FF), BF16), pltpu.VMEM((D, EXPERT_FF), BF16),
                        pltpu.VMEM((EXPERT_FF, D), BF16)],
    )
    return pl.pallas_call(
        _expert_kernel,
        grid_spec=grid_spec,
        out_shape=jax.ShapeDtypeStruct((t * TOP_K, D), F32),
        compiler_params=_cp(40 << 20, 1),
        name="experts",
    )(blk_e, nvalid, tok, tok, dst, wts, u2, wg, wu, wd)


def _mod2_kernel(x_ref, sh_ref, sc_ref, o_ref):
    o_ref[...] = x_ref[...] * (1.0 + sc_ref[...]) + sh_ref[...]


def _mod2(n_rows, x1, sh2, sc2):
    row = lambda i: (i, 0)
    return pl.pallas_call(
        _mod2_kernel,
        grid=(n_rows // TM,),
        in_specs=[pl.BlockSpec((TM, D), row), _vec_spec(TM), _vec_spec(TM)],
        out_specs=pl.BlockSpec((TM, D), row),
        out_shape=jax.ShapeDtypeStruct((n_rows, D), F32),
        compiler_params=_cp(32 << 20, 1),
        name="mod2",
    )(x1, sh2, sc2)


FM = 128


def _ffn_out_kernel(x_ref, u_ref, r_ref, g2_ref, wsg_ref, wsu_ref, wsd_ref, lng_ref, lnb_ref, o_ref):
    ub = u_ref[...].astype(BF16)
    hg = jnp.dot(ub, wsg_ref[...], preferred_element_type=F32)
    hu = jnp.dot(ub, wsu_ref[...], preferred_element_type=F32)
    f = jnp.dot(((hg * jax.nn.sigmoid(hg)) * hu).astype(BF16), wsd_ref[...], preferred_element_type=F32)
    routed = r_ref[:, 0:D]
    for kk in range(1, TOP_K):
        routed = routed + r_ref[:, kk * D:(kk + 1) * D]
    f = routed + f
    o_ref[...] = _layer_norm(ALPHA * x_ref[...] + g2_ref[...] * f, lng_ref[...], lnb_ref[...])


def _ffn_out(n_rows, x1, u2, routed, g2, wsg_bf, wsu_bf, wsd_bf, ln_g, ln_b):
    row = lambda i: (i, 0)
    full = lambda i: (0, 0)
    return pl.pallas_call(
        _ffn_out_kernel,
        grid=(n_rows // FM,),
        in_specs=[pl.BlockSpec((FM, D), row), pl.BlockSpec((FM, D), row), pl.BlockSpec((FM, TOP_K * D), row),
                  _vec_spec(FM), pl.BlockSpec((D, EXPERT_FF), full), pl.BlockSpec((D, EXPERT_FF), full),
                  pl.BlockSpec((EXPERT_FF, D), full), pl.BlockSpec((1, D), full), pl.BlockSpec((1, D), full)],
        out_specs=pl.BlockSpec((FM, D), row),
        out_shape=jax.ShapeDtypeStruct((n_rows, D), F32),
        compiler_params=_cp(40 << 20, 1),
        name="ffn_out",
    )(x1, u2, routed.reshape(n_rows, TOP_K * D), g2, wsg_bf, wsu_bf, wsd_bf, ln_g.reshape(1, D), ln_b.reshape(1, D))


def _rope_tables():
    t = jnp.arange(L, dtype=jnp.int32)
    row = (t // GRID_W).astype(F32)
    col = (t % GRID_W).astype(F32)
    inv_freq = ROPE_BASE ** (-jnp.arange(0, AXIS_DIM, 2, dtype=F32) / AXIS_DIM)
    half = AXIS_DIM // 2

    def axis(pos):
        ang = pos[:, None] * inv_freq[None, :]
        c = jnp.cos(ang)
        s = jnp.sin(ang)
        return jnp.concatenate([c, c], axis=1), jnp.concatenate([-s, s], axis=1)

    cr, sr = axis(row)
    cc, sc = axis(col)
    cos_h = jnp.concatenate([cr, cc], axis=1)
    sin_h = jnp.concatenate([sr, sc], axis=1)
    cos_l = jnp.tile(cos_h, (B, LANE // HEAD_DIM))
    sin_l = jnp.tile(sin_h, (B, LANE // HEAD_DIM))
    cos_t = jnp.concatenate([cos_l, jnp.ones((T_CTX, LANE), F32)], axis=0)
    sin_t = jnp.concatenate([sin_l, jnp.zeros((T_CTX, LANE), F32)], axis=0)
    del half
    return cos_t, sin_t


def kernel(x, c, ctx, c_ctx, w_ada, b_ada, w_in, w_out, sink, mix_g, hy_short_w, hy_short_b, hy_w1, hy_b1, hy_freq,
           hy_w2, hy_b2, hy_w3, hy_d, s5_a_re, s5_a_im, s5_log_dt, s5_b_re, s5_b_im, s5_c_re, s5_c_im, s5_d, s5_w_glu,
           ln1_g, ln1_b, ln2_g, ln2_b, w_router, router_bias, w_exp_gate, w_exp_up, w_exp_down, w_sh_gate, w_sh_up,
           w_sh_down):
    xall = jnp.concatenate([x.reshape(T_LAT, D), ctx.reshape(T_CTX, D)], axis=0)
    cvec = jnp.concatenate([c, c_ctx[None, :], jnp.zeros((8 - B - 1, D), F32)], axis=0)
    mod = _ada(cvec, w_ada, b_ada)[:, 0:B + 1, :].reshape(DEPTH, B + 1, 6, 1, D)

    cos_t, sin_t = _rope_tables()
    g_cos, g_sin, f_cos, f_sin = _dft_tables()
    g_cos_bf, g_sin_bf = g_cos.astype(BF16), g_sin.astype(BF16)
    i_cos_bf = jnp.swapaxes(g_cos, 1, 2).astype(BF16)
    i_sin_bf = jnp.swapaxes(g_sin, 1, 2).astype(BF16)
    f_cos_bf, f_sin_bf = f_cos.astype(BF16), f_sin.astype(BF16)
    k1 = jnp.arange(K1P)
    spec_w = jnp.where((k1 == 0) | (k1 == FFT_R // 2), 1.0, 2.0) * (k1 < K1_USED) / N_FFT
    spec_w = spec_w.astype(F32)
    kt = jnp.arange(2 * C, dtype=jnp.int32)
    ang_c = ((kt[:, None] * kt[None, :]) % (2 * C)).astype(F32) * (2.0 * math.pi / (2 * C))
    d_cos, d_sin = jnp.cos(ang_c), jnp.sin(ang_c)

    for l in range(DEPTH):
        last = l == DEPTH - 1
        n_rows = T_LAT if last else T_ALL
        sh1, sc1, g1, sh2, sc2, g2 = (mod[l, :, j] for j in range(6))

        q, k, v, hz, s5u = _inproj(xall, sh1, sc1, w_in[l].astype(BF16), cos_t, sin_t)
        attn = _attention(sink[l], q, k, v, with_ctx=not last)

        hu, x0c = _hyena_pre(hz, hy_short_w[l], hy_short_b[l])
        filt_args = (hy_w1[l], hy_b1[l], hy_freq[l], hy_w2[l], hy_b2[l], hy_w3[l])
        k_r, k_i = _hyena_spectrum(_hyena_filter(L, *filt_args), g_cos, g_sin, f_cos, f_sin, spec_w)
        conv = _hyena_fft(hu, g_cos_bf, g_sin_bf, i_cos_bf, i_sin_bf, f_cos_bf, f_sin_bf, k_r, k_i)
        if not last:
            conv = jnp.concatenate([conv, _hyena_ctx(hu, _hyena_filter(C, *filt_args), d_cos, d_sin)], axis=0)

        w1, wc, lam_p = _s5_matrices(s5_a_re[l], s5_a_im[l], s5_log_dt[l], s5_b_re[l], s5_b_im[l], s5_c_re[l],
                                     s5_c_im[l], s5_d[l])
        sy = _s5_in(s5u.reshape(N_CHUNK, S5_ROWW), w1)
        s_slabs = sy[:, 0:4 * S5_NSTATE].reshape(N_CHUNK * CH_ROWS, LANE)
        h_rows = _s5_scan(s_slabs, lam_p).reshape(N_CHUNK, 4 * S5_NSTATE)
        s5y = _s5_out(h_rows, wc, sy).reshape(T_ALL, S5_W)

        x1 = _merge(n_rows, xall, attn, conv, hu, x0c, s5y, g1, mix_g[l], hy_d[l], s5_w_glu[l].astype(BF16),
                    w_out[l].astype(BF16), ln1_g[l], ln1_b[l])

        top_e, gate = _router(n_rows, x1, sh2, sc2, w_router[l], router_bias[l])
        u2 = _mod2(n_rows, x1, sh2, sc2)
        disp = _dispatch(top_e[:, 0:TOP_K], gate[:, 0:TOP_K])
        routed = _experts(u2, disp, w_exp_gate, w_exp_up, w_exp_down, l)
        xall = _ffn_out(n_rows, x1, u2, routed, g2, w_sh_gate[l].astype(BF16), w_sh_up[l].astype(BF16),
                        w_sh_down[l].astype(BF16), ln2_g[l], ln2_b[l])
    return xall.reshape(B, L, D)
```

```python
import functools
import math

import jax
import jax.numpy as jnp
from jax import lax
from jax.experimental import pallas as pl
from jax.experimental.pallas import tpu as pltpu
from jax.experimental.pallas import tpu_sc as plsc

F32 = jnp.float32
BF16 = jnp.bfloat16
HIGHEST = lax.Precision.HIGHEST

D = 1024
B = 2
L = 8192
DEPTH = 2
GRID_W = 64
C = 256
T_LAT = B * L
T_CTX = B * C
T_ALL = T_LAT + T_CTX

HEAD_DIM = 64
N_Q = 8
N_KV = 2
Q_GROUP = N_Q // N_KV
ATTN_W = N_Q * HEAD_DIM
KV_W = N_KV * HEAD_DIM
HY_W = 256
S5_W = 256
MIX_W = ATTN_W + HY_W + S5_W
K_OFF = ATTN_W
V_OFF = K_OFF + KV_W
HY_OFF = V_OFF + KV_W
S5_OFF = HY_OFF + 3 * HY_W
IN_W = S5_OFF + S5_W
WINDOW = 128
BLK = 128
NEG_INF = -1e30
ROPE_BASE = 10000.0
AXIS_DIM = HEAD_DIM // 2

SHORT_K = 3
FILTER_EMB = 33
DECAY_FAST = 0.3
DECAY_SLOW = 1.5
DECAY_TARGET = 1e-2

S5_GROUP = 16
S5_GROUPS = S5_W // S5_GROUP
S5_STATE = 64
S5_NSTATE = S5_GROUPS * S5_STATE
S5_CH = 16
S5_ROWW = S5_CH * S5_W
N_CHUNK = T_ALL // S5_CH
LAT_CHUNKS = L // S5_CH
CTX_CHUNKS = C // S5_CH

N_EXPERTS = 256
TOP_K = 8
N_EGROUPS = 8
EGROUP = N_EXPERTS // N_EGROUPS
TOPK_GROUPS = 4
EXPERT_FF = 256
ROUTED_SCALE = 2.5
MOE_BLOCK = 128

ALPHA = (2 * DEPTH) ** 0.25
LN_EPS = 1e-5

N_FFT = 2 * L
FFT_R = 128
FFT_T1 = L // FFT_R
K1_USED = FFT_R // 2 + 1
K1P = 80
K1H = K1P // 2

TM = 256
LANE = 128
VMEM_CAP = 60000 * 1024


def _cp(vmem_bytes, n_axes):
    return pltpu.CompilerParams(
        dimension_semantics=("arbitrary",) * n_axes,
        vmem_limit_bytes=min(int(vmem_bytes), VMEM_CAP),
    )


def _mod_sel(rows_per_tile):
    per_batch = L // rows_per_tile
    return lambda i: jnp.minimum(i // per_batch, 2)


def _vec_spec(rows_per_tile):
    sel = _mod_sel(rows_per_tile)
    return pl.BlockSpec((None, 1, D), lambda i: (sel(i), 0, 0))


ADA_TN = 1536


def _ada_kernel(c_ref, w_ref, b_ref, o_ref):
    c = c_ref[...]
    s = c * jax.nn.sigmoid(c)
    o_ref[...] = jnp.dot(s, w_ref[...], precision=HIGHEST, preferred_element_type=F32) + b_ref[...]


def _ada(cvec, w_ada, b_ada):
    return pl.pallas_call(
        _ada_kernel,
        grid=(DEPTH, 6 * D // ADA_TN),
        in_specs=[
            pl.BlockSpec((8, D), lambda l, j: (0, 0)),
            pl.BlockSpec((None, D, ADA_TN), lambda l, j: (l, 0, j)),
            pl.BlockSpec((None, 1, ADA_TN), lambda l, j: (l, 0, j)),
        ],
        out_specs=pl.BlockSpec((None, 8, ADA_TN), lambda l, j: (l, 0, j)),
        out_shape=jax.ShapeDtypeStruct((DEPTH, 8, 6 * D), F32),
        compiler_params=_cp(40 << 20, 2),
        name="ada",
    )(cvec, w_ada, b_ada.reshape(DEPTH, 1, 6 * D))


def _inproj_kernel(x_ref, sh_ref, sc_ref, w_ref, cos_ref, sin_ref, q_ref, k_ref, v_ref, hy_ref, s5_ref):
    u = x_ref[...] * (1.0 + sc_ref[...]) + sh_ref[...]
    proj = jnp.dot(u.astype(BF16), w_ref[...], preferred_element_type=F32)
    cos = cos_ref[...]
    sin = sin_ref[...]
    lane = lax.broadcasted_iota(jnp.int32, (TM, LANE), 1)
    first_half = (lane % AXIS_DIM) < (AXIS_DIM // 2)

    def rope(xc):
        partner = jnp.where(first_half, pltpu.roll(xc, LANE - AXIS_DIM // 2, 1), pltpu.roll(xc, AXIS_DIM // 2, 1))
        return xc * cos + partner * sin

    for j in range(ATTN_W // LANE):
        q_ref[:, j * LANE:(j + 1) * LANE] = rope(proj[:, j * LANE:(j + 1) * LANE]).astype(BF16)
    k_ref[...] = rope(proj[:, K_OFF:V_OFF]).astype(BF16)
    v_ref[...] = proj[:, V_OFF:HY_OFF].astype(BF16)
    hy_ref[...] = proj[:, HY_OFF:S5_OFF]
    s5_ref[...] = proj[:, S5_OFF:IN_W].astype(BF16)


def _inproj(xall, sh, sc, w_in_bf, cos_t, sin_t):
    nt = T_ALL // TM
    row = lambda i: (i, 0)
    return pl.pallas_call(
        _inproj_kernel,
        grid=(nt,),
        in_specs=[
            pl.BlockSpec((TM, D), row),
            _vec_spec(TM),
            _vec_spec(TM),
            pl.BlockSpec((D, IN_W), lambda i: (0, 0)),
            pl.BlockSpec((TM, LANE), row),
            pl.BlockSpec((TM, LANE), row),
        ],
        out_specs=[
            pl.BlockSpec((TM, ATTN_W), row),
            pl.BlockSpec((TM, KV_W), row),
            pl.BlockSpec((TM, KV_W), row),
            pl.BlockSpec((TM, 3 * HY_W), row),
            pl.BlockSpec((TM, S5_W), row),
        ],
        out_shape=[
            jax.ShapeDtypeStruct((T_ALL, ATTN_W), BF16),
            jax.ShapeDtypeStruct((T_ALL, KV_W), BF16),
            jax.ShapeDtypeStruct((T_ALL, KV_W), BF16),
            jax.ShapeDtypeStruct((T_ALL, 3 * HY_W), F32),
            jax.ShapeDtypeStruct((T_ALL, S5_W), BF16),
        ],
        compiler_params=_cp(40 << 20, 1),
        name="inproj",
    )(xall, sh, sc, w_in_bf, cos_t, sin_t)


NB_LAT = L // BLK
NB_CTX = C // BLK


def _nt_dot(a, b):
    return lax.dot_general(a, b, (((1,), (1,)), ((), ())), preferred_element_type=F32)


def _attn_kernel(sink_ref, q_ref, kp_ref, kc_ref, kn_ref, kx_ref, vp_ref, vc_ref, vn_ref, vx_ref, o_ref):
    n = pl.program_id(1)
    is_lat = n < NB_LAT
    r = lax.broadcasted_iota(jnp.int32, (BLK, 3 * BLK), 0)
    j = lax.broadcasted_iota(jnp.int32, (BLK, 3 * BLK), 1)
    k_abs = n * BLK + j - BLK
    valid = (jnp.abs(j - BLK - r) <= WINDOW) & (k_abs >= 0) & (k_abs < L) & is_lat
    scale = HEAD_DIM ** -0.5
    q = q_ref[...]
    for h in range(N_Q):
        kh = h // Q_GROUP
        hs = slice(kh * HEAD_DIM, (kh + 1) * HEAD_DIM)
        qh = q[:, h * HEAD_DIM:(h + 1) * HEAD_DIM]
        s_w = jnp.concatenate(
            [_nt_dot(qh, kp_ref[:, hs]), _nt_dot(qh, kc_ref[:, hs]), _nt_dot(qh, kn_ref[:, hs])], axis=1) * scale
        s_w = jnp.where(valid, s_w, NEG_INF)
        s_x = _nt_dot(qh, kx_ref[:, hs]) * scale
        sk = sink_ref[h]
        m = jnp.maximum(jnp.maximum(jnp.max(s_w, axis=1, keepdims=True), jnp.max(s_x, axis=1, keepdims=True)), sk)
        e_w = jnp.exp(s_w - m)
        e_x = jnp.exp(s_x - m)
        den = jnp.sum(e_w, axis=1, keepdims=True) + jnp.sum(e_x, axis=1, keepdims=True) + jnp.exp(sk - m)
        p_w = (e_w / den).astype(BF16)
        p_x = (e_x / den).astype(BF16)
        o = (jnp.dot(p_w[:, 0:BLK], vp_ref[:, hs], preferred_element_type=F32)
             + jnp.dot(p_w[:, BLK:2 * BLK], vc_ref[:, hs], preferred_element_type=F32)
             + jnp.dot(p_w[:, 2 * BLK:3 * BLK], vn_ref[:, hs], preferred_element_type=F32)
             + jnp.dot(p_x, vx_ref[:, hs], preferred_element_type=F32))
        o_ref[:, h * HEAD_DIM:(h + 1) * HEAD_DIM] = o


def _attention(sink, q, k, v, with_ctx):
    nblk = NB_LAT + (NB_CTX if with_ctx else 0)

    def q_idx(b, n):
        return (jnp.where(n < NB_LAT, b * NB_LAT + n, B * NB_LAT + b * NB_CTX + (n - NB_LAT)), 0)

    def kv_idx(off):
        def idx(b, n):
            nn = jnp.clip(jnp.minimum(n, NB_LAT - 1) + off, 0, NB_LAT - 1)
            return (b * NB_LAT + nn, 0)
        return idx

    ctx_idx = lambda b, n: (T_LAT // C + b, 0)
    kv_specs = lambda: [pl.BlockSpec((BLK, KV_W), kv_idx(-1)), pl.BlockSpec((BLK, KV_W), kv_idx(0)),
                        pl.BlockSpec((BLK, KV_W), kv_idx(1)), pl.BlockSpec((C, KV_W), ctx_idx)]
    return pl.pallas_call(
        _attn_kernel,
        grid=(B, nblk),
        in_specs=[pl.BlockSpec(memory_space=pltpu.SMEM), pl.BlockSpec((BLK, ATTN_W), q_idx)] + kv_specs() + kv_specs(),
        out_specs=pl.BlockSpec((BLK, ATTN_W), q_idx),
        out_shape=jax.ShapeDtypeStruct((T_ALL if with_ctx else T_LAT, ATTN_W), F32),
        compiler_params=_cp(32 << 20, 2),
        name="attention",
    )(sink, q, k, k, k, k, v, v, v, v)


def _hyena_pre_kernel(z_ref, zp_ref, zn_ref, w_ref, b_ref, u_ref, x0_ref):
    i = pl.program_id(0)
    tiles_per_seq = L // TM
    is_ctx = i >= B * tiles_per_seq
    first = is_ctx | (i % tiles_per_seq == 0)
    last = is_ctx | (i % tiles_per_seq == tiles_per_seq - 1)
    z = z_ref[...]
    prev_row = jnp.where(first, 0.0, zp_ref[7:8, :])
    next_row = jnp.where(last, 0.0, zn_ref[0:1, :])
    row = lax.broadcasted_iota(jnp.int32, z.shape, 0)
    z_m1 = jnp.where(row == 0, prev_row, pltpu.roll(z, 1, 0))
    z_p1 = jnp.where(row == TM - 1, next_row, pltpu.roll(z, TM - 1, 0))
    zc = b_ref[...] + z_m1 * w_ref[0:1, :] + z * w_ref[1:2, :] + z_p1 * w_ref[2:3, :]
    u_ref[...] = zc[:, 0:HY_W] * zc[:, HY_W:2 * HY_W]
    x0_ref[...] = zc[:, 2 * HY_W:3 * HY_W]


def _hyena_pre(z, short_w, short_b):
    nt = T_ALL // TM
    sub = TM // 8
    n8 = T_ALL // 8
    return pl.pallas_call(
        _hyena_pre_kernel,
        grid=(nt,),
        in_specs=[
            pl.BlockSpec((TM, 3 * HY_W), lambda i: (i, 0)),
            pl.BlockSpec((8, 3 * HY_W), lambda i: (jnp.maximum(i * sub - 1, 0), 0)),
            pl.BlockSpec((8, 3 * HY_W), lambda i: (jnp.minimum((i + 1) * sub, n8 - 1), 0)),
            pl.BlockSpec((SHORT_K, 3 * HY_W), lambda i: (0, 0)),
            pl.BlockSpec((1, 3 * HY_W), lambda i: (0, 0)),
        ],
        out_specs=[pl.BlockSpec((TM, HY_W), lambda i: (i, 0)), pl.BlockSpec((TM, HY_W), lambda i: (i, 0))],
        out_shape=[jax.ShapeDtypeStruct((T_ALL, HY_W), F32), jax.ShapeDtypeStruct((T_ALL, HY_W), F32)],
        compiler_params=_cp(32 << 20, 1),
        name="hyena_pre",
    )(z, z, z, short_w, short_b.reshape(1, 3 * HY_W))


def _dft_tables():
    t0 = jnp.arange(FFT_R, dtype=jnp.int32)[:, None, None]
    k1 = jnp.arange(K1P, dtype=jnp.int32)[None, :, None]
    t1 = jnp.arange(FFT_T1, dtype=jnp.int32)[None, None, :]
    m = (k1 * (FFT_R * t1 + t0)) % N_FFT
    ang = m.astype(F32) * (2.0 * math.pi / N_FFT)
    used = (k1 < K1_USED).astype(F32)
    g_cos = jnp.cos(ang) * used
    g_sin = jnp.sin(ang) * used
    a = jnp.arange(FFT_R, dtype=jnp.int32)
    ang2 = ((a[:, None] * a[None, :]) % FFT_R).astype(F32) * (2.0 * math.pi / FFT_R)
    return g_cos, g_sin, jnp.cos(ang2), jnp.sin(ang2)


def _hyena_spec_kernel(k_ref, gc_ref, gs_ref, fc_ref, fs_ref, wt_ref, kr_ref, ki_ref, ar_ref, ai_ref):
    half = pl.program_id(1)
    kk = lax.broadcasted_iota(jnp.int32, (K1H, 1), 0) + half * K1H
    sign = jnp.where(kk % 2 == 0, 1.0, -1.0).astype(F32)

    def stage1(t0, carry):
        x_lo = k_ref[pl.ds(t0, FFT_T1, stride=FFT_R), :]
        x_hi = k_ref[pl.ds(L + t0, FFT_T1, stride=FFT_R), :]
        gc = gc_ref[t0]
        gs = gs_ref[t0]
        dot = lambda a, b: jnp.dot(a, b, precision=HIGHEST, preferred_element_type=F32)
        rows = pl.ds(pl.multiple_of(t0 * K1H, 8), K1H)
        ar_ref[rows, :] = dot(gc, x_lo) + sign * dot(gc, x_hi)
        ai_ref[rows, :] = -(dot(gs, x_lo) + sign * dot(gs, x_hi))
        return carry

    lax.fori_loop(0, FFT_R, stage1, 0)
    fc = fc_ref[...]
    fs = fs_ref[...]

    def stage2(kl, carry):
        a_r = ar_ref[pl.ds(kl, FFT_R, stride=K1H), :]
        a_i = ai_ref[pl.ds(kl, FFT_R, stride=K1H), :]
        dot = lambda a, b: jnp.dot(a, b, precision=HIGHEST, preferred_element_type=F32)
        w = wt_ref[half * K1H + kl]
        kr_ref[kl] = (dot(fc, a_r) + dot(fs, a_i)) * w
        ki_ref[kl] = (dot(fc, a_i) - dot(fs, a_r)) * w
        return carry

    lax.fori_loop(0, K1H, stage2, 0)


def _hyena_spectrum(kfilt, g_cos, g_sin, f_cos, f_sin, wts):
    nct = HY_W // LANE
    gspec = pl.BlockSpec((FFT_R, K1H, FFT_T1), lambda c, h: (0, h, 0))
    fspec = pl.BlockSpec((FFT_R, FFT_R), lambda c, h: (0, 0))
    ospec = pl.BlockSpec((K1H, FFT_R, LANE), lambda c, h: (h, 0, c))
    return pl.pallas_call(
        _hyena_spec_kernel,
        grid=(nct, 2),
        in_specs=[pl.BlockSpec((N_FFT, LANE), lambda c, h: (0, c)), gspec, gspec, fspec, fspec,
                  pl.BlockSpec(memory_space=pltpu.SMEM)],
        out_specs=[ospec, ospec],
        out_shape=[jax.ShapeDtypeStruct((K1P, FFT_R, HY_W), F32)] * 2,
        scratch_shapes=[pltpu.VMEM((FFT_R * K1H, LANE), F32)] * 2,
        compiler_params=_cp(56 << 20, 2),
        name="hyena_spectrum",
    )(kfilt, g_cos, g_sin, f_cos, f_sin, wts)


def _hyena_fft_kernel(u_ref, gc_ref, gs_ref, ic_ref, is_ref, fc_ref, fs_ref, kr_ref, ki_ref, o_ref, ar_ref, ai_ref):
    bdot = lambda a, b: jnp.dot(a, b.astype(BF16), preferred_element_type=F32)

    def stage1(t0, carry):
        x = u_ref[pl.ds(t0, FFT_T1, stride=FFT_R), :]
        rows = pl.ds(pl.multiple_of(t0 * K1P, 8), K1P)
        ar_ref[rows, :] = bdot(gc_ref[t0], x)
        ai_ref[rows, :] = -bdot(gs_ref[t0], x)
        return carry

    lax.fori_loop(0, FFT_R, stage1, 0)
    fc = fc_ref[...]
    fs = fs_ref[...]

    def stage23(k1, carry):
        rows = pl.ds(k1, FFT_R, stride=K1P)
        a_r = ar_ref[rows, :]
        a_i = ai_ref[rows, :]
        z_r = bdot(fc, a_r) + bdot(fs, a_i)
        z_i = bdot(fc, a_i) - bdot(fs, a_r)
        k_r = kr_ref[k1]
        k_i = ki_ref[k1]
        y_r = z_r * k_r - z_i * k_i
        y_i = z_r * k_i + z_i * k_r
        ar_ref[rows, :] = bdot(fc, y_r) - bdot(fs, y_i)
        ai_ref[rows, :] = bdot(fc, y_i) + bdot(fs, y_r)
        return carry

    lax.fori_loop(0, K1_USED, stage23, 0)

    def stage4(t0, carry):
        rows = pl.ds(pl.multiple_of(t0 * K1P, 8), K1P)
        o_ref[pl.ds(t0, FFT_T1, stride=FFT_R), :] = bdot(ic_ref[t0], ar_ref[rows, :]) - bdot(is_ref[t0], ai_ref[rows, :])
        return carry

    lax.fori_loop(0, FFT_R, stage4, 0)


def _hyena_fft(u, g_cos_bf, g_sin_bf, i_cos_bf, i_sin_bf, f_cos_bf, f_sin_bf, k_r, k_i):
    nct = HY_W // LANE
    one = pl.Buffered(1)
    gspec = pl.BlockSpec((FFT_R, K1P, FFT_T1), lambda c, b: (0, 0, 0), pipeline_mode=one)
    ispec = pl.BlockSpec((FFT_R, FFT_T1, K1P), lambda c, b: (0, 0, 0), pipeline_mode=one)
    fspec = pl.BlockSpec((FFT_R, FFT_R), lambda c, b: (0, 0), pipeline_mode=one)
    kspec = pl.BlockSpec((K1P, FFT_R, LANE), lambda c, b: (0, 0, c), pipeline_mode=one)
    return pl.pallas_call(
        _hyena_fft_kernel,
        grid=(nct, B),
        in_specs=[pl.BlockSpec((L, LANE), lambda c, b: (b, c)), gspec, gspec, ispec, ispec, fspec, fspec, kspec, kspec],
        out_specs=pl.BlockSpec((L, LANE), lambda c, b: (b, c)),
        out_shape=jax.ShapeDtypeStruct((T_LAT, HY_W), F32),
        scratch_shapes=[pltpu.VMEM((FFT_R * K1P, LANE), F32)] * 2,
        compiler_params=_cp(56 << 20, 2),
        name="hyena_fft",
    )(u, g_cos_bf, g_sin_bf, i_cos_bf, i_sin_bf, f_cos_bf, f_sin_bf, k_r, k_i)


def _hyena_ctx_kernel(u_ref, k_ref, dc_ref, ds_ref, o_ref):
    dot = lambda a, b: jnp.dot(a, b, precision=HIGHEST, preferred_element_type=F32)
    dc = dc_ref[...]
    ds = ds_ref[...]
    u = u_ref[...]
    kf = k_ref[...]
    u_r = dot(dc[:, 0:C], u)
    u_i = -dot(ds[:, 0:C], u)
    k_r = dot(dc, kf)
    k_i = -dot(ds, kf)
    y_r = u_r * k_r - u_i * k_i
    y_i = u_r * k_i + u_i * k_r
    o_ref[...] = (dot(dc[0:C, :], y_r) - dot(ds[0:C, :], y_i)) * (1.0 / (2 * C))


def _hyena_ctx(u, kfilt_ctx, d_cos, d_sin):
    full = lambda b: (0, 0)
    return pl.pallas_call(
        _hyena_ctx_kernel,
        grid=(B,),
        in_specs=[pl.BlockSpec((C, HY_W), lambda b: (T_LAT // C + b, 0)),
                  pl.BlockSpec((2 * C, HY_W), full), pl.BlockSpec((2 * C, 2 * C), full), pl.BlockSpec((2 * C, 2 * C), full)],
        out_specs=pl.BlockSpec((C, HY_W), lambda b: (b, 0)),
        out_shape=jax.ShapeDtypeStruct((T_CTX, HY_W), F32),
        compiler_params=_cp(32 << 20, 1),
        name="hyena_ctx",
    )(u, kfilt_ctx, d_cos, d_sin)


def _hyena_filter(n, w1, b1, freq, w2, b2, w3):
    t = jnp.linspace(0.0, 1.0, n, dtype=F32)[:, None]
    bands = (FILTER_EMB - 1) // 2
    w = 2.0 * math.pi * jnp.arange(n, dtype=F32)[:, None] / n
    f = jnp.linspace(1e-4, bands - 1, bands, dtype=F32)[None, :]
    z = jnp.concatenate([t, jnp.cos(f * w), -jnp.sin(f * w)], axis=-1)
    mm = functools.partial(jnp.matmul, precision=HIGHEST)
    h = jnp.sin(freq * (mm(z, w1) + b1))
    h = jnp.sin(freq * (mm(h, w2) + b2))
    h = mm(h, w3)
    deltas = jnp.abs(jnp.linspace(math.log(DECAY_TARGET) / DECAY_FAST, math.log(DECAY_TARGET) / DECAY_SLOW,
                                  HY_W, dtype=F32))
    decay = jnp.exp(-t * deltas[None, :])
    h_fwd = h[:, :HY_W] * decay
    h_bwd = h[:, HY_W:] * decay
    k = jnp.concatenate([h_fwd, jnp.zeros((1, HY_W), F32), h_bwd[:0:-1]], axis=0)
    return k / jnp.sum(jnp.abs(k), axis=0, keepdims=True)


def _s5_matrices(a_re, a_im, log_dt, b_re, b_im, c_re, c_im, d_skip):
    dt = jnp.exp(log_dt)[:, :, None]
    lam_re = jnp.minimum(a_re, -1e-4)
    mag1 = jnp.exp(lam_re * dt)
    lbr = mag1 * jnp.cos(a_im * dt)
    lbi = mag1 * jnp.sin(a_im * dt)
    den = lam_re * lam_re + a_im * a_im
    fr = ((lbr - 1.0) * lam_re + lbi * a_im) / den
    fi = (lbi * lam_re - (lbr - 1.0) * a_im) / den
    bbr = fr[..., None] * b_re - fi[..., None] * b_im
    bbi = fr[..., None] * b_im + fi[..., None] * b_re
    j = jnp.arange(S5_CH + 1, dtype=F32)[:, None, None, None]
    magj = jnp.exp(j * (lam_re * dt)[None])
    pr = magj * jnp.cos(j * (a_im * dt)[None])
    pi = magj * jnp.sin(j * (a_im * dt)[None])
    hi = functools.partial(jnp.einsum, precision=HIGHEST)
    lbr_j = pr[..., None] * bbr[None] - pi[..., None] * bbi[None]
    lbi_j = pr[..., None] * bbi[None] + pi[..., None] * bbr[None]
    m = hi('dgop,jdgpi->jdgoi', c_re, lbr_j) - hi('dgop,jdgpi->jdgoi', c_im, lbi_j)
    eye_g = jnp.eye(S5_GROUPS, dtype=F32)
    s = jnp.arange(S5_CH)
    blocks = jnp.einsum('jdgoi,gh->djgiho', m[0:S5_CH], eye_g).reshape(2, S5_CH, S5_W, S5_W)
    lag0 = blocks[0, 0] + blocks[1, 0] + jnp.diag(d_skip)
    e_all = jnp.concatenate([blocks[1, S5_CH - 1:0:-1], lag0[None], blocks[0, 1:S5_CH]], axis=0).astype(BF16)
    sf_r = lbr_j[S5_CH - 1 - s, 0]
    sf_i = lbi_j[S5_CH - 1 - s, 0]
    sb_r = lbr_j[s, 1]
    sb_i = lbi_j[s, 1]
    st = jnp.stack([sf_r, sf_i, sb_r, sb_i], axis=0)
    ws = jnp.einsum('qsgpi,gh->sgiqhp', st, eye_g).reshape(S5_ROWW, 4 * S5_NSTATE)
    tt = jnp.arange(S5_CH)
    cf_r = c_re[0][None] * pr[tt + 1, 0][:, :, None, :] - c_im[0][None] * pi[tt + 1, 0][:, :, None, :]
    cf_i = c_re[0][None] * pi[tt + 1, 0][:, :, None, :] + c_im[0][None] * pr[tt + 1, 0][:, :, None, :]
    cb_r = c_re[1][None] * pr[S5_CH - tt, 1][:, :, None, :] - c_im[1][None] * pi[S5_CH - tt, 1][:, :, None, :]
    cb_i = c_re[1][None] * pi[S5_CH - tt, 1][:, :, None, :] + c_im[1][None] * pr[S5_CH - tt, 1][:, :, None, :]
    ct = jnp.stack([cf_r, -cf_i, cb_r, -cb_i], axis=0)
    wc = jnp.einsum('qtgop,gh->qgptho', ct, eye_g).reshape(4 * S5_NSTATE, S5_ROWW)
    lam_p = jnp.stack([jnp.stack([pr[S5_CH, 0], pi[S5_CH, 0]]), jnp.stack([pr[S5_CH, 1], pi[S5_CH, 1]])])
    return ws.astype(BF16), e_all, wc.astype(BF16), lam_p.reshape(2, 2, 8, LANE)


S5_TN = 512
S5_NS = S5_ROWW // S5_W


def _s5_in_kernel(u_ref, ws_ref, e_ref, o_ref):
    j = pl.program_id(0)

    @pl.when(j < S5_NS)
    def _():
        o_ref[...] = jnp.dot(u_ref[...], ws_ref[...], preferred_element_type=F32)

    @pl.when(j >= S5_NS)
    def _():
        t = j - S5_NS
        acc = jnp.dot(u_ref[:, 0:S5_W], e_ref[S5_CH - 1 + t], preferred_element_type=F32)
        for s in range(1, S5_CH):
            acc = acc + jnp.dot(u_ref[:, s * S5_W:(s + 1) * S5_W], e_ref[S5_CH - 1 + t - s],
                                preferred_element_type=F32)
        o_ref[...] = acc


def _s5_in(u_rows, ws, e_all):
    return pl.pallas_call(
        _s5_in_kernel,
        grid=(2 * S5_NS,),
        in_specs=[pl.BlockSpec((N_CHUNK, S5_ROWW), lambda j: (0, 0)),
                  pl.BlockSpec((S5_ROWW, S5_W), lambda j: (0, jnp.minimum(j, S5_NS - 1))),
                  pl.BlockSpec((2 * S5_CH - 1, S5_W, S5_W), lambda j: (0, 0, 0))],
        out_specs=pl.BlockSpec((N_CHUNK, S5_W), lambda j: (0, j)),
        out_shape=jax.ShapeDtypeStruct((N_CHUNK, 2 * S5_ROWW), F32),
        compiler_params=_cp(48 << 20, 1),
        name="s5_in",
    )(u_rows, ws, e_all)


SLAB = 8
CH_ROWS = 4 * SLAB


def _s5_scan_kernel(s_ref, lam_ref, h_ref):
    lam = [[lam_ref[d, p] for p in range(2)] for d in range(2)]

    def step(b, d, chunk, h):
        base = pl.multiple_of(chunk * CH_ROWS + d * 2 * SLAB, SLAB)
        h_ref[pl.ds(base, SLAB), :] = h[0]
        h_ref[pl.ds(base + SLAB, SLAB), :] = h[1]
        s_r = s_ref[pl.ds(base, SLAB), :]
        s_i = s_ref[pl.ds(base + SLAB, SLAB), :]
        lr, li = lam[d]
        return (lr * h[0] - li * h[1] + s_r, lr * h[1] + li * h[0] + s_i)

    def chain_order(b, d, n_ctx_done):
        ctx0 = B * LAT_CHUNKS + b * CTX_CHUNKS
        lat0 = b * LAT_CHUNKS
        if d == 0:
            return (lambda i: ctx0 + i), (lambda i: lat0 + i)
        return (lambda i: ctx0 + CTX_CHUNKS - 1 - i), (lambda i: lat0 + LAT_CHUNKS - 1 - i)

    chains = [(b, d) for b in range(B) for d in range(2)]
    zero = jnp.zeros((SLAB, LANE), F32)
    init = tuple((zero, zero) for _ in chains)

    def phase(n_steps, which, carry):
        def body(i, hs):
            out = []
            for (b, d), h in zip(chains, hs):
                order = chain_order(b, d, 0)[which]
                out.append(step(b, d, order(i), h))
            return tuple(out)
        return lax.fori_loop(0, n_steps, body, carry)

    carry = phase(CTX_CHUNKS, 0, init)
    phase(LAT_CHUNKS, 1, carry)


def _s5_scan(s_slabs, lam_p):
    return pl.pallas_call(
        _s5_scan_kernel,
        out_shape=jax.ShapeDtypeStruct((N_CHUNK * CH_ROWS, LANE), F32),
        compiler_params=pltpu.CompilerParams(vmem_limit_bytes=48 << 20),
        name="s5_scan",
    )(s_slabs, lam_p)


S5_TM = N_CHUNK // 2


def _s5_out_kernel(h_ref, w_ref, y_ref, o_ref, hb_ref):
    @pl.when(pl.program_id(1) == 0)
    def _():
        hb_ref[...] = h_ref[...].astype(BF16)

    o_ref[...] = jnp.dot(hb_ref[...], w_ref[...], preferred_element_type=F32) + y_ref[...]


def _s5_out(h_rows, wc, sy):
    nn = S5_ROWW // S5_TN
    return pl.pallas_call(
        _s5_out_kernel,
        grid=(N_CHUNK // S5_TM, nn),
        in_specs=[pl.BlockSpec((S5_TM, 4 * S5_NSTATE), lambda i, j: (i, 0)),
                  pl.BlockSpec((4 * S5_NSTATE, S5_TN), lambda i, j: (0, j)),
                  pl.BlockSpec((S5_TM, S5_TN), lambda i, j: (i, nn + j))],
        out_specs=pl.BlockSpec((S5_TM, S5_TN), lambda i, j: (i, j)),
        out_shape=jax.ShapeDtypeStruct((N_CHUNK, S5_ROWW), F32),
        scratch_shapes=[pltpu.VMEM((S5_TM, 4 * S5_NSTATE), BF16)],
        compiler_params=_cp(48 << 20, 2),
        name="s5_out",
    )(h_rows, wc, sy)


def _rms(x):
    return x * lax.rsqrt(jnp.mean(x * x, axis=-1, keepdims=True) + LN_EPS)


def _layer_norm(x, g, b):
    mu = jnp.mean(x, axis=-1, keepdims=True)
    xc = x - mu
    var = jnp.mean(xc * xc, axis=-1, keepdims=True)
    return xc * lax.rsqrt(var + LN_EPS) * g + b


def _merge_kernel(x_ref, attn_ref, conv_ref, hu_ref, x0_ref, s5_ref, g1_ref, sh2_ref, sc2_ref, mixg_ref, hyd_ref,
                  wglu_ref, wout_ref, lng_ref, lnb_ref, o_ref, u2_ref):
    hy = (conv_ref[...] + hu_ref[...] * hyd_ref[...]) * x0_ref[...]
    g = jax.nn.gelu(s5_ref[...])
    s5 = g * jax.nn.sigmoid(jnp.dot(g.astype(BF16), wglu_ref[...], preferred_element_type=F32))
    mixg = mixg_ref[...]
    parts = [_rms(attn_ref[...]) * mixg[:, 0:ATTN_W],
             _rms(hy) * mixg[:, ATTN_W:ATTN_W + HY_W],
             _rms(s5) * mixg[:, ATTN_W + HY_W:MIX_W]]
    mix = jnp.concatenate(parts, axis=-1).astype(BF16)
    o = jnp.dot(mix, wout_ref[...], preferred_element_type=F32)
    x1 = _layer_norm(ALPHA * x_ref[...] + g1_ref[...] * o, lng_ref[...], lnb_ref[...])
    o_ref[...] = x1
    u2_ref[...] = x1 * (1.0 + sc2_ref[...]) + sh2_ref[...]


def _merge(n_rows, xall, attn, conv, hu, x0c, s5y, g1, sh2, sc2, mix_g, hy_d, wglu_bf, wout_bf, ln_g, ln_b):
    nt = n_rows // TM
    row = lambda i: (i, 0)
    full = lambda i: (0, 0)
    return pl.pallas_call(
        _merge_kernel,
        grid=(nt,),
        in_specs=[pl.BlockSpec((TM, D), row), pl.BlockSpec((TM, ATTN_W), row), pl.BlockSpec((TM, HY_W), row),
                  pl.BlockSpec((TM, HY_W), row), pl.BlockSpec((TM, HY_W), row), pl.BlockSpec((TM, S5_W), row),
                  _vec_spec(TM), _vec_spec(TM), _vec_spec(TM), pl.BlockSpec((1, MIX_W), full),
                  pl.BlockSpec((1, HY_W), full), pl.BlockSpec((S5_W, S5_W), full), pl.BlockSpec((MIX_W, D), full),
                  pl.BlockSpec((1, D), full), pl.BlockSpec((1, D), full)],
        out_specs=[pl.BlockSpec((TM, D), row), pl.BlockSpec((TM, D), row)],
        out_shape=[jax.ShapeDtypeStruct((n_rows, D), F32), jax.ShapeDtypeStruct((n_rows, D), F32)],
        compiler_params=_cp(48 << 20, 1),
        name="merge",
    )(xall, attn, conv, hu, x0c, s5y, g1, sh2, sc2, mix_g.reshape(1, MIX_W), hy_d.reshape(1, HY_W), wglu_bf, wout_bf,
      ln_g.reshape(1, D), ln_b.reshape(1, D))


def _router_kernel(u_ref, w_ref, b_ref, e_ref, g_ref):
    logits = jnp.dot(u_ref[...], w_ref[...], precision=HIGHEST, preferred_element_type=F32)
    scores = jax.nn.sigmoid(logits)
    biased = scores + b_ref[...]
    lane = lax.broadcasted_iota(jnp.int32, (TM, N_EXPERTS), 1)
    grp = lane // EGROUP
    ninf = -jnp.inf
    big = N_EXPERTS

    def first_argmax(vals):
        m = jnp.max(vals, axis=1, keepdims=True)
        idx = jnp.min(jnp.where(vals == m, lane, big), axis=1, keepdims=True)
        return m, idx

    gscore = []
    for gi in range(N_EGROUPS):
        vals = jnp.where(grp == gi, biased, ninf)
        m1, i1 = first_argmax(vals)
        m2 = jnp.max(jnp.where(lane == i1, ninf, vals), axis=1, keepdims=True)
        gscore.append(m1 + m2)
    keep = jnp.zeros((TM, N_EXPERTS), jnp.bool_)
    for gi in range(N_EGROUPS):
        rank = jnp.zeros((TM, 1), jnp.int32)
        for gj in range(N_EGROUPS):
            if gj == gi:
                continue
            ahead = (gscore[gj] > gscore[gi]) | ((gscore[gj] == gscore[gi]) & (gj < gi))
            rank = rank + ahead.astype(jnp.int32)
        keep = keep | ((grp == gi) & (rank < TOPK_GROUPS))
    masked = jnp.where(keep, biased, ninf)
    out_lane = lax.broadcasted_iota(jnp.int32, (TM, LANE), 1)
    e_out = jnp.zeros((TM, LANE), jnp.int32)
    g_out = jnp.zeros((TM, LANE), F32)
    gsum = jnp.zeros((TM, 1), F32)
    for kk in range(TOP_K):
        _, idx = first_argmax(masked)
        hit = lane == idx
        gate = jnp.sum(jnp.where(hit, scores, 0.0), axis=1, keepdims=True)
        masked = jnp.where(hit, ninf, masked)
        e_out = jnp.where(out_lane == kk, idx, e_out)
        g_out = jnp.where(out_lane == kk, gate, g_out)
        gsum = gsum + gate
    e_ref[...] = e_out
    g_ref[...] = g_out / gsum * ROUTED_SCALE


def _router(n_rows, u2, w_router, router_bias):
    nt = n_rows // TM
    row = lambda i: (i, 0)
    return pl.pallas_call(
        _router_kernel,
        grid=(nt,),
        in_specs=[pl.BlockSpec((TM, D), row),
                  pl.BlockSpec((D, N_EXPERTS), lambda i: (0, 0)), pl.BlockSpec((1, N_EXPERTS), lambda i: (0, 0))],
        out_specs=[pl.BlockSpec((TM, LANE), row), pl.BlockSpec((TM, LANE), row)],
        out_shape=[jax.ShapeDtypeStruct((n_rows, LANE), jnp.int32), jax.ShapeDtypeStruct((n_rows, LANE), F32)],
        compiler_params=_cp(32 << 20, 1),
        name="router",
    )(u2, w_router, router_bias.reshape(1, N_EXPERTS))


def _dispatch(top_e, gate):
    t = top_e.shape[0]
    tk = t * TOP_K
    nblk = tk // MOE_BLOCK
    n_steps = nblk + N_EXPERTS
    flat_e = top_e.reshape(tk)
    pos = jnp.arange(tk, dtype=jnp.int32)
    _, order, sw = lax.sort((flat_e, pos, gate.reshape(tk)), num_keys=1, is_stable=True)
    _, inv = lax.sort((order, pos), num_keys=1)
    comb_idx = inv.reshape(t, TOP_K).T.reshape(tk)
    experts = jnp.arange(N_EXPERTS, dtype=jnp.int32)
    counts = jnp.sum((flat_e[None, :] == experts[:, None]).astype(jnp.int32), axis=1)
    ends = jnp.cumsum(counts)
    starts = ends - counts
    fb = starts // MOE_BLOCK
    npairs = jnp.where(counts > 0, (ends - 1) // MOE_BLOCK - fb + 1, 0)
    pend = jnp.cumsum(npairs)
    poff = pend - npairs
    n_pairs = pend[-1]
    s = jnp.arange(n_steps, dtype=jnp.int32)
    sc = jnp.minimum(s, n_pairs - 1)
    pe = jnp.sum((pend[None, :] <= sc[:, None]).astype(jnp.int32), axis=1)
    pb = fb[pe] + (sc - poff[pe])
    lo = jnp.where(s < n_pairs, jnp.maximum(starts[pe] - pb * MOE_BLOCK, 0), 0)
    hi = jnp.where(s < n_pairs, jnp.minimum(ends[pe] - pb * MOE_BLOCK, MOE_BLOCK), 0)
    return dict(pe=pe, pb=pb, lo=lo, hi=hi, tok=order // TOP_K, comb=comb_idx, w=sw.reshape(nblk, MOE_BLOCK, 1))


def _expert_kernel(pe, pb, plo, phi, x_ref, w_ref, wg_ref, wu_ref, wd_ref, o_ref, wgb, wub, wdb):
    s = pl.program_id(0)
    prev = jnp.maximum(s - 1, 0)

    @pl.when((s == 0) | (pe[s] != pe[prev]))
    def _():
        wgb[...] = wg_ref[...].astype(BF16)
        wub[...] = wu_ref[...].astype(BF16)
        wdb[...] = wd_ref[...].astype(BF16)

    lo = plo[s]
    hi = phi[s]
    first_of_block = (s == 0) | (pb[s] != pb[prev])

    def ffn():
        xb = x_ref[...].astype(BF16)
        hg = jnp.dot(xb, wgb[...], preferred_element_type=F32)
        hu = jnp.dot(xb, wub[...], preferred_element_type=F32)
        h = (hg * jax.nn.sigmoid(hg)) * hu
        return jnp.dot(h.astype(BF16), wdb[...], preferred_element_type=F32) * w_ref[0]

    @pl.when((hi > lo) & first_of_block)
    def _():
        o_ref[...] = ffn()

    @pl.when((hi > lo) & jnp.logical_not(first_of_block))
    def _():
        row = lax.broadcasted_iota(jnp.int32, (MOE_BLOCK, D), 0)
        o_ref[...] = jnp.where((row >= lo) & (row < hi), ffn(), o_ref[...])


def _experts(xs, disp, wg, wu, wd, layer):
    tk = xs.shape[0]
    n_steps = disp['pe'].shape[0]
    blk = lambda s, pe, pb, lo, hi: (pb[s], 0)
    wspec = lambda shape: pl.BlockSpec((None, None) + shape, lambda s, pe, pb, lo, hi: (layer, pe[s], 0, 0))
    grid_spec = pltpu.PrefetchScalarGridSpec(
        num_scalar_prefetch=4,
        grid=(n_steps,),
        in_specs=[pl.BlockSpec((MOE_BLOCK, D), blk),
                  pl.BlockSpec((1, MOE_BLOCK, 1), lambda s, pe, pb, lo, hi: (pb[s], 0, 0)),
                  wspec((D, EXPERT_FF)), wspec((D, EXPERT_FF)), wspec((EXPERT_FF, D))],
        out_specs=pl.BlockSpec((MOE_BLOCK, D), blk),
        scratch_shapes=[pltpu.VMEM((D, EXPERT_FF), BF16), pltpu.VMEM((D, EXPERT_FF), BF16),
                        pltpu.VMEM((EXPERT_FF, D), BF16)],
    )
    return pl.pallas_call(
        _expert_kernel,
        grid_spec=grid_spec,
        out_shape=jax.ShapeDtypeStruct((tk, D), F32),
        compiler_params=_cp(40 << 20, 1),
        name="experts",
    )(disp['pe'], disp['pb'], disp['lo'], disp['hi'], xs, disp['w'], wg, wu, wd)


SC_ROWS = 64


def _row_gather(table, idx):
    n = idx.shape[0]
    d = table.shape[1]
    mesh = plsc.VectorSubcoreMesh(core_axis_name="c", subcore_axis_name="s")
    n_workers = mesh.num_cores * mesh.num_subcores
    per_worker = n // n_workers
    assert per_worker * n_workers == n and per_worker % SC_ROWS == 0

    @functools.partial(
        pl.kernel, mesh=mesh,
        out_type=jax.ShapeDtypeStruct((n, d), table.dtype),
        scratch_types=[pltpu.VMEM((SC_ROWS,), jnp.int32), pltpu.VMEM((SC_ROWS, d), table.dtype),
                       pltpu.SemaphoreType.DMA],
    )
    def gather(table_hbm, idx_hbm, out_hbm, idx_v, rows_v, sem):
        worker = lax.axis_index("s") * mesh.num_cores + lax.axis_index("c")
        base = worker * per_worker

        @pl.loop(0, per_worker // SC_ROWS)
        def _(c):
            off = pl.multiple_of(base + c * SC_ROWS, 8)
            pltpu.sync_copy(idx_hbm.at[pl.ds(off, SC_ROWS)], idx_v)
            pltpu.async_copy(table_hbm.at[idx_v], rows_v, sem).wait()
            pltpu.sync_copy(rows_v, out_hbm.at[pl.ds(off, SC_ROWS)])

    return gather(table, idx)


FM = 128


def _ffn_out_kernel(x_ref, u_ref, r_ref, g2_ref, wsg_ref, wsu_ref, wsd_ref, lng_ref, lnb_ref, o_ref):
    ub = u_ref[...].astype(BF16)
    hg = jnp.dot(ub, wsg_ref[...], preferred_element_type=F32)
    hu = jnp.dot(ub, wsu_ref[...], preferred_element_type=F32)
    f = jnp.dot(((hg * jax.nn.sigmoid(hg)) * hu).astype(BF16), wsd_ref[...], preferred_element_type=F32)
    routed = r_ref[0]
    for kk in range(1, TOP_K):
        routed = routed + r_ref[kk]
    f = routed + f
    o_ref[...] = _layer_norm(ALPHA * x_ref[...] + g2_ref[...] * f, lng_ref[...], lnb_ref[...])


def _ffn_out(n_rows, x1, u2, routed, g2, wsg_bf, wsu_bf, wsd_bf, ln_g, ln_b):
    row = lambda i: (i, 0)
    full = lambda i: (0, 0)
    return pl.pallas_call(
        _ffn_out_kernel,
        grid=(n_rows // FM,),
        in_specs=[pl.BlockSpec((FM, D), row), pl.BlockSpec((FM, D), row),
                  pl.BlockSpec((TOP_K, FM, D), lambda i: (0, i, 0)),
                  _vec_spec(FM), pl.BlockSpec((D, EXPERT_FF), full), pl.BlockSpec((D, EXPERT_FF), full),
                  pl.BlockSpec((EXPERT_FF, D), full), pl.BlockSpec((1, D), full), pl.BlockSpec((1, D), full)],
        out_specs=pl.BlockSpec((FM, D), row),
        out_shape=jax.ShapeDtypeStruct((n_rows, D), F32),
        compiler_params=_cp(40 << 20, 1),
        name="ffn_out",
    )(x1, u2, routed, g2, wsg_bf, wsu_bf, wsd_bf, ln_g.reshape(1, D), ln_b.reshape(1, D))


def _rope_tables():
    t = jnp.arange(L, dtype=jnp.int32)
    row = (t // GRID_W).astype(F32)
    col = (t % GRID_W).astype(F32)
    inv_freq = ROPE_BASE ** (-jnp.arange(0, AXIS_DIM, 2, dtype=F32) / AXIS_DIM)
    half = AXIS_DIM // 2

    def axis(pos):
        ang = pos[:, None] * inv_freq[None, :]
        c = jnp.cos(ang)
        s = jnp.sin(ang)
        return jnp.concatenate([c, c], axis=1), jnp.concatenate([-s, s], axis=1)

    cr, sr = axis(row)
    cc, sc = axis(col)
    cos_h = jnp.concatenate([cr, cc], axis=1)
    sin_h = jnp.concatenate([sr, sc], axis=1)
    cos_l = jnp.tile(cos_h, (B, LANE // HEAD_DIM))
    sin_l = jnp.tile(sin_h, (B, LANE // HEAD_DIM))
    cos_t = jnp.concatenate([cos_l, jnp.ones((T_CTX, LANE), F32)], axis=0)
    sin_t = jnp.concatenate([sin_l, jnp.zeros((T_CTX, LANE), F32)], axis=0)
    del half
    return cos_t, sin_t


def kernel(x, c, ctx, c_ctx, w_ada, b_ada, w_in, w_out, sink, mix_g, hy_short_w, hy_short_b, hy_w1, hy_b1, hy_freq,
           hy_w2, hy_b2, hy_w3, hy_d, s5_a_re, s5_a_im, s5_log_dt, s5_b_re, s5_b_im, s5_c_re, s5_c_im, s5_d, s5_w_glu,
           ln1_g, ln1_b, ln2_g, ln2_b, w_router, router_bias, w_exp_gate, w_exp_up, w_exp_down, w_sh_gate, w_sh_up,
           w_sh_down):
    xall = jnp.concatenate([x.reshape(T_LAT, D), ctx.reshape(T_CTX, D)], axis=0)
    cvec = jnp.concatenate([c, c_ctx[None, :], jnp.zeros((8 - B - 1, D), F32)], axis=0)
    mod = _ada(cvec, w_ada, b_ada)[:, 0:B + 1, :].reshape(DEPTH, B + 1, 6, 1, D)

    cos_t, sin_t = _rope_tables()
    g_cos, g_sin, f_cos, f_sin = _dft_tables()
    g_cos_bf, g_sin_bf = g_cos.astype(BF16), g_sin.astype(BF16)
    i_cos_bf = jnp.swapaxes(g_cos, 1, 2).astype(BF16)
    i_sin_bf = jnp.swapaxes(g_sin, 1, 2).astype(BF16)
    f_cos_bf, f_sin_bf = f_cos.astype(BF16), f_sin.astype(BF16)
    k1 = jnp.arange(K1P)
    spec_w = jnp.where((k1 == 0) | (k1 == FFT_R // 2), 1.0, 2.0) * (k1 < K1_USED) / N_FFT
    spec_w = spec_w.astype(F32)
    kt = jnp.arange(2 * C, dtype=jnp.int32)
    ang_c = ((kt[:, None] * kt[None, :]) % (2 * C)).astype(F32) * (2.0 * math.pi / (2 * C))
    d_cos, d_sin = jnp.cos(ang_c), jnp.sin(ang_c)

    for l in range(DEPTH):
        last = l == DEPTH - 1
        n_rows = T_LAT if last else T_ALL
        sh1, sc1, g1, sh2, sc2, g2 = (mod[l, :, j] for j in range(6))

        q, k, v, hz, s5u = _inproj(xall, sh1, sc1, w_in[l].astype(BF16), cos_t, sin_t)
        attn = _attention(sink[l], q, k, v, with_ctx=not last)

        hu, x0c = _hyena_pre(hz, hy_short_w[l], hy_short_b[l])
        filt_args = (hy_w1[l], hy_b1[l], hy_freq[l], hy_w2[l], hy_b2[l], hy_w3[l])
        k_r, k_i = _hyena_spectrum(_hyena_filter(L, *filt_args), g_cos, g_sin, f_cos, f_sin, spec_w)
        conv = _hyena_fft(hu, g_cos_bf, g_sin_bf, i_cos_bf, i_sin_bf, f_cos_bf, f_sin_bf, k_r, k_i)
        if not last:
            conv = jnp.concatenate([conv, _hyena_ctx(hu, _hyena_filter(C, *filt_args), d_cos, d_sin)], axis=0)

        ws, e_all, wc, lam_p = _s5_matrices(s5_a_re[l], s5_a_im[l], s5_log_dt[l], s5_b_re[l], s5_b_im[l],
                                            s5_c_re[l], s5_c_im[l], s5_d[l])
        sy = _s5_in(s5u.reshape(N_CHUNK, S5_ROWW), ws, e_all)
        s_slabs = sy[:, 0:4 * S5_NSTATE].reshape(N_CHUNK * CH_ROWS, LANE)
        h_rows = _s5_scan(s_slabs, lam_p).reshape(N_CHUNK, 4 * S5_NSTATE)
        s5y = _s5_out(h_rows, wc, sy).reshape(T_ALL, S5_W)

        x1, u2 = _merge(n_rows, xall, attn, conv, hu, x0c, s5y, g1, sh2, sc2, mix_g[l], hy_d[l],
                        s5_w_glu[l].astype(BF16), w_out[l].astype(BF16), ln1_g[l], ln1_b[l])

        top_e, gate = _router(n_rows, u2, w_router[l], router_bias[l])
        disp = _dispatch(top_e[:, 0:TOP_K], gate[:, 0:TOP_K])
        ys = _experts(_row_gather(u2, disp['tok']), disp, w_exp_gate, w_exp_up, w_exp_down, l)
        routed = _row_gather(ys, disp['comb']).reshape(TOP_K, n_rows, D)
        xall = _ffn_out(n_rows, x1, u2, routed, g2, w_sh_gate[l].astype(BF16), w_sh_up[l].astype(BF16),
                        w_sh_down[l].astype(BF16), ln2_g[l], ln2_b[l])
    return xall.reshape(B, L, D)
```

```python
import functools
import math

import jax
import jax.numpy as jnp
from jax import lax
from jax.experimental import pallas as pl
from jax.experimental.pallas import tpu as pltpu
from jax.experimental.pallas import tpu_sc as plsc

F32 = jnp.float32
BF16 = jnp.bfloat16
HIGHEST = lax.Precision.HIGHEST

D = 1024
B = 2
L = 8192
DEPTH = 2
GRID_W = 64
C = 256
T_LAT = B * L
T_CTX = B * C
T_ALL = T_LAT + T_CTX

HEAD_DIM = 64
N_Q = 8
N_KV = 2
Q_GROUP = N_Q // N_KV
ATTN_W = N_Q * HEAD_DIM
KV_W = N_KV * HEAD_DIM
HY_W = 256
S5_W = 256
MIX_W = ATTN_W + HY_W + S5_W
K_OFF = ATTN_W
V_OFF = K_OFF + KV_W
HY_OFF = V_OFF + KV_W
S5_OFF = HY_OFF + 3 * HY_W
IN_W = S5_OFF + S5_W
WINDOW = 128
BLK = 128
NEG_INF = -1e30
ROPE_BASE = 10000.0
AXIS_DIM = HEAD_DIM // 2

SHORT_K = 3
FILTER_EMB = 33
DECAY_FAST = 0.3
DECAY_SLOW = 1.5
DECAY_TARGET = 1e-2

S5_GROUP = 16
S5_GROUPS = S5_W // S5_GROUP
S5_STATE = 64
S5_NSTATE = S5_GROUPS * S5_STATE
S5_CH = 16
S5_ROWW = S5_CH * S5_W
N_CHUNK = T_ALL // S5_CH
LAT_CHUNKS = L // S5_CH
CTX_CHUNKS = C // S5_CH

N_EXPERTS = 256
TOP_K = 8
N_EGROUPS = 8
EGROUP = N_EXPERTS // N_EGROUPS
TOPK_GROUPS = 4
EXPERT_FF = 256
ROUTED_SCALE = 2.5
MOE_BLOCK = 256

ALPHA = (2 * DEPTH) ** 0.25
LN_EPS = 1e-5

N_FFT = 2 * L
FFT_R = 128
FFT_T1 = L // FFT_R
K1_USED = FFT_R // 2 + 1
K1P = 80
K1H = K1P // 2

TM = 256
LANE = 128
VMEM_CAP = 60000 * 1024


def _cp(vmem_bytes, n_axes):
    return pltpu.CompilerParams(
        dimension_semantics=("arbitrary",) * n_axes,
        vmem_limit_bytes=min(int(vmem_bytes), VMEM_CAP),
    )


def _mod_sel(rows_per_tile):
    per_batch = L // rows_per_tile
    return lambda i: jnp.minimum(i // per_batch, 2)


def _vec_spec(rows_per_tile):
    sel = _mod_sel(rows_per_tile)
    return pl.BlockSpec((None, 1, D), lambda i: (sel(i), 0, 0))


ADA_TN = 1536


def _ada_kernel(c_ref, w_ref, b_ref, o_ref):
    c = c_ref[...]
    s = c * jax.nn.sigmoid(c)
    o_ref[...] = jnp.dot(s, w_ref[...], precision=HIGHEST, preferred_element_type=F32) + b_ref[...]


def _ada(cvec, w_ada, b_ada):
    return pl.pallas_call(
        _ada_kernel,
        grid=(DEPTH, 6 * D // ADA_TN),
        in_specs=[
            pl.BlockSpec((8, D), lambda l, j: (0, 0)),
            pl.BlockSpec((None, D, ADA_TN), lambda l, j: (l, 0, j)),
            pl.BlockSpec((None, 1, ADA_TN), lambda l, j: (l, 0, j)),
        ],
        out_specs=pl.BlockSpec((None, 8, ADA_TN), lambda l, j: (l, 0, j)),
        out_shape=jax.ShapeDtypeStruct((DEPTH, 8, 6 * D), F32),
        compiler_params=_cp(40 << 20, 2),
        name="ada",
    )(cvec, w_ada, b_ada.reshape(DEPTH, 1, 6 * D))


def _inproj_kernel(x_ref, sh_ref, sc_ref, w_ref, cos_ref, sin_ref, q_ref, k_ref, v_ref, hy_ref, s5_ref):
    u = x_ref[...] * (1.0 + sc_ref[...]) + sh_ref[...]
    proj = jnp.dot(u.astype(BF16), w_ref[...], preferred_element_type=F32)
    cos = cos_ref[...]
    sin = sin_ref[...]
    lane = lax.broadcasted_iota(jnp.int32, (TM, LANE), 1)
    first_half = (lane % AXIS_DIM) < (AXIS_DIM // 2)

    def rope(xc):
        partner = jnp.where(first_half, pltpu.roll(xc, LANE - AXIS_DIM // 2, 1), pltpu.roll(xc, AXIS_DIM // 2, 1))
        return xc * cos + partner * sin

    for j in range(ATTN_W // LANE):
        q_ref[:, j * LANE:(j + 1) * LANE] = rope(proj[:, j * LANE:(j + 1) * LANE]).astype(BF16)
    k_ref[...] = rope(proj[:, K_OFF:V_OFF]).astype(BF16)
    v_ref[...] = proj[:, V_OFF:HY_OFF].astype(BF16)
    hy_ref[...] = proj[:, HY_OFF:S5_OFF]
    s5_ref[...] = proj[:, S5_OFF:IN_W].astype(BF16)


def _inproj(xall, sh, sc, w_in_bf, cos_t, sin_t):
    nt = T_ALL // TM
    row = lambda i: (i, 0)
    return pl.pallas_call(
        _inproj_kernel,
        grid=(nt,),
        in_specs=[
            pl.BlockSpec((TM, D), row),
            _vec_spec(TM),
            _vec_spec(TM),
            pl.BlockSpec((D, IN_W), lambda i: (0, 0)),
            pl.BlockSpec((TM, LANE), row),
            pl.BlockSpec((TM, LANE), row),
        ],
        out_specs=[
            pl.BlockSpec((TM, ATTN_W), row),
            pl.BlockSpec((TM, KV_W), row),
            pl.BlockSpec((TM, KV_W), row),
            pl.BlockSpec((TM, 3 * HY_W), row),
            pl.BlockSpec((TM, S5_W), row),
        ],
        out_shape=[
            jax.ShapeDtypeStruct((T_ALL, ATTN_W), BF16),
            jax.ShapeDtypeStruct((T_ALL, KV_W), BF16),
            jax.ShapeDtypeStruct((T_ALL, KV_W), BF16),
            jax.ShapeDtypeStruct((T_ALL, 3 * HY_W), F32),
            jax.ShapeDtypeStruct((T_ALL, S5_W), BF16),
        ],
        compiler_params=_cp(40 << 20, 1),
        name="inproj",
    )(xall, sh, sc, w_in_bf, cos_t, sin_t)


NB_LAT = L // BLK
NB_CTX = C // BLK


def _nt_dot(a, b):
    return lax.dot_general(a, b, (((1,), (1,)), ((), ())), preferred_element_type=F32)


def _attn_kernel(sink_ref, q_ref, kp_ref, kc_ref, kn_ref, kx_ref, vp_ref, vc_ref, vn_ref, vx_ref, o_ref):
    n = pl.program_id(1)
    is_lat = n < NB_LAT
    rows = Q_GROUP * BLK
    r = lax.broadcasted_iota(jnp.int32, (rows, BLK), 0) % BLK
    j = lax.broadcasted_iota(jnp.int32, (rows, BLK), 1)
    ok_prev = (j >= r) & (n >= 1) & is_lat
    ok_next = (j <= r) & (n + 1 < NB_LAT) & is_lat
    head_of_row = lax.broadcasted_iota(jnp.int32, (rows, 1), 0) // BLK
    q = q_ref[...] * (HEAD_DIM ** -0.5)
    dot = lambda a, b: jnp.dot(a.astype(BF16), b, preferred_element_type=F32)
    for kh in range(N_KV):
        hs = slice(kh * HEAD_DIM, (kh + 1) * HEAD_DIM)
        heads = range(kh * Q_GROUP, (kh + 1) * Q_GROUP)
        qg = jnp.concatenate([q[:, h * HEAD_DIM:(h + 1) * HEAD_DIM] for h in heads], axis=0)
        sk = jnp.zeros((rows, 1), F32)
        for g, h in enumerate(heads):
            sk = jnp.where(head_of_row == g, sink_ref[h], sk)
        s_p = jnp.where(ok_prev, _nt_dot(qg, kp_ref[:, hs]), NEG_INF)
        s_c = jnp.where(is_lat, _nt_dot(qg, kc_ref[:, hs]), NEG_INF)
        s_n = jnp.where(ok_next, _nt_dot(qg, kn_ref[:, hs]), NEG_INF)
        s_x = _nt_dot(qg, kx_ref[:, hs])
        s_x0 = s_x[:, 0:BLK]
        s_x1 = s_x[:, BLK:2 * BLK]
        m = jnp.maximum(jnp.maximum(jnp.maximum(s_p, s_c), jnp.maximum(s_n, s_x0)), s_x1)
        m = jnp.maximum(jnp.max(m, axis=1, keepdims=True), sk)
        e_p = jnp.exp(s_p - m)
        e_c = jnp.exp(s_c - m)
        e_n = jnp.exp(s_n - m)
        e_x0 = jnp.exp(s_x0 - m)
        e_x1 = jnp.exp(s_x1 - m)
        den = jnp.sum((e_p + e_c) + (e_n + e_x0) + e_x1, axis=1, keepdims=True) + jnp.exp(sk - m)
        o = (dot(e_p, vp_ref[:, hs]) + dot(e_c, vc_ref[:, hs]) + dot(e_n, vn_ref[:, hs])
             + dot(e_x0, vx_ref[0:BLK, hs]) + dot(e_x1, vx_ref[BLK:2 * BLK, hs]))
        o = o / den
        for g, h in enumerate(heads):
            o_ref[:, h * HEAD_DIM:(h + 1) * HEAD_DIM] = o[g * BLK:(g + 1) * BLK]


def _attention(sink, q, k, v, with_ctx):
    nblk = NB_LAT + (NB_CTX if with_ctx else 0)

    def q_idx(b, n):
        return (jnp.where(n < NB_LAT, b * NB_LAT + n, B * NB_LAT + b * NB_CTX + (n - NB_LAT)), 0)

    def kv_idx(off):
        def idx(b, n):
            nn = jnp.clip(jnp.minimum(n, NB_LAT - 1) + off, 0, NB_LAT - 1)
            return (b * NB_LAT + nn, 0)
        return idx

    ctx_idx = lambda b, n: (T_LAT // C + b, 0)
    kv_specs = lambda: [pl.BlockSpec((BLK, KV_W), kv_idx(-1)), pl.BlockSpec((BLK, KV_W), kv_idx(0)),
                        pl.BlockSpec((BLK, KV_W), kv_idx(1)), pl.BlockSpec((C, KV_W), ctx_idx)]
    return pl.pallas_call(
        _attn_kernel,
        grid=(B, nblk),
        in_specs=[pl.BlockSpec(memory_space=pltpu.SMEM), pl.BlockSpec((BLK, ATTN_W), q_idx)] + kv_specs() + kv_specs(),
        out_specs=pl.BlockSpec((BLK, ATTN_W), q_idx),
        out_shape=jax.ShapeDtypeStruct((T_ALL if with_ctx else T_LAT, ATTN_W), F32),
        compiler_params=_cp(32 << 20, 2),
        name="attention",
    )(sink, q, k, k, k, k, v, v, v, v)


def _hyena_pre_kernel(z_ref, zp_ref, zn_ref, w_ref, b_ref, u_ref, x0_ref):
    i = pl.program_id(0)
    tiles_per_seq = L // TM
    is_ctx = i >= B * tiles_per_seq
    first = is_ctx | (i % tiles_per_seq == 0)
    last = is_ctx | (i % tiles_per_seq == tiles_per_seq - 1)
    z = z_ref[...]
    prev_row = jnp.where(first, 0.0, zp_ref[7:8, :])
    next_row = jnp.where(last, 0.0, zn_ref[0:1, :])
    row = lax.broadcasted_iota(jnp.int32, z.shape, 0)
    z_m1 = jnp.where(row == 0, prev_row, pltpu.roll(z, 1, 0))
    z_p1 = jnp.where(row == TM - 1, next_row, pltpu.roll(z, TM - 1, 0))
    zc = b_ref[...] + z_m1 * w_ref[0:1, :] + z * w_ref[1:2, :] + z_p1 * w_ref[2:3, :]
    u_ref[...] = zc[:, 0:HY_W] * zc[:, HY_W:2 * HY_W]
    x0_ref[...] = zc[:, 2 * HY_W:3 * HY_W]


def _hyena_pre(z, short_w, short_b):
    nt = T_ALL // TM
    sub = TM // 8
    n8 = T_ALL // 8
    return pl.pallas_call(
        _hyena_pre_kernel,
        grid=(nt,),
        in_specs=[
            pl.BlockSpec((TM, 3 * HY_W), lambda i: (i, 0)),
            pl.BlockSpec((8, 3 * HY_W), lambda i: (jnp.maximum(i * sub - 1, 0), 0)),
            pl.BlockSpec((8, 3 * HY_W), lambda i: (jnp.minimum((i + 1) * sub, n8 - 1), 0)),
            pl.BlockSpec((SHORT_K, 3 * HY_W), lambda i: (0, 0)),
            pl.BlockSpec((1, 3 * HY_W), lambda i: (0, 0)),
        ],
        out_specs=[pl.BlockSpec((TM, HY_W), lambda i: (i, 0)), pl.BlockSpec((TM, HY_W), lambda i: (i, 0))],
        out_shape=[jax.ShapeDtypeStruct((T_ALL, HY_W), F32), jax.ShapeDtypeStruct((T_ALL, HY_W), F32)],
        compiler_params=_cp(32 << 20, 1),
        name="hyena_pre",
    )(z, z, z, short_w, short_b.reshape(1, 3 * HY_W))


def _dft_tables():
    t0 = jnp.arange(FFT_R, dtype=jnp.int32)[:, None, None]
    k1 = jnp.arange(K1P, dtype=jnp.int32)[None, :, None]
    t1 = jnp.arange(FFT_T1, dtype=jnp.int32)[None, None, :]
    m = (k1 * (FFT_R * t1 + t0)) % N_FFT
    ang = m.astype(F32) * (2.0 * math.pi / N_FFT)
    used = (k1 < K1_USED).astype(F32)
    g_cos = jnp.cos(ang) * used
    g_sin = jnp.sin(ang) * used
    a = jnp.arange(FFT_R, dtype=jnp.int32)
    ang2 = ((a[:, None] * a[None, :]) % FFT_R).astype(F32) * (2.0 * math.pi / FFT_R)
    return g_cos, g_sin, jnp.cos(ang2), jnp.sin(ang2)


def _hyena_spec_kernel(k_ref, gc_ref, gs_ref, fc_ref, fs_ref, wt_ref, kr_ref, ki_ref, ar_ref, ai_ref):
    half = pl.program_id(1)
    kk = lax.broadcasted_iota(jnp.int32, (K1H, 1), 0) + half * K1H
    sign = jnp.where(kk % 2 == 0, 1.0, -1.0).astype(F32)

    def stage1(t0, carry):
        x_lo = k_ref[pl.ds(t0, FFT_T1, stride=FFT_R), :]
        x_hi = k_ref[pl.ds(L + t0, FFT_T1, stride=FFT_R), :]
        gc = gc_ref[t0].astype(BF16)
        gs = gs_ref[t0].astype(BF16)
        dot = lambda a, b: jnp.dot(a, b.astype(BF16), preferred_element_type=F32)
        rows = pl.ds(pl.multiple_of(t0 * K1H, 8), K1H)
        ar_ref[rows, :] = dot(gc, x_lo) + sign * dot(gc, x_hi)
        ai_ref[rows, :] = -(dot(gs, x_lo) + sign * dot(gs, x_hi))
        return carry

    lax.fori_loop(0, FFT_R, stage1, 0)
    fc = fc_ref[...].astype(BF16)
    fs = fs_ref[...].astype(BF16)

    def stage2(kl, carry):
        a_r = ar_ref[pl.ds(kl, FFT_R, stride=K1H), :]
        a_i = ai_ref[pl.ds(kl, FFT_R, stride=K1H), :]
        dot = lambda a, b: jnp.dot(a, b.astype(BF16), preferred_element_type=F32)
        w = wt_ref[half * K1H + kl]
        kr_ref[kl] = (dot(fc, a_r) + dot(fs, a_i)) * w
        ki_ref[kl] = (dot(fc, a_i) - dot(fs, a_r)) * w
        return carry

    lax.fori_loop(0, K1H, stage2, 0)


def _hyena_spectrum(kfilt, g_cos, g_sin, f_cos, f_sin, wts):
    nct = HY_W // LANE
    gspec = pl.BlockSpec((FFT_R, K1H, FFT_T1), lambda c, h: (0, h, 0))
    fspec = pl.BlockSpec((FFT_R, FFT_R), lambda c, h: (0, 0))
    ospec = pl.BlockSpec((K1H, FFT_R, LANE), lambda c, h: (h, 0, c))
    return pl.pallas_call(
        _hyena_spec_kernel,
        grid=(nct, 2),
        in_specs=[pl.BlockSpec((N_FFT, LANE), lambda c, h: (0, c)), gspec, gspec, fspec, fspec,
                  pl.BlockSpec(memory_space=pltpu.SMEM)],
        out_specs=[ospec, ospec],
        out_shape=[jax.ShapeDtypeStruct((K1P, FFT_R, HY_W), F32)] * 2,
        scratch_shapes=[pltpu.VMEM((FFT_R * K1H, LANE), F32)] * 2,
        compiler_params=_cp(56 << 20, 2),
        name="hyena_spectrum",
    )(kfilt, g_cos, g_sin, f_cos, f_sin, wts)


def _hyena_fft_kernel(u_ref, gc_ref, gs_ref, ic_ref, is_ref, fc_ref, fs_ref, kr_ref, ki_ref, o_ref, ar_ref, ai_ref):
    bdot = lambda a, b: jnp.dot(a, b.astype(BF16), preferred_element_type=F32)

    def stage1(t0, carry):
        x = u_ref[pl.ds(t0, FFT_T1, stride=FFT_R), :]
        rows = pl.ds(pl.multiple_of(t0 * K1P, 8), K1P)
        ar_ref[rows, :] = bdot(gc_ref[t0], x)
        ai_ref[rows, :] = -bdot(gs_ref[t0], x)
        return carry

    lax.fori_loop(0, FFT_R, stage1, 0)
    fc = fc_ref[...]
    fs = fs_ref[...]

    def stage23(k1, carry):
        rows = pl.ds(k1, FFT_R, stride=K1P)
        a_r = ar_ref[rows, :]
        a_i = ai_ref[rows, :]
        z_r = bdot(fc, a_r) + bdot(fs, a_i)
        z_i = bdot(fc, a_i) - bdot(fs, a_r)
        k_r = kr_ref[k1]
        k_i = ki_ref[k1]
        y_r = z_r * k_r - z_i * k_i
        y_i = z_r * k_i + z_i * k_r
        ar_ref[rows, :] = bdot(fc, y_r) - bdot(fs, y_i)
        ai_ref[rows, :] = bdot(fc, y_i) + bdot(fs, y_r)
        return carry

    lax.fori_loop(0, K1_USED, stage23, 0)

    def stage4(t0, carry):
        rows = pl.ds(pl.multiple_of(t0 * K1P, 8), K1P)
        o_ref[pl.ds(t0, FFT_T1, stride=FFT_R), :] = bdot(ic_ref[t0], ar_ref[rows, :]) - bdot(is_ref[t0], ai_ref[rows, :])
        return carry

    lax.fori_loop(0, FFT_R, stage4, 0)


def _hyena_fft(u, g_cos_bf, g_sin_bf, i_cos_bf, i_sin_bf, f_cos_bf, f_sin_bf, k_r, k_i):
    nct = HY_W // LANE
    one = pl.Buffered(1)
    gspec = pl.BlockSpec((FFT_R, K1P, FFT_T1), lambda c, b: (0, 0, 0), pipeline_mode=one)
    ispec = pl.BlockSpec((FFT_R, FFT_T1, K1P), lambda c, b: (0, 0, 0), pipeline_mode=one)
    fspec = pl.BlockSpec((FFT_R, FFT_R), lambda c, b: (0, 0), pipeline_mode=one)
    kspec = pl.BlockSpec((K1P, FFT_R, LANE), lambda c, b: (0, 0, c), pipeline_mode=one)
    return pl.pallas_call(
        _hyena_fft_kernel,
        grid=(nct, B),
        in_specs=[pl.BlockSpec((L, LANE), lambda c, b: (b, c)), gspec, gspec, ispec, ispec, fspec, fspec, kspec, kspec],
        out_specs=pl.BlockSpec((L, LANE), lambda c, b: (b, c)),
        out_shape=jax.ShapeDtypeStruct((T_LAT, HY_W), F32),
        scratch_shapes=[pltpu.VMEM((FFT_R * K1P, LANE), F32)] * 2,
        compiler_params=_cp(56 << 20, 2),
        name="hyena_fft",
    )(u, g_cos_bf, g_sin_bf, i_cos_bf, i_sin_bf, f_cos_bf, f_sin_bf, k_r, k_i)


def _hyena_ctx_kernel(u_ref, k_ref, dc_ref, ds_ref, o_ref):
    dot = lambda a, b: jnp.dot(a, b, precision=HIGHEST, preferred_element_type=F32)
    dc = dc_ref[...]
    ds = ds_ref[...]
    u = u_ref[...]
    kf = k_ref[...]
    u_r = dot(dc[:, 0:C], u)
    u_i = -dot(ds[:, 0:C], u)
    k_r = dot(dc, kf)
    k_i = -dot(ds, kf)
    y_r = u_r * k_r - u_i * k_i
    y_i = u_r * k_i + u_i * k_r
    o_ref[...] = (dot(dc[0:C, :], y_r) - dot(ds[0:C, :], y_i)) * (1.0 / (2 * C))


def _hyena_ctx(u, kfilt_ctx, d_cos, d_sin):
    full = lambda b: (0, 0)
    return pl.pallas_call(
        _hyena_ctx_kernel,
        grid=(B,),
        in_specs=[pl.BlockSpec((C, HY_W), lambda b: (T_LAT // C + b, 0)),
                  pl.BlockSpec((2 * C, HY_W), full), pl.BlockSpec((2 * C, 2 * C), full), pl.BlockSpec((2 * C, 2 * C), full)],
        out_specs=pl.BlockSpec((C, HY_W), lambda b: (b, 0)),
        out_shape=jax.ShapeDtypeStruct((T_CTX, HY_W), F32),
        compiler_params=_cp(32 << 20, 1),
        name="hyena_ctx",
    )(u, kfilt_ctx, d_cos, d_sin)


def _hyena_filter(n, w1, b1, freq, w2, b2, w3):
    t = jnp.linspace(0.0, 1.0, n, dtype=F32)[:, None]
    bands = (FILTER_EMB - 1) // 2
    w = 2.0 * math.pi * jnp.arange(n, dtype=F32)[:, None] / n
    f = jnp.linspace(1e-4, bands - 1, bands, dtype=F32)[None, :]
    z = jnp.concatenate([t, jnp.cos(f * w), -jnp.sin(f * w)], axis=-1)
    mm = functools.partial(jnp.matmul, precision=HIGHEST)
    h = jnp.sin(freq * (mm(z, w1) + b1))
    h = jnp.sin(freq * (mm(h, w2) + b2))
    h = mm(h, w3)
    deltas = jnp.abs(jnp.linspace(math.log(DECAY_TARGET) / DECAY_FAST, math.log(DECAY_TARGET) / DECAY_SLOW,
                                  HY_W, dtype=F32))
    decay = jnp.exp(-t * deltas[None, :])
    h_fwd = h[:, :HY_W] * decay
    h_bwd = h[:, HY_W:] * decay
    k = jnp.concatenate([h_fwd, jnp.zeros((1, HY_W), F32), h_bwd[:0:-1]], axis=0)
    return k / jnp.sum(jnp.abs(k), axis=0, keepdims=True)


def _s5_matrices(a_re, a_im, log_dt, b_re, b_im, c_re, c_im, d_skip):
    dt = jnp.exp(log_dt)[:, :, None]
    lam_re = jnp.minimum(a_re, -1e-4)
    mag1 = jnp.exp(lam_re * dt)
    lbr = mag1 * jnp.cos(a_im * dt)
    lbi = mag1 * jnp.sin(a_im * dt)
    den = lam_re * lam_re + a_im * a_im
    fr = ((lbr - 1.0) * lam_re + lbi * a_im) / den
    fi = (lbi * lam_re - (lbr - 1.0) * a_im) / den
    bbr = fr[..., None] * b_re - fi[..., None] * b_im
    bbi = fr[..., None] * b_im + fi[..., None] * b_re
    j = jnp.arange(S5_CH + 1, dtype=F32)[:, None, None, None]
    magj = jnp.exp(j * (lam_re * dt)[None])
    pr = magj * jnp.cos(j * (a_im * dt)[None])
    pi = magj * jnp.sin(j * (a_im * dt)[None])
    hi = functools.partial(jnp.einsum, precision=HIGHEST)
    lbr_j = pr[..., None] * bbr[None] - pi[..., None] * bbi[None]
    lbi_j = pr[..., None] * bbi[None] + pi[..., None] * bbr[None]
    m = hi('dgop,jdgpi->jdgoi', c_re, lbr_j) - hi('dgop,jdgpi->jdgoi', c_im, lbi_j)
    eye_g = jnp.eye(S5_GROUPS, dtype=F32)
    s = jnp.arange(S5_CH)
    blocks = jnp.einsum('jdgoi,gh->djgiho', m[0:S5_CH], eye_g).reshape(2, S5_CH, S5_W, S5_W)
    lag0 = blocks[0, 0] + blocks[1, 0] + jnp.diag(d_skip)
    e_all = jnp.concatenate([blocks[1, S5_CH - 1:0:-1], lag0[None], blocks[0, 1:S5_CH]], axis=0).astype(BF16)
    sf_r = lbr_j[S5_CH - 1 - s, 0]
    sf_i = lbi_j[S5_CH - 1 - s, 0]
    sb_r = lbr_j[s, 1]
    sb_i = lbi_j[s, 1]
    st = jnp.stack([sf_r, sf_i, sb_r, sb_i], axis=0)
    ws = jnp.einsum('qsgpi,gh->sgiqhp', st, eye_g).reshape(S5_ROWW, 4 * S5_NSTATE)
    tt = jnp.arange(S5_CH)
    cf_r = c_re[0][None] * pr[tt + 1, 0][:, :, None, :] - c_im[0][None] * pi[tt + 1, 0][:, :, None, :]
    cf_i = c_re[0][None] * pi[tt + 1, 0][:, :, None, :] + c_im[0][None] * pr[tt + 1, 0][:, :, None, :]
    cb_r = c_re[1][None] * pr[S5_CH - tt, 1][:, :, None, :] - c_im[1][None] * pi[S5_CH - tt, 1][:, :, None, :]
    cb_i = c_re[1][None] * pi[S5_CH - tt, 1][:, :, None, :] + c_im[1][None] * pr[S5_CH - tt, 1][:, :, None, :]
    ct = jnp.stack([cf_r, -cf_i, cb_r, -cb_i], axis=0)
    wc = jnp.einsum('qtgop,gh->qgptho', ct, eye_g).reshape(4 * S5_NSTATE, S5_ROWW)
    lam_p = jnp.stack([jnp.stack([pr[S5_CH, 0], pi[S5_CH, 0]]), jnp.stack([pr[S5_CH, 1], pi[S5_CH, 1]])])
    return ws.astype(BF16), e_all, wc.astype(BF16), lam_p.reshape(2, 2, 8, LANE)


S5_TN = 512
S5_NS = S5_ROWW // S5_W


def _s5_in_kernel(u_ref, ws_ref, e_ref, o_ref):
    j = pl.program_id(0)

    @pl.when(j < S5_NS)
    def _():
        o_ref[...] = jnp.dot(u_ref[...], ws_ref[...], preferred_element_type=F32)

    @pl.when(j >= S5_NS)
    def _():
        t = j - S5_NS
        acc = jnp.dot(u_ref[:, 0:S5_W], e_ref[S5_CH - 1 + t], preferred_element_type=F32)
        for s in range(1, S5_CH):
            acc = acc + jnp.dot(u_ref[:, s * S5_W:(s + 1) * S5_W], e_ref[S5_CH - 1 + t - s],
                                preferred_element_type=F32)
        o_ref[...] = acc


def _s5_in(u_rows, ws, e_all):
    return pl.pallas_call(
        _s5_in_kernel,
        grid=(2 * S5_NS,),
        in_specs=[pl.BlockSpec((N_CHUNK, S5_ROWW), lambda j: (0, 0)),
                  pl.BlockSpec((S5_ROWW, S5_W), lambda j: (0, jnp.minimum(j, S5_NS - 1))),
                  pl.BlockSpec((2 * S5_CH - 1, S5_W, S5_W), lambda j: (0, 0, 0))],
        out_specs=pl.BlockSpec((N_CHUNK, S5_W), lambda j: (0, j)),
        out_shape=jax.ShapeDtypeStruct((N_CHUNK, 2 * S5_ROWW), F32),
        compiler_params=_cp(48 << 20, 1),
        name="s5_in",
    )(u_rows, ws, e_all)


SLAB = 8
CH_ROWS = 4 * SLAB


def _s5_scan_kernel(s_ref, lam_ref, h_ref):
    lam = [[lam_ref[d, p] for p in range(2)] for d in range(2)]

    def step(b, d, chunk, h):
        base = pl.multiple_of(chunk * CH_ROWS + d * 2 * SLAB, SLAB)
        h_ref[pl.ds(base, SLAB), :] = h[0]
        h_ref[pl.ds(base + SLAB, SLAB), :] = h[1]
        s_r = s_ref[pl.ds(base, SLAB), :]
        s_i = s_ref[pl.ds(base + SLAB, SLAB), :]
        lr, li = lam[d]
        return (lr * h[0] - li * h[1] + s_r, lr * h[1] + li * h[0] + s_i)

    def chain_order(b, d, n_ctx_done):
        ctx0 = B * LAT_CHUNKS + b * CTX_CHUNKS
        lat0 = b * LAT_CHUNKS
        if d == 0:
            return (lambda i: ctx0 + i), (lambda i: lat0 + i)
        return (lambda i: ctx0 + CTX_CHUNKS - 1 - i), (lambda i: lat0 + LAT_CHUNKS - 1 - i)

    chains = [(b, d) for b in range(B) for d in range(2)]
    zero = jnp.zeros((SLAB, LANE), F32)
    init = tuple((zero, zero) for _ in chains)

    def phase(n_steps, which, carry):
        def body(i, hs):
            out = []
            for (b, d), h in zip(chains, hs):
                order = chain_order(b, d, 0)[which]
                out.append(step(b, d, order(i), h))
            return tuple(out)
        return lax.fori_loop(0, n_steps, body, carry)

    carry = phase(CTX_CHUNKS, 0, init)
    phase(LAT_CHUNKS, 1, carry)


def _s5_scan(s_slabs, lam_p):
    return pl.pallas_call(
        _s5_scan_kernel,
        out_shape=jax.ShapeDtypeStruct((N_CHUNK * CH_ROWS, LANE), F32),
        compiler_params=pltpu.CompilerParams(vmem_limit_bytes=48 << 20),
        name="s5_scan",
    )(s_slabs, lam_p)


S5_TM = N_CHUNK // 2


def _s5_out_kernel(h_ref, w_ref, y_ref, o_ref, hb_ref):
    @pl.when(pl.program_id(1) == 0)
    def _():
        hb_ref[...] = h_ref[...].astype(BF16)

    o_ref[...] = jnp.dot(hb_ref[...], w_ref[...], preferred_element_type=F32) + y_ref[...]


def _s5_out(h_rows, wc, sy):
    nn = S5_ROWW // S5_TN
    return pl.pallas_call(
        _s5_out_kernel,
        grid=(N_CHUNK // S5_TM, nn),
        in_specs=[pl.BlockSpec((S5_TM, 4 * S5_NSTATE), lambda i, j: (i, 0)),
                  pl.BlockSpec((4 * S5_NSTATE, S5_TN), lambda i, j: (0, j)),
                  pl.BlockSpec((S5_TM, S5_TN), lambda i, j: (i, nn + j))],
        out_specs=pl.BlockSpec((S5_TM, S5_TN), lambda i, j: (i, j)),
        out_shape=jax.ShapeDtypeStruct((N_CHUNK, S5_ROWW), F32),
        scratch_shapes=[pltpu.VMEM((S5_TM, 4 * S5_NSTATE), BF16)],
        compiler_params=_cp(48 << 20, 2),
        name="s5_out",
    )(h_rows, wc, sy)


def _rms(x):
    return x * lax.rsqrt(jnp.mean(x * x, axis=-1, keepdims=True) + LN_EPS)


def _layer_norm(x, g, b):
    mu = jnp.mean(x, axis=-1, keepdims=True)
    xc = x - mu
    var = jnp.mean(xc * xc, axis=-1, keepdims=True)
    return xc * lax.rsqrt(var + LN_EPS) * g + b


def _merge_kernel(x_ref, attn_ref, conv_ref, hu_ref, x0_ref, s5_ref, g1_ref, sh2_ref, sc2_ref, mixg_ref, hyd_ref,
                  wglu_ref, wout_ref, lng_ref, lnb_ref, o_ref, u2_ref):
    hy = (conv_ref[...] + hu_ref[...] * hyd_ref[...]) * x0_ref[...]
    g = jax.nn.gelu(s5_ref[...])
    s5 = g * jax.nn.sigmoid(jnp.dot(g.astype(BF16), wglu_ref[...], preferred_element_type=F32))
    mixg = mixg_ref[...]
    parts = [_rms(attn_ref[...]) * mixg[:, 0:ATTN_W],
             _rms(hy) * mixg[:, ATTN_W:ATTN_W + HY_W],
             _rms(s5) * mixg[:, ATTN_W + HY_W:MIX_W]]
    mix = jnp.concatenate(parts, axis=-1).astype(BF16)
    o = jnp.dot(mix, wout_ref[...], preferred_element_type=F32)
    x1 = _layer_norm(ALPHA * x_ref[...] + g1_ref[...] * o, lng_ref[...], lnb_ref[...])
    o_ref[...] = x1
    u2_ref[...] = x1 * (1.0 + sc2_ref[...]) + sh2_ref[...]


def _merge(n_rows, xall, attn, conv, hu, x0c, s5y, g1, sh2, sc2, mix_g, hy_d, wglu_bf, wout_bf, ln_g, ln_b):
    nt = n_rows // TM
    row = lambda i: (i, 0)
    full = lambda i: (0, 0)
    return pl.pallas_call(
        _merge_kernel,
        grid=(nt,),
        in_specs=[pl.BlockSpec((TM, D), row), pl.BlockSpec((TM, ATTN_W), row), pl.BlockSpec((TM, HY_W), row),
                  pl.BlockSpec((TM, HY_W), row), pl.BlockSpec((TM, HY_W), row), pl.BlockSpec((TM, S5_W), row),
                  _vec_spec(TM), _vec_spec(TM), _vec_spec(TM), pl.BlockSpec((1, MIX_W), full),
                  pl.BlockSpec((1, HY_W), full), pl.BlockSpec((S5_W, S5_W), full), pl.BlockSpec((MIX_W, D), full),
                  pl.BlockSpec((1, D), full), pl.BlockSpec((1, D), full)],
        out_specs=[pl.BlockSpec((TM, D), row), pl.BlockSpec((TM, D), row)],
        out_shape=[jax.ShapeDtypeStruct((n_rows, D), F32), jax.ShapeDtypeStruct((n_rows, D), F32)],
        compiler_params=_cp(48 << 20, 1),
        name="merge",
    )(xall, attn, conv, hu, x0c, s5y, g1, sh2, sc2, mix_g.reshape(1, MIX_W), hy_d.reshape(1, HY_W), wglu_bf, wout_bf,
      ln_g.reshape(1, D), ln_b.reshape(1, D))


def _router_kernel(u_ref, w_ref, b_ref, e_ref, g_ref):
    logits = jnp.dot(u_ref[...], w_ref[...], precision=HIGHEST, preferred_element_type=F32)
    scores = jax.nn.sigmoid(logits)
    biased = scores + b_ref[...]
    lane = lax.broadcasted_iota(jnp.int32, (TM, N_EXPERTS), 1)
    grp = lane // EGROUP
    ninf = -jnp.inf
    big = N_EXPERTS

    def first_argmax(vals):
        m = jnp.max(vals, axis=1, keepdims=True)
        idx = jnp.min(jnp.where(vals == m, lane, big), axis=1, keepdims=True)
        return m, idx

    gscore = []
    for gi in range(N_EGROUPS):
        vals = jnp.where(grp == gi, biased, ninf)
        m1, i1 = first_argmax(vals)
        m2 = jnp.max(jnp.where(lane == i1, ninf, vals), axis=1, keepdims=True)
        gscore.append(m1 + m2)
    keep = jnp.zeros((TM, N_EXPERTS), jnp.bool_)
    for gi in range(N_EGROUPS):
        rank = jnp.zeros((TM, 1), jnp.int32)
        for gj in range(N_EGROUPS):
            if gj == gi:
                continue
            ahead = (gscore[gj] > gscore[gi]) | ((gscore[gj] == gscore[gi]) & (gj < gi))
            rank = rank + ahead.astype(jnp.int32)
        keep = keep | ((grp == gi) & (rank < TOPK_GROUPS))
    masked = jnp.where(keep, biased, ninf)
    out_lane = lax.broadcasted_iota(jnp.int32, (TM, LANE), 1)
    e_out = jnp.zeros((TM, LANE), jnp.int32)
    g_out = jnp.zeros((TM, LANE), F32)
    gsum = jnp.zeros((TM, 1), F32)
    for kk in range(TOP_K):
        _, idx = first_argmax(masked)
        hit = lane == idx
        gate = jnp.sum(jnp.where(hit, scores, 0.0), axis=1, keepdims=True)
        masked = jnp.where(hit, ninf, masked)
        e_out = jnp.where(out_lane == kk, idx, e_out)
        g_out = jnp.where(out_lane == kk, gate, g_out)
        gsum = gsum + gate
    e_ref[...] = e_out
    g_ref[...] = g_out / gsum * ROUTED_SCALE


def _router(n_rows, u2, w_router, router_bias):
    nt = n_rows // TM
    row = lambda i: (i, 0)
    return pl.pallas_call(
        _router_kernel,
        grid=(nt,),
        in_specs=[pl.BlockSpec((TM, D), row),
                  pl.BlockSpec((D, N_EXPERTS), lambda i: (0, 0)), pl.BlockSpec((1, N_EXPERTS), lambda i: (0, 0))],
        out_specs=[pl.BlockSpec((TM, LANE), row), pl.BlockSpec((TM, LANE), row)],
        out_shape=[jax.ShapeDtypeStruct((n_rows, LANE), jnp.int32), jax.ShapeDtypeStruct((n_rows, LANE), F32)],
        compiler_params=_cp(32 << 20, 1),
        name="router",
    )(u2, w_router, router_bias.reshape(1, N_EXPERTS))


def _dispatch(top_e, gate):
    t = top_e.shape[0]
    tk = t * TOP_K
    nblk = tk // MOE_BLOCK
    n_steps = nblk + N_EXPERTS
    flat_e = top_e.reshape(tk)
    pos = jnp.arange(tk, dtype=jnp.int32)
    _, order, sw = lax.sort((flat_e, pos, gate.reshape(tk)), num_keys=1, is_stable=True)
    _, inv = lax.sort((order, pos), num_keys=1)
    comb_idx = inv.reshape(t, TOP_K).T.reshape(tk)
    experts = jnp.arange(N_EXPERTS, dtype=jnp.int32)
    counts = jnp.sum((flat_e[None, :] == experts[:, None]).astype(jnp.int32), axis=1)
    ends = jnp.cumsum(counts)
    starts = ends - counts
    fb = starts // MOE_BLOCK
    npairs = jnp.where(counts > 0, (ends - 1) // MOE_BLOCK - fb + 1, 0)
    pend = jnp.cumsum(npairs)
    poff = pend - npairs
    n_pairs = pend[-1]
    s = jnp.arange(n_steps, dtype=jnp.int32)
    sc = jnp.minimum(s, n_pairs - 1)
    pe = jnp.sum((pend[None, :] <= sc[:, None]).astype(jnp.int32), axis=1)
    pb = fb[pe] + (sc - poff[pe])
    lo = jnp.where(s < n_pairs, jnp.maximum(starts[pe] - pb * MOE_BLOCK, 0), 0)
    hi = jnp.where(s < n_pairs, jnp.minimum(ends[pe] - pb * MOE_BLOCK, MOE_BLOCK), 0)
    return dict(pe=pe, pb=pb, lo=lo, hi=hi, tok=order // TOP_K, comb=comb_idx, w=sw.reshape(nblk, MOE_BLOCK, 1))


def _expert_kernel(pe, pb, plo, phi, x_ref, w_ref, wg_ref, wu_ref, wd_ref, o_ref, wgb, wub, wdb):
    s = pl.program_id(0)
    prev = jnp.maximum(s - 1, 0)

    @pl.when((s == 0) | (pe[s] != pe[prev]))
    def _():
        wgb[...] = wg_ref[...].astype(BF16)
        wub[...] = wu_ref[...].astype(BF16)
        wdb[...] = wd_ref[...].astype(BF16)

    lo = plo[s]
    hi = phi[s]
    first_of_block = (s == 0) | (pb[s] != pb[prev])

    def ffn():
        xb = x_ref[...].astype(BF16)
        hg = jnp.dot(xb, wgb[...], preferred_element_type=F32)
        hu = jnp.dot(xb, wub[...], preferred_element_type=F32)
        h = (hg * jax.nn.sigmoid(hg)) * hu
        return jnp.dot(h.astype(BF16), wdb[...], preferred_element_type=F32) * w_ref[0]

    @pl.when((hi > lo) & first_of_block)
    def _():
        o_ref[...] = ffn()

    @pl.when((hi > lo) & jnp.logical_not(first_of_block))
    def _():
        row = lax.broadcasted_iota(jnp.int32, (MOE_BLOCK, D), 0)
        o_ref[...] = jnp.where((row >= lo) & (row < hi), ffn(), o_ref[...])


def _experts(xs, disp, wg, wu, wd, layer):
    tk = xs.shape[0]
    n_steps = disp['pe'].shape[0]
    blk = lambda s, pe, pb, lo, hi: (pb[s], 0)
    wspec = lambda shape: pl.BlockSpec((None, None) + shape, lambda s, pe, pb, lo, hi: (layer, pe[s], 0, 0))
    grid_spec = pltpu.PrefetchScalarGridSpec(
        num_scalar_prefetch=4,
        grid=(n_steps,),
        in_specs=[pl.BlockSpec((MOE_BLOCK, D), blk),
                  pl.BlockSpec((1, MOE_BLOCK, 1), lambda s, pe, pb, lo, hi: (pb[s], 0, 0)),
                  wspec((D, EXPERT_FF)), wspec((D, EXPERT_FF)), wspec((EXPERT_FF, D))],
        out_specs=pl.BlockSpec((MOE_BLOCK, D), blk),
        scratch_shapes=[pltpu.VMEM((D, EXPERT_FF), BF16), pltpu.VMEM((D, EXPERT_FF), BF16),
                        pltpu.VMEM((EXPERT_FF, D), BF16)],
    )
    return pl.pallas_call(
        _expert_kernel,
        grid_spec=grid_spec,
        out_shape=jax.ShapeDtypeStruct((tk, D), F32),
        compiler_params=_cp(40 << 20, 1),
        name="experts",
    )(disp['pe'], disp['pb'], disp['lo'], disp['hi'], xs, disp['w'], wg, wu, wd)


SC_ROWS = 64


def _row_gather(table, idx):
    n = idx.shape[0]
    d = table.shape[1]
    mesh = plsc.VectorSubcoreMesh(core_axis_name="c", subcore_axis_name="s")
    n_workers = mesh.num_cores * mesh.num_subcores
    per_worker = n // n_workers
    assert per_worker * n_workers == n and per_worker % SC_ROWS == 0

    @functools.partial(
        pl.kernel, mesh=mesh,
        out_type=jax.ShapeDtypeStruct((n, d), table.dtype),
        scratch_types=[pltpu.VMEM((SC_ROWS,), jnp.int32), pltpu.VMEM((SC_ROWS, d), table.dtype),
                       pltpu.SemaphoreType.DMA],
    )
    def gather(table_hbm, idx_hbm, out_hbm, idx_v, rows_v, sem):
        worker = lax.axis_index("s") * mesh.num_cores + lax.axis_index("c")
        base = worker * per_worker

        @pl.loop(0, per_worker // SC_ROWS)
        def _(c):
            off = pl.multiple_of(base + c * SC_ROWS, 8)
            pltpu.sync_copy(idx_hbm.at[pl.ds(off, SC_ROWS)], idx_v)
            pltpu.async_copy(table_hbm.at[idx_v], rows_v, sem).wait()
            pltpu.sync_copy(rows_v, out_hbm.at[pl.ds(off, SC_ROWS)])

    return gather(table, idx)


FM = 128


def _ffn_out_kernel(x_ref, u_ref, r_ref, g2_ref, wsg_ref, wsu_ref, wsd_ref, lng_ref, lnb_ref, o_ref):
    ub = u_ref[...].astype(BF16)
    hg = jnp.dot(ub, wsg_ref[...], preferred_element_type=F32)
    hu = jnp.dot(ub, wsu_ref[...], preferred_element_type=F32)
    f = jnp.dot(((hg * jax.nn.sigmoid(hg)) * hu).astype(BF16), wsd_ref[...], preferred_element_type=F32)
    routed = r_ref[0]
    for kk in range(1, TOP_K):
        routed = routed + r_ref[kk]
    f = routed + f
    o_ref[...] = _layer_norm(ALPHA * x_ref[...] + g2_ref[...] * f, lng_ref[...], lnb_ref[...])


def _ffn_out(n_rows, x1, u2, routed, g2, wsg_bf, wsu_bf, wsd_bf, ln_g, ln_b):
    row = lambda i: (i, 0)
    full = lambda i: (0, 0)
    return pl.pallas_call(
        _ffn_out_kernel,
        grid=(n_rows // FM,),
        in_specs=[pl.BlockSpec((FM, D), row), pl.BlockSpec((FM, D), row),
                  pl.BlockSpec((TOP_K, FM, D), lambda i: (0, i, 0)),
                  _vec_spec(FM), pl.BlockSpec((D, EXPERT_FF), full), pl.BlockSpec((D, EXPERT_FF), full),
                  pl.BlockSpec((EXPERT_FF, D), full), pl.BlockSpec((1, D), full), pl.BlockSpec((1, D), full)],
        out_specs=pl.BlockSpec((FM, D), row),
        out_shape=jax.ShapeDtypeStruct((n_rows, D), F32),
        compiler_params=_cp(40 << 20, 1),
        name="ffn_out",
    )(x1, u2, routed, g2, wsg_bf, wsu_bf, wsd_bf, ln_g.reshape(1, D), ln_b.reshape(1, D))


def _rope_tables():
    t = jnp.arange(L, dtype=jnp.int32)
    row = (t // GRID_W).astype(F32)
    col = (t % GRID_W).astype(F32)
    inv_freq = ROPE_BASE ** (-jnp.arange(0, AXIS_DIM, 2, dtype=F32) / AXIS_DIM)
    half = AXIS_DIM // 2

    def axis(pos):
        ang = pos[:, None] * inv_freq[None, :]
        c = jnp.cos(ang)
        s = jnp.sin(ang)
        return jnp.concatenate([c, c], axis=1), jnp.concatenate([-s, s], axis=1)

    cr, sr = axis(row)
    cc, sc = axis(col)
    cos_h = jnp.concatenate([cr, cc], axis=1)
    sin_h = jnp.concatenate([sr, sc], axis=1)
    cos_l = jnp.tile(cos_h, (B, LANE // HEAD_DIM))
    sin_l = jnp.tile(sin_h, (B, LANE // HEAD_DIM))
    cos_t = jnp.concatenate([cos_l, jnp.ones((T_CTX, LANE), F32)], axis=0)
    sin_t = jnp.concatenate([sin_l, jnp.zeros((T_CTX, LANE), F32)], axis=0)
    del half
    return cos_t, sin_t


def kernel(x, c, ctx, c_ctx, w_ada, b_ada, w_in, w_out, sink, mix_g, hy_short_w, hy_short_b, hy_w1, hy_b1, hy_freq,
           hy_w2, hy_b2, hy_w3, hy_d, s5_a_re, s5_a_im, s5_log_dt, s5_b_re, s5_b_im, s5_c_re, s5_c_im, s5_d, s5_w_glu,
           ln1_g, ln1_b, ln2_g, ln2_b, w_router, router_bias, w_exp_gate, w_exp_up, w_exp_down, w_sh_gate, w_sh_up,
           w_sh_down):
    xall = jnp.concatenate([x.reshape(T_LAT, D), ctx.reshape(T_CTX, D)], axis=0)
    cvec = jnp.concatenate([c, c_ctx[None, :], jnp.zeros((8 - B - 1, D), F32)], axis=0)
    mod = _ada(cvec, w_ada, b_ada)[:, 0:B + 1, :].reshape(DEPTH, B + 1, 6, 1, D)

    cos_t, sin_t = _rope_tables()
    g_cos, g_sin, f_cos, f_sin = _dft_tables()
    g_cos_bf, g_sin_bf = g_cos.astype(BF16), g_sin.astype(BF16)
    i_cos_bf = jnp.swapaxes(g_cos, 1, 2).astype(BF16)
    i_sin_bf = jnp.swapaxes(g_sin, 1, 2).astype(BF16)
    f_cos_bf, f_sin_bf = f_cos.astype(BF16), f_sin.astype(BF16)
    k1 = jnp.arange(K1P)
    spec_w = jnp.where((k1 == 0) | (k1 == FFT_R // 2), 1.0, 2.0) * (k1 < K1_USED) / N_FFT
    spec_w = spec_w.astype(F32)
    kt = jnp.arange(2 * C, dtype=jnp.int32)
    ang_c = ((kt[:, None] * kt[None, :]) % (2 * C)).astype(F32) * (2.0 * math.pi / (2 * C))
    d_cos, d_sin = jnp.cos(ang_c), jnp.sin(ang_c)

    for l in range(DEPTH):
        last = l == DEPTH - 1
        n_rows = T_LAT if last else T_ALL
        sh1, sc1, g1, sh2, sc2, g2 = (mod[l, :, j] for j in range(6))

        q, k, v, hz, s5u = _inproj(xall, sh1, sc1, w_in[l].astype(BF16), cos_t, sin_t)
        attn = _attention(sink[l], q, k, v, with_ctx=not last)

        hu, x0c = _hyena_pre(hz, hy_short_w[l], hy_short_b[l])
        filt_args = (hy_w1[l], hy_b1[l], hy_freq[l], hy_w2[l], hy_b2[l], hy_w3[l])
        k_r, k_i = _hyena_spectrum(_hyena_filter(L, *filt_args), g_cos, g_sin, f_cos, f_sin, spec_w)
        conv = _hyena_fft(hu, g_cos_bf, g_sin_bf, i_cos_bf, i_sin_bf, f_cos_bf, f_sin_bf, k_r, k_i)
        if not last:
            conv = jnp.concatenate([conv, _hyena_ctx(hu, _hyena_filter(C, *filt_args), d_cos, d_sin)], axis=0)

        ws, e_all, wc, lam_p = _s5_matrices(s5_a_re[l], s5_a_im[l], s5_log_dt[l], s5_b_re[l], s5_b_im[l],
                                            s5_c_re[l], s5_c_im[l], s5_d[l])
        sy = _s5_in(s5u.reshape(N_CHUNK, S5_ROWW), ws, e_all)
        s_slabs = sy[:, 0:4 * S5_NSTATE].reshape(N_CHUNK * CH_ROWS, LANE)
        h_rows = _s5_scan(s_slabs, lam_p).reshape(N_CHUNK, 4 * S5_NSTATE)
        s5y = _s5_out(h_rows, wc, sy).reshape(T_ALL, S5_W)

        x1, u2 = _merge(n_rows, xall, attn, conv, hu, x0c, s5y, g1, sh2, sc2, mix_g[l], hy_d[l],
                        s5_w_glu[l].astype(BF16), w_out[l].astype(BF16), ln1_g[l], ln1_b[l])

        top_e, gate = _router(n_rows, u2, w_router[l], router_bias[l])
        disp = _dispatch(top_e[:, 0:TOP_K], gate[:, 0:TOP_K])
        ys = _experts(_row_gather(u2, disp['tok']), disp, w_exp_gate, w_exp_up, w_exp_down, l)
        routed = _row_gather(ys, disp['comb']).reshape(TOP_K, n_rows, D)
        xall = _ffn_out(n_rows, x1, u2, routed, g2, w_sh_gate[l].astype(BF16), w_sh_up[l].astype(BF16),
                        w_sh_down[l].astype(BF16), ln2_g[l], ln2_b[l])
    return xall.reshape(B, L, D)
```

```python
import functools
import math

import jax
import jax.numpy as jnp
from jax import lax
from jax.experimental import pallas as pl
from jax.experimental.pallas import tpu as pltpu
from jax.experimental.pallas import tpu_sc as plsc

F32 = jnp.float32
BF16 = jnp.bfloat16
HIGHEST = lax.Precision.HIGHEST

D = 1024
B = 2
L = 8192
DEPTH = 2
GRID_W = 64
C = 256
T_LAT = B * L
T_CTX = B * C
T_ALL = T_LAT + T_CTX

HEAD_DIM = 64
N_Q = 8
N_KV = 2
Q_GROUP = N_Q // N_KV
ATTN_W = N_Q * HEAD_DIM
KV_W = N_KV * HEAD_DIM
HY_W = 256
S5_W = 256
MIX_W = ATTN_W + HY_W + S5_W
K_OFF = ATTN_W
V_OFF = K_OFF + KV_W
HY_OFF = V_OFF + KV_W
S5_OFF = HY_OFF + 3 * HY_W
IN_W = S5_OFF + S5_W
WINDOW = 128
BLK = 128
NEG_INF = -1e30
ROPE_BASE = 10000.0
AXIS_DIM = HEAD_DIM // 2

SHORT_K = 3
FILTER_EMB = 33
DECAY_FAST = 0.3
DECAY_SLOW = 1.5
DECAY_TARGET = 1e-2

S5_GROUP = 16
S5_GROUPS = S5_W // S5_GROUP
S5_STATE = 64
S5_NSTATE = S5_GROUPS * S5_STATE
S5_CH = 16
S5_ROWW = S5_CH * S5_W
N_CHUNK = T_ALL // S5_CH
LAT_CHUNKS = L // S5_CH
CTX_CHUNKS = C // S5_CH

N_EXPERTS = 256
TOP_K = 8
N_EGROUPS = 8
EGROUP = N_EXPERTS // N_EGROUPS
TOPK_GROUPS = 4
EXPERT_FF = 256
ROUTED_SCALE = 2.5
MOE_BLOCK = 256

ALPHA = (2 * DEPTH) ** 0.25
LN_EPS = 1e-5

N_FFT = 2 * L
FFT_R = 128
FFT_T1 = L // FFT_R
K1_USED = FFT_R // 2 + 1
K1P = 80
K1H = K1P // 2

TM = 256
LANE = 128
VMEM_CAP = 60000 * 1024


def _cp(vmem_bytes, n_axes):
    return pltpu.CompilerParams(
        dimension_semantics=("arbitrary",) * n_axes,
        vmem_limit_bytes=min(int(vmem_bytes), VMEM_CAP),
    )


def _mod_sel(rows_per_tile):
    per_batch = L // rows_per_tile
    return lambda i: jnp.minimum(i // per_batch, 2)


def _vec_spec(rows_per_tile):
    sel = _mod_sel(rows_per_tile)
    return pl.BlockSpec((None, 1, D), lambda i: (sel(i), 0, 0))


ADA_TN = 1536


def _ada_kernel(c_ref, w_ref, b_ref, o_ref):
    c = c_ref[...]
    s = c * jax.nn.sigmoid(c)
    o_ref[...] = jnp.dot(s, w_ref[...], precision=HIGHEST, preferred_element_type=F32) + b_ref[...]


def _ada(cvec, w_ada, b_ada):
    return pl.pallas_call(
        _ada_kernel,
        grid=(DEPTH, 6 * D // ADA_TN),
        in_specs=[
            pl.BlockSpec((8, D), lambda l, j: (0, 0)),
            pl.BlockSpec((None, D, ADA_TN), lambda l, j: (l, 0, j)),
            pl.BlockSpec((None, 1, ADA_TN), lambda l, j: (l, 0, j)),
        ],
        out_specs=pl.BlockSpec((None, 8, ADA_TN), lambda l, j: (l, 0, j)),
        out_shape=jax.ShapeDtypeStruct((DEPTH, 8, 6 * D), F32),
        compiler_params=_cp(40 << 20, 2),
        name="ada",
    )(cvec, w_ada, b_ada.reshape(DEPTH, 1, 6 * D))


def _inproj_kernel(x_ref, sh_ref, sc_ref, w_ref, cos_ref, sin_ref, q_ref, k_ref, v_ref, hy_ref, s5_ref, *s5_tok_refs):
    u = x_ref[...] * (1.0 + sc_ref[...]) + sh_ref[...]
    proj = jnp.dot(u.astype(BF16), w_ref[...], preferred_element_type=F32)
    cos = cos_ref[...]
    sin = sin_ref[...]
    lane = lax.broadcasted_iota(jnp.int32, (TM, LANE), 1)
    first_half = (lane % AXIS_DIM) < (AXIS_DIM // 2)

    def rope(xc):
        partner = jnp.where(first_half, pltpu.roll(xc, LANE - AXIS_DIM // 2, 1), pltpu.roll(xc, AXIS_DIM // 2, 1))
        return xc * cos + partner * sin

    for j in range(ATTN_W // LANE):
        q_ref[:, j * LANE:(j + 1) * LANE] = rope(proj[:, j * LANE:(j + 1) * LANE]).astype(BF16)
    k_ref[...] = rope(proj[:, K_OFF:V_OFF]).astype(BF16)
    v_ref[...] = proj[:, V_OFF:HY_OFF].astype(BF16)
    hy_ref[...] = proj[:, HY_OFF:S5_OFF]
    for h, tok_ref in enumerate(s5_tok_refs):
        tok_ref[...] = proj[:, S5_OFF + h * LANE:S5_OFF + (h + 1) * LANE]
    for s in range(S5_CH):
        for h, tok_ref in enumerate(s5_tok_refs):
            s5_ref[:, s * S5_W + h * LANE:s * S5_W + (h + 1) * LANE] = (
                tok_ref[pl.ds(s, TM // S5_CH, stride=S5_CH), :].astype(BF16))


def _inproj(xall, sh, sc, w_in_bf, cos_t, sin_t):
    nt = T_ALL // TM
    row = lambda i: (i, 0)
    return pl.pallas_call(
        _inproj_kernel,
        grid=(nt,),
        in_specs=[
            pl.BlockSpec((TM, D), row),
            _vec_spec(TM),
            _vec_spec(TM),
            pl.BlockSpec((D, IN_W), lambda i: (0, 0)),
            pl.BlockSpec((TM, LANE), row),
            pl.BlockSpec((TM, LANE), row),
        ],
        out_specs=[
            pl.BlockSpec((TM, ATTN_W), row),
            pl.BlockSpec((TM, KV_W), row),
            pl.BlockSpec((TM, KV_W), row),
            pl.BlockSpec((TM, 3 * HY_W), row),
            pl.BlockSpec((TM // S5_CH, S5_ROWW), row),
        ],
        out_shape=[
            jax.ShapeDtypeStruct((T_ALL, ATTN_W), BF16),
            jax.ShapeDtypeStruct((T_ALL, KV_W), BF16),
            jax.ShapeDtypeStruct((T_ALL, KV_W), BF16),
            jax.ShapeDtypeStruct((T_ALL, 3 * HY_W), F32),
            jax.ShapeDtypeStruct((N_CHUNK, S5_ROWW), BF16),
        ],
        scratch_shapes=[pltpu.VMEM((TM, LANE), F32)] * (S5_W // LANE),
        compiler_params=_cp(40 << 20, 1),
        name="inproj",
    )(xall, sh, sc, w_in_bf, cos_t, sin_t)


NB_LAT = L // BLK
NB_CTX = C // BLK


def _nt_dot(a, b):
    return lax.dot_general(a, b, (((1,), (1,)), ((), ())), preferred_element_type=F32)


def _attn_kernel(sink_ref, q_ref, kp_ref, kc_ref, kn_ref, kx_ref, vp_ref, vc_ref, vn_ref, vx_ref, o_ref):
    n = pl.program_id(1)
    is_lat = n < NB_LAT
    rows = Q_GROUP * BLK
    r = lax.broadcasted_iota(jnp.int32, (rows, BLK), 0) % BLK
    j = lax.broadcasted_iota(jnp.int32, (rows, BLK), 1)
    ok_prev = (j >= r) & (n >= 1) & is_lat
    ok_next = (j <= r) & (n + 1 < NB_LAT) & is_lat
    head_of_row = lax.broadcasted_iota(jnp.int32, (rows, 1), 0) // BLK
    q = q_ref[...] * (HEAD_DIM ** -0.5)
    dot = lambda a, b: jnp.dot(a.astype(BF16), b, preferred_element_type=F32)
    for kh in range(N_KV):
        hs = slice(kh * HEAD_DIM, (kh + 1) * HEAD_DIM)
        heads = range(kh * Q_GROUP, (kh + 1) * Q_GROUP)
        qg = jnp.concatenate([q[:, h * HEAD_DIM:(h + 1) * HEAD_DIM] for h in heads], axis=0)
        sk = jnp.zeros((rows, 1), F32)
        for g, h in enumerate(heads):
            sk = jnp.where(head_of_row == g, sink_ref[h], sk)
        s_p = jnp.where(ok_prev, _nt_dot(qg, kp_ref[:, hs]), NEG_INF)
        s_c = jnp.where(is_lat, _nt_dot(qg, kc_ref[:, hs]), NEG_INF)
        s_n = jnp.where(ok_next, _nt_dot(qg, kn_ref[:, hs]), NEG_INF)
        s_x = _nt_dot(qg, kx_ref[:, hs])
        s_x0 = s_x[:, 0:BLK]
        s_x1 = s_x[:, BLK:2 * BLK]
        m = jnp.maximum(jnp.maximum(jnp.maximum(s_p, s_c), jnp.maximum(s_n, s_x0)), s_x1)
        m = jnp.maximum(jnp.max(m, axis=1, keepdims=True), sk)
        e_p = jnp.exp(s_p - m)
        e_c = jnp.exp(s_c - m)
        e_n = jnp.exp(s_n - m)
        e_x0 = jnp.exp(s_x0 - m)
        e_x1 = jnp.exp(s_x1 - m)
        den = jnp.sum((e_p + e_c) + (e_n + e_x0) + e_x1, axis=1, keepdims=True) + jnp.exp(sk - m)
        o = (dot(e_p, vp_ref[:, hs]) + dot(e_c, vc_ref[:, hs]) + dot(e_n, vn_ref[:, hs])
             + dot(e_x0, vx_ref[0:BLK, hs]) + dot(e_x1, vx_ref[BLK:2 * BLK, hs]))
        o = o / den
        for g, h in enumerate(heads):
            o_ref[:, h * HEAD_DIM:(h + 1) * HEAD_DIM] = o[g * BLK:(g + 1) * BLK]


def _attention(sink, q, k, v, with_ctx):
    nblk = NB_LAT + (NB_CTX if with_ctx else 0)

    def q_idx(b, n):
        return (jnp.where(n < NB_LAT, b * NB_LAT + n, B * NB_LAT + b * NB_CTX + (n - NB_LAT)), 0)

    def kv_idx(off):
        def idx(b, n):
            nn = jnp.clip(jnp.minimum(n, NB_LAT - 1) + off, 0, NB_LAT - 1)
            return (b * NB_LAT + nn, 0)
        return idx

    ctx_idx = lambda b, n: (T_LAT // C + b, 0)
    kv_specs = lambda: [pl.BlockSpec((BLK, KV_W), kv_idx(-1)), pl.BlockSpec((BLK, KV_W), kv_idx(0)),
                        pl.BlockSpec((BLK, KV_W), kv_idx(1)), pl.BlockSpec((C, KV_W), ctx_idx)]
    return pl.pallas_call(
        _attn_kernel,
        grid=(B, nblk),
        in_specs=[pl.BlockSpec(memory_space=pltpu.SMEM), pl.BlockSpec((BLK, ATTN_W), q_idx)] + kv_specs() + kv_specs(),
        out_specs=pl.BlockSpec((BLK, ATTN_W), q_idx),
        out_shape=jax.ShapeDtypeStruct((T_ALL if with_ctx else T_LAT, ATTN_W), F32),
        compiler_params=_cp(32 << 20, 2),
        name="attention",
    )(sink, q, k, k, k, k, v, v, v, v)


def _hyena_pre_kernel(z_ref, zp_ref, zn_ref, w_ref, b_ref, u_ref, x0_ref):
    i = pl.program_id(0)
    tiles_per_seq = L // TM
    is_ctx = i >= B * tiles_per_seq
    first = is_ctx | (i % tiles_per_seq == 0)
    last = is_ctx | (i % tiles_per_seq == tiles_per_seq - 1)
    z = z_ref[...]
    prev_row = jnp.where(first, 0.0, zp_ref[7:8, :])
    next_row = jnp.where(last, 0.0, zn_ref[0:1, :])
    row = lax.broadcasted_iota(jnp.int32, z.shape, 0)
    z_m1 = jnp.where(row == 0, prev_row, pltpu.roll(z, 1, 0))
    z_p1 = jnp.where(row == TM - 1, next_row, pltpu.roll(z, TM - 1, 0))
    zc = b_ref[...] + z_m1 * w_ref[0:1, :] + z * w_ref[1:2, :] + z_p1 * w_ref[2:3, :]
    u_ref[...] = zc[:, 0:HY_W] * zc[:, HY_W:2 * HY_W]
    x0_ref[...] = zc[:, 2 * HY_W:3 * HY_W]


def _hyena_pre(z, short_w, short_b):
    nt = T_ALL // TM
    sub = TM // 8
    n8 = T_ALL // 8
    return pl.pallas_call(
        _hyena_pre_kernel,
        grid=(nt,),
        in_specs=[
            pl.BlockSpec((TM, 3 * HY_W), lambda i: (i, 0)),
            pl.BlockSpec((8, 3 * HY_W), lambda i: (jnp.maximum(i * sub - 1, 0), 0)),
            pl.BlockSpec((8, 3 * HY_W), lambda i: (jnp.minimum((i + 1) * sub, n8 - 1), 0)),
            pl.BlockSpec((SHORT_K, 3 * HY_W), lambda i: (0, 0)),
            pl.BlockSpec((1, 3 * HY_W), lambda i: (0, 0)),
        ],
        out_specs=[pl.BlockSpec((TM, HY_W), lambda i: (i, 0)), pl.BlockSpec((TM, HY_W), lambda i: (i, 0))],
        out_shape=[jax.ShapeDtypeStruct((T_ALL, HY_W), F32), jax.ShapeDtypeStruct((T_ALL, HY_W), F32)],
        compiler_params=_cp(32 << 20, 1),
        name="hyena_pre",
    )(z, z, z, short_w, short_b.reshape(1, 3 * HY_W))


def _dft_tables():
    t0 = jnp.arange(FFT_R, dtype=jnp.int32)[:, None, None]
    k1 = jnp.arange(K1P, dtype=jnp.int32)[None, :, None]
    t1 = jnp.arange(FFT_T1, dtype=jnp.int32)[None, None, :]
    m = (k1 * (FFT_R * t1 + t0)) % N_FFT
    ang = m.astype(F32) * (2.0 * math.pi / N_FFT)
    used = (k1 < K1_USED).astype(F32)
    g_cos = jnp.cos(ang) * used
    g_sin = jnp.sin(ang) * used
    a = jnp.arange(FFT_R, dtype=jnp.int32)
    ang2 = ((a[:, None] * a[None, :]) % FFT_R).astype(F32) * (2.0 * math.pi / FFT_R)
    return g_cos, g_sin, jnp.cos(ang2), jnp.sin(ang2)


def _hyena_spec_kernel(k_ref, gc_ref, gs_ref, fc_ref, fs_ref, wt_ref, kr_ref, ki_ref, ar_ref, ai_ref):
    half = pl.program_id(1)
    kk = lax.broadcasted_iota(jnp.int32, (K1H, 1), 0) + half * K1H
    sign = jnp.where(kk % 2 == 0, 1.0, -1.0).astype(F32)

    def stage1(t0, carry):
        x_lo = k_ref[pl.ds(t0, FFT_T1, stride=FFT_R), :]
        x_hi = k_ref[pl.ds(L + t0, FFT_T1, stride=FFT_R), :]
        gc = gc_ref[t0].astype(BF16)
        gs = gs_ref[t0].astype(BF16)
        dot = lambda a, b: jnp.dot(a, b.astype(BF16), preferred_element_type=F32)
        rows = pl.ds(pl.multiple_of(t0 * K1H, 8), K1H)
        ar_ref[rows, :] = dot(gc, x_lo) + sign * dot(gc, x_hi)
        ai_ref[rows, :] = -(dot(gs, x_lo) + sign * dot(gs, x_hi))
        return carry

    lax.fori_loop(0, FFT_R, stage1, 0)
    fc = fc_ref[...].astype(BF16)
    fs = fs_ref[...].astype(BF16)

    def stage2(kl, carry):
        a_r = ar_ref[pl.ds(kl, FFT_R, stride=K1H), :]
        a_i = ai_ref[pl.ds(kl, FFT_R, stride=K1H), :]
        dot = lambda a, b: jnp.dot(a, b.astype(BF16), preferred_element_type=F32)
        w = wt_ref[half * K1H + kl]
        kr_ref[kl] = (dot(fc, a_r) + dot(fs, a_i)) * w
        ki_ref[kl] = (dot(fc, a_i) - dot(fs, a_r)) * w
        return carry

    lax.fori_loop(0, K1H, stage2, 0)


def _hyena_spectrum(kfilt, g_cos, g_sin, f_cos, f_sin, wts):
    nct = HY_W // LANE
    gspec = pl.BlockSpec((FFT_R, K1H, FFT_T1), lambda c, h: (0, h, 0))
    fspec = pl.BlockSpec((FFT_R, FFT_R), lambda c, h: (0, 0))
    ospec = pl.BlockSpec((K1H, FFT_R, LANE), lambda c, h: (h, 0, c))
    return pl.pallas_call(
        _hyena_spec_kernel,
        grid=(nct, 2),
        in_specs=[pl.BlockSpec((N_FFT, LANE), lambda c, h: (0, c)), gspec, gspec, fspec, fspec,
                  pl.BlockSpec(memory_space=pltpu.SMEM)],
        out_specs=[ospec, ospec],
        out_shape=[jax.ShapeDtypeStruct((K1P, FFT_R, HY_W), F32)] * 2,
        scratch_shapes=[pltpu.VMEM((FFT_R * K1H, LANE), F32)] * 2,
        compiler_params=_cp(56 << 20, 2),
        name="hyena_spectrum",
    )(kfilt, g_cos, g_sin, f_cos, f_sin, wts)


def _hyena_fft_kernel(u_ref, gc_ref, gs_ref, ic_ref, is_ref, fc_ref, fs_ref, kr_ref, ki_ref, o_ref, ar_ref, ai_ref):
    bdot = lambda a, b: jnp.dot(a, b.astype(BF16), preferred_element_type=F32)

    def stage1(t0, carry):
        x = u_ref[pl.ds(t0, FFT_T1, stride=FFT_R), :]
        rows = pl.ds(pl.multiple_of(t0 * K1P, 8), K1P)
        ar_ref[rows, :] = bdot(gc_ref[t0], x)
        ai_ref[rows, :] = -bdot(gs_ref[t0], x)
        return carry

    lax.fori_loop(0, FFT_R, stage1, 0)
    fc = fc_ref[...]
    fs = fs_ref[...]

    def stage23(k1, carry):
        rows = pl.ds(k1, FFT_R, stride=K1P)
        a_r = ar_ref[rows, :]
        a_i = ai_ref[rows, :]
        z_r = bdot(fc, a_r) + bdot(fs, a_i)
        z_i = bdot(fc, a_i) - bdot(fs, a_r)
        k_r = kr_ref[k1]
        k_i = ki_ref[k1]
        y_r = z_r * k_r - z_i * k_i
        y_i = z_r * k_i + z_i * k_r
        ar_ref[rows, :] = bdot(fc, y_r) - bdot(fs, y_i)
        ai_ref[rows, :] = bdot(fc, y_i) + bdot(fs, y_r)
        return carry

    lax.fori_loop(0, K1_USED, stage23, 0)

    def stage4(t0, carry):
        rows = pl.ds(pl.multiple_of(t0 * K1P, 8), K1P)
        o_ref[pl.ds(t0, FFT_T1, stride=FFT_R), :] = bdot(ic_ref[t0], ar_ref[rows, :]) - bdot(is_ref[t0], ai_ref[rows, :])
        return carry

    lax.fori_loop(0, FFT_R, stage4, 0)


def _hyena_fft(u, g_cos_bf, g_sin_bf, i_cos_bf, i_sin_bf, f_cos_bf, f_sin_bf, k_r, k_i):
    nct = HY_W // LANE
    one = pl.Buffered(1)
    gspec = pl.BlockSpec((FFT_R, K1P, FFT_T1), lambda c, b: (0, 0, 0), pipeline_mode=one)
    ispec = pl.BlockSpec((FFT_R, FFT_T1, K1P), lambda c, b: (0, 0, 0), pipeline_mode=one)
    fspec = pl.BlockSpec((FFT_R, FFT_R), lambda c, b: (0, 0), pipeline_mode=one)
    kspec = pl.BlockSpec((K1P, FFT_R, LANE), lambda c, b: (0, 0, c), pipeline_mode=one)
    return pl.pallas_call(
        _hyena_fft_kernel,
        grid=(nct, B),
        in_specs=[pl.BlockSpec((L, LANE), lambda c, b: (b, c)), gspec, gspec, ispec, ispec, fspec, fspec, kspec, kspec],
        out_specs=pl.BlockSpec((L, LANE), lambda c, b: (b, c)),
        out_shape=jax.ShapeDtypeStruct((T_LAT, HY_W), F32),
        scratch_shapes=[pltpu.VMEM((FFT_R * K1P, LANE), F32)] * 2,
        compiler_params=_cp(56 << 20, 2),
        name="hyena_fft",
    )(u, g_cos_bf, g_sin_bf, i_cos_bf, i_sin_bf, f_cos_bf, f_sin_bf, k_r, k_i)


def _hyena_ctx_kernel(u_ref, k_ref, dc_ref, ds_ref, o_ref):
    dot = lambda a, b: jnp.dot(a, b, precision=HIGHEST, preferred_element_type=F32)
    dc = dc_ref[...]
    ds = ds_ref[...]
    u = u_ref[...]
    kf = k_ref[...]
    u_r = dot(dc[:, 0:C], u)
    u_i = -dot(ds[:, 0:C], u)
    k_r = dot(dc, kf)
    k_i = -dot(ds, kf)
    y_r = u_r * k_r - u_i * k_i
    y_i = u_r * k_i + u_i * k_r
    o_ref[...] = (dot(dc[0:C, :], y_r) - dot(ds[0:C, :], y_i)) * (1.0 / (2 * C))


def _hyena_ctx(u, kfilt_ctx, d_cos, d_sin):
    full = lambda b: (0, 0)
    return pl.pallas_call(
        _hyena_ctx_kernel,
        grid=(B,),
        in_specs=[pl.BlockSpec((C, HY_W), lambda b: (T_LAT // C + b, 0)),
                  pl.BlockSpec((2 * C, HY_W), full), pl.BlockSpec((2 * C, 2 * C), full), pl.BlockSpec((2 * C, 2 * C), full)],
        out_specs=pl.BlockSpec((C, HY_W), lambda b: (b, 0)),
        out_shape=jax.ShapeDtypeStruct((T_CTX, HY_W), F32),
        compiler_params=_cp(32 << 20, 1),
        name="hyena_ctx",
    )(u, kfilt_ctx, d_cos, d_sin)


def _hyena_filter(n, w1, b1, freq, w2, b2, w3):
    t = jnp.linspace(0.0, 1.0, n, dtype=F32)[:, None]
    bands = (FILTER_EMB - 1) // 2
    w = 2.0 * math.pi * jnp.arange(n, dtype=F32)[:, None] / n
    f = jnp.linspace(1e-4, bands - 1, bands, dtype=F32)[None, :]
    z = jnp.concatenate([t, jnp.cos(f * w), -jnp.sin(f * w)], axis=-1)
    mm = functools.partial(jnp.matmul, precision=HIGHEST)
    h = jnp.sin(freq * (mm(z, w1) + b1))
    h = jnp.sin(freq * (mm(h, w2) + b2))
    h = mm(h, w3)
    deltas = jnp.abs(jnp.linspace(math.log(DECAY_TARGET) / DECAY_FAST, math.log(DECAY_TARGET) / DECAY_SLOW,
                                  HY_W, dtype=F32))
    decay = jnp.exp(-t * deltas[None, :])
    h_fwd = h[:, :HY_W] * decay
    h_bwd = h[:, HY_W:] * decay
    k = jnp.concatenate([h_fwd, jnp.zeros((1, HY_W), F32), h_bwd[:0:-1]], axis=0)
    return k / jnp.sum(jnp.abs(k), axis=0, keepdims=True)


def _s5_matrices(a_re, a_im, log_dt, b_re, b_im, c_re, c_im, d_skip):
    dt = jnp.exp(log_dt)[:, :, None]
    lam_re = jnp.minimum(a_re, -1e-4)
    mag1 = jnp.exp(lam_re * dt)
    lbr = mag1 * jnp.cos(a_im * dt)
    lbi = mag1 * jnp.sin(a_im * dt)
    den = lam_re * lam_re + a_im * a_im
    fr = ((lbr - 1.0) * lam_re + lbi * a_im) / den
    fi = (lbi * lam_re - (lbr - 1.0) * a_im) / den
    bbr = fr[..., None] * b_re - fi[..., None] * b_im
    bbi = fr[..., None] * b_im + fi[..., None] * b_re
    j = jnp.arange(S5_CH + 1, dtype=F32)[:, None, None, None]
    magj = jnp.exp(j * (lam_re * dt)[None])
    pr = magj * jnp.cos(j * (a_im * dt)[None])
    pi = magj * jnp.sin(j * (a_im * dt)[None])
    hi = functools.partial(jnp.einsum, precision=HIGHEST)
    lbr_j = pr[..., None] * bbr[None] - pi[..., None] * bbi[None]
    lbi_j = pr[..., None] * bbi[None] + pi[..., None] * bbr[None]
    m = hi('dgop,jdgpi->jdgoi', c_re, lbr_j) - hi('dgop,jdgpi->jdgoi', c_im, lbi_j)
    eye_g = jnp.eye(S5_GROUPS, dtype=F32)
    s = jnp.arange(S5_CH)
    blocks = jnp.einsum('jdgoi,gh->djgiho', m[0:S5_CH], eye_g).reshape(2, S5_CH, S5_W, S5_W)
    lag0 = blocks[0, 0] + blocks[1, 0] + jnp.diag(d_skip)
    e_all = jnp.concatenate([blocks[1, S5_CH - 1:0:-1], lag0[None], blocks[0, 1:S5_CH]], axis=0).astype(BF16)
    sf_r = lbr_j[S5_CH - 1 - s, 0]
    sf_i = lbi_j[S5_CH - 1 - s, 0]
    sb_r = lbr_j[s, 1]
    sb_i = lbi_j[s, 1]
    st = jnp.stack([sf_r, sf_i, sb_r, sb_i], axis=0)
    ws = jnp.einsum('qsgpi,gh->sgiqhp', st, eye_g).reshape(S5_ROWW, 4 * S5_NSTATE)
    tt = jnp.arange(S5_CH)
    cf_r = c_re[0][None] * pr[tt + 1, 0][:, :, None, :] - c_im[0][None] * pi[tt + 1, 0][:, :, None, :]
    cf_i = c_re[0][None] * pi[tt + 1, 0][:, :, None, :] + c_im[0][None] * pr[tt + 1, 0][:, :, None, :]
    cb_r = c_re[1][None] * pr[S5_CH - tt, 1][:, :, None, :] - c_im[1][None] * pi[S5_CH - tt, 1][:, :, None, :]
    cb_i = c_re[1][None] * pi[S5_CH - tt, 1][:, :, None, :] + c_im[1][None] * pr[S5_CH - tt, 1][:, :, None, :]
    ct = jnp.stack([cf_r, -cf_i, cb_r, -cb_i], axis=0)
    wc = jnp.einsum('qtgop,gh->qgptho', ct, eye_g).reshape(4 * S5_NSTATE, S5_ROWW)
    lam_p = jnp.stack([jnp.stack([pr[S5_CH, 0], pi[S5_CH, 0]]), jnp.stack([pr[S5_CH, 1], pi[S5_CH, 1]])])
    return ws.astype(BF16), e_all, wc.astype(BF16), lam_p.reshape(2, 2, 1, S5_NSTATE)


S5_TN = 512
S5_NS = S5_ROWW // S5_W


def _s5_in_kernel(u_ref, ws_ref, e_ref, s_ref, o_ref):
    j = pl.program_id(0)

    @pl.when(j < S5_NS)
    def _():
        s_ref[...] = jnp.dot(u_ref[...], ws_ref[...], preferred_element_type=F32)

    @pl.when(j >= S5_NS)
    def _():
        t = j - S5_NS
        acc = jnp.dot(u_ref[:, 0:S5_W], e_ref[S5_CH - 1 + t], preferred_element_type=F32)
        for s in range(1, S5_CH):
            acc = acc + jnp.dot(u_ref[:, s * S5_W:(s + 1) * S5_W], e_ref[S5_CH - 1 + t - s],
                                preferred_element_type=F32)
        o_ref[...] = acc


def _s5_in(u_rows, ws, e_all):
    return pl.pallas_call(
        _s5_in_kernel,
        grid=(2 * S5_NS,),
        in_specs=[pl.BlockSpec((N_CHUNK, S5_ROWW), lambda j: (0, 0)),
                  pl.BlockSpec((S5_ROWW, S5_W), lambda j: (0, jnp.minimum(j, S5_NS - 1))),
                  pl.BlockSpec((2 * S5_CH - 1, S5_W, S5_W), lambda j: (0, 0, 0))],
        out_specs=[pl.BlockSpec((N_CHUNK, S5_W), lambda j: (0, jnp.minimum(j, S5_NS - 1))),
                   pl.BlockSpec((N_CHUNK, S5_W), lambda j: (0, jnp.maximum(j - S5_NS, 0)))],
        out_shape=[jax.ShapeDtypeStruct((N_CHUNK, 4 * S5_NSTATE), F32), jax.ShapeDtypeStruct((N_CHUNK, S5_ROWW), F32)],
        compiler_params=_cp(48 << 20, 1),
        name="s5_in",
    )(u_rows, ws, e_all)


def _s5_scan_kernel(s_ref, lam_ref, h_ref):
    lam = [[lam_ref[d, p] for p in range(2)] for d in range(2)]

    def step(b, d, chunk, h):
        row = pl.ds(chunk, 1)
        cols_r = pl.ds(d * 2 * S5_NSTATE, S5_NSTATE)
        cols_i = pl.ds(d * 2 * S5_NSTATE + S5_NSTATE, S5_NSTATE)
        h_ref[row, cols_r] = h[0]
        h_ref[row, cols_i] = h[1]
        s_r = s_ref[row, cols_r]
        s_i = s_ref[row, cols_i]
        lr, li = lam[d]
        return (lr * h[0] - li * h[1] + s_r, lr * h[1] + li * h[0] + s_i)

    def chain_order(b, d, n_ctx_done):
        ctx0 = B * LAT_CHUNKS + b * CTX_CHUNKS
        lat0 = b * LAT_CHUNKS
        if d == 0:
            return (lambda i: ctx0 + i), (lambda i: lat0 + i)
        return (lambda i: ctx0 + CTX_CHUNKS - 1 - i), (lambda i: lat0 + LAT_CHUNKS - 1 - i)

    chains = [(b, d) for b in range(B) for d in range(2)]
    zero = jnp.zeros((1, S5_NSTATE), F32)
    init = tuple((zero, zero) for _ in chains)

    def phase(n_steps, which, carry):
        def body(i, hs):
            out = []
            for (b, d), h in zip(chains, hs):
                order = chain_order(b, d, 0)[which]
                out.append(step(b, d, order(i), h))
            return tuple(out)
        return lax.fori_loop(0, n_steps, body, carry)

    carry = phase(CTX_CHUNKS, 0, init)
    phase(LAT_CHUNKS, 1, carry)


def _s5_scan(s_rows, lam_p):
    return pl.pallas_call(
        _s5_scan_kernel,
        out_shape=jax.ShapeDtypeStruct((N_CHUNK, 4 * S5_NSTATE), F32),
        compiler_params=pltpu.CompilerParams(vmem_limit_bytes=48 << 20),
        name="s5_scan",
    )(s_rows, lam_p)


S5_TM = N_CHUNK // 2
S5_TPN = S5_TN // S5_W


def _s5_out_kernel(h_ref, w_ref, y_ref, *rest):
    o_refs, hb_ref = rest[:-1], rest[-1]
    j = pl.program_id(1)

    @pl.when(j == 0)
    def _():
        hb_ref[...] = h_ref[...].astype(BF16)

    acc = jnp.dot(hb_ref[...], w_ref[...], preferred_element_type=F32) + y_ref[...]
    for tt in range(S5_TPN):
        for h, o_ref in enumerate(o_refs):
            o_ref[pl.ds(j * S5_TPN + tt, S5_TM, stride=S5_CH), :] = (
                acc[:, tt * S5_W + h * LANE:tt * S5_W + (h + 1) * LANE])


def _s5_out(h_rows, wc, y_in):
    nn = S5_ROWW // S5_TN
    n_out = S5_W // LANE
    return pl.pallas_call(
        _s5_out_kernel,
        grid=(N_CHUNK // S5_TM, nn),
        in_specs=[pl.BlockSpec((S5_TM, 4 * S5_NSTATE), lambda i, j: (i, 0), pipeline_mode=pl.Buffered(1)),
                  pl.BlockSpec((4 * S5_NSTATE, S5_TN), lambda i, j: (0, j)),
                  pl.BlockSpec((S5_TM, S5_TN), lambda i, j: (i, j))],
        out_specs=[pl.BlockSpec((S5_TM * S5_CH, LANE), lambda i, j: (i, 0))] * n_out,
        out_shape=[jax.ShapeDtypeStruct((T_ALL, LANE), F32)] * n_out,
        scratch_shapes=[pltpu.VMEM((S5_TM, 4 * S5_NSTATE), BF16)],
        compiler_params=_cp(52 << 20, 2),
        name="s5_out",
    )(h_rows, wc, y_in)


def _rms(x):
    return x * lax.rsqrt(jnp.mean(x * x, axis=-1, keepdims=True) + LN_EPS)


def _layer_norm(x, g, b):
    mu = jnp.mean(x, axis=-1, keepdims=True)
    xc = x - mu
    var = jnp.mean(xc * xc, axis=-1, keepdims=True)
    return xc * lax.rsqrt(var + LN_EPS) * g + b


def _merge_kernel(x_ref, attn_ref, conv_ref, hu_ref, x0_ref, s5a_ref, s5b_ref, g1_ref, sh2_ref, sc2_ref, mixg_ref,
                  hyd_ref, wglu_ref, wout_ref, lng_ref, lnb_ref, o_ref, u2_ref):
    hy = (conv_ref[...] + hu_ref[...] * hyd_ref[...]) * x0_ref[...]
    g = jax.nn.gelu(jnp.concatenate([s5a_ref[...], s5b_ref[...]], axis=-1))
    s5 = g * jax.nn.sigmoid(jnp.dot(g.astype(BF16), wglu_ref[...], preferred_element_type=F32))
    mixg = mixg_ref[...]
    parts = [_rms(attn_ref[...]) * mixg[:, 0:ATTN_W],
             _rms(hy) * mixg[:, ATTN_W:ATTN_W + HY_W],
             _rms(s5) * mixg[:, ATTN_W + HY_W:MIX_W]]
    mix = jnp.concatenate(parts, axis=-1).astype(BF16)
    o = jnp.dot(mix, wout_ref[...], preferred_element_type=F32)
    x1 = _layer_norm(ALPHA * x_ref[...] + g1_ref[...] * o, lng_ref[...], lnb_ref[...])
    o_ref[...] = x1
    u2_ref[...] = x1 * (1.0 + sc2_ref[...]) + sh2_ref[...]


def _merge(n_rows, xall, attn, conv, hu, x0c, s5y, g1, sh2, sc2, mix_g, hy_d, wglu_bf, wout_bf, ln_g, ln_b):
    nt = n_rows // TM
    row = lambda i: (i, 0)
    full = lambda i: (0, 0)
    return pl.pallas_call(
        _merge_kernel,
        grid=(nt,),
        in_specs=[pl.BlockSpec((TM, D), row), pl.BlockSpec((TM, ATTN_W), row), pl.BlockSpec((TM, HY_W), row),
                  pl.BlockSpec((TM, HY_W), row), pl.BlockSpec((TM, HY_W), row),
                  pl.BlockSpec((TM, LANE), row), pl.BlockSpec((TM, LANE), row),
                  _vec_spec(TM), _vec_spec(TM), _vec_spec(TM), pl.BlockSpec((1, MIX_W), full),
                  pl.BlockSpec((1, HY_W), full), pl.BlockSpec((S5_W, S5_W), full), pl.BlockSpec((MIX_W, D), full),
                  pl.BlockSpec((1, D), full), pl.BlockSpec((1, D), full)],
        out_specs=[pl.BlockSpec((TM, D), row), pl.BlockSpec((TM, D), row)],
        out_shape=[jax.ShapeDtypeStruct((n_rows, D), F32), jax.ShapeDtypeStruct((n_rows, D), F32)],
        compiler_params=_cp(48 << 20, 1),
        name="merge",
    )(xall, attn, conv, hu, x0c, s5y[0], s5y[1], g1, sh2, sc2, mix_g.reshape(1, MIX_W), hy_d.reshape(1, HY_W), wglu_bf,
      wout_bf, ln_g.reshape(1, D), ln_b.reshape(1, D))


def _router_kernel(u_ref, wt_ref, b_ref, e_ref, g_ref):
    logits = lax.dot_general(wt_ref[...], u_ref[...], (((1,), (1,)), ((), ())), precision=HIGHEST,
                             preferred_element_type=F32)
    scores = jax.nn.sigmoid(logits)
    biased = scores + b_ref[...]
    ninf = -jnp.inf
    grow = lax.broadcasted_iota(jnp.int32, (EGROUP, TM), 0)
    groups = [biased[gi * EGROUP:(gi + 1) * EGROUP] for gi in range(N_EGROUPS)]
    gscore = []
    for vals in groups:
        m1 = jnp.max(vals, axis=0, keepdims=True)
        i1 = jnp.min(jnp.where(vals == m1, grow, EGROUP), axis=0, keepdims=True)
        m2 = jnp.max(jnp.where(grow == i1, ninf, vals), axis=0, keepdims=True)
        gscore.append(m1 + m2)
    kept = []
    for gi in range(N_EGROUPS):
        rank = jnp.zeros((1, TM), jnp.int32)
        for gj in range(N_EGROUPS):
            if gj == gi:
                continue
            ahead = (gscore[gj] > gscore[gi]) | ((gscore[gj] == gscore[gi]) & (gj < gi))
            rank = rank + ahead.astype(jnp.int32)
        kept.append(jnp.where(rank < TOPK_GROUPS, groups[gi], ninf))
    masked = jnp.concatenate(kept, axis=0)
    row = lax.broadcasted_iota(jnp.int32, (N_EXPERTS, TM), 0)
    gates = []
    gsum = jnp.zeros((1, TM), F32)
    for kk in range(TOP_K):
        m = jnp.max(masked, axis=0, keepdims=True)
        idx = jnp.min(jnp.where(masked == m, row, N_EXPERTS), axis=0, keepdims=True)
        hit = row == idx
        gate = jnp.sum(jnp.where(hit, scores, 0.0), axis=0, keepdims=True)
        masked = jnp.where(hit, ninf, masked)
        e_ref[kk:kk + 1, :] = idx
        gates.append(gate)
        gsum = gsum + gate
    for kk in range(TOP_K):
        g_ref[kk:kk + 1, :] = gates[kk] / gsum * ROUTED_SCALE


def _router(n_rows, u2, w_router_t, router_bias):
    nt = n_rows // TM
    col = lambda i: (0, i)
    return pl.pallas_call(
        _router_kernel,
        grid=(nt,),
        in_specs=[pl.BlockSpec((TM, D), lambda i: (i, 0)),
                  pl.BlockSpec((N_EXPERTS, D), lambda i: (0, 0)), pl.BlockSpec((N_EXPERTS, 1), lambda i: (0, 0))],
        out_specs=[pl.BlockSpec((TOP_K, TM), col), pl.BlockSpec((TOP_K, TM), col)],
        out_shape=[jax.ShapeDtypeStruct((TOP_K, n_rows), jnp.int32), jax.ShapeDtypeStruct((TOP_K, n_rows), F32)],
        compiler_params=_cp(32 << 20, 1),
        name="router",
    )(u2, w_router_t, router_bias.reshape(N_EXPERTS, 1))


def _dispatch(top_e, gate):
    t = top_e.shape[1]
    tk = t * TOP_K
    nblk = tk // MOE_BLOCK
    n_steps = nblk + N_EXPERTS
    flat_e = top_e.reshape(tk)
    pos = jnp.arange(tk, dtype=jnp.int32)
    _, order, sw = lax.sort((flat_e, pos, gate.reshape(tk)), num_keys=1, is_stable=True)
    _, inv = lax.sort((order, pos), num_keys=1)
    experts = jnp.arange(N_EXPERTS, dtype=jnp.int32)
    counts = jnp.sum((flat_e[None, :] == experts[:, None]).astype(jnp.int32), axis=1)
    ends = jnp.cumsum(counts)
    starts = ends - counts
    fb = starts // MOE_BLOCK
    npairs = jnp.where(counts > 0, (ends - 1) // MOE_BLOCK - fb + 1, 0)
    pend = jnp.cumsum(npairs)
    poff = pend - npairs
    n_pairs = pend[-1]
    s = jnp.arange(n_steps, dtype=jnp.int32)
    sc = jnp.minimum(s, n_pairs - 1)
    pe = jnp.sum((pend[None, :] <= sc[:, None]).astype(jnp.int32), axis=1)
    pb = fb[pe] + (sc - poff[pe])
    lo = jnp.where(s < n_pairs, jnp.maximum(starts[pe] - pb * MOE_BLOCK, 0), 0)
    hi = jnp.where(s < n_pairs, jnp.minimum(ends[pe] - pb * MOE_BLOCK, MOE_BLOCK), 0)
    return dict(pe=pe, pb=pb, lo=lo, hi=hi, tok=order % t, comb=inv, w=sw.reshape(nblk, MOE_BLOCK, 1))


def _expert_kernel(pe, pb, plo, phi, x_ref, w_ref, wg_ref, wu_ref, wd_ref, o_ref, wgb, wub, wdb):
    s = pl.program_id(0)
    prev = jnp.maximum(s - 1, 0)

    @pl.when((s == 0) | (pe[s] != pe[prev]))
    def _():
        wgb[...] = wg_ref[...].astype(BF16)
        wub[...] = wu_ref[...].astype(BF16)
        wdb[...] = wd_ref[...].astype(BF16)

    lo = plo[s]
    hi = phi[s]
    first_of_block = (s == 0) | (pb[s] != pb[prev])

    def ffn():
        xb = x_ref[...].astype(BF16)
        hg = jnp.dot(xb, wgb[...], preferred_element_type=F32)
        hu = jnp.dot(xb, wub[...], preferred_element_type=F32)
        h = (hg * jax.nn.sigmoid(hg)) * hu
        return jnp.dot(h.astype(BF16), wdb[...], preferred_element_type=F32) * w_ref[0]

    @pl.when((hi > lo) & first_of_block)
    def _():
        o_ref[...] = ffn()

    @pl.when((hi > lo) & jnp.logical_not(first_of_block))
    def _():
        row = lax.broadcasted_iota(jnp.int32, (MOE_BLOCK, D), 0)
        o_ref[...] = jnp.where((row >= lo) & (row < hi), ffn(), o_ref[...])


def _experts(xs, disp, wg, wu, wd, layer):
    tk = xs.shape[0]
    n_steps = disp['pe'].shape[0]
    blk = lambda s, pe, pb, lo, hi: (pb[s], 0)
    wspec = lambda shape: pl.BlockSpec((None, None) + shape, lambda s, pe, pb, lo, hi: (layer, pe[s], 0, 0))
    grid_spec = pltpu.PrefetchScalarGridSpec(
        num_scalar_prefetch=4,
        grid=(n_steps,),
        in_specs=[pl.BlockSpec((MOE_BLOCK, D), blk),
                  pl.BlockSpec((1, MOE_BLOCK, 1), lambda s, pe, pb, lo, hi: (pb[s], 0, 0)),
                  wspec((D, EXPERT_FF)), wspec((D, EXPERT_FF)), wspec((EXPERT_FF, D))],
        out_specs=pl.BlockSpec((MOE_BLOCK, D), blk),
        scratch_shapes=[pltpu.VMEM((D, EXPERT_FF), BF16), pltpu.VMEM((D, EXPERT_FF), BF16),
                        pltpu.VMEM((EXPERT_FF, D), BF16)],
    )
    return pl.pallas_call(
        _expert_kernel,
        grid_spec=grid_spec,
        out_shape=jax.ShapeDtypeStruct((tk, D), F32),
        compiler_params=_cp(40 << 20, 1),
        name="experts",
    )(disp['pe'], disp['pb'], disp['lo'], disp['hi'], xs, disp['w'], wg, wu, wd)


SC_ROWS = 64


def _row_gather(table, idx):
    n = idx.shape[0]
    d = table.shape[1]
    mesh = plsc.VectorSubcoreMesh(core_axis_name="c", subcore_axis_name="s")
    n_workers = mesh.num_cores * mesh.num_subcores
    per_worker = n // n_workers
    assert per_worker * n_workers == n and per_worker % SC_ROWS == 0

    @functools.partial(
        pl.kernel, mesh=mesh,
        out_type=jax.ShapeDtypeStruct((n, d), table.dtype),
        scratch_types=[pltpu.VMEM((SC_ROWS,), jnp.int32), pltpu.VMEM((SC_ROWS, d), table.dtype),
                       pltpu.SemaphoreType.DMA],
    )
    def gather(table_hbm, idx_hbm, out_hbm, idx_v, rows_v, sem):
        worker = lax.axis_index("s") * mesh.num_cores + lax.axis_index("c")
        base = worker * per_worker

        @pl.loop(0, per_worker // SC_ROWS)
        def _(c):
            off = pl.multiple_of(base + c * SC_ROWS, 8)
            pltpu.sync_copy(idx_hbm.at[pl.ds(off, SC_ROWS)], idx_v)
            pltpu.async_copy(table_hbm.at[idx_v], rows_v, sem).wait()
            pltpu.sync_copy(rows_v, out_hbm.at[pl.ds(off, SC_ROWS)])

    return gather(table, idx)


FM = 128


def _ffn_out_kernel(x_ref, u_ref, r_ref, g2_ref, wsg_ref, wsu_ref, wsd_ref, lng_ref, lnb_ref, o_ref):
    ub = u_ref[...].astype(BF16)
    hg = jnp.dot(ub, wsg_ref[...], preferred_element_type=F32)
    hu = jnp.dot(ub, wsu_ref[...], preferred_element_type=F32)
    f = jnp.dot(((hg * jax.nn.sigmoid(hg)) * hu).astype(BF16), wsd_ref[...], preferred_element_type=F32)
    routed = r_ref[0]
    for kk in range(1, TOP_K):
        routed = routed + r_ref[kk]
    f = routed + f
    o_ref[...] = _layer_norm(ALPHA * x_ref[...] + g2_ref[...] * f, lng_ref[...], lnb_ref[...])


def _ffn_out(n_rows, x1, u2, routed, g2, wsg_bf, wsu_bf, wsd_bf, ln_g, ln_b):
    row = lambda i: (i, 0)
    full = lambda i: (0, 0)
    return pl.pallas_call(
        _ffn_out_kernel,
        grid=(n_rows // FM,),
        in_specs=[pl.BlockSpec((FM, D), row), pl.BlockSpec((FM, D), row),
                  pl.BlockSpec((TOP_K, FM, D), lambda i: (0, i, 0)),
                  _vec_spec(FM), pl.BlockSpec((D, EXPERT_FF), full), pl.BlockSpec((D, EXPERT_FF), full),
                  pl.BlockSpec((EXPERT_FF, D), full), pl.BlockSpec((1, D), full), pl.BlockSpec((1, D), full)],
        out_specs=pl.BlockSpec((FM, D), row),
        out_shape=jax.ShapeDtypeStruct((n_rows, D), F32),
        compiler_params=_cp(40 << 20, 1),
        name="ffn_out",
    )(x1, u2, routed, g2, wsg_bf, wsu_bf, wsd_bf, ln_g.reshape(1, D), ln_b.reshape(1, D))


def _rope_tables():
    t = jnp.arange(L, dtype=jnp.int32)
    row = (t // GRID_W).astype(F32)
    col = (t % GRID_W).astype(F32)
    inv_freq = ROPE_BASE ** (-jnp.arange(0, AXIS_DIM, 2, dtype=F32) / AXIS_DIM)
    half = AXIS_DIM // 2

    def axis(pos):
        ang = pos[:, None] * inv_freq[None, :]
        c = jnp.cos(ang)
        s = jnp.sin(ang)
        return jnp.concatenate([c, c], axis=1), jnp.concatenate([-s, s], axis=1)

    cr, sr = axis(row)
    cc, sc = axis(col)
    cos_h = jnp.concatenate([cr, cc], axis=1)
    sin_h = jnp.concatenate([sr, sc], axis=1)
    cos_l = jnp.tile(cos_h, (B, LANE // HEAD_DIM))
    sin_l = jnp.tile(sin_h, (B, LANE // HEAD_DIM))
    cos_t = jnp.concatenate([cos_l, jnp.ones((T_CTX, LANE), F32)], axis=0)
    sin_t = jnp.concatenate([sin_l, jnp.zeros((T_CTX, LANE), F32)], axis=0)
    del half
    return cos_t, sin_t


def kernel(x, c, ctx, c_ctx, w_ada, b_ada, w_in, w_out, sink, mix_g, hy_short_w, hy_short_b, hy_w1, hy_b1, hy_freq,
           hy_w2, hy_b2, hy_w3, hy_d, s5_a_re, s5_a_im, s5_log_dt, s5_b_re, s5_b_im, s5_c_re, s5_c_im, s5_d, s5_w_glu,
           ln1_g, ln1_b, ln2_g, ln2_b, w_router, router_bias, w_exp_gate, w_exp_up, w_exp_down, w_sh_gate, w_sh_up,
           w_sh_down):
    xall = jnp.concatenate([x.reshape(T_LAT, D), ctx.reshape(T_CTX, D)], axis=0)
    cvec = jnp.concatenate([c, c_ctx[None, :], jnp.zeros((8 - B - 1, D), F32)], axis=0)
    mod = _ada(cvec, w_ada, b_ada)[:, 0:B + 1, :].reshape(DEPTH, B + 1, 6, 1, D)

    cos_t, sin_t = _rope_tables()
    g_cos, g_sin, f_cos, f_sin = _dft_tables()
    g_cos_bf, g_sin_bf = g_cos.astype(BF16), g_sin.astype(BF16)
    i_cos_bf = jnp.swapaxes(g_cos, 1, 2).astype(BF16)
    i_sin_bf = jnp.swapaxes(g_sin, 1, 2).astype(BF16)
    f_cos_bf, f_sin_bf = f_cos.astype(BF16), f_sin.astype(BF16)
    k1 = jnp.arange(K1P)
    spec_w = jnp.where((k1 == 0) | (k1 == FFT_R // 2), 1.0, 2.0) * (k1 < K1_USED) / N_FFT
    spec_w = spec_w.astype(F32)
    kt = jnp.arange(2 * C, dtype=jnp.int32)
    ang_c = ((kt[:, None] * kt[None, :]) % (2 * C)).astype(F32) * (2.0 * math.pi / (2 * C))
    d_cos, d_sin = jnp.cos(ang_c), jnp.sin(ang_c)

    for l in range(DEPTH):
        last = l == DEPTH - 1
        n_rows = T_LAT if last else T_ALL
        sh1, sc1, g1, sh2, sc2, g2 = (mod[l, :, j] for j in range(6))

        q, k, v, hz, s5u = _inproj(xall, sh1, sc1, w_in[l].astype(BF16), cos_t, sin_t)
        attn = _attention(sink[l], q, k, v, with_ctx=not last)

        hu, x0c = _hyena_pre(hz, hy_short_w[l], hy_short_b[l])
        filt_args = (hy_w1[l], hy_b1[l], hy_freq[l], hy_w2[l], hy_b2[l], hy_w3[l])
        k_r, k_i = _hyena_spectrum(_hyena_filter(L, *filt_args), g_cos, g_sin, f_cos, f_sin, spec_w)
        conv = _hyena_fft(hu, g_cos_bf, g_sin_bf, i_cos_bf, i_sin_bf, f_cos_bf, f_sin_bf, k_r, k_i)
        if not last:
            conv = jnp.concatenate([conv, _hyena_ctx(hu, _hyena_filter(C, *filt_args), d_cos, d_sin)], axis=0)

        ws, e_all, wc, lam_p = _s5_matrices(s5_a_re[l], s5_a_im[l], s5_log_dt[l], s5_b_re[l], s5_b_im[l],
                                            s5_c_re[l], s5_c_im[l], s5_d[l])
        s_rows, y_in = _s5_in(s5u, ws, e_all)
        s5y = _s5_out(_s5_scan(s_rows, lam_p), wc, y_in)

        x1, u2 = _merge(n_rows, xall, attn, conv, hu, x0c, s5y, g1, sh2, sc2, mix_g[l], hy_d[l],
                        s5_w_glu[l].astype(BF16), w_out[l].astype(BF16), ln1_g[l], ln1_b[l])

        top_e, gate = _router(n_rows, u2, w_router[l].T, router_bias[l])
        disp = _dispatch(top_e, gate)
        ys = _experts(_row_gather(u2, disp['tok']), disp, w_exp_gate, w_exp_up, w_exp_down, l)
        routed = _row_gather(ys, disp['comb']).reshape(TOP_K, n_rows, D)
        xall = _ffn_out(n_rows, x1, u2, routed, g2, w_sh_gate[l].astype(BF16), w_sh_up[l].astype(BF16),
                        w_sh_down[l].astype(BF16), ln2_g[l], ln2_b[l])
    return xall.reshape(B, L, D)
```

```python
import functools
import math

import jax
import jax.numpy as jnp
from jax import lax
from jax.experimental import pallas as pl
from jax.experimental.pallas import tpu as pltpu
from jax.experimental.pallas import tpu_sc as plsc

F32 = jnp.float32
BF16 = jnp.bfloat16
HIGHEST = lax.Precision.HIGHEST

D = 1024
B = 2
L = 8192
DEPTH = 2
GRID_W = 64
C = 256
T_LAT = B * L
T_CTX = B * C
T_ALL = T_LAT + T_CTX

HEAD_DIM = 64
N_Q = 8
N_KV = 2
Q_GROUP = N_Q // N_KV
ATTN_W = N_Q * HEAD_DIM
KV_W = N_KV * HEAD_DIM
HY_W = 256
S5_W = 256
MIX_W = ATTN_W + HY_W + S5_W
K_OFF = ATTN_W
V_OFF = K_OFF + KV_W
HY_OFF = V_OFF + KV_W
S5_OFF = HY_OFF + 3 * HY_W
IN_W = S5_OFF + S5_W
WINDOW = 128
BLK = 128
NEG_INF = -1e30
ROPE_BASE = 10000.0
AXIS_DIM = HEAD_DIM // 2

SHORT_K = 3
FILTER_EMB = 33
DECAY_FAST = 0.3
DECAY_SLOW = 1.5
DECAY_TARGET = 1e-2

S5_GROUP = 16
S5_GROUPS = S5_W // S5_GROUP
S5_STATE = 64
S5_NSTATE = S5_GROUPS * S5_STATE
S5_CH = 16
S5_ROWW = S5_CH * S5_W
N_CHUNK = T_ALL // S5_CH
LAT_CHUNKS = L // S5_CH
CTX_CHUNKS = C // S5_CH

N_EXPERTS = 256
TOP_K = 8
N_EGROUPS = 8
EGROUP = N_EXPERTS // N_EGROUPS
TOPK_GROUPS = 4
EXPERT_FF = 256
ROUTED_SCALE = 2.5
MOE_BLOCK = 256

ALPHA = (2 * DEPTH) ** 0.25
LN_EPS = 1e-5

N_FFT = 2 * L
FFT_R = 128
FFT_T1 = L // FFT_R
K1_USED = FFT_R // 2 + 1
K1P = 80
K1H = K1P // 2
FFT_UNROLL = 4

TM = 256
LANE = 128
VMEM_CAP = 60000 * 1024


def _cp(vmem_bytes, n_axes):
    return pltpu.CompilerParams(
        dimension_semantics=("arbitrary",) * n_axes,
        vmem_limit_bytes=min(int(vmem_bytes), VMEM_CAP),
    )


def _mod_sel(rows_per_tile):
    per_batch = L // rows_per_tile
    return lambda i: jnp.minimum(i // per_batch, 2)


def _vec_spec(rows_per_tile):
    sel = _mod_sel(rows_per_tile)
    return pl.BlockSpec((None, 1, D), lambda i: (sel(i), 0, 0))


ADA_TN = 1536


def _ada_kernel(c_ref, w_ref, b_ref, o_ref):
    c = c_ref[...]
    s = c * jax.nn.sigmoid(c)
    o_ref[...] = jnp.dot(s, w_ref[...], precision=HIGHEST, preferred_element_type=F32) + b_ref[...]


def _ada(cvec, w_ada, b_ada):
    return pl.pallas_call(
        _ada_kernel,
        grid=(DEPTH, 6 * D // ADA_TN),
        in_specs=[
            pl.BlockSpec((8, D), lambda l, j: (0, 0)),
            pl.BlockSpec((None, D, ADA_TN), lambda l, j: (l, 0, j)),
            pl.BlockSpec((None, 1, ADA_TN), lambda l, j: (l, 0, j)),
        ],
        out_specs=pl.BlockSpec((None, 8, ADA_TN), lambda l, j: (l, 0, j)),
        out_shape=jax.ShapeDtypeStruct((DEPTH, 8, 6 * D), F32),
        compiler_params=_cp(40 << 20, 2),
        name="ada",
    )(cvec, w_ada, b_ada.reshape(DEPTH, 1, 6 * D))


def _inproj_kernel(x_ref, sh_ref, sc_ref, w_ref, cos_ref, sin_ref, q_ref, k_ref, v_ref, hy_ref, s5_ref, *s5_tok_refs):
    u = x_ref[...] * (1.0 + sc_ref[...]) + sh_ref[...]
    proj = jnp.dot(u.astype(BF16), w_ref[...], preferred_element_type=F32)
    cos = cos_ref[...]
    sin = sin_ref[...]
    lane = lax.broadcasted_iota(jnp.int32, (TM, LANE), 1)
    first_half = (lane % AXIS_DIM) < (AXIS_DIM // 2)

    def rope(xc):
        partner = jnp.where(first_half, pltpu.roll(xc, LANE - AXIS_DIM // 2, 1), pltpu.roll(xc, AXIS_DIM // 2, 1))
        return xc * cos + partner * sin

    for j in range(ATTN_W // LANE):
        q_ref[:, j * LANE:(j + 1) * LANE] = rope(proj[:, j * LANE:(j + 1) * LANE]).astype(BF16)
    k_ref[...] = rope(proj[:, K_OFF:V_OFF]).astype(BF16)
    v_ref[...] = proj[:, V_OFF:HY_OFF].astype(BF16)
    hy_ref[...] = proj[:, HY_OFF:S5_OFF]
    for h, tok_ref in enumerate(s5_tok_refs):
        tok_ref[...] = proj[:, S5_OFF + h * LANE:S5_OFF + (h + 1) * LANE]
    for s in range(S5_CH):
        for h, tok_ref in enumerate(s5_tok_refs):
            s5_ref[:, s * S5_W + h * LANE:s * S5_W + (h + 1) * LANE] = (
                tok_ref[pl.ds(s, TM // S5_CH, stride=S5_CH), :].astype(BF16))


def _inproj(xall, sh, sc, w_in_bf, cos_t, sin_t):
    nt = T_ALL // TM
    row = lambda i: (i, 0)
    return pl.pallas_call(
        _inproj_kernel,
        grid=(nt,),
        in_specs=[
            pl.BlockSpec((TM, D), row),
            _vec_spec(TM),
            _vec_spec(TM),
            pl.BlockSpec((D, IN_W), lambda i: (0, 0)),
            pl.BlockSpec((TM, LANE), row),
            pl.BlockSpec((TM, LANE), row),
        ],
        out_specs=[
            pl.BlockSpec((TM, ATTN_W), row),
            pl.BlockSpec((TM, KV_W), row),
            pl.BlockSpec((TM, KV_W), row),
            pl.BlockSpec((TM, 3 * HY_W), row),
            pl.BlockSpec((TM // S5_CH, S5_ROWW), row),
        ],
        out_shape=[
            jax.ShapeDtypeStruct((T_ALL, ATTN_W), BF16),
            jax.ShapeDtypeStruct((T_ALL, KV_W), BF16),
            jax.ShapeDtypeStruct((T_ALL, KV_W), BF16),
            jax.ShapeDtypeStruct((T_ALL, 3 * HY_W), F32),
            jax.ShapeDtypeStruct((N_CHUNK, S5_ROWW), BF16),
        ],
        scratch_shapes=[pltpu.VMEM((TM, LANE), F32)] * (S5_W // LANE),
        compiler_params=_cp(40 << 20, 1),
        name="inproj",
    )(xall, sh, sc, w_in_bf, cos_t, sin_t)


NB_LAT = L // BLK
NB_CTX = C // BLK


def _nt_dot(a, b):
    return lax.dot_general(a, b, (((1,), (1,)), ((), ())), preferred_element_type=F32)


def _attn_kernel(sink_ref, q_ref, kp_ref, kc_ref, kn_ref, kx_ref, vp_ref, vc_ref, vn_ref, vx_ref, o_ref):
    n = pl.program_id(1)
    is_lat = n < NB_LAT
    rows = Q_GROUP * BLK
    r = lax.broadcasted_iota(jnp.int32, (rows, BLK), 0) % BLK
    j = lax.broadcasted_iota(jnp.int32, (rows, BLK), 1)
    ok_prev = (j >= r) & (n >= 1) & is_lat
    ok_next = (j <= r) & (n + 1 < NB_LAT) & is_lat
    head_of_row = lax.broadcasted_iota(jnp.int32, (rows, 1), 0) // BLK
    q = q_ref[...] * (HEAD_DIM ** -0.5)
    dot = lambda a, b: jnp.dot(a.astype(BF16), b, preferred_element_type=F32)
    for kh in range(N_KV):
        hs = slice(kh * HEAD_DIM, (kh + 1) * HEAD_DIM)
        heads = range(kh * Q_GROUP, (kh + 1) * Q_GROUP)
        qg = jnp.concatenate([q[:, h * HEAD_DIM:(h + 1) * HEAD_DIM] for h in heads], axis=0)
        sk = jnp.zeros((rows, 1), F32)
        for g, h in enumerate(heads):
            sk = jnp.where(head_of_row == g, sink_ref[h], sk)
        s_p = jnp.where(ok_prev, _nt_dot(qg, kp_ref[:, hs]), NEG_INF)
        s_c = jnp.where(is_lat, _nt_dot(qg, kc_ref[:, hs]), NEG_INF)
        s_n = jnp.where(ok_next, _nt_dot(qg, kn_ref[:, hs]), NEG_INF)
        s_x = _nt_dot(qg, kx_ref[:, hs])
        s_x0 = s_x[:, 0:BLK]
        s_x1 = s_x[:, BLK:2 * BLK]
        m = jnp.maximum(jnp.maximum(jnp.maximum(s_p, s_c), jnp.maximum(s_n, s_x0)), s_x1)
        m = jnp.maximum(jnp.max(m, axis=1, keepdims=True), sk)
        e_p = jnp.exp(s_p - m)
        e_c = jnp.exp(s_c - m)
        e_n = jnp.exp(s_n - m)
        e_x0 = jnp.exp(s_x0 - m)
        e_x1 = jnp.exp(s_x1 - m)
        den = jnp.sum((e_p + e_c) + (e_n + e_x0) + e_x1, axis=1, keepdims=True) + jnp.exp(sk - m)
        o = (dot(e_p, vp_ref[:, hs]) + dot(e_c, vc_ref[:, hs]) + dot(e_n, vn_ref[:, hs])
             + dot(e_x0, vx_ref[0:BLK, hs]) + dot(e_x1, vx_ref[BLK:2 * BLK, hs]))
        o = o / den
        for g, h in enumerate(heads):
            o_ref[:, h * HEAD_DIM:(h + 1) * HEAD_DIM] = o[g * BLK:(g + 1) * BLK]


def _attention(sink, q, k, v, with_ctx):
    nblk = NB_LAT + (NB_CTX if with_ctx else 0)

    def q_idx(b, n):
        return (jnp.where(n < NB_LAT, b * NB_LAT + n, B * NB_LAT + b * NB_CTX + (n - NB_LAT)), 0)

    def kv_idx(off):
        def idx(b, n):
            nn = jnp.clip(jnp.minimum(n, NB_LAT - 1) + off, 0, NB_LAT - 1)
            return (b * NB_LAT + nn, 0)
        return idx

    ctx_idx = lambda b, n: (T_LAT // C + b, 0)
    kv_specs = lambda: [pl.BlockSpec((BLK, KV_W), kv_idx(-1)), pl.BlockSpec((BLK, KV_W), kv_idx(0)),
                        pl.BlockSpec((BLK, KV_W), kv_idx(1)), pl.BlockSpec((C, KV_W), ctx_idx)]
    return pl.pallas_call(
        _attn_kernel,
        grid=(B, nblk),
        in_specs=[pl.BlockSpec(memory_space=pltpu.SMEM), pl.BlockSpec((BLK, ATTN_W), q_idx)] + kv_specs() + kv_specs(),
        out_specs=pl.BlockSpec((BLK, ATTN_W), q_idx),
        out_shape=jax.ShapeDtypeStruct((T_ALL if with_ctx else T_LAT, ATTN_W), F32),
        compiler_params=_cp(32 << 20, 2),
        name="attention",
    )(sink, q, k, k, k, k, v, v, v, v)


def _hyena_pre_kernel(z_ref, zp_ref, zn_ref, w_ref, b_ref, u_ref, x0_ref):
    i = pl.program_id(0)
    tiles_per_seq = L // TM
    is_ctx = i >= B * tiles_per_seq
    first = is_ctx | (i % tiles_per_seq == 0)
    last = is_ctx | (i % tiles_per_seq == tiles_per_seq - 1)
    z = z_ref[...]
    prev_row = jnp.where(first, 0.0, zp_ref[7:8, :])
    next_row = jnp.where(last, 0.0, zn_ref[0:1, :])
    row = lax.broadcasted_iota(jnp.int32, z.shape, 0)
    z_m1 = jnp.where(row == 0, prev_row, pltpu.roll(z, 1, 0))
    z_p1 = jnp.where(row == TM - 1, next_row, pltpu.roll(z, TM - 1, 0))
    zc = b_ref[...] + z_m1 * w_ref[0:1, :] + z * w_ref[1:2, :] + z_p1 * w_ref[2:3, :]
    u_ref[...] = zc[:, 0:HY_W] * zc[:, HY_W:2 * HY_W]
    x0_ref[...] = zc[:, 2 * HY_W:3 * HY_W]


def _hyena_pre(z, short_w, short_b):
    nt = T_ALL // TM
    sub = TM // 8
    n8 = T_ALL // 8
    return pl.pallas_call(
        _hyena_pre_kernel,
        grid=(nt,),
        in_specs=[
            pl.BlockSpec((TM, 3 * HY_W), lambda i: (i, 0)),
            pl.BlockSpec((8, 3 * HY_W), lambda i: (jnp.maximum(i * sub - 1, 0), 0)),
            pl.BlockSpec((8, 3 * HY_W), lambda i: (jnp.minimum((i + 1) * sub, n8 - 1), 0)),
            pl.BlockSpec((SHORT_K, 3 * HY_W), lambda i: (0, 0)),
            pl.BlockSpec((1, 3 * HY_W), lambda i: (0, 0)),
        ],
        out_specs=[pl.BlockSpec((TM, HY_W), lambda i: (i, 0)), pl.BlockSpec((TM, HY_W), lambda i: (i, 0))],
        out_shape=[jax.ShapeDtypeStruct((T_ALL, HY_W), F32), jax.ShapeDtypeStruct((T_ALL, HY_W), F32)],
        compiler_params=_cp(32 << 20, 1),
        name="hyena_pre",
    )(z, z, z, short_w, short_b.reshape(1, 3 * HY_W))


def _dft_tables():
    t0 = jnp.arange(FFT_R, dtype=jnp.int32)[:, None, None]
    k1 = jnp.arange(K1P, dtype=jnp.int32)[None, :, None]
    t1 = jnp.arange(FFT_T1, dtype=jnp.int32)[None, None, :]
    m = (k1 * (FFT_R * t1 + t0)) % N_FFT
    ang = m.astype(F32) * (2.0 * math.pi / N_FFT)
    used = (k1 < K1_USED).astype(F32)
    g_cos = jnp.cos(ang) * used
    g_sin = jnp.sin(ang) * used
    a = jnp.arange(FFT_R, dtype=jnp.int32)
    ang2 = ((a[:, None] * a[None, :]) % FFT_R).astype(F32) * (2.0 * math.pi / FFT_R)
    return g_cos, g_sin, jnp.cos(ang2), jnp.sin(ang2)


def _hyena_spec_kernel(k_ref, gc_ref, gs_ref, fc_ref, fs_ref, wt_ref, kr_ref, ki_ref, ar_ref, ai_ref):
    half = pl.program_id(1)
    kk = lax.broadcasted_iota(jnp.int32, (K1H, 1), 0) + half * K1H
    sign = jnp.where(kk % 2 == 0, 1.0, -1.0).astype(F32)

    dot = lambda a, b: jnp.dot(a, b.astype(BF16), preferred_element_type=F32)

    def stage1(i, carry):
        t0s = [i * FFT_UNROLL + u for u in range(FFT_UNROLL)]
        loaded = [(k_ref[pl.ds(t0, FFT_T1, stride=FFT_R), :], k_ref[pl.ds(L + t0, FFT_T1, stride=FFT_R), :],
                   gc_ref[t0].astype(BF16), gs_ref[t0].astype(BF16)) for t0 in t0s]
        res = [(dot(gc, x_lo) + sign * dot(gc, x_hi), -(dot(gs, x_lo) + sign * dot(gs, x_hi)))
               for x_lo, x_hi, gc, gs in loaded]
        for t0, (a_r, a_i) in zip(t0s, res):
            rows = pl.ds(pl.multiple_of(t0 * K1H, 8), K1H)
            ar_ref[rows, :] = a_r
            ai_ref[rows, :] = a_i
        return carry

    lax.fori_loop(0, FFT_R // FFT_UNROLL, stage1, 0)
    fc = fc_ref[...].astype(BF16)
    fs = fs_ref[...].astype(BF16)

    def stage2(i, carry):
        kls = [i * FFT_UNROLL + u for u in range(FFT_UNROLL)]
        loaded = [(ar_ref[pl.ds(kl, FFT_R, stride=K1H), :], ai_ref[pl.ds(kl, FFT_R, stride=K1H), :]) for kl in kls]
        for kl, (a_r, a_i) in zip(kls, loaded):
            w = wt_ref[half * K1H + kl]
            kr_ref[kl] = (dot(fc, a_r) + dot(fs, a_i)) * w
            ki_ref[kl] = (dot(fc, a_i) - dot(fs, a_r)) * w
        return carry

    lax.fori_loop(0, K1H // FFT_UNROLL, stage2, 0)


def _hyena_spectrum(kfilt, g_cos, g_sin, f_cos, f_sin, wts):
    nct = HY_W // LANE
    gspec = pl.BlockSpec((FFT_R, K1H, FFT_T1), lambda c, h: (0, h, 0))
    fspec = pl.BlockSpec((FFT_R, FFT_R), lambda c, h: (0, 0))
    ospec = pl.BlockSpec((K1H, FFT_R, LANE), lambda c, h: (h, 0, c))
    return pl.pallas_call(
        _hyena_spec_kernel,
        grid=(nct, 2),
        in_specs=[pl.BlockSpec((N_FFT, LANE), lambda c, h: (0, c)), gspec, gspec, fspec, fspec,
                  pl.BlockSpec(memory_space=pltpu.SMEM)],
        out_specs=[ospec, ospec],
        out_shape=[jax.ShapeDtypeStruct((K1P, FFT_R, HY_W), F32)] * 2,
        scratch_shapes=[pltpu.VMEM((FFT_R * K1H, LANE), F32)] * 2,
        compiler_params=_cp(56 << 20, 2),
        name="hyena_spectrum",
    )(kfilt, g_cos, g_sin, f_cos, f_sin, wts)


def _hyena_fft_kernel(u_ref, gc_ref, gs_ref, ic_ref, is_ref, fc_ref, fs_ref, kr_ref, ki_ref, o_ref, ar_ref, ai_ref):
    bdot = lambda a, b: jnp.dot(a, b.astype(BF16), preferred_element_type=F32)

    def stage1(i, carry):
        t0s = [i * FFT_UNROLL + u for u in range(FFT_UNROLL)]
        xs = [u_ref[pl.ds(t0, FFT_T1, stride=FFT_R), :] for t0 in t0s]
        res = [(bdot(gc_ref[t0], x), -bdot(gs_ref[t0], x)) for t0, x in zip(t0s, xs)]
        for t0, (a_r, a_i) in zip(t0s, res):
            rows = pl.ds(pl.multiple_of(t0 * K1P, 8), K1P)
            ar_ref[rows, :] = a_r
            ai_ref[rows, :] = a_i
        return carry

    lax.fori_loop(0, FFT_R // FFT_UNROLL, stage1, 0)
    fc = fc_ref[...]
    fs = fs_ref[...]

    def stage23(i, carry):
        k1s = [i * FFT_UNROLL + u for u in range(FFT_UNROLL)]
        loaded = [(ar_ref[pl.ds(k1, FFT_R, stride=K1P), :], ai_ref[pl.ds(k1, FFT_R, stride=K1P), :],
                   kr_ref[k1], ki_ref[k1]) for k1 in k1s]
        res = []
        for a_r, a_i, k_r, k_i in loaded:
            z_r = bdot(fc, a_r) + bdot(fs, a_i)
            z_i = bdot(fc, a_i) - bdot(fs, a_r)
            y_r = z_r * k_r - z_i * k_i
            y_i = z_r * k_i + z_i * k_r
            res.append((bdot(fc, y_r) - bdot(fs, y_i), bdot(fc, y_i) + bdot(fs, y_r)))
        for k1, (b_r, b_i) in zip(k1s, res):
            ar_ref[pl.ds(k1, FFT_R, stride=K1P), :] = b_r
            ai_ref[pl.ds(k1, FFT_R, stride=K1P), :] = b_i
        return carry

    lax.fori_loop(0, -(-K1_USED // FFT_UNROLL), stage23, 0)

    def stage4(i, carry):
        t0s = [i * FFT_UNROLL + u for u in range(FFT_UNROLL)]
        loaded = []
        for t0 in t0s:
            rows = pl.ds(pl.multiple_of(t0 * K1P, 8), K1P)
            loaded.append((ar_ref[rows, :], ai_ref[rows, :]))
        res = [bdot(ic_ref[t0], b_r) - bdot(is_ref[t0], b_i) for t0, (b_r, b_i) in zip(t0s, loaded)]
        for t0, y in zip(t0s, res):
            o_ref[pl.ds(t0, FFT_T1, stride=FFT_R), :] = y
        return carry

    lax.fori_loop(0, FFT_R // FFT_UNROLL, stage4, 0)


def _hyena_fft(u, g_cos_bf, g_sin_bf, i_cos_bf, i_sin_bf, f_cos_bf, f_sin_bf, k_r, k_i):
    nct = HY_W // LANE
    one = pl.Buffered(1)
    gspec = pl.BlockSpec((FFT_R, K1P, FFT_T1), lambda c, b: (0, 0, 0), pipeline_mode=one)
    ispec = pl.BlockSpec((FFT_R, FFT_T1, K1P), lambda c, b: (0, 0, 0), pipeline_mode=one)
    fspec = pl.BlockSpec((FFT_R, FFT_R), lambda c, b: (0, 0), pipeline_mode=one)
    kspec = pl.BlockSpec((K1P, FFT_R, LANE), lambda c, b: (0, 0, c), pipeline_mode=one)
    return pl.pallas_call(
        _hyena_fft_kernel,
        grid=(nct, B),
        in_specs=[pl.BlockSpec((L, LANE), lambda c, b: (b, c)), gspec, gspec, ispec, ispec, fspec, fspec, kspec, kspec],
        out_specs=pl.BlockSpec((L, LANE), lambda c, b: (b, c)),
        out_shape=jax.ShapeDtypeStruct((T_LAT, HY_W), F32),
        scratch_shapes=[pltpu.VMEM((FFT_R * K1P, LANE), F32)] * 2,
        compiler_params=_cp(56 << 20, 2),
        name="hyena_fft",
    )(u, g_cos_bf, g_sin_bf, i_cos_bf, i_sin_bf, f_cos_bf, f_sin_bf, k_r, k_i)


def _hyena_ctx_kernel(u_ref, k_ref, dc_ref, ds_ref, o_ref):
    dot = lambda a, b: jnp.dot(a, b, precision=HIGHEST, preferred_element_type=F32)
    dc = dc_ref[...]
    ds = ds_ref[...]
    u = u_ref[...]
    kf = k_ref[...]
    u_r = dot(dc[:, 0:C], u)
    u_i = -dot(ds[:, 0:C], u)
    k_r = dot(dc, kf)
    k_i = -dot(ds, kf)
    y_r = u_r * k_r - u_i * k_i
    y_i = u_r * k_i + u_i * k_r
    o_ref[...] = (dot(dc[0:C, :], y_r) - dot(ds[0:C, :], y_i)) * (1.0 / (2 * C))


def _hyena_ctx(u, kfilt_ctx, d_cos, d_sin):
    full = lambda b: (0, 0)
    return pl.pallas_call(
        _hyena_ctx_kernel,
        grid=(B,),
        in_specs=[pl.BlockSpec((C, HY_W), lambda b: (T_LAT // C + b, 0)),
                  pl.BlockSpec((2 * C, HY_W), full), pl.BlockSpec((2 * C, 2 * C), full), pl.BlockSpec((2 * C, 2 * C), full)],
        out_specs=pl.BlockSpec((C, HY_W), lambda b: (b, 0)),
        out_shape=jax.ShapeDtypeStruct((T_CTX, HY_W), F32),
        compiler_params=_cp(32 << 20, 1),
        name="hyena_ctx",
    )(u, kfilt_ctx, d_cos, d_sin)


def _hyena_filter(n, w1, b1, freq, w2, b2, w3):
    t = jnp.linspace(0.0, 1.0, n, dtype=F32)[:, None]
    bands = (FILTER_EMB - 1) // 2
    w = 2.0 * math.pi * jnp.arange(n, dtype=F32)[:, None] / n
    f = jnp.linspace(1e-4, bands - 1, bands, dtype=F32)[None, :]
    z = jnp.concatenate([t, jnp.cos(f * w), -jnp.sin(f * w)], axis=-1)
    mm = functools.partial(jnp.matmul, precision=HIGHEST)
    h = jnp.sin(freq * (mm(z, w1) + b1))
    h = jnp.sin(freq * (mm(h, w2) + b2))
    h = mm(h, w3)
    deltas = jnp.abs(jnp.linspace(math.log(DECAY_TARGET) / DECAY_FAST, math.log(DECAY_TARGET) / DECAY_SLOW,
                                  HY_W, dtype=F32))
    decay = jnp.exp(-t * deltas[None, :])
    h_fwd = h[:, :HY_W] * decay
    h_bwd = h[:, HY_W:] * decay
    k = jnp.concatenate([h_fwd, jnp.zeros((1, HY_W), F32), h_bwd[:0:-1]], axis=0)
    return k / jnp.sum(jnp.abs(k), axis=0, keepdims=True)


def _s5_matrices(a_re, a_im, log_dt, b_re, b_im, c_re, c_im, d_skip):
    dt = jnp.exp(log_dt)[:, :, None]
    lam_re = jnp.minimum(a_re, -1e-4)
    mag1 = jnp.exp(lam_re * dt)
    lbr = mag1 * jnp.cos(a_im * dt)
    lbi = mag1 * jnp.sin(a_im * dt)
    den = lam_re * lam_re + a_im * a_im
    fr = ((lbr - 1.0) * lam_re + lbi * a_im) / den
    fi = (lbi * lam_re - (lbr - 1.0) * a_im) / den
    bbr = fr[..., None] * b_re - fi[..., None] * b_im
    bbi = fr[..., None] * b_im + fi[..., None] * b_re
    j = jnp.arange(S5_CH + 1, dtype=F32)[:, None, None, None]
    magj = jnp.exp(j * (lam_re * dt)[None])
    pr = magj * jnp.cos(j * (a_im * dt)[None])
    pi = magj * jnp.sin(j * (a_im * dt)[None])
    hi = functools.partial(jnp.einsum, precision=HIGHEST)
    lbr_j = pr[..., None] * bbr[None] - pi[..., None] * bbi[None]
    lbi_j = pr[..., None] * bbi[None] + pi[..., None] * bbr[None]
    m = hi('dgop,jdgpi->jdgoi', c_re, lbr_j) - hi('dgop,jdgpi->jdgoi', c_im, lbi_j)
    eye_g = jnp.eye(S5_GROUPS, dtype=F32)
    s = jnp.arange(S5_CH)
    blocks = jnp.einsum('jdgoi,gh->djgiho', m[0:S5_CH], eye_g).reshape(2, S5_CH, S5_W, S5_W)
    lag0 = blocks[0, 0] + blocks[1, 0] + jnp.diag(d_skip)
    e_all = jnp.concatenate([blocks[1, S5_CH - 1:0:-1], lag0[None], blocks[0, 1:S5_CH]], axis=0).astype(BF16)
    sf_r = lbr_j[S5_CH - 1 - s, 0]
    sf_i = lbi_j[S5_CH - 1 - s, 0]
    sb_r = lbr_j[s, 1]
    sb_i = lbi_j[s, 1]
    st = jnp.stack([sf_r, sf_i, sb_r, sb_i], axis=0)
    rr = jnp.arange(S5_ROWW, dtype=jnp.int32)
    cc = jnp.arange(S5_W, dtype=jnp.int32)
    ws = _s5_expand(jnp.transpose(st, (1, 2, 4, 0, 3)).reshape(S5_ROWW, S5_W),
                    (cc[:, None] // S5_STATE == rr[None, :] // S5_NSTATE)
                    & (cc[:, None] % S5_STATE == rr[None, :] % S5_STATE),
                    (rr // S5_GROUP) % S5_GROUPS, (rr % S5_NSTATE) // S5_STATE)
    tt = jnp.arange(S5_CH)
    cf_r = c_re[0][None] * pr[tt + 1, 0][:, :, None, :] - c_im[0][None] * pi[tt + 1, 0][:, :, None, :]
    cf_i = c_re[0][None] * pi[tt + 1, 0][:, :, None, :] + c_im[0][None] * pr[tt + 1, 0][:, :, None, :]
    cb_r = c_re[1][None] * pr[S5_CH - tt, 1][:, :, None, :] - c_im[1][None] * pi[S5_CH - tt, 1][:, :, None, :]
    cb_i = c_re[1][None] * pi[S5_CH - tt, 1][:, :, None, :] + c_im[1][None] * pr[S5_CH - tt, 1][:, :, None, :]
    ct = jnp.stack([cf_r, -cf_i, cb_r, -cb_i], axis=0)
    wc = _s5_expand(jnp.transpose(ct, (0, 2, 4, 1, 3)).reshape(4 * S5_NSTATE, S5_W),
                    (cc[:, None] // S5_GROUP == rr[None, :] // S5_W)
                    & (cc[:, None] % S5_GROUP == rr[None, :] % S5_GROUP),
                    (rr % S5_NSTATE) // S5_STATE, (rr % S5_W) // S5_GROUP)
    lam_p = jnp.stack([jnp.stack([pr[S5_CH, 0], pi[S5_CH, 0]]), jnp.stack([pr[S5_CH, 1], pi[S5_CH, 1]])])
    return ws, e_all, wc, lam_p.reshape(2, 2, 1, S5_NSTATE)


S5_TN = 512
S5_NS = S5_ROWW // S5_W


def _s5_expand_kernel(a_ref, ex_ref, rg_ref, cg_ref, o_ref):
    v = jnp.dot(a_ref[...], ex_ref[...], preferred_element_type=F32)
    o_ref[...] = jnp.where(rg_ref[...] == cg_ref[...], v, 0.0).astype(BF16)


def _s5_expand(compact, placement, row_group, col_group):
    n = compact.shape[0]
    return pl.pallas_call(
        _s5_expand_kernel,
        grid=(S5_ROWW // S5_TN,),
        in_specs=[pl.BlockSpec((n, S5_W), lambda j: (0, 0)), pl.BlockSpec((S5_W, S5_TN), lambda j: (0, j)),
                  pl.BlockSpec((n, 1), lambda j: (0, 0)), pl.BlockSpec((1, S5_TN), lambda j: (0, j))],
        out_specs=pl.BlockSpec((n, S5_TN), lambda j: (0, j)),
        out_shape=jax.ShapeDtypeStruct((n, S5_ROWW), BF16),
        compiler_params=_cp(32 << 20, 1),
        name="s5_expand",
    )(compact.astype(BF16), placement.astype(BF16), row_group.reshape(n, 1), col_group.reshape(1, S5_ROWW))


def _s5_in_kernel(u_ref, ws_ref, e_ref, s_ref, o_ref):
    j = pl.program_id(0)

    @pl.when(j < S5_NS)
    def _():
        s_ref[...] = jnp.dot(u_ref[...], ws_ref[...], preferred_element_type=F32)

    @pl.when(j >= S5_NS)
    def _():
        t = j - S5_NS
        acc = jnp.dot(u_ref[:, 0:S5_W], e_ref[S5_CH - 1 + t], preferred_element_type=F32)
        for s in range(1, S5_CH):
            acc = acc + jnp.dot(u_ref[:, s * S5_W:(s + 1) * S5_W], e_ref[S5_CH - 1 + t - s],
                                preferred_element_type=F32)
        o_ref[...] = acc


def _s5_in(u_rows, ws, e_all):
    return pl.pallas_call(
        _s5_in_kernel,
        grid=(2 * S5_NS,),
        in_specs=[pl.BlockSpec((N_CHUNK, S5_ROWW), lambda j: (0, 0)),
                  pl.BlockSpec((S5_ROWW, S5_W), lambda j: (0, jnp.minimum(j, S5_NS - 1))),
                  pl.BlockSpec((2 * S5_CH - 1, S5_W, S5_W), lambda j: (0, 0, 0))],
        out_specs=[pl.BlockSpec((N_CHUNK, S5_W), lambda j: (0, jnp.minimum(j, S5_NS - 1))),
                   pl.BlockSpec((N_CHUNK, S5_W), lambda j: (0, jnp.maximum(j - S5_NS, 0)))],
        out_shape=[jax.ShapeDtypeStruct((N_CHUNK, 4 * S5_NSTATE), F32), jax.ShapeDtypeStruct((N_CHUNK, S5_ROWW), F32)],
        compiler_params=_cp(48 << 20, 1),
        name="s5_in",
    )(u_rows, ws, e_all)


def _s5_scan_kernel(s_ref, lam_ref, h_ref):
    lam = [[lam_ref[d, p] for p in range(2)] for d in range(2)]

    def step(b, d, chunk, h):
        row = pl.ds(chunk, 1)
        cols_r = pl.ds(d * 2 * S5_NSTATE, S5_NSTATE)
        cols_i = pl.ds(d * 2 * S5_NSTATE + S5_NSTATE, S5_NSTATE)
        h_ref[row, cols_r] = h[0]
        h_ref[row, cols_i] = h[1]
        s_r = s_ref[row, cols_r]
        s_i = s_ref[row, cols_i]
        lr, li = lam[d]
        return (lr * h[0] - li * h[1] + s_r, lr * h[1] + li * h[0] + s_i)

    def chain_order(b, d, n_ctx_done):
        ctx0 = B * LAT_CHUNKS + b * CTX_CHUNKS
        lat0 = b * LAT_CHUNKS
        if d == 0:
            return (lambda i: ctx0 + i), (lambda i: lat0 + i)
        return (lambda i: ctx0 + CTX_CHUNKS - 1 - i), (lambda i: lat0 + LAT_CHUNKS - 1 - i)

    chains = [(b, d) for b in range(B) for d in range(2)]
    zero = jnp.zeros((1, S5_NSTATE), F32)
    init = tuple((zero, zero) for _ in chains)

    def phase(n_steps, which, carry):
        def body(i, hs):
            out = []
            for (b, d), h in zip(chains, hs):
                order = chain_order(b, d, 0)[which]
                out.append(step(b, d, order(i), h))
            return tuple(out)
        return lax.fori_loop(0, n_steps, body, carry)

    carry = phase(CTX_CHUNKS, 0, init)
    phase(LAT_CHUNKS, 1, carry)


def _s5_scan(s_rows, lam_p):
    return pl.pallas_call(
        _s5_scan_kernel,
        out_shape=jax.ShapeDtypeStruct((N_CHUNK, 4 * S5_NSTATE), F32),
        compiler_params=pltpu.CompilerParams(vmem_limit_bytes=48 << 20),
        name="s5_scan",
    )(s_rows, lam_p)


S5_TM = N_CHUNK // 2
S5_TPN = S5_TN // S5_W


def _s5_out_kernel(h_ref, w_ref, y_ref, *rest):
    o_refs, hb_ref = rest[:-1], rest[-1]
    j = pl.program_id(1)

    @pl.when(j == 0)
    def _():
        hb_ref[...] = h_ref[...].astype(BF16)

    acc = jnp.dot(hb_ref[...], w_ref[...], preferred_element_type=F32) + y_ref[...]
    for tt in range(S5_TPN):
        for h, o_ref in enumerate(o_refs):
            o_ref[pl.ds(j * S5_TPN + tt, S5_TM, stride=S5_CH), :] = (
                acc[:, tt * S5_W + h * LANE:tt * S5_W + (h + 1) * LANE])


def _s5_out(h_rows, wc, y_in):
    nn = S5_ROWW // S5_TN
    n_out = S5_W // LANE
    return pl.pallas_call(
        _s5_out_kernel,
        grid=(N_CHUNK // S5_TM, nn),
        in_specs=[pl.BlockSpec((S5_TM, 4 * S5_NSTATE), lambda i, j: (i, 0), pipeline_mode=pl.Buffered(1)),
                  pl.BlockSpec((4 * S5_NSTATE, S5_TN), lambda i, j: (0, j)),
                  pl.BlockSpec((S5_TM, S5_TN), lambda i, j: (i, j))],
        out_specs=[pl.BlockSpec((S5_TM * S5_CH, LANE), lambda i, j: (i, 0))] * n_out,
        out_shape=[jax.ShapeDtypeStruct((T_ALL, LANE), F32)] * n_out,
        scratch_shapes=[pltpu.VMEM((S5_TM, 4 * S5_NSTATE), BF16)],
        compiler_params=_cp(52 << 20, 2),
        name="s5_out",
    )(h_rows, wc, y_in)


def _rms(x):
    return x * lax.rsqrt(jnp.mean(x * x, axis=-1, keepdims=True) + LN_EPS)


def _layer_norm(x, g, b):
    mu = jnp.mean(x, axis=-1, keepdims=True)
    xc = x - mu
    var = jnp.mean(xc * xc, axis=-1, keepdims=True)
    return xc * lax.rsqrt(var + LN_EPS) * g + b


def _merge_kernel(x_ref, attn_ref, conv_ref, hu_ref, x0_ref, s5a_ref, s5b_ref, g1_ref, sh2_ref, sc2_ref, mixg_ref,
                  hyd_ref, wglu_ref, wout_ref, lng_ref, lnb_ref, o_ref, u2_ref):
    hy = (conv_ref[...] + hu_ref[...] * hyd_ref[...]) * x0_ref[...]
    g = jax.nn.gelu(jnp.concatenate([s5a_ref[...], s5b_ref[...]], axis=-1))
    s5 = g * jax.nn.sigmoid(jnp.dot(g.astype(BF16), wglu_ref[...], preferred_element_type=F32))
    mixg = mixg_ref[...]
    parts = [_rms(attn_ref[...]) * mixg[:, 0:ATTN_W],
             _rms(hy) * mixg[:, ATTN_W:ATTN_W + HY_W],
             _rms(s5) * mixg[:, ATTN_W + HY_W:MIX_W]]
    mix = jnp.concatenate(parts, axis=-1).astype(BF16)
    o = jnp.dot(mix, wout_ref[...], preferred_element_type=F32)
    x1 = _layer_norm(ALPHA * x_ref[...] + g1_ref[...] * o, lng_ref[...], lnb_ref[...])
    o_ref[...] = x1
    u2_ref[...] = x1 * (1.0 + sc2_ref[...]) + sh2_ref[...]


def _merge(n_rows, xall, attn, conv, hu, x0c, s5y, g1, sh2, sc2, mix_g, hy_d, wglu_bf, wout_bf, ln_g, ln_b):
    nt = n_rows // TM
    row = lambda i: (i, 0)
    full = lambda i: (0, 0)
    return pl.pallas_call(
        _merge_kernel,
        grid=(nt,),
        in_specs=[pl.BlockSpec((TM, D), row), pl.BlockSpec((TM, ATTN_W), row), pl.BlockSpec((TM, HY_W), row),
                  pl.BlockSpec((TM, HY_W), row), pl.BlockSpec((TM, HY_W), row),
                  pl.BlockSpec((TM, LANE), row), pl.BlockSpec((TM, LANE), row),
                  _vec_spec(TM), _vec_spec(TM), _vec_spec(TM), pl.BlockSpec((1, MIX_W), full),
                  pl.BlockSpec((1, HY_W), full), pl.BlockSpec((S5_W, S5_W), full), pl.BlockSpec((MIX_W, D), full),
                  pl.BlockSpec((1, D), full), pl.BlockSpec((1, D), full)],
        out_specs=[pl.BlockSpec((TM, D), row), pl.BlockSpec((TM, D), row)],
        out_shape=[jax.ShapeDtypeStruct((n_rows, D), F32), jax.ShapeDtypeStruct((n_rows, D), F32)],
        compiler_params=_cp(48 << 20, 1),
        name="merge",
    )(xall, attn, conv, hu, x0c, s5y[0], s5y[1], g1, sh2, sc2, mix_g.reshape(1, MIX_W), hy_d.reshape(1, HY_W), wglu_bf,
      wout_bf, ln_g.reshape(1, D), ln_b.reshape(1, D))


def _router_kernel(u_ref, wt_ref, b_ref, e_ref, g_ref):
    logits = lax.dot_general(wt_ref[...], u_ref[...], (((1,), (1,)), ((), ())), precision=HIGHEST,
                             preferred_element_type=F32)
    scores = jax.nn.sigmoid(logits)
    biased = scores + b_ref[...]
    ninf = -jnp.inf
    grow = lax.broadcasted_iota(jnp.int32, (EGROUP, TM), 0)
    groups = [biased[gi * EGROUP:(gi + 1) * EGROUP] for gi in range(N_EGROUPS)]
    gscore = []
    for vals in groups:
        m1 = jnp.max(vals, axis=0, keepdims=True)
        i1 = jnp.min(jnp.where(vals == m1, grow, EGROUP), axis=0, keepdims=True)
        m2 = jnp.max(jnp.where(grow == i1, ninf, vals), axis=0, keepdims=True)
        gscore.append(m1 + m2)
    kept = []
    for gi in range(N_EGROUPS):
        rank = jnp.zeros((1, TM), jnp.int32)
        for gj in range(N_EGROUPS):
            if gj == gi:
                continue
            ahead = (gscore[gj] > gscore[gi]) | ((gscore[gj] == gscore[gi]) & (gj < gi))
            rank = rank + ahead.astype(jnp.int32)
        kept.append(jnp.where(rank < TOPK_GROUPS, groups[gi], ninf))
    masked = jnp.concatenate(kept, axis=0)
    row = lax.broadcasted_iota(jnp.int32, (N_EXPERTS, TM), 0)
    gates = []
    gsum = jnp.zeros((1, TM), F32)
    for kk in range(TOP_K):
        m = jnp.max(masked, axis=0, keepdims=True)
        idx = jnp.min(jnp.where(masked == m, row, N_EXPERTS), axis=0, keepdims=True)
        hit = row == idx
        gate = jnp.sum(jnp.where(hit, scores, 0.0), axis=0, keepdims=True)
        masked = jnp.where(hit, ninf, masked)
        e_ref[kk:kk + 1, :] = idx
        gates.append(gate)
        gsum = gsum + gate
    for kk in range(TOP_K):
        g_ref[kk:kk + 1, :] = gates[kk] / gsum * ROUTED_SCALE


def _router(n_rows, u2, w_router_t, router_bias):
    nt = n_rows // TM
    col = lambda i: (0, i)
    return pl.pallas_call(
        _router_kernel,
        grid=(nt,),
        in_specs=[pl.BlockSpec((TM, D), lambda i: (i, 0)),
                  pl.BlockSpec((N_EXPERTS, D), lambda i: (0, 0)), pl.BlockSpec((N_EXPERTS, 1), lambda i: (0, 0))],
        out_specs=[pl.BlockSpec((TOP_K, TM), col), pl.BlockSpec((TOP_K, TM), col)],
        out_shape=[jax.ShapeDtypeStruct((TOP_K, n_rows), jnp.int32), jax.ShapeDtypeStruct((TOP_K, n_rows), F32)],
        compiler_params=_cp(32 << 20, 1),
        name="router",
    )(u2, w_router_t, router_bias.reshape(N_EXPERTS, 1))


def _dispatch(top_e, gate):
    t = top_e.shape[1]
    tk = t * TOP_K
    nblk = tk // MOE_BLOCK
    n_steps = nblk + N_EXPERTS
    flat_e = top_e.reshape(tk)
    pos = jnp.arange(tk, dtype=jnp.int32)
    _, order, sw = lax.sort((flat_e, pos, gate.reshape(tk)), num_keys=1, is_stable=True)
    _, inv = lax.sort((order, pos), num_keys=1)
    experts = jnp.arange(N_EXPERTS, dtype=jnp.int32)
    counts = jnp.sum((flat_e[None, :] == experts[:, None]).astype(jnp.int32), axis=1)
    ends = jnp.cumsum(counts)
    starts = ends - counts
    fb = starts // MOE_BLOCK
    npairs = jnp.where(counts > 0, (ends - 1) // MOE_BLOCK - fb + 1, 0)
    pend = jnp.cumsum(npairs)
    poff = pend - npairs
    n_pairs = pend[-1]
    s = jnp.arange(n_steps, dtype=jnp.int32)
    sc = jnp.minimum(s, n_pairs - 1)
    pe = jnp.sum((pend[None, :] <= sc[:, None]).astype(jnp.int32), axis=1)
    pb = fb[pe] + (sc - poff[pe])
    lo = jnp.where(s < n_pairs, jnp.maximum(starts[pe] - pb * MOE_BLOCK, 0), 0)
    hi = jnp.where(s < n_pairs, jnp.minimum(ends[pe] - pb * MOE_BLOCK, MOE_BLOCK), 0)
    return dict(pe=pe, pb=pb, lo=lo, hi=hi, tok=order % t, comb=inv, w=sw.reshape(nblk, MOE_BLOCK, 1))


def _expert_kernel(pe, pb, plo, phi, x_ref, w_ref, wg_ref, wu_ref, wd_ref, o_ref, wgb, wub, wdb):
    s = pl.program_id(0)
    prev = jnp.maximum(s - 1, 0)

    @pl.when((s == 0) | (pe[s] != pe[prev]))
    def _():
        wgb[...] = wg_ref[...].astype(BF16)
        wub[...] = wu_ref[...].astype(BF16)
        wdb[...] = wd_ref[...].astype(BF16)

    lo = plo[s]
    hi = phi[s]
    first_of_block = (s == 0) | (pb[s] != pb[prev])

    def ffn():
        xb = x_ref[...].astype(BF16)
        hg = jnp.dot(xb, wgb[...], preferred_element_type=F32)
        hu = jnp.dot(xb, wub[...], preferred_element_type=F32)
        h = (hg * jax.nn.sigmoid(hg)) * hu
        return jnp.dot(h.astype(BF16), wdb[...], preferred_element_type=F32) * w_ref[0]

    @pl.when((hi > lo) & first_of_block)
    def _():
        o_ref[...] = ffn()

    @pl.when((hi > lo) & jnp.logical_not(first_of_block))
    def _():
        row = lax.broadcasted_iota(jnp.int32, (MOE_BLOCK, D), 0)
        o_ref[...] = jnp.where((row >= lo) & (row < hi), ffn(), o_ref[...])


def _experts(xs, disp, wg, wu, wd, layer):
    tk = xs.shape[0]
    n_steps = disp['pe'].shape[0]
    blk = lambda s, pe, pb, lo, hi: (pb[s], 0)
    wspec = lambda shape: pl.BlockSpec((None, None) + shape, lambda s, pe, pb, lo, hi: (layer, pe[s], 0, 0))
    grid_spec = pltpu.PrefetchScalarGridSpec(
        num_scalar_prefetch=4,
        grid=(n_steps,),
        in_specs=[pl.BlockSpec((MOE_BLOCK, D), blk),
                  pl.BlockSpec((1, MOE_BLOCK, 1), lambda s, pe, pb, lo, hi: (pb[s], 0, 0)),
                  wspec((D, EXPERT_FF)), wspec((D, EXPERT_FF)), wspec((EXPERT_FF, D))],
        out_specs=pl.BlockSpec((MOE_BLOCK, D), blk),
        scratch_shapes=[pltpu.VMEM((D, EXPERT_FF), BF16), pltpu.VMEM((D, EXPERT_FF), BF16),
                        pltpu.VMEM((EXPERT_FF, D), BF16)],
    )
    return pl.pallas_call(
        _expert_kernel,
        grid_spec=grid_spec,
        out_shape=jax.ShapeDtypeStruct((tk, D), F32),
        compiler_params=_cp(40 << 20, 1),
        name="experts",
    )(disp['pe'], disp['pb'], disp['lo'], disp['hi'], xs, disp['w'], wg, wu, wd)


SC_ROWS = 64


def _row_gather(table, idx):
    n = idx.shape[0]
    d = table.shape[1]
    mesh = plsc.VectorSubcoreMesh(core_axis_name="c", subcore_axis_name="s")
    n_workers = mesh.num_cores * mesh.num_subcores
    per_worker = n // n_workers
    assert per_worker * n_workers == n and per_worker % SC_ROWS == 0

    @functools.partial(
        pl.kernel, mesh=mesh,
        out_type=jax.ShapeDtypeStruct((n, d), table.dtype),
        scratch_types=[pltpu.VMEM((SC_ROWS,), jnp.int32), pltpu.VMEM((SC_ROWS, d), table.dtype),
                       pltpu.SemaphoreType.DMA],
    )
    def gather(table_hbm, idx_hbm, out_hbm, idx_v, rows_v, sem):
        worker = lax.axis_index("s") * mesh.num_cores + lax.axis_index("c")
        base = worker * per_worker

        @pl.loop(0, per_worker // SC_ROWS)
        def _(c):
            off = pl.multiple_of(base + c * SC_ROWS, 8)
            pltpu.sync_copy(idx_hbm.at[pl.ds(off, SC_ROWS)], idx_v)
            pltpu.async_copy(table_hbm.at[idx_v], rows_v, sem).wait()
            pltpu.sync_copy(rows_v, out_hbm.at[pl.ds(off, SC_ROWS)])

    return gather(table, idx)


FM = 128


def _ffn_out_kernel(x_ref, u_ref, r_ref, g2_ref, wsg_ref, wsu_ref, wsd_ref, lng_ref, lnb_ref, o_ref):
    ub = u_ref[...].astype(BF16)
    hg = jnp.dot(ub, wsg_ref[...], preferred_element_type=F32)
    hu = jnp.dot(ub, wsu_ref[...], preferred_element_type=F32)
    f = jnp.dot(((hg * jax.nn.sigmoid(hg)) * hu).astype(BF16), wsd_ref[...], preferred_element_type=F32)
    routed = r_ref[0]
    for kk in range(1, TOP_K):
        routed = routed + r_ref[kk]
    f = routed + f
    o_ref[...] = _layer_norm(ALPHA * x_ref[...] + g2_ref[...] * f, lng_ref[...], lnb_ref[...])


def _ffn_out(n_rows, x1, u2, routed, g2, wsg_bf, wsu_bf, wsd_bf, ln_g, ln_b):
    row = lambda i: (i, 0)
    full = lambda i: (0, 0)
    return pl.pallas_call(
        _ffn_out_kernel,
        grid=(n_rows // FM,),
        in_specs=[pl.BlockSpec((FM, D), row), pl.BlockSpec((FM, D), row),
                  pl.BlockSpec((TOP_K, FM, D), lambda i: (0, i, 0)),
                  _vec_spec(FM), pl.BlockSpec((D, EXPERT_FF), full), pl.BlockSpec((D, EXPERT_FF), full),
                  pl.BlockSpec((EXPERT_FF, D), full), pl.BlockSpec((1, D), full), pl.BlockSpec((1, D), full)],
        out_specs=pl.BlockSpec((FM, D), row),
        out_shape=jax.ShapeDtypeStruct((n_rows, D), F32),
        compiler_params=_cp(40 << 20, 1),
        name="ffn_out",
    )(x1, u2, routed, g2, wsg_bf, wsu_bf, wsd_bf, ln_g.reshape(1, D), ln_b.reshape(1, D))


def _rope_tables():
    t = jnp.arange(L, dtype=jnp.int32)
    row = (t // GRID_W).astype(F32)
    col = (t % GRID_W).astype(F32)
    inv_freq = ROPE_BASE ** (-jnp.arange(0, AXIS_DIM, 2, dtype=F32) / AXIS_DIM)
    half = AXIS_DIM // 2

    def axis(pos):
        ang = pos[:, None] * inv_freq[None, :]
        c = jnp.cos(ang)
        s = jnp.sin(ang)
        return jnp.concatenate([c, c], axis=1), jnp.concatenate([-s, s], axis=1)

    cr, sr = axis(row)
    cc, sc = axis(col)
    cos_h = jnp.concatenate([cr, cc], axis=1)
    sin_h = jnp.concatenate([sr, sc], axis=1)
    cos_l = jnp.tile(cos_h, (B, LANE // HEAD_DIM))
    sin_l = jnp.tile(sin_h, (B, LANE // HEAD_DIM))
    cos_t = jnp.concatenate([cos_l, jnp.ones((T_CTX, LANE), F32)], axis=0)
    sin_t = jnp.concatenate([sin_l, jnp.zeros((T_CTX, LANE), F32)], axis=0)
    del half
    return cos_t, sin_t


def kernel(x, c, ctx, c_ctx, w_ada, b_ada, w_in, w_out, sink, mix_g, hy_short_w, hy_short_b, hy_w1, hy_b1, hy_freq,
           hy_w2, hy_b2, hy_w3, hy_d, s5_a_re, s5_a_im, s5_log_dt, s5_b_re, s5_b_im, s5_c_re, s5_c_im, s5_d, s5_w_glu,
           ln1_g, ln1_b, ln2_g, ln2_b, w_router, router_bias, w_exp_gate, w_exp_up, w_exp_down, w_sh_gate, w_sh_up,
           w_sh_down):
    xall = jnp.concatenate([x.reshape(T_LAT, D), ctx.reshape(T_CTX, D)], axis=0)
    cvec = jnp.concatenate([c, c_ctx[None, :], jnp.zeros((8 - B - 1, D), F32)], axis=0)
    mod = _ada(cvec, w_ada, b_ada)[:, 0:B + 1, :].reshape(DEPTH, B + 1, 6, 1, D)

    cos_t, sin_t = _rope_tables()
    g_cos, g_sin, f_cos, f_sin = _dft_tables()
    g_cos_bf, g_sin_bf = g_cos.astype(BF16), g_sin.astype(BF16)
    i_cos_bf = jnp.swapaxes(g_cos, 1, 2).astype(BF16)
    i_sin_bf = jnp.swapaxes(g_sin, 1, 2).astype(BF16)
    f_cos_bf, f_sin_bf = f_cos.astype(BF16), f_sin.astype(BF16)
    k1 = jnp.arange(K1P)
    spec_w = jnp.where((k1 == 0) | (k1 == FFT_R // 2), 1.0, 2.0) * (k1 < K1_USED) / N_FFT
    spec_w = spec_w.astype(F32)
    kt = jnp.arange(2 * C, dtype=jnp.int32)
    ang_c = ((kt[:, None] * kt[None, :]) % (2 * C)).astype(F32) * (2.0 * math.pi / (2 * C))
    d_cos, d_sin = jnp.cos(ang_c), jnp.sin(ang_c)

    for l in range(DEPTH):
        last = l == DEPTH - 1
        n_rows = T_LAT if last else T_ALL
        sh1, sc1, g1, sh2, sc2, g2 = (mod[l, :, j] for j in range(6))

        q, k, v, hz, s5u = _inproj(xall, sh1, sc1, w_in[l].astype(BF16), cos_t, sin_t)
        attn = _attention(sink[l], q, k, v, with_ctx=not last)

        hu, x0c = _hyena_pre(hz, hy_short_w[l], hy_short_b[l])
        filt_args = (hy_w1[l], hy_b1[l], hy_freq[l], hy_w2[l], hy_b2[l], hy_w3[l])
        k_r, k_i = _hyena_spectrum(_hyena_filter(L, *filt_args), g_cos, g_sin, f_cos, f_sin, spec_w)
        conv = _hyena_fft(hu, g_cos_bf, g_sin_bf, i_cos_bf, i_sin_bf, f_cos_bf, f_sin_bf, k_r, k_i)
        if not last:
            conv = jnp.concatenate([conv, _hyena_ctx(hu, _hyena_filter(C, *filt_args), d_cos, d_sin)], axis=0)

        ws, e_all, wc, lam_p = _s5_matrices(s5_a_re[l], s5_a_im[l], s5_log_dt[l], s5_b_re[l], s5_b_im[l],
                                            s5_c_re[l], s5_c_im[l], s5_d[l])
        s_rows, y_in = _s5_in(s5u, ws, e_all)
        s5y = _s5_out(_s5_scan(s_rows, lam_p), wc, y_in)

        x1, u2 = _merge(n_rows, xall, attn, conv, hu, x0c, s5y, g1, sh2, sc2, mix_g[l], hy_d[l],
                        s5_w_glu[l].astype(BF16), w_out[l].astype(BF16), ln1_g[l], ln1_b[l])

        top_e, gate = _router(n_rows, u2, w_router[l].T, router_bias[l])
        disp = _dispatch(top_e, gate)
        ys = _experts(_row_gather(u2, disp['tok']), disp, w_exp_gate, w_exp_up, w_exp_down, l)
        routed = _row_gather(ys, disp['comb']).reshape(TOP_K, n_rows, D)
        xall = _ffn_out(n_rows, x1, u2, routed, g2, w_sh_gate[l].astype(BF16), w_sh_up[l].astype(BF16),
                        w_sh_down[l].astype(BF16), ln2_g[l], ln2_b[l])
    return xall.reshape(B, L, D)
```

```python
import functools
import math

import jax
import jax.numpy as jnp
from jax import lax
from jax.experimental import pallas as pl
from jax.experimental.pallas import tpu as pltpu
from jax.experimental.pallas import tpu_sc as plsc

F32 = jnp.float32
BF16 = jnp.bfloat16
HIGHEST = lax.Precision.HIGHEST

D = 1024
B = 2
L = 8192
DEPTH = 2
GRID_W = 64
C = 256
T_LAT = B * L
T_CTX = B * C
T_ALL = T_LAT + T_CTX

HEAD_DIM = 64
N_Q = 8
N_KV = 2
Q_GROUP = N_Q // N_KV
ATTN_W = N_Q * HEAD_DIM
KV_W = N_KV * HEAD_DIM
HY_W = 256
S5_W = 256
MIX_W = ATTN_W + HY_W + S5_W
K_OFF = ATTN_W
V_OFF = K_OFF + KV_W
HY_OFF = V_OFF + KV_W
S5_OFF = HY_OFF + 3 * HY_W
IN_W = S5_OFF + S5_W
WINDOW = 128
BLK = 128
NEG_INF = -1e30
ROPE_BASE = 10000.0
AXIS_DIM = HEAD_DIM // 2

SHORT_K = 3
FILTER_EMB = 33
DECAY_FAST = 0.3
DECAY_SLOW = 1.5
DECAY_TARGET = 1e-2

S5_GROUP = 16
S5_GROUPS = S5_W // S5_GROUP
S5_STATE = 64
S5_NSTATE = S5_GROUPS * S5_STATE
S5_CH = 16
S5_ROWW = S5_CH * S5_W
N_CHUNK = T_ALL // S5_CH
LAT_CHUNKS = L // S5_CH
CTX_CHUNKS = C // S5_CH

N_EXPERTS = 256
TOP_K = 8
N_EGROUPS = 8
EGROUP = N_EXPERTS // N_EGROUPS
TOPK_GROUPS = 4
EXPERT_FF = 256
ROUTED_SCALE = 2.5
MOE_BLOCK = 256

ALPHA = (2 * DEPTH) ** 0.25
LN_EPS = 1e-5

N_FFT = 2 * L
FFT_R = 128
FFT_T1 = L // FFT_R
K1_USED = FFT_R // 2 + 1
K1P = 80
K1H = K1P // 2
FFT_UNROLL = 4

TM = 256
LANE = 128
VMEM_CAP = 60000 * 1024


def _cp(vmem_bytes, n_axes):
    return pltpu.CompilerParams(
        dimension_semantics=("arbitrary",) * n_axes,
        vmem_limit_bytes=min(int(vmem_bytes), VMEM_CAP),
    )


HALF_D = D // 2
HIGH_HALF_WORD = 0xFFFF0000


def _pack_pairs(x):
    bits = lax.bitcast_convert_type(x.astype(BF16).astype(F32), jnp.uint32)
    packed = (bits[:, 0:HALF_D] >> 16) | (bits[:, HALF_D:D] & jnp.uint32(HIGH_HALF_WORD))
    return lax.bitcast_convert_type(packed, jnp.int32)


def _unpack_pairs(p):
    bits = lax.bitcast_convert_type(p, jnp.uint32)
    low = lax.bitcast_convert_type(bits << 16, F32)
    high = lax.bitcast_convert_type(bits & jnp.uint32(HIGH_HALF_WORD), F32)
    return low, high


def _mod_sel(rows_per_tile):
    per_batch = L // rows_per_tile
    return lambda i: jnp.minimum(i // per_batch, 2)


def _vec_spec(rows_per_tile):
    sel = _mod_sel(rows_per_tile)
    return pl.BlockSpec((None, 1, D), lambda i: (sel(i), 0, 0))


ADA_TN = 1536


def _ada_kernel(c_ref, w_ref, b_ref, o_ref):
    c = c_ref[...]
    s = c * jax.nn.sigmoid(c)
    o_ref[...] = jnp.dot(s, w_ref[...], precision=HIGHEST, preferred_element_type=F32) + b_ref[...]


def _ada(cvec, w_ada, b_ada):
    return pl.pallas_call(
        _ada_kernel,
        grid=(DEPTH, 6 * D // ADA_TN),
        in_specs=[
            pl.BlockSpec((8, D), lambda l, j: (0, 0)),
            pl.BlockSpec((None, D, ADA_TN), lambda l, j: (l, 0, j)),
            pl.BlockSpec((None, 1, ADA_TN), lambda l, j: (l, 0, j)),
        ],
        out_specs=pl.BlockSpec((None, 8, ADA_TN), lambda l, j: (l, 0, j)),
        out_shape=jax.ShapeDtypeStruct((DEPTH, 8, 6 * D), F32),
        compiler_params=_cp(40 << 20, 2),
        name="ada",
    )(cvec, w_ada, b_ada.reshape(DEPTH, 1, 6 * D))


def _inproj_kernel(x_ref, sh_ref, sc_ref, w_ref, cos_ref, sin_ref, q_ref, k_ref, v_ref, hy_ref, s5_ref, *s5_tok_refs):
    u = x_ref[...] * (1.0 + sc_ref[...]) + sh_ref[...]
    proj = jnp.dot(u.astype(BF16), w_ref[...], preferred_element_type=F32)
    cos = cos_ref[...]
    sin = sin_ref[...]
    lane = lax.broadcasted_iota(jnp.int32, (TM, LANE), 1)
    first_half = (lane % AXIS_DIM) < (AXIS_DIM // 2)

    def rope(xc):
        partner = jnp.where(first_half, pltpu.roll(xc, LANE - AXIS_DIM // 2, 1), pltpu.roll(xc, AXIS_DIM // 2, 1))
        return xc * cos + partner * sin

    for j in range(ATTN_W // LANE):
        q_ref[:, j * LANE:(j + 1) * LANE] = rope(proj[:, j * LANE:(j + 1) * LANE]).astype(BF16)
    k_ref[...] = rope(proj[:, K_OFF:V_OFF]).astype(BF16)
    v_ref[...] = proj[:, V_OFF:HY_OFF].astype(BF16)
    hy_ref[...] = proj[:, HY_OFF:S5_OFF]
    for h, tok_ref in enumerate(s5_tok_refs):
        tok_ref[...] = proj[:, S5_OFF + h * LANE:S5_OFF + (h + 1) * LANE]
    for s in range(S5_CH):
        for h, tok_ref in enumerate(s5_tok_refs):
            s5_ref[:, s * S5_W + h * LANE:s * S5_W + (h + 1) * LANE] = (
                tok_ref[pl.ds(s, TM // S5_CH, stride=S5_CH), :].astype(BF16))


def _inproj(xall, sh, sc, w_in_bf, cos_t, sin_t):
    nt = T_ALL // TM
    row = lambda i: (i, 0)
    return pl.pallas_call(
        _inproj_kernel,
        grid=(nt,),
        in_specs=[
            pl.BlockSpec((TM, D), row),
            _vec_spec(TM),
            _vec_spec(TM),
            pl.BlockSpec((D, IN_W), lambda i: (0, 0)),
            pl.BlockSpec((TM, LANE), row),
            pl.BlockSpec((TM, LANE), row),
        ],
        out_specs=[
            pl.BlockSpec((TM, ATTN_W), row),
            pl.BlockSpec((TM, KV_W), row),
            pl.BlockSpec((TM, KV_W), row),
            pl.BlockSpec((TM, 3 * HY_W), row),
            pl.BlockSpec((TM // S5_CH, S5_ROWW), row),
        ],
        out_shape=[
            jax.ShapeDtypeStruct((T_ALL, ATTN_W), BF16),
            jax.ShapeDtypeStruct((T_ALL, KV_W), BF16),
            jax.ShapeDtypeStruct((T_ALL, KV_W), BF16),
            jax.ShapeDtypeStruct((T_ALL, 3 * HY_W), F32),
            jax.ShapeDtypeStruct((N_CHUNK, S5_ROWW), BF16),
        ],
        scratch_shapes=[pltpu.VMEM((TM, LANE), F32)] * (S5_W // LANE),
        compiler_params=_cp(40 << 20, 1),
        name="inproj",
    )(xall, sh, sc, w_in_bf, cos_t, sin_t)


NB_LAT = L // BLK
NB_CTX = C // BLK


def _nt_dot(a, b):
    return lax.dot_general(a, b, (((1,), (1,)), ((), ())), preferred_element_type=F32)


def _attn_kernel(sink_ref, q_ref, kp_ref, kc_ref, kn_ref, kx_ref, vp_ref, vc_ref, vn_ref, vx_ref, o_ref):
    n = pl.program_id(1)
    is_lat = n < NB_LAT
    rows = Q_GROUP * BLK
    r = lax.broadcasted_iota(jnp.int32, (rows, BLK), 0) % BLK
    j = lax.broadcasted_iota(jnp.int32, (rows, BLK), 1)
    ok_prev = (j >= r) & (n >= 1) & is_lat
    ok_next = (j <= r) & (n + 1 < NB_LAT) & is_lat
    head_of_row = lax.broadcasted_iota(jnp.int32, (rows, 1), 0) // BLK
    q = q_ref[...] * (HEAD_DIM ** -0.5)
    dot = lambda a, b: jnp.dot(a.astype(BF16), b, preferred_element_type=F32)
    for kh in range(N_KV):
        hs = slice(kh * HEAD_DIM, (kh + 1) * HEAD_DIM)
        heads = range(kh * Q_GROUP, (kh + 1) * Q_GROUP)
        qg = jnp.concatenate([q[:, h * HEAD_DIM:(h + 1) * HEAD_DIM] for h in heads], axis=0)
        sk = jnp.zeros((rows, 1), F32)
        for g, h in enumerate(heads):
            sk = jnp.where(head_of_row == g, sink_ref[h], sk)
        s_p = jnp.where(ok_prev, _nt_dot(qg, kp_ref[:, hs]), NEG_INF)
        s_c = jnp.where(is_lat, _nt_dot(qg, kc_ref[:, hs]), NEG_INF)
        s_n = jnp.where(ok_next, _nt_dot(qg, kn_ref[:, hs]), NEG_INF)
        s_x = _nt_dot(qg, kx_ref[:, hs])
        s_x0 = s_x[:, 0:BLK]
        s_x1 = s_x[:, BLK:2 * BLK]
        m = jnp.maximum(jnp.maximum(jnp.maximum(s_p, s_c), jnp.maximum(s_n, s_x0)), s_x1)
        m = jnp.maximum(jnp.max(m, axis=1, keepdims=True), sk)
        e_p = jnp.exp(s_p - m)
        e_c = jnp.exp(s_c - m)
        e_n = jnp.exp(s_n - m)
        e_x0 = jnp.exp(s_x0 - m)
        e_x1 = jnp.exp(s_x1 - m)
        den = jnp.sum((e_p + e_c) + (e_n + e_x0) + e_x1, axis=1, keepdims=True) + jnp.exp(sk - m)
        o = (dot(e_p, vp_ref[:, hs]) + dot(e_c, vc_ref[:, hs]) + dot(e_n, vn_ref[:, hs])
             + dot(e_x0, vx_ref[0:BLK, hs]) + dot(e_x1, vx_ref[BLK:2 * BLK, hs]))
        o = o / den
        for g, h in enumerate(heads):
            o_ref[:, h * HEAD_DIM:(h + 1) * HEAD_DIM] = o[g * BLK:(g + 1) * BLK]


def _attention(sink, q, k, v, with_ctx):
    nblk = NB_LAT + (NB_CTX if with_ctx else 0)

    def q_idx(b, n):
        return (jnp.where(n < NB_LAT, b * NB_LAT + n, B * NB_LAT + b * NB_CTX + (n - NB_LAT)), 0)

    def kv_idx(off):
        def idx(b, n):
            nn = jnp.clip(jnp.minimum(n, NB_LAT - 1) + off, 0, NB_LAT - 1)
            return (b * NB_LAT + nn, 0)
        return idx

    ctx_idx = lambda b, n: (T_LAT // C + b, 0)
    kv_specs = lambda: [pl.BlockSpec((BLK, KV_W), kv_idx(-1)), pl.BlockSpec((BLK, KV_W), kv_idx(0)),
                        pl.BlockSpec((BLK, KV_W), kv_idx(1)), pl.BlockSpec((C, KV_W), ctx_idx)]
    return pl.pallas_call(
        _attn_kernel,
        grid=(B, nblk),
        in_specs=[pl.BlockSpec(memory_space=pltpu.SMEM), pl.BlockSpec((BLK, ATTN_W), q_idx)] + kv_specs() + kv_specs(),
        out_specs=pl.BlockSpec((BLK, ATTN_W), q_idx),
        out_shape=jax.ShapeDtypeStruct((T_ALL if with_ctx else T_LAT, ATTN_W), F32),
        compiler_params=_cp(32 << 20, 2),
        name="attention",
    )(sink, q, k, k, k, k, v, v, v, v)


def _hyena_pre_kernel(z_ref, zp_ref, zn_ref, w_ref, b_ref, u_ref, x0_ref):
    i = pl.program_id(0)
    tiles_per_seq = L // TM
    is_ctx = i >= B * tiles_per_seq
    first = is_ctx | (i % tiles_per_seq == 0)
    last = is_ctx | (i % tiles_per_seq == tiles_per_seq - 1)
    z = z_ref[...]
    prev_row = jnp.where(first, 0.0, zp_ref[7:8, :])
    next_row = jnp.where(last, 0.0, zn_ref[0:1, :])
    row = lax.broadcasted_iota(jnp.int32, z.shape, 0)
    z_m1 = jnp.where(row == 0, prev_row, pltpu.roll(z, 1, 0))
    z_p1 = jnp.where(row == TM - 1, next_row, pltpu.roll(z, TM - 1, 0))
    zc = b_ref[...] + z_m1 * w_ref[0:1, :] + z * w_ref[1:2, :] + z_p1 * w_ref[2:3, :]
    u_ref[...] = zc[:, 0:HY_W] * zc[:, HY_W:2 * HY_W]
    x0_ref[...] = zc[:, 2 * HY_W:3 * HY_W]


def _hyena_pre(z, short_w, short_b):
    nt = T_ALL // TM
    sub = TM // 8
    n8 = T_ALL // 8
    return pl.pallas_call(
        _hyena_pre_kernel,
        grid=(nt,),
        in_specs=[
            pl.BlockSpec((TM, 3 * HY_W), lambda i: (i, 0)),
            pl.BlockSpec((8, 3 * HY_W), lambda i: (jnp.maximum(i * sub - 1, 0), 0)),
            pl.BlockSpec((8, 3 * HY_W), lambda i: (jnp.minimum((i + 1) * sub, n8 - 1), 0)),
            pl.BlockSpec((SHORT_K, 3 * HY_W), lambda i: (0, 0)),
            pl.BlockSpec((1, 3 * HY_W), lambda i: (0, 0)),
        ],
        out_specs=[pl.BlockSpec((TM, HY_W), lambda i: (i, 0)), pl.BlockSpec((TM, HY_W), lambda i: (i, 0))],
        out_shape=[jax.ShapeDtypeStruct((T_ALL, HY_W), F32), jax.ShapeDtypeStruct((T_ALL, HY_W), F32)],
        compiler_params=_cp(32 << 20, 1),
        name="hyena_pre",
    )(z, z, z, short_w, short_b.reshape(1, 3 * HY_W))


def _dft_tables():
    t0 = jnp.arange(FFT_R, dtype=jnp.int32)[:, None, None]
    k1 = jnp.arange(K1P, dtype=jnp.int32)[None, :, None]
    t1 = jnp.arange(FFT_T1, dtype=jnp.int32)[None, None, :]
    m = (k1 * (FFT_R * t1 + t0)) % N_FFT
    ang = m.astype(F32) * (2.0 * math.pi / N_FFT)
    used = (k1 < K1_USED).astype(F32)
    g_cos = jnp.cos(ang) * used
    g_sin = jnp.sin(ang) * used
    a = jnp.arange(FFT_R, dtype=jnp.int32)
    ang2 = ((a[:, None] * a[None, :]) % FFT_R).astype(F32) * (2.0 * math.pi / FFT_R)
    return g_cos, g_sin, jnp.cos(ang2), jnp.sin(ang2)


def _hyena_spec_kernel(k_ref, gc_ref, gs_ref, fc_ref, fs_ref, wt_ref, kr_ref, ki_ref, ar_ref, ai_ref):
    half = pl.program_id(1)
    kk = lax.broadcasted_iota(jnp.int32, (K1H, 1), 0) + half * K1H
    sign = jnp.where(kk % 2 == 0, 1.0, -1.0).astype(F32)

    dot = lambda a, b: jnp.dot(a, b.astype(BF16), preferred_element_type=F32)

    def stage1(i, carry):
        t0s = [i * FFT_UNROLL + u for u in range(FFT_UNROLL)]
        loaded = [(k_ref[pl.ds(t0, FFT_T1, stride=FFT_R), :], k_ref[pl.ds(L + t0, FFT_T1, stride=FFT_R), :],
                   gc_ref[t0].astype(BF16), gs_ref[t0].astype(BF16)) for t0 in t0s]
        res = [(dot(gc, x_lo) + sign * dot(gc, x_hi), -(dot(gs, x_lo) + sign * dot(gs, x_hi)))
               for x_lo, x_hi, gc, gs in loaded]
        for t0, (a_r, a_i) in zip(t0s, res):
            rows = pl.ds(pl.multiple_of(t0 * K1H, 8), K1H)
            ar_ref[rows, :] = a_r
            ai_ref[rows, :] = a_i
        return carry

    lax.fori_loop(0, FFT_R // FFT_UNROLL, stage1, 0)
    fc = fc_ref[...].astype(BF16)
    fs = fs_ref[...].astype(BF16)

    def stage2(i, carry):
        kls = [i * FFT_UNROLL + u for u in range(FFT_UNROLL)]
        loaded = [(ar_ref[pl.ds(kl, FFT_R, stride=K1H), :], ai_ref[pl.ds(kl, FFT_R, stride=K1H), :]) for kl in kls]
        for kl, (a_r, a_i) in zip(kls, loaded):
            w = wt_ref[half * K1H + kl]
            kr_ref[kl] = (dot(fc, a_r) + dot(fs, a_i)) * w
            ki_ref[kl] = (dot(fc, a_i) - dot(fs, a_r)) * w
        return carry

    lax.fori_loop(0, K1H // FFT_UNROLL, stage2, 0)


def _hyena_spectrum(kfilt, g_cos, g_sin, f_cos, f_sin, wts):
    nct = HY_W // LANE
    gspec = pl.BlockSpec((FFT_R, K1H, FFT_T1), lambda c, h: (0, h, 0))
    fspec = pl.BlockSpec((FFT_R, FFT_R), lambda c, h: (0, 0))
    ospec = pl.BlockSpec((K1H, FFT_R, LANE), lambda c, h: (h, 0, c))
    return pl.pallas_call(
        _hyena_spec_kernel,
        grid=(nct, 2),
        in_specs=[pl.BlockSpec((N_FFT, LANE), lambda c, h: (0, c)), gspec, gspec, fspec, fspec,
                  pl.BlockSpec(memory_space=pltpu.SMEM)],
        out_specs=[ospec, ospec],
        out_shape=[jax.ShapeDtypeStruct((K1P, FFT_R, HY_W), F32)] * 2,
        scratch_shapes=[pltpu.VMEM((FFT_R * K1H, LANE), F32)] * 2,
        compiler_params=_cp(56 << 20, 2),
        name="hyena_spectrum",
    )(kfilt, g_cos, g_sin, f_cos, f_sin, wts)


def _hyena_fft_kernel(u_ref, gc_ref, gs_ref, ic_ref, is_ref, fc_ref, fs_ref, kr_ref, ki_ref, o_ref, ar_ref, ai_ref):
    bdot = lambda a, b: jnp.dot(a, b.astype(BF16), preferred_element_type=F32)

    def stage1(i, carry):
        t0s = [i * FFT_UNROLL + u for u in range(FFT_UNROLL)]
        xs = [u_ref[pl.ds(t0, FFT_T1, stride=FFT_R), :] for t0 in t0s]
        res = [(bdot(gc_ref[t0], x), -bdot(gs_ref[t0], x)) for t0, x in zip(t0s, xs)]
        for t0, (a_r, a_i) in zip(t0s, res):
            rows = pl.ds(pl.multiple_of(t0 * K1P, 8), K1P)
            ar_ref[rows, :] = a_r
            ai_ref[rows, :] = a_i
        return carry

    lax.fori_loop(0, FFT_R // FFT_UNROLL, stage1, 0)
    fc = fc_ref[...]
    fs = fs_ref[...]

    def stage23(i, carry):
        k1s = [i * FFT_UNROLL + u for u in range(FFT_UNROLL)]
        loaded = [(ar_ref[pl.ds(k1, FFT_R, stride=K1P), :], ai_ref[pl.ds(k1, FFT_R, stride=K1P), :],
                   kr_ref[k1], ki_ref[k1]) for k1 in k1s]
        res = []
        for a_r, a_i, k_r, k_i in loaded:
            z_r = bdot(fc, a_r) + bdot(fs, a_i)
            z_i = bdot(fc, a_i) - bdot(fs, a_r)
            y_r = z_r * k_r - z_i * k_i
            y_i = z_r * k_i + z_i * k_r
            res.append((bdot(fc, y_r) - bdot(fs, y_i), bdot(fc, y_i) + bdot(fs, y_r)))
        for k1, (b_r, b_i) in zip(k1s, res):
            ar_ref[pl.ds(k1, FFT_R, stride=K1P), :] = b_r
            ai_ref[pl.ds(k1, FFT_R, stride=K1P), :] = b_i
        return carry

    lax.fori_loop(0, -(-K1_USED // FFT_UNROLL), stage23, 0)

    def stage4(i, carry):
        t0s = [i * FFT_UNROLL + u for u in range(FFT_UNROLL)]
        loaded = []
        for t0 in t0s:
            rows = pl.ds(pl.multiple_of(t0 * K1P, 8), K1P)
            loaded.append((ar_ref[rows, :], ai_ref[rows, :]))
        res = [bdot(ic_ref[t0], b_r) - bdot(is_ref[t0], b_i) for t0, (b_r, b_i) in zip(t0s, loaded)]
        for t0, y in zip(t0s, res):
            o_ref[pl.ds(t0, FFT_T1, stride=FFT_R), :] = y
        return carry

    lax.fori_loop(0, FFT_R // FFT_UNROLL, stage4, 0)


def _hyena_fft(u, g_cos_bf, g_sin_bf, i_cos_bf, i_sin_bf, f_cos_bf, f_sin_bf, k_r, k_i):
    nct = HY_W // LANE
    one = pl.Buffered(1)
    gspec = pl.BlockSpec((FFT_R, K1P, FFT_T1), lambda c, b: (0, 0, 0), pipeline_mode=one)
    ispec = pl.BlockSpec((FFT_R, FFT_T1, K1P), lambda c, b: (0, 0, 0), pipeline_mode=one)
    fspec = pl.BlockSpec((FFT_R, FFT_R), lambda c, b: (0, 0), pipeline_mode=one)
    kspec = pl.BlockSpec((K1P, FFT_R, LANE), lambda c, b: (0, 0, c), pipeline_mode=one)
    return pl.pallas_call(
        _hyena_fft_kernel,
        grid=(nct, B),
        in_specs=[pl.BlockSpec((L, LANE), lambda c, b: (b, c)), gspec, gspec, ispec, ispec, fspec, fspec, kspec, kspec],
        out_specs=pl.BlockSpec((L, LANE), lambda c, b: (b, c)),
        out_shape=jax.ShapeDtypeStruct((T_LAT, HY_W), F32),
        scratch_shapes=[pltpu.VMEM((FFT_R * K1P, LANE), F32)] * 2,
        compiler_params=_cp(56 << 20, 2),
        name="hyena_fft",
    )(u, g_cos_bf, g_sin_bf, i_cos_bf, i_sin_bf, f_cos_bf, f_sin_bf, k_r, k_i)


def _hyena_ctx_kernel(u_ref, k_ref, dc_ref, ds_ref, o_ref):
    dot = lambda a, b: jnp.dot(a, b, precision=HIGHEST, preferred_element_type=F32)
    dc = dc_ref[...]
    ds = ds_ref[...]
    u = u_ref[...]
    kf = k_ref[...]
    u_r = dot(dc[:, 0:C], u)
    u_i = -dot(ds[:, 0:C], u)
    k_r = dot(dc, kf)
    k_i = -dot(ds, kf)
    y_r = u_r * k_r - u_i * k_i
    y_i = u_r * k_i + u_i * k_r
    o_ref[...] = (dot(dc[0:C, :], y_r) - dot(ds[0:C, :], y_i)) * (1.0 / (2 * C))


def _hyena_ctx(u, kfilt_ctx, d_cos, d_sin):
    full = lambda b: (0, 0)
    return pl.pallas_call(
        _hyena_ctx_kernel,
        grid=(B,),
        in_specs=[pl.BlockSpec((C, HY_W), lambda b: (T_LAT // C + b, 0)),
                  pl.BlockSpec((2 * C, HY_W), full), pl.BlockSpec((2 * C, 2 * C), full), pl.BlockSpec((2 * C, 2 * C), full)],
        out_specs=pl.BlockSpec((C, HY_W), lambda b: (b, 0)),
        out_shape=jax.ShapeDtypeStruct((T_CTX, HY_W), F32),
        compiler_params=_cp(32 << 20, 1),
        name="hyena_ctx",
    )(u, kfilt_ctx, d_cos, d_sin)


def _hyena_filter(n, w1, b1, freq, w2, b2, w3):
    t = jnp.linspace(0.0, 1.0, n, dtype=F32)[:, None]
    bands = (FILTER_EMB - 1) // 2
    w = 2.0 * math.pi * jnp.arange(n, dtype=F32)[:, None] / n
    f = jnp.linspace(1e-4, bands - 1, bands, dtype=F32)[None, :]
    z = jnp.concatenate([t, jnp.cos(f * w), -jnp.sin(f * w)], axis=-1)
    mm = functools.partial(jnp.matmul, precision=HIGHEST)
    h = jnp.sin(freq * (mm(z, w1) + b1))
    h = jnp.sin(freq * (mm(h, w2) + b2))
    h = mm(h, w3)
    deltas = jnp.abs(jnp.linspace(math.log(DECAY_TARGET) / DECAY_FAST, math.log(DECAY_TARGET) / DECAY_SLOW,
                                  HY_W, dtype=F32))
    decay = jnp.exp(-t * deltas[None, :])
    h_fwd = h[:, :HY_W] * decay
    h_bwd = h[:, HY_W:] * decay
    k = jnp.concatenate([h_fwd, jnp.zeros((1, HY_W), F32), h_bwd[:0:-1]], axis=0)
    return k / jnp.sum(jnp.abs(k), axis=0, keepdims=True)


def _s5_matrices(a_re, a_im, log_dt, b_re, b_im, c_re, c_im, d_skip):
    dt = jnp.exp(log_dt)[:, :, None]
    lam_re = jnp.minimum(a_re, -1e-4)
    mag1 = jnp.exp(lam_re * dt)
    lbr = mag1 * jnp.cos(a_im * dt)
    lbi = mag1 * jnp.sin(a_im * dt)
    den = lam_re * lam_re + a_im * a_im
    fr = ((lbr - 1.0) * lam_re + lbi * a_im) / den
    fi = (lbi * lam_re - (lbr - 1.0) * a_im) / den
    bbr = fr[..., None] * b_re - fi[..., None] * b_im
    bbi = fr[..., None] * b_im + fi[..., None] * b_re
    j = jnp.arange(S5_CH + 1, dtype=F32)[:, None, None, None]
    magj = jnp.exp(j * (lam_re * dt)[None])
    pr = magj * jnp.cos(j * (a_im * dt)[None])
    pi = magj * jnp.sin(j * (a_im * dt)[None])
    hi = functools.partial(jnp.einsum, precision=HIGHEST)
    lbr_j = pr[..., None] * bbr[None] - pi[..., None] * bbi[None]
    lbi_j = pr[..., None] * bbi[None] + pi[..., None] * bbr[None]
    m = hi('dgop,jdgpi->jdgoi', c_re, lbr_j) - hi('dgop,jdgpi->jdgoi', c_im, lbi_j)
    eye_g = jnp.eye(S5_GROUPS, dtype=F32)
    s = jnp.arange(S5_CH)
    blocks = jnp.einsum('jdgoi,gh->djgiho', m[0:S5_CH], eye_g).reshape(2, S5_CH, S5_W, S5_W)
    lag0 = blocks[0, 0] + blocks[1, 0] + jnp.diag(d_skip)
    e_all = jnp.concatenate([blocks[1, S5_CH - 1:0:-1], lag0[None], blocks[0, 1:S5_CH]], axis=0).astype(BF16)
    sf_r = lbr_j[S5_CH - 1 - s, 0]
    sf_i = lbi_j[S5_CH - 1 - s, 0]
    sb_r = lbr_j[s, 1]
    sb_i = lbi_j[s, 1]
    st = jnp.stack([sf_r, sf_i, sb_r, sb_i], axis=0)
    rr = jnp.arange(S5_ROWW, dtype=jnp.int32)
    cc = jnp.arange(S5_W, dtype=jnp.int32)
    ws = _s5_expand(jnp.transpose(st, (1, 2, 4, 0, 3)).reshape(S5_ROWW, S5_W),
                    (cc[:, None] // S5_STATE == rr[None, :] // S5_NSTATE)
                    & (cc[:, None] % S5_STATE == rr[None, :] % S5_STATE),
                    (rr // S5_GROUP) % S5_GROUPS, (rr % S5_NSTATE) // S5_STATE)
    tt = jnp.arange(S5_CH)
    cf_r = c_re[0][None] * pr[tt + 1, 0][:, :, None, :] - c_im[0][None] * pi[tt + 1, 0][:, :, None, :]
    cf_i = c_re[0][None] * pi[tt + 1, 0][:, :, None, :] + c_im[0][None] * pr[tt + 1, 0][:, :, None, :]
    cb_r = c_re[1][None] * pr[S5_CH - tt, 1][:, :, None, :] - c_im[1][None] * pi[S5_CH - tt, 1][:, :, None, :]
    cb_i = c_re[1][None] * pi[S5_CH - tt, 1][:, :, None, :] + c_im[1][None] * pr[S5_CH - tt, 1][:, :, None, :]
    ct = jnp.stack([cf_r, -cf_i, cb_r, -cb_i], axis=0)
    wc = _s5_expand(jnp.transpose(ct, (0, 2, 4, 1, 3)).reshape(4 * S5_NSTATE, S5_W),
                    (cc[:, None] // S5_GROUP == rr[None, :] // S5_W)
                    & (cc[:, None] % S5_GROUP == rr[None, :] % S5_GROUP),
                    (rr % S5_NSTATE) // S5_STATE, (rr % S5_W) // S5_GROUP)
    lam_p = jnp.stack([jnp.stack([pr[S5_CH, 0], pi[S5_CH, 0]]), jnp.stack([pr[S5_CH, 1], pi[S5_CH, 1]])])
    return ws, e_all, wc, lam_p.reshape(2, 2, 1, S5_NSTATE)


S5_TN = 512
S5_NS = S5_ROWW // S5_W


def _s5_expand_kernel(a_ref, ex_ref, rg_ref, cg_ref, o_ref):
    v = jnp.dot(a_ref[...], ex_ref[...], preferred_element_type=F32)
    o_ref[...] = jnp.where(rg_ref[...] == cg_ref[...], v, 0.0).astype(BF16)


def _s5_expand(compact, placement, row_group, col_group):
    n = compact.shape[0]
    return pl.pallas_call(
        _s5_expand_kernel,
        grid=(S5_ROWW // S5_TN,),
        in_specs=[pl.BlockSpec((n, S5_W), lambda j: (0, 0)), pl.BlockSpec((S5_W, S5_TN), lambda j: (0, j)),
                  pl.BlockSpec((n, 1), lambda j: (0, 0)), pl.BlockSpec((1, S5_TN), lambda j: (0, j))],
        out_specs=pl.BlockSpec((n, S5_TN), lambda j: (0, j)),
        out_shape=jax.ShapeDtypeStruct((n, S5_ROWW), BF16),
        compiler_params=_cp(32 << 20, 1),
        name="s5_expand",
    )(compact.astype(BF16), placement.astype(BF16), row_group.reshape(n, 1), col_group.reshape(1, S5_ROWW))


def _s5_in_kernel(u_ref, ws_ref, e_ref, s_ref, o_ref):
    j = pl.program_id(0)

    @pl.when(j < S5_NS)
    def _():
        s_ref[...] = jnp.dot(u_ref[...], ws_ref[...], preferred_element_type=F32)

    @pl.when(j >= S5_NS)
    def _():
        t = j - S5_NS
        acc = jnp.dot(u_ref[:, 0:S5_W], e_ref[S5_CH - 1 + t], preferred_element_type=F32)
        for s in range(1, S5_CH):
            acc = acc + jnp.dot(u_ref[:, s * S5_W:(s + 1) * S5_W], e_ref[S5_CH - 1 + t - s],
                                preferred_element_type=F32)
        o_ref[...] = acc


def _s5_in(u_rows, ws, e_all):
    return pl.pallas_call(
        _s5_in_kernel,
        grid=(2 * S5_NS,),
        in_specs=[pl.BlockSpec((N_CHUNK, S5_ROWW), lambda j: (0, 0)),
                  pl.BlockSpec((S5_ROWW, S5_W), lambda j: (0, jnp.minimum(j, S5_NS - 1))),
                  pl.BlockSpec((2 * S5_CH - 1, S5_W, S5_W), lambda j: (0, 0, 0))],
        out_specs=[pl.BlockSpec((N_CHUNK, S5_W), lambda j: (0, jnp.minimum(j, S5_NS - 1))),
                   pl.BlockSpec((N_CHUNK, S5_W), lambda j: (0, jnp.maximum(j - S5_NS, 0)))],
        out_shape=[jax.ShapeDtypeStruct((N_CHUNK, 4 * S5_NSTATE), F32), jax.ShapeDtypeStruct((N_CHUNK, S5_ROWW), F32)],
        compiler_params=_cp(48 << 20, 1),
        name="s5_in",
    )(u_rows, ws, e_all)


def _s5_scan_kernel(s_ref, lam_ref, h_ref):
    lam = [[lam_ref[d, p] for p in range(2)] for d in range(2)]

    def step(b, d, chunk, h):
        row = pl.ds(chunk, 1)
        cols_r = pl.ds(d * 2 * S5_NSTATE, S5_NSTATE)
        cols_i = pl.ds(d * 2 * S5_NSTATE + S5_NSTATE, S5_NSTATE)
        h_ref[row, cols_r] = h[0]
        h_ref[row, cols_i] = h[1]
        s_r = s_ref[row, cols_r]
        s_i = s_ref[row, cols_i]
        lr, li = lam[d]
        return (lr * h[0] - li * h[1] + s_r, lr * h[1] + li * h[0] + s_i)

    def chain_order(b, d, n_ctx_done):
        ctx0 = B * LAT_CHUNKS + b * CTX_CHUNKS
        lat0 = b * LAT_CHUNKS
        if d == 0:
            return (lambda i: ctx0 + i), (lambda i: lat0 + i)
        return (lambda i: ctx0 + CTX_CHUNKS - 1 - i), (lambda i: lat0 + LAT_CHUNKS - 1 - i)

    chains = [(b, d) for b in range(B) for d in range(2)]
    zero = jnp.zeros((1, S5_NSTATE), F32)
    init = tuple((zero, zero) for _ in chains)

    def phase(n_steps, which, carry):
        def body(i, hs):
            out = []
            for (b, d), h in zip(chains, hs):
                order = chain_order(b, d, 0)[which]
                out.append(step(b, d, order(i), h))
            return tuple(out)
        return lax.fori_loop(0, n_steps, body, carry)

    carry = phase(CTX_CHUNKS, 0, init)
    phase(LAT_CHUNKS, 1, carry)


def _s5_scan(s_rows, lam_p):
    return pl.pallas_call(
        _s5_scan_kernel,
        out_shape=jax.ShapeDtypeStruct((N_CHUNK, 4 * S5_NSTATE), F32),
        compiler_params=pltpu.CompilerParams(vmem_limit_bytes=48 << 20),
        name="s5_scan",
    )(s_rows, lam_p)


S5_TM = N_CHUNK // 2
S5_TPN = S5_TN // S5_W


def _s5_out_kernel(h_ref, w_ref, y_ref, *rest):
    o_refs, hb_ref = rest[:-1], rest[-1]
    j = pl.program_id(1)

    @pl.when(j == 0)
    def _():
        hb_ref[...] = h_ref[...].astype(BF16)

    acc = jnp.dot(hb_ref[...], w_ref[...], preferred_element_type=F32) + y_ref[...]
    for tt in range(S5_TPN):
        for h, o_ref in enumerate(o_refs):
            o_ref[pl.ds(j * S5_TPN + tt, S5_TM, stride=S5_CH), :] = (
                acc[:, tt * S5_W + h * LANE:tt * S5_W + (h + 1) * LANE])


def _s5_out(h_rows, wc, y_in):
    nn = S5_ROWW // S5_TN
    n_out = S5_W // LANE
    return pl.pallas_call(
        _s5_out_kernel,
        grid=(N_CHUNK // S5_TM, nn),
        in_specs=[pl.BlockSpec((S5_TM, 4 * S5_NSTATE), lambda i, j: (i, 0), pipeline_mode=pl.Buffered(1)),
                  pl.BlockSpec((4 * S5_NSTATE, S5_TN), lambda i, j: (0, j)),
                  pl.BlockSpec((S5_TM, S5_TN), lambda i, j: (i, j))],
        out_specs=[pl.BlockSpec((S5_TM * S5_CH, LANE), lambda i, j: (i, 0))] * n_out,
        out_shape=[jax.ShapeDtypeStruct((T_ALL, LANE), F32)] * n_out,
        scratch_shapes=[pltpu.VMEM((S5_TM, 4 * S5_NSTATE), BF16)],
        compiler_params=_cp(52 << 20, 2),
        name="s5_out",
    )(h_rows, wc, y_in)


def _rms(x):
    return x * lax.rsqrt(jnp.mean(x * x, axis=-1, keepdims=True) + LN_EPS)


def _layer_norm(x, g, b):
    mu = jnp.mean(x, axis=-1, keepdims=True)
    xc = x - mu
    var = jnp.mean(xc * xc, axis=-1, keepdims=True)
    return xc * lax.rsqrt(var + LN_EPS) * g + b


def _merge_kernel(x_ref, attn_ref, conv_ref, hu_ref, x0_ref, s5a_ref, s5b_ref, g1_ref, sh2_ref, sc2_ref, mixg_ref,
                  hyd_ref, wglu_ref, wout_ref, lng_ref, lnb_ref, o_ref, u2_ref, u2p_ref):
    hy = (conv_ref[...] + hu_ref[...] * hyd_ref[...]) * x0_ref[...]
    g = jax.nn.gelu(jnp.concatenate([s5a_ref[...], s5b_ref[...]], axis=-1))
    s5 = g * jax.nn.sigmoid(jnp.dot(g.astype(BF16), wglu_ref[...], preferred_element_type=F32))
    mixg = mixg_ref[...]
    parts = [_rms(attn_ref[...]) * mixg[:, 0:ATTN_W],
             _rms(hy) * mixg[:, ATTN_W:ATTN_W + HY_W],
             _rms(s5) * mixg[:, ATTN_W + HY_W:MIX_W]]
    mix = jnp.concatenate(parts, axis=-1).astype(BF16)
    o = jnp.dot(mix, wout_ref[...], preferred_element_type=F32)
    x1 = _layer_norm(ALPHA * x_ref[...] + g1_ref[...] * o, lng_ref[...], lnb_ref[...])
    o_ref[...] = x1
    u2 = x1 * (1.0 + sc2_ref[...]) + sh2_ref[...]
    u2_ref[...] = u2
    u2p_ref[...] = _pack_pairs(u2)


def _merge(n_rows, xall, attn, conv, hu, x0c, s5y, g1, sh2, sc2, mix_g, hy_d, wglu_bf, wout_bf, ln_g, ln_b):
    nt = n_rows // TM
    row = lambda i: (i, 0)
    full = lambda i: (0, 0)
    return pl.pallas_call(
        _merge_kernel,
        grid=(nt,),
        in_specs=[pl.BlockSpec((TM, D), row), pl.BlockSpec((TM, ATTN_W), row), pl.BlockSpec((TM, HY_W), row),
                  pl.BlockSpec((TM, HY_W), row), pl.BlockSpec((TM, HY_W), row),
                  pl.BlockSpec((TM, LANE), row), pl.BlockSpec((TM, LANE), row),
                  _vec_spec(TM), _vec_spec(TM), _vec_spec(TM), pl.BlockSpec((1, MIX_W), full),
                  pl.BlockSpec((1, HY_W), full), pl.BlockSpec((S5_W, S5_W), full), pl.BlockSpec((MIX_W, D), full),
                  pl.BlockSpec((1, D), full), pl.BlockSpec((1, D), full)],
        out_specs=[pl.BlockSpec((TM, D), row), pl.BlockSpec((TM, D), row), pl.BlockSpec((TM, HALF_D), row)],
        out_shape=[jax.ShapeDtypeStruct((n_rows, D), F32), jax.ShapeDtypeStruct((n_rows, D), F32),
                   jax.ShapeDtypeStruct((n_rows, HALF_D), jnp.int32)],
        compiler_params=_cp(48 << 20, 1),
        name="merge",
    )(xall, attn, conv, hu, x0c, s5y[0], s5y[1], g1, sh2, sc2, mix_g.reshape(1, MIX_W), hy_d.reshape(1, HY_W), wglu_bf,
      wout_bf, ln_g.reshape(1, D), ln_b.reshape(1, D))


def _router_kernel(u_ref, wt_ref, b_ref, e_ref, g_ref):
    logits = lax.dot_general(wt_ref[...], u_ref[...], (((1,), (1,)), ((), ())), precision=HIGHEST,
                             preferred_element_type=F32)
    scores = jax.nn.sigmoid(logits)
    biased = scores + b_ref[...]
    ninf = -jnp.inf
    grow = lax.broadcasted_iota(jnp.int32, (EGROUP, TM), 0)
    groups = [biased[gi * EGROUP:(gi + 1) * EGROUP] for gi in range(N_EGROUPS)]
    gscore = []
    for vals in groups:
        m1 = jnp.max(vals, axis=0, keepdims=True)
        i1 = jnp.min(jnp.where(vals == m1, grow, EGROUP), axis=0, keepdims=True)
        m2 = jnp.max(jnp.where(grow == i1, ninf, vals), axis=0, keepdims=True)
        gscore.append(m1 + m2)
    kept = []
    for gi in range(N_EGROUPS):
        rank = jnp.zeros((1, TM), jnp.int32)
        for gj in range(N_EGROUPS):
            if gj == gi:
                continue
            ahead = (gscore[gj] > gscore[gi]) | ((gscore[gj] == gscore[gi]) & (gj < gi))
            rank = rank + ahead.astype(jnp.int32)
        kept.append(jnp.where(rank < TOPK_GROUPS, groups[gi], ninf))
    masked = jnp.concatenate(kept, axis=0)
    row = lax.broadcasted_iota(jnp.int32, (N_EXPERTS, TM), 0)
    gates = []
    gsum = jnp.zeros((1, TM), F32)
    for kk in range(TOP_K):
        m = jnp.max(masked, axis=0, keepdims=True)
        idx = jnp.min(jnp.where(masked == m, row, N_EXPERTS), axis=0, keepdims=True)
        hit = row == idx
        gate = jnp.sum(jnp.where(hit, scores, 0.0), axis=0, keepdims=True)
        masked = jnp.where(hit, ninf, masked)
        e_ref[kk:kk + 1, :] = idx
        gates.append(gate)
        gsum = gsum + gate
    for kk in range(TOP_K):
        g_ref[kk:kk + 1, :] = gates[kk] / gsum * ROUTED_SCALE


def _router(n_rows, u2, w_router_t, router_bias):
    nt = n_rows // TM
    col = lambda i: (0, i)
    return pl.pallas_call(
        _router_kernel,
        grid=(nt,),
        in_specs=[pl.BlockSpec((TM, D), lambda i: (i, 0)),
                  pl.BlockSpec((N_EXPERTS, D), lambda i: (0, 0)), pl.BlockSpec((N_EXPERTS, 1), lambda i: (0, 0))],
        out_specs=[pl.BlockSpec((TOP_K, TM), col), pl.BlockSpec((TOP_K, TM), col)],
        out_shape=[jax.ShapeDtypeStruct((TOP_K, n_rows), jnp.int32), jax.ShapeDtypeStruct((TOP_K, n_rows), F32)],
        compiler_params=_cp(32 << 20, 1),
        name="router",
    )(u2, w_router_t, router_bias.reshape(N_EXPERTS, 1))


def _dispatch(top_e, gate):
    t = top_e.shape[1]
    tk = t * TOP_K
    nblk = tk // MOE_BLOCK
    n_steps = nblk + N_EXPERTS
    flat_e = top_e.reshape(tk)
    pos = jnp.arange(tk, dtype=jnp.int32)
    _, order, sw = lax.sort((flat_e, pos, gate.reshape(tk)), num_keys=1, is_stable=True)
    _, inv = lax.sort((order, pos), num_keys=1)
    experts = jnp.arange(N_EXPERTS, dtype=jnp.int32)
    counts = jnp.sum((flat_e[None, :] == experts[:, None]).astype(jnp.int32), axis=1)
    ends = jnp.cumsum(counts)
    starts = ends - counts
    fb = starts // MOE_BLOCK
    npairs = jnp.where(counts > 0, (ends - 1) // MOE_BLOCK - fb + 1, 0)
    pend = jnp.cumsum(npairs)
    poff = pend - npairs
    n_pairs = pend[-1]
    s = jnp.arange(n_steps, dtype=jnp.int32)
    sc = jnp.minimum(s, n_pairs - 1)
    pe = jnp.sum((pend[None, :] <= sc[:, None]).astype(jnp.int32), axis=1)
    pb = fb[pe] + (sc - poff[pe])
    lo = jnp.where(s < n_pairs, jnp.maximum(starts[pe] - pb * MOE_BLOCK, 0), 0)
    hi = jnp.where(s < n_pairs, jnp.minimum(ends[pe] - pb * MOE_BLOCK, MOE_BLOCK), 0)
    last_of_block = ((s == n_pairs - 1) | ((s + 1 < n_pairs) & (jnp.roll(pb, -1) != pb))).astype(jnp.int32)
    return dict(pe=pe, pb=pb, lo=lo, hi=hi, last=last_of_block, tok=order % t, comb=inv,
                w=sw.reshape(nblk, MOE_BLOCK, 1))


def _expert_kernel(pe, pb, plo, phi, plast, x_ref, w_ref, wg_ref, wu_ref, wd_ref, out_hbm, obuf, osem, wgb, wub, wdb):
    s = pl.program_id(0)
    prev = jnp.maximum(s - 1, 0)
    n_blocks = out_hbm.shape[0] // MOE_BLOCK

    @pl.when((s == 0) | (pe[s] != pe[prev]))
    def _():
        wgb[...] = wg_ref[...].astype(BF16)
        wub[...] = wu_ref[...].astype(BF16)
        wdb[...] = wd_ref[...].astype(BF16)

    lo = plo[s]
    hi = phi[s]
    blk = pb[s]
    slot = blk % 2
    first_of_block = (s == 0) | (blk != pb[prev])

    def out_copy(b, sl):
        rows = pl.ds(pl.multiple_of(b * MOE_BLOCK, MOE_BLOCK), MOE_BLOCK)
        return pltpu.make_async_copy(obuf.at[sl], out_hbm.at[rows], osem.at[sl])

    def ffn():
        x_lo, x_hi = _unpack_pairs(x_ref[...])
        x_lo = x_lo.astype(BF16)
        x_hi = x_hi.astype(BF16)
        hg = (jnp.dot(x_lo, wgb[0:HALF_D], preferred_element_type=F32)
              + jnp.dot(x_hi, wgb[HALF_D:D], preferred_element_type=F32))
        hu = (jnp.dot(x_lo, wub[0:HALF_D], preferred_element_type=F32)
              + jnp.dot(x_hi, wub[HALF_D:D], preferred_element_type=F32))
        h = (hg * jax.nn.sigmoid(hg)) * hu
        return _pack_pairs(jnp.dot(h.astype(BF16), wdb[...], preferred_element_type=F32) * w_ref[0])

    @pl.when((hi > lo) & first_of_block)
    def _():
        @pl.when(blk >= 2)
        def _():
            out_copy(blk - 2, slot).wait()
        obuf[slot] = ffn()

    @pl.when((hi > lo) & jnp.logical_not(first_of_block))
    def _():
        row = lax.broadcasted_iota(jnp.int32, (MOE_BLOCK, HALF_D), 0)
        obuf[slot] = jnp.where((row >= lo) & (row < hi), ffn(), obuf[slot])

    @pl.when(plast[s] == 1)
    def _():
        out_copy(blk, slot).start()

    @pl.when(s == pl.num_programs(0) - 1)
    def _():
        out_copy(n_blocks - 2, (n_blocks - 2) % 2).wait()
        out_copy(n_blocks - 1, (n_blocks - 1) % 2).wait()


def _experts(xs, disp, wg, wu, wd, layer):
    tk = xs.shape[0]
    n_steps = disp['pe'].shape[0]
    assert tk // MOE_BLOCK >= 2
    blk = lambda s, pe, pb, lo, hi, last: (pb[s], 0)
    wspec = lambda shape: pl.BlockSpec((None, None) + shape, lambda s, pe, pb, lo, hi, last: (layer, pe[s], 0, 0))
    grid_spec = pltpu.PrefetchScalarGridSpec(
        num_scalar_prefetch=5,
        grid=(n_steps,),
        in_specs=[pl.BlockSpec((MOE_BLOCK, HALF_D), blk),
                  pl.BlockSpec((1, MOE_BLOCK, 1), lambda s, pe, pb, lo, hi, last: (pb[s], 0, 0)),
                  wspec((D, EXPERT_FF)), wspec((D, EXPERT_FF)), wspec((EXPERT_FF, D))],
        out_specs=pl.BlockSpec(memory_space=pl.ANY),
        scratch_shapes=[pltpu.VMEM((2, MOE_BLOCK, HALF_D), jnp.int32), pltpu.SemaphoreType.DMA((2,)),
                        pltpu.VMEM((D, EXPERT_FF), BF16), pltpu.VMEM((D, EXPERT_FF), BF16),
                        pltpu.VMEM((EXPERT_FF, D), BF16)],
    )
    return pl.pallas_call(
        _expert_kernel,
        grid_spec=grid_spec,
        out_shape=jax.ShapeDtypeStruct((tk, HALF_D), jnp.int32),
        compiler_params=_cp(40 << 20, 1),
        name="experts",
    )(disp['pe'], disp['pb'], disp['lo'], disp['hi'], disp['last'], xs, disp['w'], wg, wu, wd)


SC_ROWS = 64


def _row_gather(table, idx):
    n = idx.shape[0]
    d = table.shape[1]
    mesh = plsc.VectorSubcoreMesh(core_axis_name="c", subcore_axis_name="s")
    n_workers = mesh.num_cores * mesh.num_subcores
    per_worker = n // n_workers
    assert per_worker * n_workers == n and per_worker % SC_ROWS == 0

    @functools.partial(
        pl.kernel, mesh=mesh,
        out_type=jax.ShapeDtypeStruct((n, d), table.dtype),
        scratch_types=[pltpu.VMEM((SC_ROWS,), jnp.int32), pltpu.VMEM((SC_ROWS, d), table.dtype),
                       pltpu.SemaphoreType.DMA],
    )
    def gather(table_hbm, idx_hbm, out_hbm, idx_v, rows_v, sem):
        worker = lax.axis_index("s") * mesh.num_cores + lax.axis_index("c")
        base = worker * per_worker

        @pl.loop(0, per_worker // SC_ROWS)
        def _(c):
            off = pl.multiple_of(base + c * SC_ROWS, 8)
            pltpu.sync_copy(idx_hbm.at[pl.ds(off, SC_ROWS)], idx_v)
            pltpu.async_copy(table_hbm.at[idx_v], rows_v, sem).wait()
            pltpu.sync_copy(rows_v, out_hbm.at[pl.ds(off, SC_ROWS)])

    return gather(table, idx)


FM = 128


def _ffn_out_kernel(x_ref, u_ref, r_ref, g2_ref, wsg_ref, wsu_ref, wsd_ref, lng_ref, lnb_ref, o_ref):
    u_lo, u_hi = _unpack_pairs(u_ref[...])
    u_lo = u_lo.astype(BF16)
    u_hi = u_hi.astype(BF16)
    hg = (jnp.dot(u_lo, wsg_ref[0:HALF_D], preferred_element_type=F32)
          + jnp.dot(u_hi, wsg_ref[HALF_D:D], preferred_element_type=F32))
    hu = (jnp.dot(u_lo, wsu_ref[0:HALF_D], preferred_element_type=F32)
          + jnp.dot(u_hi, wsu_ref[HALF_D:D], preferred_element_type=F32))
    f = jnp.dot(((hg * jax.nn.sigmoid(hg)) * hu).astype(BF16), wsd_ref[...], preferred_element_type=F32)
    r_lo, r_hi = _unpack_pairs(r_ref[0])
    for kk in range(1, TOP_K):
        k_lo, k_hi = _unpack_pairs(r_ref[kk])
        r_lo = r_lo + k_lo
        r_hi = r_hi + k_hi
    f = jnp.concatenate([r_lo, r_hi], axis=1) + f
    o_ref[...] = _layer_norm(ALPHA * x_ref[...] + g2_ref[...] * f, lng_ref[...], lnb_ref[...])


def _ffn_out(n_rows, x1, u2p, routed, g2, wsg_bf, wsu_bf, wsd_bf, ln_g, ln_b):
    row = lambda i: (i, 0)
    full = lambda i: (0, 0)
    return pl.pallas_call(
        _ffn_out_kernel,
        grid=(n_rows // FM,),
        in_specs=[pl.BlockSpec((FM, D), row), pl.BlockSpec((FM, HALF_D), row),
                  pl.BlockSpec((TOP_K, FM, HALF_D), lambda i: (0, i, 0)),
                  _vec_spec(FM), pl.BlockSpec((D, EXPERT_FF), full), pl.BlockSpec((D, EXPERT_FF), full),
                  pl.BlockSpec((EXPERT_FF, D), full), pl.BlockSpec((1, D), full), pl.BlockSpec((1, D), full)],
        out_specs=pl.BlockSpec((FM, D), row),
        out_shape=jax.ShapeDtypeStruct((n_rows, D), F32),
        compiler_params=_cp(40 << 20, 1),
        name="ffn_out",
    )(x1, u2p, routed, g2, wsg_bf, wsu_bf, wsd_bf, ln_g.reshape(1, D), ln_b.reshape(1, D))


def _rope_tables():
    t = jnp.arange(L, dtype=jnp.int32)
    row = (t // GRID_W).astype(F32)
    col = (t % GRID_W).astype(F32)
    inv_freq = ROPE_BASE ** (-jnp.arange(0, AXIS_DIM, 2, dtype=F32) / AXIS_DIM)
    half = AXIS_DIM // 2

    def axis(pos):
        ang = pos[:, None] * inv_freq[None, :]
        c = jnp.cos(ang)
        s = jnp.sin(ang)
        return jnp.concatenate([c, c], axis=1), jnp.concatenate([-s, s], axis=1)

    cr, sr = axis(row)
    cc, sc = axis(col)
    cos_h = jnp.concatenate([cr, cc], axis=1)
    sin_h = jnp.concatenate([sr, sc], axis=1)
    cos_l = jnp.tile(cos_h, (B, LANE // HEAD_DIM))
    sin_l = jnp.tile(sin_h, (B, LANE // HEAD_DIM))
    cos_t = jnp.concatenate([cos_l, jnp.ones((T_CTX, LANE), F32)], axis=0)
    sin_t = jnp.concatenate([sin_l, jnp.zeros((T_CTX, LANE), F32)], axis=0)
    del half
    return cos_t, sin_t


def kernel(x, c, ctx, c_ctx, w_ada, b_ada, w_in, w_out, sink, mix_g, hy_short_w, hy_short_b, hy_w1, hy_b1, hy_freq,
           hy_w2, hy_b2, hy_w3, hy_d, s5_a_re, s5_a_im, s5_log_dt, s5_b_re, s5_b_im, s5_c_re, s5_c_im, s5_d, s5_w_glu,
           ln1_g, ln1_b, ln2_g, ln2_b, w_router, router_bias, w_exp_gate, w_exp_up, w_exp_down, w_sh_gate, w_sh_up,
           w_sh_down):
    xall = jnp.concatenate([x.reshape(T_LAT, D), ctx.reshape(T_CTX, D)], axis=0)
    cvec = jnp.concatenate([c, c_ctx[None, :], jnp.zeros((8 - B - 1, D), F32)], axis=0)
    mod = _ada(cvec, w_ada, b_ada)[:, 0:B + 1, :].reshape(DEPTH, B + 1, 6, 1, D)

    cos_t, sin_t = _rope_tables()
    g_cos, g_sin, f_cos, f_sin = _dft_tables()
    g_cos_bf, g_sin_bf = g_cos.astype(BF16), g_sin.astype(BF16)
    i_cos_bf = jnp.swapaxes(g_cos, 1, 2).astype(BF16)
    i_sin_bf = jnp.swapaxes(g_sin, 1, 2).astype(BF16)
    f_cos_bf, f_sin_bf = f_cos.astype(BF16), f_sin.astype(BF16)
    k1 = jnp.arange(K1P)
    spec_w = jnp.where((k1 == 0) | (k1 == FFT_R // 2), 1.0, 2.0) * (k1 < K1_USED) / N_FFT
    spec_w = spec_w.astype(F32)
    kt = jnp.arange(2 * C, dtype=jnp.int32)
    ang_c = ((kt[:, None] * kt[None, :]) % (2 * C)).astype(F32) * (2.0 * math.pi / (2 * C))
    d_cos, d_sin = jnp.cos(ang_c), jnp.sin(ang_c)

    for l in range(DEPTH):
        last = l == DEPTH - 1
        n_rows = T_LAT if last else T_ALL
        sh1, sc1, g1, sh2, sc2, g2 = (mod[l, :, j] for j in range(6))

        q, k, v, hz, s5u = _inproj(xall, sh1, sc1, w_in[l].astype(BF16), cos_t, sin_t)
        attn = _attention(sink[l], q, k, v, with_ctx=not last)

        hu, x0c = _hyena_pre(hz, hy_short_w[l], hy_short_b[l])
        filt_args = (hy_w1[l], hy_b1[l], hy_freq[l], hy_w2[l], hy_b2[l], hy_w3[l])
        k_r, k_i = _hyena_spectrum(_hyena_filter(L, *filt_args), g_cos, g_sin, f_cos, f_sin, spec_w)
        conv = _hyena_fft(hu, g_cos_bf, g_sin_bf, i_cos_bf, i_sin_bf, f_cos_bf, f_sin_bf, k_r, k_i)
        if not last:
            conv = jnp.concatenate([conv, _hyena_ctx(hu, _hyena_filter(C, *filt_args), d_cos, d_sin)], axis=0)

        ws, e_all, wc, lam_p = _s5_matrices(s5_a_re[l], s5_a_im[l], s5_log_dt[l], s5_b_re[l], s5_b_im[l],
                                            s5_c_re[l], s5_c_im[l], s5_d[l])
        s_rows, y_in = _s5_in(s5u, ws, e_all)
        s5y = _s5_out(_s5_scan(s_rows, lam_p), wc, y_in)

        x1, u2, u2p = _merge(n_rows, xall, attn, conv, hu, x0c, s5y, g1, sh2, sc2, mix_g[l], hy_d[l],
                             s5_w_glu[l].astype(BF16), w_out[l].astype(BF16), ln1_g[l], ln1_b[l])

        top_e, gate = _router(n_rows, u2, w_router[l].T, router_bias[l])
        disp = _dispatch(top_e, gate)
        ys = _experts(_row_gather(u2p, disp['tok']), disp, w_exp_gate, w_exp_up, w_exp_down, l)
        routed = _row_gather(ys, disp['comb']).reshape(TOP_K, n_rows, HALF_D)
        xall = _ffn_out(n_rows, x1, u2p, routed, g2, w_sh_gate[l].astype(BF16), w_sh_up[l].astype(BF16),
                        w_sh_down[l].astype(BF16), ln2_g[l], ln2_b[l])
    return xall.reshape(B, L, D)
```

```python
import functools
import math

import jax
import jax.numpy as jnp
from jax import lax
from jax.experimental import pallas as pl
from jax.experimental.pallas import tpu as pltpu
from jax.experimental.pallas import tpu_sc as plsc

F32 = jnp.float32
BF16 = jnp.bfloat16
HIGHEST = lax.Precision.HIGHEST

D = 1024
B = 2
L = 8192
DEPTH = 2
GRID_W = 64
C = 256
T_LAT = B * L
T_CTX = B * C
T_ALL = T_LAT + T_CTX

HEAD_DIM = 64
N_Q = 8
N_KV = 2
Q_GROUP = N_Q // N_KV
ATTN_W = N_Q * HEAD_DIM
KV_W = N_KV * HEAD_DIM
HY_W = 256
S5_W = 256
MIX_W = ATTN_W + HY_W + S5_W
K_OFF = ATTN_W
V_OFF = K_OFF + KV_W
HY_OFF = V_OFF + KV_W
S5_OFF = HY_OFF + 3 * HY_W
IN_W = S5_OFF + S5_W
WINDOW = 128
BLK = 128
NEG_INF = -1e30
ROPE_BASE = 10000.0
AXIS_DIM = HEAD_DIM // 2

SHORT_K = 3
FILTER_EMB = 33
DECAY_FAST = 0.3
DECAY_SLOW = 1.5
DECAY_TARGET = 1e-2

S5_GROUP = 16
S5_GROUPS = S5_W // S5_GROUP
S5_STATE = 64
S5_NSTATE = S5_GROUPS * S5_STATE
S5_CH = 16
S5_ROWW = S5_CH * S5_W
N_CHUNK = T_ALL // S5_CH
LAT_CHUNKS = L // S5_CH
CTX_CHUNKS = C // S5_CH

N_EXPERTS = 256
TOP_K = 8
N_EGROUPS = 8
EGROUP = N_EXPERTS // N_EGROUPS
TOPK_GROUPS = 4
EXPERT_FF = 256
ROUTED_SCALE = 2.5
MOE_BLOCK = 256

ALPHA = (2 * DEPTH) ** 0.25
LN_EPS = 1e-5

N_FFT = 2 * L
FFT_R = 128
FFT_T1 = L // FFT_R
K1_USED = FFT_R // 2 + 1
K1P = 80
K1H = K1P // 2
FFT_UNROLL = 4

TM = 256
LANE = 128
VMEM_CAP = 60000 * 1024


def _cp(vmem_bytes, n_axes):
    return pltpu.CompilerParams(
        dimension_semantics=("arbitrary",) * n_axes,
        vmem_limit_bytes=min(int(vmem_bytes), VMEM_CAP),
    )


HALF_D = D // 2
HIGH_HALF_WORD = 0xFFFF0000


def _pack_pairs(x):
    bits = lax.bitcast_convert_type(x.astype(BF16).astype(F32), jnp.uint32)
    packed = (bits[:, 0:HALF_D] >> 16) | (bits[:, HALF_D:D] & jnp.uint32(HIGH_HALF_WORD))
    return lax.bitcast_convert_type(packed, jnp.int32)


def _unpack_pairs(p):
    bits = lax.bitcast_convert_type(p, jnp.uint32)
    low = lax.bitcast_convert_type(bits << 16, F32)
    high = lax.bitcast_convert_type(bits & jnp.uint32(HIGH_HALF_WORD), F32)
    return low, high


def _mod_sel(rows_per_tile):
    per_batch = L // rows_per_tile
    return lambda i: jnp.minimum(i // per_batch, 2)


def _vec_spec(rows_per_tile):
    sel = _mod_sel(rows_per_tile)
    return pl.BlockSpec((None, 1, D), lambda i: (sel(i), 0, 0))


ADA_TN = 1536


def _ada_kernel(c_ref, w_ref, b_ref, o_ref):
    c = c_ref[...]
    s = c * jax.nn.sigmoid(c)
    o_ref[...] = jnp.dot(s, w_ref[...], precision=HIGHEST, preferred_element_type=F32) + b_ref[...]


def _ada(cvec, w_ada, b_ada):
    return pl.pallas_call(
        _ada_kernel,
        grid=(DEPTH, 6 * D // ADA_TN),
        in_specs=[
            pl.BlockSpec((8, D), lambda l, j: (0, 0)),
            pl.BlockSpec((None, D, ADA_TN), lambda l, j: (l, 0, j)),
            pl.BlockSpec((None, 1, ADA_TN), lambda l, j: (l, 0, j)),
        ],
        out_specs=pl.BlockSpec((None, 8, ADA_TN), lambda l, j: (l, 0, j)),
        out_shape=jax.ShapeDtypeStruct((DEPTH, 8, 6 * D), F32),
        compiler_params=_cp(40 << 20, 2),
        name="ada",
    )(cvec, w_ada, b_ada.reshape(DEPTH, 1, 6 * D))


def _inproj_kernel(x_ref, sh_ref, sc_ref, w_ref, cos_ref, sin_ref, q_ref, k_ref, v_ref, hy_ref, s5_ref, *s5_tok_refs):
    u = x_ref[...] * (1.0 + sc_ref[...]) + sh_ref[...]
    proj = jnp.dot(u.astype(BF16), w_ref[...], preferred_element_type=F32)
    cos = cos_ref[...]
    sin = sin_ref[...]
    lane = lax.broadcasted_iota(jnp.int32, (TM, LANE), 1)
    first_half = (lane % AXIS_DIM) < (AXIS_DIM // 2)

    def rope(xc):
        partner = jnp.where(first_half, pltpu.roll(xc, LANE - AXIS_DIM // 2, 1), pltpu.roll(xc, AXIS_DIM // 2, 1))
        return xc * cos + partner * sin

    for j in range(ATTN_W // LANE):
        q_ref[:, j * LANE:(j + 1) * LANE] = rope(proj[:, j * LANE:(j + 1) * LANE]).astype(BF16)
    k_ref[...] = rope(proj[:, K_OFF:V_OFF]).astype(BF16)
    v_ref[...] = proj[:, V_OFF:HY_OFF].astype(BF16)
    hy_ref[...] = proj[:, HY_OFF:S5_OFF]
    for h, tok_ref in enumerate(s5_tok_refs):
        tok_ref[...] = proj[:, S5_OFF + h * LANE:S5_OFF + (h + 1) * LANE]
    for s in range(S5_CH):
        for h, tok_ref in enumerate(s5_tok_refs):
            s5_ref[:, s * S5_W + h * LANE:s * S5_W + (h + 1) * LANE] = (
                tok_ref[pl.ds(s, TM // S5_CH, stride=S5_CH), :].astype(BF16))


def _inproj(xall, sh, sc, w_in_bf, cos_t, sin_t):
    nt = T_ALL // TM
    row = lambda i: (i, 0)
    return pl.pallas_call(
        _inproj_kernel,
        grid=(nt,),
        in_specs=[
            pl.BlockSpec((TM, D), row),
            _vec_spec(TM),
            _vec_spec(TM),
            pl.BlockSpec((D, IN_W), lambda i: (0, 0)),
            pl.BlockSpec((TM, LANE), row),
            pl.BlockSpec((TM, LANE), row),
        ],
        out_specs=[
            pl.BlockSpec((TM, ATTN_W), row),
            pl.BlockSpec((TM, KV_W), row),
            pl.BlockSpec((TM, KV_W), row),
            pl.BlockSpec((TM, 3 * HY_W), row),
            pl.BlockSpec((TM // S5_CH, S5_ROWW), row),
        ],
        out_shape=[
            jax.ShapeDtypeStruct((T_ALL, ATTN_W), BF16),
            jax.ShapeDtypeStruct((T_ALL, KV_W), BF16),
            jax.ShapeDtypeStruct((T_ALL, KV_W), BF16),
            jax.ShapeDtypeStruct((T_ALL, 3 * HY_W), F32),
            jax.ShapeDtypeStruct((N_CHUNK, S5_ROWW), BF16),
        ],
        scratch_shapes=[pltpu.VMEM((TM, LANE), F32)] * (S5_W // LANE),
        compiler_params=_cp(40 << 20, 1),
        name="inproj",
    )(xall, sh, sc, w_in_bf, cos_t, sin_t)


NB_LAT = L // BLK
NB_CTX = C // BLK


def _nt_dot(a, b):
    return lax.dot_general(a, b, (((1,), (1,)), ((), ())), preferred_element_type=F32)


def _attn_kernel(sink_ref, q_ref, kp_ref, kc_ref, kn_ref, kx_ref, vp_ref, vc_ref, vn_ref, vx_ref, o_ref):
    n = pl.program_id(1)
    is_lat = n < NB_LAT
    rows = Q_GROUP * BLK
    r = lax.broadcasted_iota(jnp.int32, (rows, BLK), 0) % BLK
    j = lax.broadcasted_iota(jnp.int32, (rows, BLK), 1)
    ok_prev = (j >= r) & (n >= 1) & is_lat
    ok_next = (j <= r) & (n + 1 < NB_LAT) & is_lat
    head_of_row = lax.broadcasted_iota(jnp.int32, (rows, 1), 0) // BLK
    q = q_ref[...] * (HEAD_DIM ** -0.5)
    dot = lambda a, b: jnp.dot(a.astype(BF16), b, preferred_element_type=F32)
    for kh in range(N_KV):
        hs = slice(kh * HEAD_DIM, (kh + 1) * HEAD_DIM)
        heads = range(kh * Q_GROUP, (kh + 1) * Q_GROUP)
        qg = jnp.concatenate([q[:, h * HEAD_DIM:(h + 1) * HEAD_DIM] for h in heads], axis=0)
        sk = jnp.zeros((rows, 1), F32)
        for g, h in enumerate(heads):
            sk = jnp.where(head_of_row == g, sink_ref[h], sk)
        s_p = jnp.where(ok_prev, _nt_dot(qg, kp_ref[:, hs]), NEG_INF)
        s_c = jnp.where(is_lat, _nt_dot(qg, kc_ref[:, hs]), NEG_INF)
        s_n = jnp.where(ok_next, _nt_dot(qg, kn_ref[:, hs]), NEG_INF)
        s_x = _nt_dot(qg, kx_ref[:, hs])
        s_x0 = s_x[:, 0:BLK]
        s_x1 = s_x[:, BLK:2 * BLK]
        m = jnp.maximum(jnp.maximum(jnp.maximum(s_p, s_c), jnp.maximum(s_n, s_x0)), s_x1)
        m = jnp.maximum(jnp.max(m, axis=1, keepdims=True), sk)
        e_p = jnp.exp(s_p - m)
        e_c = jnp.exp(s_c - m)
        e_n = jnp.exp(s_n - m)
        e_x0 = jnp.exp(s_x0 - m)
        e_x1 = jnp.exp(s_x1 - m)
        den = jnp.sum((e_p + e_c) + (e_n + e_x0) + e_x1, axis=1, keepdims=True) + jnp.exp(sk - m)
        o = (dot(e_p, vp_ref[:, hs]) + dot(e_c, vc_ref[:, hs]) + dot(e_n, vn_ref[:, hs])
             + dot(e_x0, vx_ref[0:BLK, hs]) + dot(e_x1, vx_ref[BLK:2 * BLK, hs]))
        o = o / den
        for g, h in enumerate(heads):
            o_ref[:, h * HEAD_DIM:(h + 1) * HEAD_DIM] = o[g * BLK:(g + 1) * BLK]


def _attention(sink, q, k, v, with_ctx):
    nblk = NB_LAT + (NB_CTX if with_ctx else 0)

    def q_idx(b, n):
        return (jnp.where(n < NB_LAT, b * NB_LAT + n, B * NB_LAT + b * NB_CTX + (n - NB_LAT)), 0)

    def kv_idx(off):
        def idx(b, n):
            nn = jnp.clip(jnp.minimum(n, NB_LAT - 1) + off, 0, NB_LAT - 1)
            return (b * NB_LAT + nn, 0)
        return idx

    ctx_idx = lambda b, n: (T_LAT // C + b, 0)
    kv_specs = lambda: [pl.BlockSpec((BLK, KV_W), kv_idx(-1)), pl.BlockSpec((BLK, KV_W), kv_idx(0)),
                        pl.BlockSpec((BLK, KV_W), kv_idx(1)), pl.BlockSpec((C, KV_W), ctx_idx)]
    return pl.pallas_call(
        _attn_kernel,
        grid=(B, nblk),
        in_specs=[pl.BlockSpec(memory_space=pltpu.SMEM), pl.BlockSpec((BLK, ATTN_W), q_idx)] + kv_specs() + kv_specs(),
        out_specs=pl.BlockSpec((BLK, ATTN_W), q_idx),
        out_shape=jax.ShapeDtypeStruct((T_ALL if with_ctx else T_LAT, ATTN_W), F32),
        compiler_params=_cp(32 << 20, 2),
        name="attention",
    )(sink, q, k, k, k, k, v, v, v, v)


def _hyena_pre_kernel(z_ref, zp_ref, zn_ref, w_ref, b_ref, u_ref, x0_ref):
    i = pl.program_id(0)
    tiles_per_seq = L // TM
    is_ctx = i >= B * tiles_per_seq
    first = is_ctx | (i % tiles_per_seq == 0)
    last = is_ctx | (i % tiles_per_seq == tiles_per_seq - 1)
    z = z_ref[...]
    prev_row = jnp.where(first, 0.0, zp_ref[7:8, :])
    next_row = jnp.where(last, 0.0, zn_ref[0:1, :])
    row = lax.broadcasted_iota(jnp.int32, z.shape, 0)
    z_m1 = jnp.where(row == 0, prev_row, pltpu.roll(z, 1, 0))
    z_p1 = jnp.where(row == TM - 1, next_row, pltpu.roll(z, TM - 1, 0))
    zc = b_ref[...] + z_m1 * w_ref[0:1, :] + z * w_ref[1:2, :] + z_p1 * w_ref[2:3, :]
    u_ref[...] = zc[:, 0:HY_W] * zc[:, HY_W:2 * HY_W]
    x0_ref[...] = zc[:, 2 * HY_W:3 * HY_W]


def _hyena_pre(z, short_w, short_b):
    nt = T_ALL // TM
    sub = TM // 8
    n8 = T_ALL // 8
    return pl.pallas_call(
        _hyena_pre_kernel,
        grid=(nt,),
        in_specs=[
            pl.BlockSpec((TM, 3 * HY_W), lambda i: (i, 0)),
            pl.BlockSpec((8, 3 * HY_W), lambda i: (jnp.maximum(i * sub - 1, 0), 0)),
            pl.BlockSpec((8, 3 * HY_W), lambda i: (jnp.minimum((i + 1) * sub, n8 - 1), 0)),
            pl.BlockSpec((SHORT_K, 3 * HY_W), lambda i: (0, 0)),
            pl.BlockSpec((1, 3 * HY_W), lambda i: (0, 0)),
        ],
        out_specs=[pl.BlockSpec((TM, HY_W), lambda i: (i, 0)), pl.BlockSpec((TM, HY_W), lambda i: (i, 0))],
        out_shape=[jax.ShapeDtypeStruct((T_ALL, HY_W), F32), jax.ShapeDtypeStruct((T_ALL, HY_W), F32)],
        compiler_params=_cp(32 << 20, 1),
        name="hyena_pre",
    )(z, z, z, short_w, short_b.reshape(1, 3 * HY_W))


def _dft_tables():
    t0 = jnp.arange(FFT_R, dtype=jnp.int32)[:, None, None]
    k1 = jnp.arange(K1P, dtype=jnp.int32)[None, :, None]
    t1 = jnp.arange(FFT_T1, dtype=jnp.int32)[None, None, :]
    m = (k1 * (FFT_R * t1 + t0)) % N_FFT
    ang = m.astype(F32) * (2.0 * math.pi / N_FFT)
    used = (k1 < K1_USED).astype(F32)
    g_cos = jnp.cos(ang) * used
    g_sin = jnp.sin(ang) * used
    a = jnp.arange(FFT_R, dtype=jnp.int32)
    ang2 = ((a[:, None] * a[None, :]) % FFT_R).astype(F32) * (2.0 * math.pi / FFT_R)
    return g_cos, g_sin, jnp.cos(ang2), jnp.sin(ang2)


def _hyena_spec_kernel(k_ref, gc_ref, gs_ref, fc_ref, fs_ref, wt_ref, kr_ref, ki_ref, ar_ref, ai_ref):
    half = pl.program_id(1)
    kk = lax.broadcasted_iota(jnp.int32, (K1H, 1), 0) + half * K1H
    sign = jnp.where(kk % 2 == 0, 1.0, -1.0).astype(F32)

    dot = lambda a, b: jnp.dot(a, b.astype(BF16), preferred_element_type=F32)

    def stage1(i, carry):
        t0s = [i * FFT_UNROLL + u for u in range(FFT_UNROLL)]
        loaded = [(k_ref[pl.ds(t0, FFT_T1, stride=FFT_R), :], k_ref[pl.ds(L + t0, FFT_T1, stride=FFT_R), :],
                   gc_ref[t0].astype(BF16), gs_ref[t0].astype(BF16)) for t0 in t0s]
        res = [(dot(gc, x_lo) + sign * dot(gc, x_hi), -(dot(gs, x_lo) + sign * dot(gs, x_hi)))
               for x_lo, x_hi, gc, gs in loaded]
        for t0, (a_r, a_i) in zip(t0s, res):
            rows = pl.ds(pl.multiple_of(t0 * K1H, 8), K1H)
            ar_ref[rows, :] = a_r
            ai_ref[rows, :] = a_i
        return carry

    lax.fori_loop(0, FFT_R // FFT_UNROLL, stage1, 0)
    fc = fc_ref[...].astype(BF16)
    fs = fs_ref[...].astype(BF16)

    def stage2(i, carry):
        kls = [i * FFT_UNROLL + u for u in range(FFT_UNROLL)]
        loaded = [(ar_ref[pl.ds(kl, FFT_R, stride=K1H), :], ai_ref[pl.ds(kl, FFT_R, stride=K1H), :]) for kl in kls]
        for kl, (a_r, a_i) in zip(kls, loaded):
            w = wt_ref[half * K1H + kl]
            kr_ref[kl] = (dot(fc, a_r) + dot(fs, a_i)) * w
            ki_ref[kl] = (dot(fc, a_i) - dot(fs, a_r)) * w
        return carry

    lax.fori_loop(0, K1H // FFT_UNROLL, stage2, 0)


def _hyena_spectrum(kfilt, g_cos, g_sin, f_cos, f_sin, wts):
    nct = HY_W // LANE
    gspec = pl.BlockSpec((FFT_R, K1H, FFT_T1), lambda c, h: (0, h, 0))
    fspec = pl.BlockSpec((FFT_R, FFT_R), lambda c, h: (0, 0))
    ospec = pl.BlockSpec((K1H, FFT_R, LANE), lambda c, h: (h, 0, c))
    return pl.pallas_call(
        _hyena_spec_kernel,
        grid=(nct, 2),
        in_specs=[pl.BlockSpec((N_FFT, LANE), lambda c, h: (0, c)), gspec, gspec, fspec, fspec,
                  pl.BlockSpec(memory_space=pltpu.SMEM)],
        out_specs=[ospec, ospec],
        out_shape=[jax.ShapeDtypeStruct((K1P, FFT_R, HY_W), F32)] * 2,
        scratch_shapes=[pltpu.VMEM((FFT_R * K1H, LANE), F32)] * 2,
        compiler_params=_cp(56 << 20, 2),
        name="hyena_spectrum",
    )(kfilt, g_cos, g_sin, f_cos, f_sin, wts)


def _hyena_fft_kernel(u_ref, gc_ref, gs_ref, ic_ref, is_ref, fc_ref, fs_ref, kr_ref, ki_ref, o_ref, ar_ref, ai_ref):
    bdot = lambda a, b: jnp.dot(a, b.astype(BF16), preferred_element_type=F32)

    def stage1(i, carry):
        t0s = [i * FFT_UNROLL + u for u in range(FFT_UNROLL)]
        xs = [u_ref[pl.ds(t0, FFT_T1, stride=FFT_R), :] for t0 in t0s]
        res = [(bdot(gc_ref[t0], x), -bdot(gs_ref[t0], x)) for t0, x in zip(t0s, xs)]
        for t0, (a_r, a_i) in zip(t0s, res):
            rows = pl.ds(pl.multiple_of(t0 * K1P, 8), K1P)
            ar_ref[rows, :] = a_r
            ai_ref[rows, :] = a_i
        return carry

    lax.fori_loop(0, FFT_R // FFT_UNROLL, stage1, 0)
    fc = fc_ref[...]
    fs = fs_ref[...]

    def stage23(i, carry):
        k1s = [i * FFT_UNROLL + u for u in range(FFT_UNROLL)]
        loaded = [(ar_ref[pl.ds(k1, FFT_R, stride=K1P), :], ai_ref[pl.ds(k1, FFT_R, stride=K1P), :],
                   kr_ref[k1], ki_ref[k1]) for k1 in k1s]
        res = []
        for a_r, a_i, k_r, k_i in loaded:
            z_r = bdot(fc, a_r) + bdot(fs, a_i)
            z_i = bdot(fc, a_i) - bdot(fs, a_r)
            y_r = z_r * k_r - z_i * k_i
            y_i = z_r * k_i + z_i * k_r
            res.append((bdot(fc, y_r) - bdot(fs, y_i), bdot(fc, y_i) + bdot(fs, y_r)))
        for k1, (b_r, b_i) in zip(k1s, res):
            ar_ref[pl.ds(k1, FFT_R, stride=K1P), :] = b_r
            ai_ref[pl.ds(k1, FFT_R, stride=K1P), :] = b_i
        return carry

    lax.fori_loop(0, -(-K1_USED // FFT_UNROLL), stage23, 0)

    def stage4(i, carry):
        t0s = [i * FFT_UNROLL + u for u in range(FFT_UNROLL)]
        loaded = []
        for t0 in t0s:
            rows = pl.ds(pl.multiple_of(t0 * K1P, 8), K1P)
            loaded.append((ar_ref[rows, :], ai_ref[rows, :]))
        res = [bdot(ic_ref[t0], b_r) - bdot(is_ref[t0], b_i) for t0, (b_r, b_i) in zip(t0s, loaded)]
        for t0, y in zip(t0s, res):
            o_ref[pl.ds(t0, FFT_T1, stride=FFT_R), :] = y
        return carry

    lax.fori_loop(0, FFT_R // FFT_UNROLL, stage4, 0)


def _hyena_fft(u, g_cos_bf, g_sin_bf, i_cos_bf, i_sin_bf, f_cos_bf, f_sin_bf, k_r, k_i):
    nct = HY_W // LANE
    one = pl.Buffered(1)
    gspec = pl.BlockSpec((FFT_R, K1P, FFT_T1), lambda c, b: (0, 0, 0), pipeline_mode=one)
    ispec = pl.BlockSpec((FFT_R, FFT_T1, K1P), lambda c, b: (0, 0, 0), pipeline_mode=one)
    fspec = pl.BlockSpec((FFT_R, FFT_R), lambda c, b: (0, 0), pipeline_mode=one)
    kspec = pl.BlockSpec((K1P, FFT_R, LANE), lambda c, b: (0, 0, c), pipeline_mode=one)
    return pl.pallas_call(
        _hyena_fft_kernel,
        grid=(nct, B),
        in_specs=[pl.BlockSpec((L, LANE), lambda c, b: (b, c)), gspec, gspec, ispec, ispec, fspec, fspec, kspec, kspec],
        out_specs=pl.BlockSpec((L, LANE), lambda c, b: (b, c)),
        out_shape=jax.ShapeDtypeStruct((T_LAT, HY_W), F32),
        scratch_shapes=[pltpu.VMEM((FFT_R * K1P, LANE), F32)] * 2,
        compiler_params=_cp(56 << 20, 2),
        name="hyena_fft",
    )(u, g_cos_bf, g_sin_bf, i_cos_bf, i_sin_bf, f_cos_bf, f_sin_bf, k_r, k_i)


def _hyena_ctx_kernel(u_ref, k_ref, dc_ref, ds_ref, o_ref):
    dot = lambda a, b: jnp.dot(a, b, precision=HIGHEST, preferred_element_type=F32)
    dc = dc_ref[...]
    ds = ds_ref[...]
    u = u_ref[...]
    kf = k_ref[...]
    u_r = dot(dc[:, 0:C], u)
    u_i = -dot(ds[:, 0:C], u)
    k_r = dot(dc, kf)
    k_i = -dot(ds, kf)
    y_r = u_r * k_r - u_i * k_i
    y_i = u_r * k_i + u_i * k_r
    o_ref[...] = (dot(dc[0:C, :], y_r) - dot(ds[0:C, :], y_i)) * (1.0 / (2 * C))


def _hyena_ctx(u, kfilt_ctx, d_cos, d_sin):
    full = lambda b: (0, 0)
    return pl.pallas_call(
        _hyena_ctx_kernel,
        grid=(B,),
        in_specs=[pl.BlockSpec((C, HY_W), lambda b: (T_LAT // C + b, 0)),
                  pl.BlockSpec((2 * C, HY_W), full), pl.BlockSpec((2 * C, 2 * C), full), pl.BlockSpec((2 * C, 2 * C), full)],
        out_specs=pl.BlockSpec((C, HY_W), lambda b: (b, 0)),
        out_shape=jax.ShapeDtypeStruct((T_CTX, HY_W), F32),
        compiler_params=_cp(32 << 20, 1),
        name="hyena_ctx",
    )(u, kfilt_ctx, d_cos, d_sin)


def _hyena_filter(n, w1, b1, freq, w2, b2, w3):
    t = jnp.linspace(0.0, 1.0, n, dtype=F32)[:, None]
    bands = (FILTER_EMB - 1) // 2
    w = 2.0 * math.pi * jnp.arange(n, dtype=F32)[:, None] / n
    f = jnp.linspace(1e-4, bands - 1, bands, dtype=F32)[None, :]
    z = jnp.concatenate([t, jnp.cos(f * w), -jnp.sin(f * w)], axis=-1)
    mm = functools.partial(jnp.matmul, precision=HIGHEST)
    h = jnp.sin(freq * (mm(z, w1) + b1))
    h = jnp.sin(freq * (mm(h, w2) + b2))
    h = mm(h, w3)
    deltas = jnp.abs(jnp.linspace(math.log(DECAY_TARGET) / DECAY_FAST, math.log(DECAY_TARGET) / DECAY_SLOW,
                                  HY_W, dtype=F32))
    decay = jnp.exp(-t * deltas[None, :])
    h_fwd = h[:, :HY_W] * decay
    h_bwd = h[:, HY_W:] * decay
    k = jnp.concatenate([h_fwd, jnp.zeros((1, HY_W), F32), h_bwd[:0:-1]], axis=0)
    return k / jnp.sum(jnp.abs(k), axis=0, keepdims=True)


def _s5_matrices(a_re, a_im, log_dt, b_re, b_im, c_re, c_im, d_skip):
    dt = jnp.exp(log_dt)[:, :, None]
    lam_re = jnp.minimum(a_re, -1e-4)
    mag1 = jnp.exp(lam_re * dt)
    lbr = mag1 * jnp.cos(a_im * dt)
    lbi = mag1 * jnp.sin(a_im * dt)
    den = lam_re * lam_re + a_im * a_im
    fr = ((lbr - 1.0) * lam_re + lbi * a_im) / den
    fi = (lbi * lam_re - (lbr - 1.0) * a_im) / den
    bbr = fr[..., None] * b_re - fi[..., None] * b_im
    bbi = fr[..., None] * b_im + fi[..., None] * b_re
    j = jnp.arange(S5_CH + 1, dtype=F32)[:, None, None, None]
    magj = jnp.exp(j * (lam_re * dt)[None])
    pr = magj * jnp.cos(j * (a_im * dt)[None])
    pi = magj * jnp.sin(j * (a_im * dt)[None])
    hi = functools.partial(jnp.einsum, precision=HIGHEST)
    lbr_j = pr[..., None] * bbr[None] - pi[..., None] * bbi[None]
    lbi_j = pr[..., None] * bbi[None] + pi[..., None] * bbr[None]
    m = hi('dgop,jdgpi->jdgoi', c_re, lbr_j) - hi('dgop,jdgpi->jdgoi', c_im, lbi_j)
    eye_g = jnp.eye(S5_GROUPS, dtype=F32)
    s = jnp.arange(S5_CH)
    blocks = jnp.einsum('jdgoi,gh->djgiho', m[0:S5_CH], eye_g).reshape(2, S5_CH, S5_W, S5_W)
    lag0 = blocks[0, 0] + blocks[1, 0] + jnp.diag(d_skip)
    e_all = jnp.concatenate([blocks[1, S5_CH - 1:0:-1], lag0[None], blocks[0, 1:S5_CH]], axis=0).astype(BF16)
    sf_r = lbr_j[S5_CH - 1 - s, 0]
    sf_i = lbi_j[S5_CH - 1 - s, 0]
    sb_r = lbr_j[s, 1]
    sb_i = lbi_j[s, 1]
    st = jnp.stack([sf_r, sf_i, sb_r, sb_i], axis=0)
    rr = jnp.arange(S5_ROWW, dtype=jnp.int32)
    cc = jnp.arange(S5_W, dtype=jnp.int32)
    ws = _s5_expand(jnp.transpose(st, (1, 2, 4, 0, 3)).reshape(S5_ROWW, S5_W),
                    (cc[:, None] // S5_STATE == rr[None, :] // S5_NSTATE)
                    & (cc[:, None] % S5_STATE == rr[None, :] % S5_STATE),
                    (rr // S5_GROUP) % S5_GROUPS, (rr % S5_NSTATE) // S5_STATE)
    tt = jnp.arange(S5_CH)
    cf_r = c_re[0][None] * pr[tt + 1, 0][:, :, None, :] - c_im[0][None] * pi[tt + 1, 0][:, :, None, :]
    cf_i = c_re[0][None] * pi[tt + 1, 0][:, :, None, :] + c_im[0][None] * pr[tt + 1, 0][:, :, None, :]
    cb_r = c_re[1][None] * pr[S5_CH - tt, 1][:, :, None, :] - c_im[1][None] * pi[S5_CH - tt, 1][:, :, None, :]
    cb_i = c_re[1][None] * pi[S5_CH - tt, 1][:, :, None, :] + c_im[1][None] * pr[S5_CH - tt, 1][:, :, None, :]
    ct = jnp.stack([cf_r, -cf_i, cb_r, -cb_i], axis=0)
    wc = _s5_expand(jnp.transpose(ct, (0, 2, 4, 1, 3)).reshape(4 * S5_NSTATE, S5_W),
                    (cc[:, None] // S5_GROUP == rr[None, :] // S5_W)
                    & (cc[:, None] % S5_GROUP == rr[None, :] % S5_GROUP),
                    (rr % S5_NSTATE) // S5_STATE, (rr % S5_W) // S5_GROUP)
    lam_p = jnp.stack([jnp.stack([pr[S5_CH, 0], pi[S5_CH, 0]]), jnp.stack([pr[S5_CH, 1], pi[S5_CH, 1]])])
    return ws, e_all, wc, lam_p.reshape(2, 2, 1, S5_NSTATE)


S5_TN = 512
S5_NS = S5_ROWW // S5_W


def _s5_expand_kernel(a_ref, ex_ref, rg_ref, cg_ref, o_ref):
    v = jnp.dot(a_ref[...], ex_ref[...], preferred_element_type=F32)
    o_ref[...] = jnp.where(rg_ref[...] == cg_ref[...], v, 0.0).astype(BF16)


def _s5_expand(compact, placement, row_group, col_group):
    n = compact.shape[0]
    return pl.pallas_call(
        _s5_expand_kernel,
        grid=(S5_ROWW // S5_TN,),
        in_specs=[pl.BlockSpec((n, S5_W), lambda j: (0, 0)), pl.BlockSpec((S5_W, S5_TN), lambda j: (0, j)),
                  pl.BlockSpec((n, 1), lambda j: (0, 0)), pl.BlockSpec((1, S5_TN), lambda j: (0, j))],
        out_specs=pl.BlockSpec((n, S5_TN), lambda j: (0, j)),
        out_shape=jax.ShapeDtypeStruct((n, S5_ROWW), BF16),
        compiler_params=_cp(32 << 20, 1),
        name="s5_expand",
    )(compact.astype(BF16), placement.astype(BF16), row_group.reshape(n, 1), col_group.reshape(1, S5_ROWW))


def _s5_in_kernel(u_ref, ws_ref, e_ref, s_ref, o_ref):
    j = pl.program_id(0)

    @pl.when(j < S5_NS)
    def _():
        s_ref[...] = jnp.dot(u_ref[...], ws_ref[...], preferred_element_type=F32)

    @pl.when(j >= S5_NS)
    def _():
        t = j - S5_NS
        acc = jnp.dot(u_ref[:, 0:S5_W], e_ref[S5_CH - 1 + t], preferred_element_type=F32)
        for s in range(1, S5_CH):
            acc = acc + jnp.dot(u_ref[:, s * S5_W:(s + 1) * S5_W], e_ref[S5_CH - 1 + t - s],
                                preferred_element_type=F32)
        o_ref[...] = acc


def _s5_in(u_rows, ws, e_all):
    return pl.pallas_call(
        _s5_in_kernel,
        grid=(2 * S5_NS,),
        in_specs=[pl.BlockSpec((N_CHUNK, S5_ROWW), lambda j: (0, 0)),
                  pl.BlockSpec((S5_ROWW, S5_W), lambda j: (0, jnp.minimum(j, S5_NS - 1))),
                  pl.BlockSpec((2 * S5_CH - 1, S5_W, S5_W), lambda j: (0, 0, 0))],
        out_specs=[pl.BlockSpec((N_CHUNK, S5_W), lambda j: (0, jnp.minimum(j, S5_NS - 1))),
                   pl.BlockSpec((N_CHUNK, S5_W), lambda j: (0, jnp.maximum(j - S5_NS, 0)))],
        out_shape=[jax.ShapeDtypeStruct((N_CHUNK, 4 * S5_NSTATE), F32), jax.ShapeDtypeStruct((N_CHUNK, S5_ROWW), F32)],
        compiler_params=_cp(48 << 20, 1),
        name="s5_in",
    )(u_rows, ws, e_all)


def _s5_scan_kernel(s_ref, lam_ref, h_ref):
    lam = [[lam_ref[d, p] for p in range(2)] for d in range(2)]

    def step(b, d, chunk, h):
        row = pl.ds(chunk, 1)
        cols_r = pl.ds(d * 2 * S5_NSTATE, S5_NSTATE)
        cols_i = pl.ds(d * 2 * S5_NSTATE + S5_NSTATE, S5_NSTATE)
        h_ref[row, cols_r] = h[0]
        h_ref[row, cols_i] = h[1]
        s_r = s_ref[row, cols_r]
        s_i = s_ref[row, cols_i]
        lr, li = lam[d]
        return (lr * h[0] - li * h[1] + s_r, lr * h[1] + li * h[0] + s_i)

    def chain_order(b, d, n_ctx_done):
        ctx0 = B * LAT_CHUNKS + b * CTX_CHUNKS
        lat0 = b * LAT_CHUNKS
        if d == 0:
            return (lambda i: ctx0 + i), (lambda i: lat0 + i)
        return (lambda i: ctx0 + CTX_CHUNKS - 1 - i), (lambda i: lat0 + LAT_CHUNKS - 1 - i)

    chains = [(b, d) for b in range(B) for d in range(2)]
    zero = jnp.zeros((1, S5_NSTATE), F32)
    init = tuple((zero, zero) for _ in chains)

    def phase(n_steps, which, carry):
        def body(i, hs):
            out = []
            for (b, d), h in zip(chains, hs):
                order = chain_order(b, d, 0)[which]
                out.append(step(b, d, order(i), h))
            return tuple(out)
        return lax.fori_loop(0, n_steps, body, carry)

    carry = phase(CTX_CHUNKS, 0, init)
    phase(LAT_CHUNKS, 1, carry)


def _s5_scan(s_rows, lam_p):
    return pl.pallas_call(
        _s5_scan_kernel,
        out_shape=jax.ShapeDtypeStruct((N_CHUNK, 4 * S5_NSTATE), F32),
        compiler_params=pltpu.CompilerParams(vmem_limit_bytes=48 << 20),
        name="s5_scan",
    )(s_rows, lam_p)


S5_TM = N_CHUNK // 2
S5_TPN = S5_TN // S5_W


def _s5_out_kernel(h_ref, w_ref, y_ref, *rest):
    o_refs, hb_ref = rest[:-1], rest[-1]
    j = pl.program_id(1)

    @pl.when(j == 0)
    def _():
        hb_ref[...] = h_ref[...].astype(BF16)

    acc = jnp.dot(hb_ref[...], w_ref[...], preferred_element_type=F32) + y_ref[...]
    for tt in range(S5_TPN):
        for h, o_ref in enumerate(o_refs):
            o_ref[pl.ds(j * S5_TPN + tt, S5_TM, stride=S5_CH), :] = (
                acc[:, tt * S5_W + h * LANE:tt * S5_W + (h + 1) * LANE])


def _s5_out(h_rows, wc, y_in):
    nn = S5_ROWW // S5_TN
    n_out = S5_W // LANE
    return pl.pallas_call(
        _s5_out_kernel,
        grid=(N_CHUNK // S5_TM, nn),
        in_specs=[pl.BlockSpec((S5_TM, 4 * S5_NSTATE), lambda i, j: (i, 0), pipeline_mode=pl.Buffered(1)),
                  pl.BlockSpec((4 * S5_NSTATE, S5_TN), lambda i, j: (0, j)),
                  pl.BlockSpec((S5_TM, S5_TN), lambda i, j: (i, j))],
        out_specs=[pl.BlockSpec((S5_TM * S5_CH, LANE), lambda i, j: (i, 0))] * n_out,
        out_shape=[jax.ShapeDtypeStruct((T_ALL, LANE), F32)] * n_out,
        scratch_shapes=[pltpu.VMEM((S5_TM, 4 * S5_NSTATE), BF16)],
        compiler_params=_cp(52 << 20, 2),
        name="s5_out",
    )(h_rows, wc, y_in)


def _rms(x):
    return x * lax.rsqrt(jnp.mean(x * x, axis=-1, keepdims=True) + LN_EPS)


def _layer_norm(x, g, b):
    mu = jnp.mean(x, axis=-1, keepdims=True)
    xc = x - mu
    var = jnp.mean(xc * xc, axis=-1, keepdims=True)
    return xc * lax.rsqrt(var + LN_EPS) * g + b


def _merge_kernel(x_ref, attn_ref, conv_ref, hu_ref, x0_ref, s5a_ref, s5b_ref, g1_ref, sh2_ref, sc2_ref, mixg_ref,
                  hyd_ref, wglu_ref, wout_ref, lng_ref, lnb_ref, o_ref, u2_ref, u2p_ref):
    hy = (conv_ref[...] + hu_ref[...] * hyd_ref[...]) * x0_ref[...]
    g = jax.nn.gelu(jnp.concatenate([s5a_ref[...], s5b_ref[...]], axis=-1))
    s5 = g * jax.nn.sigmoid(jnp.dot(g.astype(BF16), wglu_ref[...], preferred_element_type=F32))
    mixg = mixg_ref[...]
    parts = [_rms(attn_ref[...]) * mixg[:, 0:ATTN_W],
             _rms(hy) * mixg[:, ATTN_W:ATTN_W + HY_W],
             _rms(s5) * mixg[:, ATTN_W + HY_W:MIX_W]]
    mix = jnp.concatenate(parts, axis=-1).astype(BF16)
    o = jnp.dot(mix, wout_ref[...], preferred_element_type=F32)
    x1 = _layer_norm(ALPHA * x_ref[...] + g1_ref[...] * o, lng_ref[...], lnb_ref[...])
    o_ref[...] = x1
    u2 = x1 * (1.0 + sc2_ref[...]) + sh2_ref[...]
    u2_ref[...] = u2
    u2p_ref[...] = _pack_pairs(u2)


def _merge(n_rows, xall, attn, conv, hu, x0c, s5y, g1, sh2, sc2, mix_g, hy_d, wglu_bf, wout_bf, ln_g, ln_b):
    nt = n_rows // TM
    row = lambda i: (i, 0)
    full = lambda i: (0, 0)
    return pl.pallas_call(
        _merge_kernel,
        grid=(nt,),
        in_specs=[pl.BlockSpec((TM, D), row), pl.BlockSpec((TM, ATTN_W), row), pl.BlockSpec((TM, HY_W), row),
                  pl.BlockSpec((TM, HY_W), row), pl.BlockSpec((TM, HY_W), row),
                  pl.BlockSpec((TM, LANE), row), pl.BlockSpec((TM, LANE), row),
                  _vec_spec(TM), _vec_spec(TM), _vec_spec(TM), pl.BlockSpec((1, MIX_W), full),
                  pl.BlockSpec((1, HY_W), full), pl.BlockSpec((S5_W, S5_W), full), pl.BlockSpec((MIX_W, D), full),
                  pl.BlockSpec((1, D), full), pl.BlockSpec((1, D), full)],
        out_specs=[pl.BlockSpec((TM, D), row), pl.BlockSpec((TM, D), row), pl.BlockSpec((TM, HALF_D), row)],
        out_shape=[jax.ShapeDtypeStruct((n_rows, D), F32), jax.ShapeDtypeStruct((n_rows, D), F32),
                   jax.ShapeDtypeStruct((n_rows, HALF_D), jnp.int32)],
        compiler_params=_cp(48 << 20, 1),
        name="merge",
    )(xall, attn, conv, hu, x0c, s5y[0], s5y[1], g1, sh2, sc2, mix_g.reshape(1, MIX_W), hy_d.reshape(1, HY_W), wglu_bf,
      wout_bf, ln_g.reshape(1, D), ln_b.reshape(1, D))


def _router_kernel(u_ref, wt_ref, b_ref, e_ref, g_ref):
    logits = lax.dot_general(wt_ref[...], u_ref[...], (((1,), (1,)), ((), ())), precision=HIGHEST,
                             preferred_element_type=F32)
    scores = jax.nn.sigmoid(logits)
    biased = scores + b_ref[...]
    ninf = -jnp.inf
    grow = lax.broadcasted_iota(jnp.int32, (EGROUP, TM), 0)
    groups = [biased[gi * EGROUP:(gi + 1) * EGROUP] for gi in range(N_EGROUPS)]
    gscore = []
    for vals in groups:
        m1 = jnp.max(vals, axis=0, keepdims=True)
        i1 = jnp.min(jnp.where(vals == m1, grow, EGROUP), axis=0, keepdims=True)
        m2 = jnp.max(jnp.where(grow == i1, ninf, vals), axis=0, keepdims=True)
        gscore.append(m1 + m2)
    kept = []
    for gi in range(N_EGROUPS):
        rank = jnp.zeros((1, TM), jnp.int32)
        for gj in range(N_EGROUPS):
            if gj == gi:
                continue
            ahead = (gscore[gj] > gscore[gi]) | ((gscore[gj] == gscore[gi]) & (gj < gi))
            rank = rank + ahead.astype(jnp.int32)
        kept.append(jnp.where(rank < TOPK_GROUPS, groups[gi], ninf))
    masked = jnp.concatenate(kept, axis=0)
    row = lax.broadcasted_iota(jnp.int32, (N_EXPERTS, TM), 0)
    gates = []
    gsum = jnp.zeros((1, TM), F32)
    for kk in range(TOP_K):
        m = jnp.max(masked, axis=0, keepdims=True)
        idx = jnp.min(jnp.where(masked == m, row, N_EXPERTS), axis=0, keepdims=True)
        hit = row == idx
        gate = jnp.sum(jnp.where(hit, scores, 0.0), axis=0, keepdims=True)
        masked = jnp.where(hit, ninf, masked)
        e_ref[kk:kk + 1, :] = idx
        gates.append(gate)
        gsum = gsum + gate
    for kk in range(TOP_K):
        g_ref[kk:kk + 1, :] = gates[kk] / gsum * ROUTED_SCALE


def _router(n_rows, u2, w_router_t, router_bias):
    nt = n_rows // TM
    col = lambda i: (0, i)
    return pl.pallas_call(
        _router_kernel,
        grid=(nt,),
        in_specs=[pl.BlockSpec((TM, D), lambda i: (i, 0)),
                  pl.BlockSpec((N_EXPERTS, D), lambda i: (0, 0)), pl.BlockSpec((N_EXPERTS, 1), lambda i: (0, 0))],
        out_specs=[pl.BlockSpec((TOP_K, TM), col), pl.BlockSpec((TOP_K, TM), col)],
        out_shape=[jax.ShapeDtypeStruct((TOP_K, n_rows), jnp.int32), jax.ShapeDtypeStruct((TOP_K, n_rows), F32)],
        compiler_params=_cp(32 << 20, 1),
        name="router",
    )(u2, w_router_t, router_bias.reshape(N_EXPERTS, 1))


def _dispatch(top_e):
    t = top_e.shape[1]
    tk = t * TOP_K
    nblk = tk // MOE_BLOCK
    n_steps = nblk + N_EXPERTS
    flat_e = top_e.reshape(tk)
    pos = jnp.arange(tk, dtype=jnp.int32)
    pos_bits = (tk - 1).bit_length()
    assert pos_bits + (N_EXPERTS - 1).bit_length() < 32
    order = lax.sort(flat_e * (1 << pos_bits) + pos) % (1 << pos_bits)
    _, inv = lax.sort((order, pos), num_keys=1)
    experts = jnp.arange(N_EXPERTS, dtype=jnp.int32)
    counts = jnp.sum((flat_e[None, :] == experts[:, None]).astype(jnp.int32), axis=1)
    ends = jnp.cumsum(counts)
    starts = ends - counts
    fb = starts // MOE_BLOCK
    npairs = jnp.where(counts > 0, (ends - 1) // MOE_BLOCK - fb + 1, 0)
    pend = jnp.cumsum(npairs)
    poff = pend - npairs
    n_pairs = pend[-1]
    s = jnp.arange(n_steps, dtype=jnp.int32)
    sc = jnp.minimum(s, n_pairs - 1)
    pe = jnp.sum((pend[None, :] <= sc[:, None]).astype(jnp.int32), axis=1)
    pb = fb[pe] + (sc - poff[pe])
    lo = jnp.where(s < n_pairs, jnp.maximum(starts[pe] - pb * MOE_BLOCK, 0), 0)
    hi = jnp.where(s < n_pairs, jnp.minimum(ends[pe] - pb * MOE_BLOCK, MOE_BLOCK), 0)
    last_of_block = ((s == n_pairs - 1) | ((s + 1 < n_pairs) & (jnp.roll(pb, -1) != pb))).astype(jnp.int32)
    used = counts > 0
    ordinal = jnp.cumsum(used.astype(jnp.int32)) - 1
    later = jnp.where(used, experts, N_EXPERTS)
    next_used = lax.cummin(jnp.concatenate([later[1:], jnp.full((1,), N_EXPERTS, jnp.int32)]), reverse=True)
    nxt = next_used[pe]
    nxt = jnp.where(nxt < N_EXPERTS, nxt, pe)
    parity = ordinal[pe] % 2
    w_even = jnp.where(parity == 0, pe, nxt)
    w_odd = jnp.where(parity == 1, pe, nxt)
    return dict(pe=pe, pb=pb, lo=lo, hi=hi, last=last_of_block, parity=parity, w_even=w_even, w_odd=w_odd,
                tok=order % t, comb=inv)


def _expert_kernel(pe, pb, plo, phi, plast, ppar, pwe, pwo, x_ref, wg0_ref, wu0_ref, wd0_ref, wg1_ref, wu1_ref,
                   wd1_ref, out_hbm, obuf, osem, wgb, wub, wdb):
    del pwe, pwo
    s = pl.program_id(0)
    prev = jnp.maximum(s - 1, 0)
    n_blocks = out_hbm.shape[0] // MOE_BLOCK
    new_expert = (s == 0) | (pe[s] != pe[prev])

    for par, (wg_ref, wu_ref, wd_ref) in enumerate(((wg0_ref, wu0_ref, wd0_ref), (wg1_ref, wu1_ref, wd1_ref))):
        @pl.when(new_expert & (ppar[s] == par))
        def _():
            wgb[...] = wg_ref[...].astype(BF16)
            wub[...] = wu_ref[...].astype(BF16)
            wdb[...] = wd_ref[...].astype(BF16)

    lo = plo[s]
    hi = phi[s]
    blk = pb[s]
    slot = blk % 2
    first_of_block = (s == 0) | (blk != pb[prev])

    def out_copy(b, sl):
        rows = pl.ds(pl.multiple_of(b * MOE_BLOCK, MOE_BLOCK), MOE_BLOCK)
        return pltpu.make_async_copy(obuf.at[sl], out_hbm.at[rows], osem.at[sl])

    def ffn():
        x_lo, x_hi = _unpack_pairs(x_ref[...])
        x_lo = x_lo.astype(BF16)
        x_hi = x_hi.astype(BF16)
        hg = (jnp.dot(x_lo, wgb[0:HALF_D], preferred_element_type=F32)
              + jnp.dot(x_hi, wgb[HALF_D:D], preferred_element_type=F32))
        hu = (jnp.dot(x_lo, wub[0:HALF_D], preferred_element_type=F32)
              + jnp.dot(x_hi, wub[HALF_D:D], preferred_element_type=F32))
        h = (hg * jax.nn.sigmoid(hg)) * hu
        return _pack_pairs(jnp.dot(h.astype(BF16), wdb[...], preferred_element_type=F32))

    @pl.when((hi > lo) & first_of_block)
    def _():
        @pl.when(blk >= 2)
        def _():
            out_copy(blk - 2, slot).wait()
        obuf[slot] = ffn()

    @pl.when((hi > lo) & jnp.logical_not(first_of_block))
    def _():
        row = lax.broadcasted_iota(jnp.int32, (MOE_BLOCK, HALF_D), 0)
        obuf[slot] = jnp.where((row >= lo) & (row < hi), ffn(), obuf[slot])

    @pl.when(plast[s] == 1)
    def _():
        out_copy(blk, slot).start()

    @pl.when(s == pl.num_programs(0) - 1)
    def _():
        out_copy(n_blocks - 2, (n_blocks - 2) % 2).wait()
        out_copy(n_blocks - 1, (n_blocks - 1) % 2).wait()


def _experts(xs, disp, wg, wu, wd, layer):
    tk = xs.shape[0]
    n_steps = disp['pe'].shape[0]
    assert tk // MOE_BLOCK >= 2
    blk = lambda s, pe, pb, *_: (pb[s], 0)
    weven = lambda shape: pl.BlockSpec((None, None) + shape, lambda s, *p: (layer, p[6][s], 0, 0))
    wodd = lambda shape: pl.BlockSpec((None, None) + shape, lambda s, *p: (layer, p[7][s], 0, 0))
    grid_spec = pltpu.PrefetchScalarGridSpec(
        num_scalar_prefetch=8,
        grid=(n_steps,),
        in_specs=[pl.BlockSpec((MOE_BLOCK, HALF_D), blk),
                  weven((D, EXPERT_FF)), weven((D, EXPERT_FF)), weven((EXPERT_FF, D)),
                  wodd((D, EXPERT_FF)), wodd((D, EXPERT_FF)), wodd((EXPERT_FF, D))],
        out_specs=pl.BlockSpec(memory_space=pl.ANY),
        scratch_shapes=[pltpu.VMEM((2, MOE_BLOCK, HALF_D), jnp.int32), pltpu.SemaphoreType.DMA((2,)),
                        pltpu.VMEM((D, EXPERT_FF), BF16), pltpu.VMEM((D, EXPERT_FF), BF16),
                        pltpu.VMEM((EXPERT_FF, D), BF16)],
    )
    return pl.pallas_call(
        _expert_kernel,
        grid_spec=grid_spec,
        out_shape=jax.ShapeDtypeStruct((tk, HALF_D), jnp.int32),
        compiler_params=_cp(40 << 20, 1),
        name="experts",
    )(disp['pe'], disp['pb'], disp['lo'], disp['hi'], disp['last'], disp['parity'], disp['w_even'], disp['w_odd'],
      xs, wg, wu, wd, wg, wu, wd)


SC_ROWS = 64


def _row_gather(table, idx):
    n = idx.shape[0]
    d = table.shape[1]
    mesh = plsc.VectorSubcoreMesh(core_axis_name="c", subcore_axis_name="s")
    n_workers = mesh.num_cores * mesh.num_subcores
    per_worker = n // n_workers
    assert per_worker * n_workers == n and per_worker % SC_ROWS == 0

    @functools.partial(
        pl.kernel, mesh=mesh,
        out_type=jax.ShapeDtypeStruct((n, d), table.dtype),
        scratch_types=[pltpu.VMEM((SC_ROWS,), jnp.int32), pltpu.VMEM((SC_ROWS, d), table.dtype),
                       pltpu.SemaphoreType.DMA],
    )
    def gather(table_hbm, idx_hbm, out_hbm, idx_v, rows_v, sem):
        worker = lax.axis_index("s") * mesh.num_cores + lax.axis_index("c")
        base = worker * per_worker

        @pl.loop(0, per_worker // SC_ROWS)
        def _(c):
            off = pl.multiple_of(base + c * SC_ROWS, 8)
            pltpu.sync_copy(idx_hbm.at[pl.ds(off, SC_ROWS)], idx_v)
            pltpu.async_copy(table_hbm.at[idx_v], rows_v, sem).wait()
            pltpu.sync_copy(rows_v, out_hbm.at[pl.ds(off, SC_ROWS)])

    return gather(table, idx)


FM = 128


def _ffn_out_kernel(x_ref, u_ref, r_ref, gate_ref, g2_ref, wsg_ref, wsu_ref, wsd_ref, lng_ref, lnb_ref, o_ref):
    u_lo, u_hi = _unpack_pairs(u_ref[...])
    u_lo = u_lo.astype(BF16)
    u_hi = u_hi.astype(BF16)
    hg = (jnp.dot(u_lo, wsg_ref[0:HALF_D], preferred_element_type=F32)
          + jnp.dot(u_hi, wsg_ref[HALF_D:D], preferred_element_type=F32))
    hu = (jnp.dot(u_lo, wsu_ref[0:HALF_D], preferred_element_type=F32)
          + jnp.dot(u_hi, wsu_ref[HALF_D:D], preferred_element_type=F32))
    f = jnp.dot(((hg * jax.nn.sigmoid(hg)) * hu).astype(BF16), wsd_ref[...], preferred_element_type=F32)
    gates = jnp.transpose(jnp.concatenate([gate_ref[...], jnp.zeros((FM - TOP_K, FM), F32)], axis=0))
    r_lo = jnp.zeros((FM, HALF_D), F32)
    r_hi = jnp.zeros((FM, HALF_D), F32)
    for kk in range(TOP_K):
        k_lo, k_hi = _unpack_pairs(r_ref[kk])
        gk = gates[:, kk:kk + 1]
        r_lo = r_lo + k_lo * gk
        r_hi = r_hi + k_hi * gk
    f = jnp.concatenate([r_lo, r_hi], axis=1) + f
    o_ref[...] = _layer_norm(ALPHA * x_ref[...] + g2_ref[...] * f, lng_ref[...], lnb_ref[...])


def _ffn_out(n_rows, x1, u2p, routed, gate, g2, wsg_bf, wsu_bf, wsd_bf, ln_g, ln_b):
    row = lambda i: (i, 0)
    full = lambda i: (0, 0)
    return pl.pallas_call(
        _ffn_out_kernel,
        grid=(n_rows // FM,),
        in_specs=[pl.BlockSpec((FM, D), row), pl.BlockSpec((FM, HALF_D), row),
                  pl.BlockSpec((TOP_K, FM, HALF_D), lambda i: (0, i, 0)), pl.BlockSpec((TOP_K, FM), lambda i: (0, i)),
                  _vec_spec(FM), pl.BlockSpec((D, EXPERT_FF), full), pl.BlockSpec((D, EXPERT_FF), full),
                  pl.BlockSpec((EXPERT_FF, D), full), pl.BlockSpec((1, D), full), pl.BlockSpec((1, D), full)],
        out_specs=pl.BlockSpec((FM, D), row),
        out_shape=jax.ShapeDtypeStruct((n_rows, D), F32),
        compiler_params=_cp(40 << 20, 1),
        name="ffn_out",
    )(x1, u2p, routed, gate, g2, wsg_bf, wsu_bf, wsd_bf, ln_g.reshape(1, D), ln_b.reshape(1, D))


def _rope_tables():
    t = jnp.arange(L, dtype=jnp.int32)
    row = (t // GRID_W).astype(F32)
    col = (t % GRID_W).astype(F32)
    inv_freq = ROPE_BASE ** (-jnp.arange(0, AXIS_DIM, 2, dtype=F32) / AXIS_DIM)
    half = AXIS_DIM // 2

    def axis(pos):
        ang = pos[:, None] * inv_freq[None, :]
        c = jnp.cos(ang)
        s = jnp.sin(ang)
        return jnp.concatenate([c, c], axis=1), jnp.concatenate([-s, s], axis=1)

    cr, sr = axis(row)
    cc, sc = axis(col)
    cos_h = jnp.concatenate([cr, cc], axis=1)
    sin_h = jnp.concatenate([sr, sc], axis=1)
    cos_l = jnp.tile(cos_h, (B, LANE // HEAD_DIM))
    sin_l = jnp.tile(sin_h, (B, LANE // HEAD_DIM))
    cos_t = jnp.concatenate([cos_l, jnp.ones((T_CTX, LANE), F32)], axis=0)
    sin_t = jnp.concatenate([sin_l, jnp.zeros((T_CTX, LANE), F32)], axis=0)
    del half
    return cos_t, sin_t


def kernel(x, c, ctx, c_ctx, w_ada, b_ada, w_in, w_out, sink, mix_g, hy_short_w, hy_short_b, hy_w1, hy_b1, hy_freq,
           hy_w2, hy_b2, hy_w3, hy_d, s5_a_re, s5_a_im, s5_log_dt, s5_b_re, s5_b_im, s5_c_re, s5_c_im, s5_d, s5_w_glu,
           ln1_g, ln1_b, ln2_g, ln2_b, w_router, router_bias, w_exp_gate, w_exp_up, w_exp_down, w_sh_gate, w_sh_up,
           w_sh_down):
    xall = jnp.concatenate([x.reshape(T_LAT, D), ctx.reshape(T_CTX, D)], axis=0)
    cvec = jnp.concatenate([c, c_ctx[None, :], jnp.zeros((8 - B - 1, D), F32)], axis=0)
    mod = _ada(cvec, w_ada, b_ada)[:, 0:B + 1, :].reshape(DEPTH, B + 1, 6, 1, D)

    cos_t, sin_t = _rope_tables()
    g_cos, g_sin, f_cos, f_sin = _dft_tables()
    g_cos_bf, g_sin_bf = g_cos.astype(BF16), g_sin.astype(BF16)
    i_cos_bf = jnp.swapaxes(g_cos, 1, 2).astype(BF16)
    i_sin_bf = jnp.swapaxes(g_sin, 1, 2).astype(BF16)
    f_cos_bf, f_sin_bf = f_cos.astype(BF16), f_sin.astype(BF16)
    k1 = jnp.arange(K1P)
    spec_w = jnp.where((k1 == 0) | (k1 == FFT_R // 2), 1.0, 2.0) * (k1 < K1_USED) / N_FFT
    spec_w = spec_w.astype(F32)
    kt = jnp.arange(2 * C, dtype=jnp.int32)
    ang_c = ((kt[:, None] * kt[None, :]) % (2 * C)).astype(F32) * (2.0 * math.pi / (2 * C))
    d_cos, d_sin = jnp.cos(ang_c), jnp.sin(ang_c)

    for l in range(DEPTH):
        last = l == DEPTH - 1
        n_rows = T_LAT if last else T_ALL
        sh1, sc1, g1, sh2, sc2, g2 = (mod[l, :, j] for j in range(6))

        q, k, v, hz, s5u = _inproj(xall, sh1, sc1, w_in[l].astype(BF16), cos_t, sin_t)
        attn = _attention(sink[l], q, k, v, with_ctx=not last)

        hu, x0c = _hyena_pre(hz, hy_short_w[l], hy_short_b[l])
        filt_args = (hy_w1[l], hy_b1[l], hy_freq[l], hy_w2[l], hy_b2[l], hy_w3[l])
        k_r, k_i = _hyena_spectrum(_hyena_filter(L, *filt_args), g_cos, g_sin, f_cos, f_sin, spec_w)
        conv = _hyena_fft(hu, g_cos_bf, g_sin_bf, i_cos_bf, i_sin_bf, f_cos_bf, f_sin_bf, k_r, k_i)
        if not last:
            conv = jnp.concatenate([conv, _hyena_ctx(hu, _hyena_filter(C, *filt_args), d_cos, d_sin)], axis=0)

        ws, e_all, wc, lam_p = _s5_matrices(s5_a_re[l], s5_a_im[l], s5_log_dt[l], s5_b_re[l], s5_b_im[l],
                                            s5_c_re[l], s5_c_im[l], s5_d[l])
        s_rows, y_in = _s5_in(s5u, ws, e_all)
        s5y = _s5_out(_s5_scan(s_rows, lam_p), wc, y_in)

        x1, u2, u2p = _merge(n_rows, xall, attn, conv, hu, x0c, s5y, g1, sh2, sc2, mix_g[l], hy_d[l],
                             s5_w_glu[l].astype(BF16), w_out[l].astype(BF16), ln1_g[l], ln1_b[l])

        top_e, gate = _router(n_rows, u2, w_router[l].T, router_bias[l])
        disp = _dispatch(top_e)
        ys = _experts(_row_gather(u2p, disp['tok']), disp, w_exp_gate, w_exp_up, w_exp_down, l)
        routed = _row_gather(ys, disp['comb']).reshape(TOP_K, n_rows, HALF_D)
        xall = _ffn_out(n_rows, x1, u2p, routed, gate, g2, w_sh_gate[l].astype(BF16), w_sh_up[l].astype(BF16),
                        w_sh_down[l].astype(BF16), ln2_g[l], ln2_b[l])
    return xall.reshape(B, L, D)
```

```python
import functools
import math

import jax
import jax.numpy as jnp
from jax import lax
from jax.experimental import pallas as pl
from jax.experimental.pallas import tpu as pltpu
from jax.experimental.pallas import tpu_sc as plsc

F32 = jnp.float32
BF16 = jnp.bfloat16
HIGHEST = lax.Precision.HIGHEST

D = 1024
B = 2
L = 8192
DEPTH = 2
GRID_W = 64
C = 256
T_LAT = B * L
T_CTX = B * C
T_ALL = T_LAT + T_CTX

HEAD_DIM = 64
N_Q = 8
N_KV = 2
Q_GROUP = N_Q // N_KV
ATTN_W = N_Q * HEAD_DIM
KV_W = N_KV * HEAD_DIM
HY_W = 256
S5_W = 256
MIX_W = ATTN_W + HY_W + S5_W
K_OFF = ATTN_W
V_OFF = K_OFF + KV_W
HY_OFF = V_OFF + KV_W
S5_OFF = HY_OFF + 3 * HY_W
IN_W = S5_OFF + S5_W
WINDOW = 128
BLK = 128
NEG_INF = -1e30
ROPE_BASE = 10000.0
AXIS_DIM = HEAD_DIM // 2

SHORT_K = 3
FILTER_EMB = 33
DECAY_FAST = 0.3
DECAY_SLOW = 1.5
DECAY_TARGET = 1e-2

S5_GROUP = 16
S5_GROUPS = S5_W // S5_GROUP
S5_STATE = 64
S5_NSTATE = S5_GROUPS * S5_STATE
S5_CH = 16
S5_ROWW = S5_CH * S5_W
N_CHUNK = T_ALL // S5_CH
LAT_CHUNKS = L // S5_CH
CTX_CHUNKS = C // S5_CH

N_EXPERTS = 256
TOP_K = 8
N_EGROUPS = 8
EGROUP = N_EXPERTS // N_EGROUPS
TOPK_GROUPS = 4
EXPERT_FF = 256
ROUTED_SCALE = 2.5
MOE_BLOCK = 256

ALPHA = (2 * DEPTH) ** 0.25
LN_EPS = 1e-5

N_FFT = 2 * L
FFT_R = 128
FFT_T1 = L // FFT_R
K1_USED = FFT_R // 2 + 1
K1P = 80
K1H = K1P // 2
FFT_UNROLL = 4

TM = 256
LANE = 128
VMEM_CAP = 60000 * 1024


def _cp(vmem_bytes, n_axes):
    return pltpu.CompilerParams(
        dimension_semantics=("arbitrary",) * n_axes if n_axes else None,
        vmem_limit_bytes=min(int(vmem_bytes), VMEM_CAP),
    )


HALF_D = D // 2
HIGH_HALF_WORD = 0xFFFF0000


def _pack_pairs(x):
    bits = lax.bitcast_convert_type(x.astype(BF16).astype(F32), jnp.uint32)
    packed = (bits[:, 0:HALF_D] >> 16) | (bits[:, HALF_D:D] & jnp.uint32(HIGH_HALF_WORD))
    return lax.bitcast_convert_type(packed, jnp.int32)


def _unpack_pairs(p):
    bits = lax.bitcast_convert_type(p, jnp.uint32)
    low = lax.bitcast_convert_type(bits << 16, F32)
    high = lax.bitcast_convert_type(bits & jnp.uint32(HIGH_HALF_WORD), F32)
    return low, high


def _mod_sel(rows_per_tile):
    per_batch = L // rows_per_tile
    return lambda i: jnp.minimum(i // per_batch, 2)


def _vec_spec(rows_per_tile):
    sel = _mod_sel(rows_per_tile)
    return pl.BlockSpec((None, 1, D), lambda i: (sel(i), 0, 0))


ADA_TN = 1536


def _ada_kernel(c_ref, w_ref, b_ref, o_ref):
    c = c_ref[...]
    s = c * jax.nn.sigmoid(c)
    o_ref[...] = jnp.dot(s, w_ref[...], precision=HIGHEST, preferred_element_type=F32) + b_ref[...]


def _ada(cvec, w_ada, b_ada):
    return pl.pallas_call(
        _ada_kernel,
        grid=(DEPTH, 6 * D // ADA_TN),
        in_specs=[
            pl.BlockSpec((8, D), lambda l, j: (0, 0)),
            pl.BlockSpec((None, D, ADA_TN), lambda l, j: (l, 0, j)),
            pl.BlockSpec((None, 1, ADA_TN), lambda l, j: (l, 0, j)),
        ],
        out_specs=pl.BlockSpec((None, 8, ADA_TN), lambda l, j: (l, 0, j)),
        out_shape=jax.ShapeDtypeStruct((DEPTH, 8, 6 * D), F32),
        compiler_params=_cp(40 << 20, 2),
        name="ada",
    )(cvec, w_ada, b_ada.reshape(DEPTH, 1, 6 * D))


def _inproj_kernel(x_ref, sh_ref, sc_ref, w_ref, cos_ref, sin_ref, q_ref, k_ref, v_ref, hy_ref, s5_ref, *s5_tok_refs):
    u = x_ref[...] * (1.0 + sc_ref[...]) + sh_ref[...]
    proj = jnp.dot(u.astype(BF16), w_ref[...], preferred_element_type=F32)
    cos = cos_ref[...]
    sin = sin_ref[...]
    lane = lax.broadcasted_iota(jnp.int32, (TM, LANE), 1)
    first_half = (lane % AXIS_DIM) < (AXIS_DIM // 2)

    def rope(xc):
        partner = jnp.where(first_half, pltpu.roll(xc, LANE - AXIS_DIM // 2, 1), pltpu.roll(xc, AXIS_DIM // 2, 1))
        return xc * cos + partner * sin

    for j in range(ATTN_W // LANE):
        q_ref[:, j * LANE:(j + 1) * LANE] = rope(proj[:, j * LANE:(j + 1) * LANE]).astype(BF16)
    k_ref[...] = rope(proj[:, K_OFF:V_OFF]).astype(BF16)
    v_ref[...] = proj[:, V_OFF:HY_OFF].astype(BF16)
    hy_ref[...] = proj[:, HY_OFF:S5_OFF]
    for h, tok_ref in enumerate(s5_tok_refs):
        tok_ref[...] = proj[:, S5_OFF + h * LANE:S5_OFF + (h + 1) * LANE]
    for s in range(S5_CH):
        for h, tok_ref in enumerate(s5_tok_refs):
            s5_ref[:, s * S5_W + h * LANE:s * S5_W + (h + 1) * LANE] = (
                tok_ref[pl.ds(s, TM // S5_CH, stride=S5_CH), :].astype(BF16))


def _inproj(xall, sh, sc, w_in_bf, cos_t, sin_t):
    nt = T_ALL // TM
    row = lambda i: (i, 0)
    return pl.pallas_call(
        _inproj_kernel,
        grid=(nt,),
        in_specs=[
            pl.BlockSpec((TM, D), row),
            _vec_spec(TM),
            _vec_spec(TM),
            pl.BlockSpec((D, IN_W), lambda i: (0, 0)),
            pl.BlockSpec((TM, LANE), row),
            pl.BlockSpec((TM, LANE), row),
        ],
        out_specs=[
            pl.BlockSpec((TM, ATTN_W), row),
            pl.BlockSpec((TM, KV_W), row),
            pl.BlockSpec((TM, KV_W), row),
            pl.BlockSpec((TM, 3 * HY_W), row),
            pl.BlockSpec((TM // S5_CH, S5_ROWW), row),
        ],
        out_shape=[
            jax.ShapeDtypeStruct((T_ALL, ATTN_W), BF16),
            jax.ShapeDtypeStruct((T_ALL, KV_W), BF16),
            jax.ShapeDtypeStruct((T_ALL, KV_W), BF16),
            jax.ShapeDtypeStruct((T_ALL, 3 * HY_W), F32),
            jax.ShapeDtypeStruct((N_CHUNK, S5_ROWW), BF16),
        ],
        scratch_shapes=[pltpu.VMEM((TM, LANE), F32)] * (S5_W // LANE),
        compiler_params=_cp(40 << 20, 1),
        name="inproj",
    )(xall, sh, sc, w_in_bf, cos_t, sin_t)


NB_LAT = L // BLK
NB_CTX = C // BLK


def _nt_dot(a, b):
    return lax.dot_general(a, b, (((1,), (1,)), ((), ())), preferred_element_type=F32)


def _attn_kernel(sink_ref, q_ref, kp_ref, kc_ref, kn_ref, kx_ref, vp_ref, vc_ref, vn_ref, vx_ref, o_ref):
    n = pl.program_id(1)
    is_lat = n < NB_LAT
    rows = Q_GROUP * BLK
    r = lax.broadcasted_iota(jnp.int32, (rows, BLK), 0) % BLK
    j = lax.broadcasted_iota(jnp.int32, (rows, BLK), 1)
    ok_prev = (j >= r) & (n >= 1) & is_lat
    ok_next = (j <= r) & (n + 1 < NB_LAT) & is_lat
    head_of_row = lax.broadcasted_iota(jnp.int32, (rows, 1), 0) // BLK
    q = q_ref[...] * (HEAD_DIM ** -0.5)
    dot = lambda a, b: jnp.dot(a.astype(BF16), b, preferred_element_type=F32)
    for kh in range(N_KV):
        hs = slice(kh * HEAD_DIM, (kh + 1) * HEAD_DIM)
        heads = range(kh * Q_GROUP, (kh + 1) * Q_GROUP)
        qg = jnp.concatenate([q[:, h * HEAD_DIM:(h + 1) * HEAD_DIM] for h in heads], axis=0)
        sk = jnp.zeros((rows, 1), F32)
        for g, h in enumerate(heads):
            sk = jnp.where(head_of_row == g, sink_ref[h], sk)
        s_p = jnp.where(ok_prev, _nt_dot(qg, kp_ref[:, hs]), NEG_INF)
        s_c = jnp.where(is_lat, _nt_dot(qg, kc_ref[:, hs]), NEG_INF)
        s_n = jnp.where(ok_next, _nt_dot(qg, kn_ref[:, hs]), NEG_INF)
        s_x = _nt_dot(qg, kx_ref[:, hs])
        s_x0 = s_x[:, 0:BLK]
        s_x1 = s_x[:, BLK:2 * BLK]
        m = jnp.maximum(jnp.maximum(jnp.maximum(s_p, s_c), jnp.maximum(s_n, s_x0)), s_x1)
        m = jnp.maximum(jnp.max(m, axis=1, keepdims=True), sk)
        e_p = jnp.exp(s_p - m)
        e_c = jnp.exp(s_c - m)
        e_n = jnp.exp(s_n - m)
        e_x0 = jnp.exp(s_x0 - m)
        e_x1 = jnp.exp(s_x1 - m)
        den = jnp.sum((e_p + e_c) + (e_n + e_x0) + e_x1, axis=1, keepdims=True) + jnp.exp(sk - m)
        o = (dot(e_p, vp_ref[:, hs]) + dot(e_c, vc_ref[:, hs]) + dot(e_n, vn_ref[:, hs])
             + dot(e_x0, vx_ref[0:BLK, hs]) + dot(e_x1, vx_ref[BLK:2 * BLK, hs]))
        o = o / den
        for g, h in enumerate(heads):
            o_ref[:, h * HEAD_DIM:(h + 1) * HEAD_DIM] = o[g * BLK:(g + 1) * BLK]


def _attention(sink, q, k, v, with_ctx):
    nblk = NB_LAT + (NB_CTX if with_ctx else 0)

    def q_idx(b, n):
        return (jnp.where(n < NB_LAT, b * NB_LAT + n, B * NB_LAT + b * NB_CTX + (n - NB_LAT)), 0)

    def kv_idx(off):
        def idx(b, n):
            nn = jnp.clip(jnp.minimum(n, NB_LAT - 1) + off, 0, NB_LAT - 1)
            return (b * NB_LAT + nn, 0)
        return idx

    ctx_idx = lambda b, n: (T_LAT // C + b, 0)
    kv_specs = lambda: [pl.BlockSpec((BLK, KV_W), kv_idx(-1)), pl.BlockSpec((BLK, KV_W), kv_idx(0)),
                        pl.BlockSpec((BLK, KV_W), kv_idx(1)), pl.BlockSpec((C, KV_W), ctx_idx)]
    return pl.pallas_call(
        _attn_kernel,
        grid=(B, nblk),
        in_specs=[pl.BlockSpec(memory_space=pltpu.SMEM), pl.BlockSpec((BLK, ATTN_W), q_idx)] + kv_specs() + kv_specs(),
        out_specs=pl.BlockSpec((BLK, ATTN_W), q_idx),
        out_shape=jax.ShapeDtypeStruct((T_ALL if with_ctx else T_LAT, ATTN_W), F32),
        compiler_params=_cp(32 << 20, 2),
        name="attention",
    )(sink, q, k, k, k, k, v, v, v, v)


def _hyena_pre_kernel(z_ref, zp_ref, zn_ref, w_ref, b_ref, u_ref, x0_ref):
    i = pl.program_id(0)
    tiles_per_seq = L // TM
    is_ctx = i >= B * tiles_per_seq
    first = is_ctx | (i % tiles_per_seq == 0)
    last = is_ctx | (i % tiles_per_seq == tiles_per_seq - 1)
    z = z_ref[...]
    prev_row = jnp.where(first, 0.0, zp_ref[7:8, :])
    next_row = jnp.where(last, 0.0, zn_ref[0:1, :])
    row = lax.broadcasted_iota(jnp.int32, z.shape, 0)
    z_m1 = jnp.where(row == 0, prev_row, pltpu.roll(z, 1, 0))
    z_p1 = jnp.where(row == TM - 1, next_row, pltpu.roll(z, TM - 1, 0))
    zc = b_ref[...] + z_m1 * w_ref[0:1, :] + z * w_ref[1:2, :] + z_p1 * w_ref[2:3, :]
    u_ref[...] = zc[:, 0:HY_W] * zc[:, HY_W:2 * HY_W]
    x0_ref[...] = zc[:, 2 * HY_W:3 * HY_W]


def _hyena_pre(z, short_w, short_b):
    nt = T_ALL // TM
    sub = TM // 8
    n8 = T_ALL // 8
    return pl.pallas_call(
        _hyena_pre_kernel,
        grid=(nt,),
        in_specs=[
            pl.BlockSpec((TM, 3 * HY_W), lambda i: (i, 0)),
            pl.BlockSpec((8, 3 * HY_W), lambda i: (jnp.maximum(i * sub - 1, 0), 0)),
            pl.BlockSpec((8, 3 * HY_W), lambda i: (jnp.minimum((i + 1) * sub, n8 - 1), 0)),
            pl.BlockSpec((SHORT_K, 3 * HY_W), lambda i: (0, 0)),
            pl.BlockSpec((1, 3 * HY_W), lambda i: (0, 0)),
        ],
        out_specs=[pl.BlockSpec((TM, HY_W), lambda i: (i, 0)), pl.BlockSpec((TM, HY_W), lambda i: (i, 0))],
        out_shape=[jax.ShapeDtypeStruct((T_ALL, HY_W), F32), jax.ShapeDtypeStruct((T_ALL, HY_W), F32)],
        compiler_params=_cp(32 << 20, 1),
        name="hyena_pre",
    )(z, z, z, short_w, short_b.reshape(1, 3 * HY_W))


def _dft_tables():
    t0 = jnp.arange(FFT_R, dtype=jnp.int32)[:, None, None]
    k1 = jnp.arange(K1P, dtype=jnp.int32)[None, :, None]
    t1 = jnp.arange(FFT_T1, dtype=jnp.int32)[None, None, :]
    m = (k1 * (FFT_R * t1 + t0)) % N_FFT
    ang = m.astype(F32) * (2.0 * math.pi / N_FFT)
    used = (k1 < K1_USED).astype(F32)
    g_cos = jnp.cos(ang) * used
    g_sin = jnp.sin(ang) * used
    a = jnp.arange(FFT_R, dtype=jnp.int32)
    ang2 = ((a[:, None] * a[None, :]) % FFT_R).astype(F32) * (2.0 * math.pi / FFT_R)
    return g_cos, g_sin, jnp.cos(ang2), jnp.sin(ang2)


def _hyena_spec_kernel(k_ref, gc_ref, gs_ref, fc_ref, fs_ref, wt_ref, kr_ref, ki_ref, ar_ref, ai_ref):
    half = pl.program_id(1)
    kk = lax.broadcasted_iota(jnp.int32, (K1H, 1), 0) + half * K1H
    sign = jnp.where(kk % 2 == 0, 1.0, -1.0).astype(F32)

    dot = lambda a, b: jnp.dot(a, b.astype(BF16), preferred_element_type=F32)

    def stage1(i, carry):
        t0s = [i * FFT_UNROLL + u for u in range(FFT_UNROLL)]
        loaded = [(k_ref[pl.ds(t0, FFT_T1, stride=FFT_R), :], k_ref[pl.ds(L + t0, FFT_T1, stride=FFT_R), :],
                   gc_ref[t0].astype(BF16), gs_ref[t0].astype(BF16)) for t0 in t0s]
        res = [(dot(gc, x_lo) + sign * dot(gc, x_hi), -(dot(gs, x_lo) + sign * dot(gs, x_hi)))
               for x_lo, x_hi, gc, gs in loaded]
        for t0, (a_r, a_i) in zip(t0s, res):
            rows = pl.ds(pl.multiple_of(t0 * K1H, 8), K1H)
            ar_ref[rows, :] = a_r
            ai_ref[rows, :] = a_i
        return carry

    lax.fori_loop(0, FFT_R // FFT_UNROLL, stage1, 0)
    fc = fc_ref[...].astype(BF16)
    fs = fs_ref[...].astype(BF16)

    def stage2(i, carry):
        kls = [i * FFT_UNROLL + u for u in range(FFT_UNROLL)]
        loaded = [(ar_ref[pl.ds(kl, FFT_R, stride=K1H), :], ai_ref[pl.ds(kl, FFT_R, stride=K1H), :]) for kl in kls]
        for kl, (a_r, a_i) in zip(kls, loaded):
            w = wt_ref[half * K1H + kl]
            kr_ref[kl] = (dot(fc, a_r) + dot(fs, a_i)) * w
            ki_ref[kl] = (dot(fc, a_i) - dot(fs, a_r)) * w
        return carry

    lax.fori_loop(0, K1H // FFT_UNROLL, stage2, 0)


def _hyena_spectrum(kfilt, g_cos, g_sin, f_cos, f_sin, wts):
    nct = HY_W // LANE
    gspec = pl.BlockSpec((FFT_R, K1H, FFT_T1), lambda c, h: (0, h, 0))
    fspec = pl.BlockSpec((FFT_R, FFT_R), lambda c, h: (0, 0))
    ospec = pl.BlockSpec((K1H, FFT_R, LANE), lambda c, h: (h, 0, c))
    return pl.pallas_call(
        _hyena_spec_kernel,
        grid=(nct, 2),
        in_specs=[pl.BlockSpec((N_FFT, LANE), lambda c, h: (0, c)), gspec, gspec, fspec, fspec,
                  pl.BlockSpec(memory_space=pltpu.SMEM)],
        out_specs=[ospec, ospec],
        out_shape=[jax.ShapeDtypeStruct((K1P, FFT_R, HY_W), F32)] * 2,
        scratch_shapes=[pltpu.VMEM((FFT_R * K1H, LANE), F32)] * 2,
        compiler_params=_cp(56 << 20, 0),
        name="hyena_spectrum",
    )(kfilt, g_cos, g_sin, f_cos, f_sin, wts)


def _hyena_fft_kernel(u_ref, gc_ref, gs_ref, ic_ref, is_ref, fc_ref, fs_ref, kr_ref, ki_ref, o_ref, ar_ref, ai_ref):
    bdot = lambda a, b: jnp.dot(a, b.astype(BF16), preferred_element_type=F32)

    def stage1(i, carry):
        t0s = [i * FFT_UNROLL + u for u in range(FFT_UNROLL)]
        xs = [u_ref[pl.ds(t0, FFT_T1, stride=FFT_R), :] for t0 in t0s]
        res = [(bdot(gc_ref[t0], x), -bdot(gs_ref[t0], x)) for t0, x in zip(t0s, xs)]
        for t0, (a_r, a_i) in zip(t0s, res):
            rows = pl.ds(pl.multiple_of(t0 * K1P, 8), K1P)
            ar_ref[rows, :] = a_r
            ai_ref[rows, :] = a_i
        return carry

    lax.fori_loop(0, FFT_R // FFT_UNROLL, stage1, 0)
    fc = fc_ref[...]
    fs = fs_ref[...]

    def stage23(i, carry):
        k1s = [i * FFT_UNROLL + u for u in range(FFT_UNROLL)]
        loaded = [(ar_ref[pl.ds(k1, FFT_R, stride=K1P), :], ai_ref[pl.ds(k1, FFT_R, stride=K1P), :],
                   kr_ref[k1], ki_ref[k1]) for k1 in k1s]
        res = []
        for a_r, a_i, k_r, k_i in loaded:
            z_r = bdot(fc, a_r) + bdot(fs, a_i)
            z_i = bdot(fc, a_i) - bdot(fs, a_r)
            y_r = z_r * k_r - z_i * k_i
            y_i = z_r * k_i + z_i * k_r
            res.append((bdot(fc, y_r) - bdot(fs, y_i), bdot(fc, y_i) + bdot(fs, y_r)))
        for k1, (b_r, b_i) in zip(k1s, res):
            ar_ref[pl.ds(k1, FFT_R, stride=K1P), :] = b_r
            ai_ref[pl.ds(k1, FFT_R, stride=K1P), :] = b_i
        return carry

    lax.fori_loop(0, -(-K1_USED // FFT_UNROLL), stage23, 0)

    def stage4(i, carry):
        t0s = [i * FFT_UNROLL + u for u in range(FFT_UNROLL)]
        loaded = []
        for t0 in t0s:
            rows = pl.ds(pl.multiple_of(t0 * K1P, 8), K1P)
            loaded.append((ar_ref[rows, :], ai_ref[rows, :]))
        res = [bdot(ic_ref[t0], b_r) - bdot(is_ref[t0], b_i) for t0, (b_r, b_i) in zip(t0s, loaded)]
        for t0, y in zip(t0s, res):
            o_ref[pl.ds(t0, FFT_T1, stride=FFT_R), :] = y
        return carry

    lax.fori_loop(0, FFT_R // FFT_UNROLL, stage4, 0)


def _hyena_fft(u, g_cos_bf, g_sin_bf, i_cos_bf, i_sin_bf, f_cos_bf, f_sin_bf, k_r, k_i, layer):
    nct = HY_W // LANE
    one = pl.Buffered(1)
    gspec = pl.BlockSpec((FFT_R, K1P, FFT_T1), lambda c, b: (0, 0, 0), pipeline_mode=one)
    ispec = pl.BlockSpec((FFT_R, FFT_T1, K1P), lambda c, b: (0, 0, 0), pipeline_mode=one)
    fspec = pl.BlockSpec((FFT_R, FFT_R), lambda c, b: (0, 0), pipeline_mode=one)
    kspec = pl.BlockSpec((None, K1P, FFT_R, LANE), lambda c, b: (layer, 0, 0, c), pipeline_mode=one)
    return pl.pallas_call(
        _hyena_fft_kernel,
        grid=(nct, B),
        in_specs=[pl.BlockSpec((L, LANE), lambda c, b: (b, c)), gspec, gspec, ispec, ispec, fspec, fspec, kspec, kspec],
        out_specs=pl.BlockSpec((L, LANE), lambda c, b: (b, c)),
        out_shape=jax.ShapeDtypeStruct((T_LAT, HY_W), F32),
        scratch_shapes=[pltpu.VMEM((FFT_R * K1P, LANE), F32)] * 2,
        compiler_params=_cp(56 << 20, 2),
        name="hyena_fft",
    )(u, g_cos_bf, g_sin_bf, i_cos_bf, i_sin_bf, f_cos_bf, f_sin_bf, k_r, k_i)


def _hyena_ctx_kernel(u_ref, k_ref, dc_ref, ds_ref, o_ref):
    dot = lambda a, b: jnp.dot(a, b, precision=HIGHEST, preferred_element_type=F32)
    dc = dc_ref[...]
    ds = ds_ref[...]
    u = u_ref[...]
    kf = k_ref[...]
    u_r = dot(dc[:, 0:C], u)
    u_i = -dot(ds[:, 0:C], u)
    k_r = dot(dc, kf)
    k_i = -dot(ds, kf)
    y_r = u_r * k_r - u_i * k_i
    y_i = u_r * k_i + u_i * k_r
    o_ref[...] = (dot(dc[0:C, :], y_r) - dot(ds[0:C, :], y_i)) * (1.0 / (2 * C))


def _hyena_ctx(u, kfilt_ctx, d_cos, d_sin):
    full = lambda b: (0, 0)
    return pl.pallas_call(
        _hyena_ctx_kernel,
        grid=(B,),
        in_specs=[pl.BlockSpec((C, HY_W), lambda b: (T_LAT // C + b, 0)),
                  pl.BlockSpec((2 * C, HY_W), full), pl.BlockSpec((2 * C, 2 * C), full), pl.BlockSpec((2 * C, 2 * C), full)],
        out_specs=pl.BlockSpec((C, HY_W), lambda b: (b, 0)),
        out_shape=jax.ShapeDtypeStruct((T_CTX, HY_W), F32),
        compiler_params=_cp(32 << 20, 1),
        name="hyena_ctx",
    )(u, kfilt_ctx, d_cos, d_sin)


def _hyena_filter(n, w1, b1, freq, w2, b2, w3):
    t = jnp.linspace(0.0, 1.0, n, dtype=F32)[:, None]
    bands = (FILTER_EMB - 1) // 2
    w = 2.0 * math.pi * jnp.arange(n, dtype=F32)[:, None] / n
    f = jnp.linspace(1e-4, bands - 1, bands, dtype=F32)[None, :]
    z = jnp.concatenate([t, jnp.cos(f * w), -jnp.sin(f * w)], axis=-1)
    mm = functools.partial(jnp.matmul, precision=HIGHEST)
    h = jnp.sin(freq * (mm(z, w1) + b1))
    h = jnp.sin(freq * (mm(h, w2) + b2))
    deltas = jnp.abs(jnp.linspace(math.log(DECAY_TARGET) / DECAY_FAST, math.log(DECAY_TARGET) / DECAY_SLOW,
                                  HY_W, dtype=F32))
    decay = jnp.exp(-t * deltas[None, :])
    h_fwd = mm(h, w3[:, :HY_W]) * decay
    h_bwd_rev = mm(h[::-1], w3[:, HY_W:]) * decay[::-1]
    k = jnp.concatenate([h_fwd, jnp.zeros((1, HY_W), F32), h_bwd_rev[:-1]], axis=0)
    return k / jnp.sum(jnp.abs(k), axis=0, keepdims=True)


def _s5_matrices(a_re, a_im, log_dt, b_re, b_im, c_re, c_im, d_skip):
    dt = jnp.exp(log_dt)[:, :, None]
    lam_re = jnp.minimum(a_re, -1e-4)
    mag1 = jnp.exp(lam_re * dt)
    lbr = mag1 * jnp.cos(a_im * dt)
    lbi = mag1 * jnp.sin(a_im * dt)
    den = lam_re * lam_re + a_im * a_im
    fr = ((lbr - 1.0) * lam_re + lbi * a_im) / den
    fi = (lbi * lam_re - (lbr - 1.0) * a_im) / den
    bbr = fr[..., None] * b_re - fi[..., None] * b_im
    bbi = fr[..., None] * b_im + fi[..., None] * b_re
    j = jnp.arange(S5_CH + 1, dtype=F32)[:, None, None, None]
    magj = jnp.exp(j * (lam_re * dt)[None])
    pr = magj * jnp.cos(j * (a_im * dt)[None])
    pi = magj * jnp.sin(j * (a_im * dt)[None])
    hi = functools.partial(jnp.einsum, precision=HIGHEST)
    lbr_j = pr[..., None] * bbr[None] - pi[..., None] * bbi[None]
    lbi_j = pr[..., None] * bbi[None] + pi[..., None] * bbr[None]
    m = hi('dgop,jdgpi->jdgoi', c_re, lbr_j) - hi('dgop,jdgpi->jdgoi', c_im, lbi_j)
    eye_g = jnp.eye(S5_GROUPS, dtype=F32)
    s = jnp.arange(S5_CH)
    blocks = jnp.einsum('jdgoi,gh->djgiho', m[0:S5_CH], eye_g).reshape(2, S5_CH, S5_W, S5_W)
    lag0 = blocks[0, 0] + blocks[1, 0] + jnp.diag(d_skip)
    e_all = jnp.concatenate([blocks[1, S5_CH - 1:0:-1], lag0[None], blocks[0, 1:S5_CH]], axis=0).astype(BF16)
    sf_r = lbr_j[S5_CH - 1 - s, 0]
    sf_i = lbi_j[S5_CH - 1 - s, 0]
    sb_r = lbr_j[s, 1]
    sb_i = lbi_j[s, 1]
    st = jnp.stack([sf_r, sf_i, sb_r, sb_i], axis=0)
    rr = jnp.arange(S5_ROWW, dtype=jnp.int32)
    cc = jnp.arange(S5_W, dtype=jnp.int32)
    ws = _s5_expand(jnp.transpose(st, (1, 2, 4, 0, 3)).reshape(S5_ROWW, S5_W),
                    (cc[:, None] // S5_STATE == rr[None, :] // S5_NSTATE)
                    & (cc[:, None] % S5_STATE == rr[None, :] % S5_STATE),
                    (rr // S5_GROUP) % S5_GROUPS, (rr % S5_NSTATE) // S5_STATE)
    tt = jnp.arange(S5_CH)
    cf_r = c_re[0][None] * pr[tt + 1, 0][:, :, None, :] - c_im[0][None] * pi[tt + 1, 0][:, :, None, :]
    cf_i = c_re[0][None] * pi[tt + 1, 0][:, :, None, :] + c_im[0][None] * pr[tt + 1, 0][:, :, None, :]
    cb_r = c_re[1][None] * pr[S5_CH - tt, 1][:, :, None, :] - c_im[1][None] * pi[S5_CH - tt, 1][:, :, None, :]
    cb_i = c_re[1][None] * pi[S5_CH - tt, 1][:, :, None, :] + c_im[1][None] * pr[S5_CH - tt, 1][:, :, None, :]
    ct = jnp.stack([cf_r, -cf_i, cb_r, -cb_i], axis=0)
    wc = _s5_expand(jnp.transpose(ct, (0, 2, 4, 1, 3)).reshape(4 * S5_NSTATE, S5_W),
                    (cc[:, None] // S5_GROUP == rr[None, :] // S5_W)
                    & (cc[:, None] % S5_GROUP == rr[None, :] % S5_GROUP),
                    (rr % S5_NSTATE) // S5_STATE, (rr % S5_W) // S5_GROUP)
    lam_p = jnp.stack([jnp.stack([pr[S5_CH, 0], pi[S5_CH, 0]]), jnp.stack([pr[S5_CH, 1], pi[S5_CH, 1]])])
    return ws, e_all, wc, lam_p.reshape(2, 2, 1, S5_NSTATE)


S5_TN = 512
S5_NS = S5_ROWW // S5_W


def _s5_expand_kernel(a_ref, ex_ref, rg_ref, cg_ref, o_ref):
    v = jnp.dot(a_ref[...], ex_ref[...], preferred_element_type=F32)
    o_ref[...] = jnp.where(rg_ref[...] == cg_ref[...], v, 0.0).astype(BF16)


def _s5_expand(compact, placement, row_group, col_group):
    n = compact.shape[0]
    return pl.pallas_call(
        _s5_expand_kernel,
        grid=(S5_ROWW // S5_TN,),
        in_specs=[pl.BlockSpec((n, S5_W), lambda j: (0, 0)), pl.BlockSpec((S5_W, S5_TN), lambda j: (0, j)),
                  pl.BlockSpec((n, 1), lambda j: (0, 0)), pl.BlockSpec((1, S5_TN), lambda j: (0, j))],
        out_specs=pl.BlockSpec((n, S5_TN), lambda j: (0, j)),
        out_shape=jax.ShapeDtypeStruct((n, S5_ROWW), BF16),
        compiler_params=_cp(32 << 20, 0),
        name="s5_expand",
    )(compact.astype(BF16), placement.astype(BF16), row_group.reshape(n, 1), col_group.reshape(1, S5_ROWW))


def _s5_in_kernel(u_ref, ws_ref, e_ref, s_ref, o_ref):
    j = pl.program_id(0)

    @pl.when(j < S5_NS)
    def _():
        s_ref[...] = jnp.dot(u_ref[...], ws_ref[...], preferred_element_type=F32)

    @pl.when(j >= S5_NS)
    def _():
        t = j - S5_NS
        acc = jnp.dot(u_ref[:, 0:S5_W], e_ref[S5_CH - 1 + t], preferred_element_type=F32)
        for s in range(1, S5_CH):
            acc = acc + jnp.dot(u_ref[:, s * S5_W:(s + 1) * S5_W], e_ref[S5_CH - 1 + t - s],
                                preferred_element_type=F32)
        o_ref[...] = acc


def _s5_in(u_rows, ws, e_all, layer):
    return pl.pallas_call(
        _s5_in_kernel,
        grid=(2 * S5_NS,),
        in_specs=[pl.BlockSpec((N_CHUNK, S5_ROWW), lambda j: (0, 0)),
                  pl.BlockSpec((None, S5_ROWW, S5_W), lambda j: (layer, 0, jnp.minimum(j, S5_NS - 1))),
                  pl.BlockSpec((None, 2 * S5_CH - 1, S5_W, S5_W), lambda j: (layer, 0, 0, 0))],
        out_specs=[pl.BlockSpec((N_CHUNK, S5_W), lambda j: (0, jnp.minimum(j, S5_NS - 1))),
                   pl.BlockSpec((N_CHUNK, S5_W), lambda j: (0, jnp.maximum(j - S5_NS, 0)))],
        out_shape=[jax.ShapeDtypeStruct((N_CHUNK, 4 * S5_NSTATE), F32), jax.ShapeDtypeStruct((N_CHUNK, S5_ROWW), F32)],
        compiler_params=_cp(48 << 20, 1),
        name="s5_in",
    )(u_rows, ws, e_all)


def _s5_scan_kernel(s_ref, lam_ref, h_ref):
    lam = [[lam_ref[d, p] for p in range(2)] for d in range(2)]

    def step(b, d, chunk, h):
        row = pl.ds(chunk, 1)
        cols_r = pl.ds(d * 2 * S5_NSTATE, S5_NSTATE)
        cols_i = pl.ds(d * 2 * S5_NSTATE + S5_NSTATE, S5_NSTATE)
        h_ref[row, cols_r] = h[0]
        h_ref[row, cols_i] = h[1]
        s_r = s_ref[row, cols_r]
        s_i = s_ref[row, cols_i]
        lr, li = lam[d]
        return (lr * h[0] - li * h[1] + s_r, lr * h[1] + li * h[0] + s_i)

    def chain_order(b, d, n_ctx_done):
        ctx0 = B * LAT_CHUNKS + b * CTX_CHUNKS
        lat0 = b * LAT_CHUNKS
        if d == 0:
            return (lambda i: ctx0 + i), (lambda i: lat0 + i)
        return (lambda i: ctx0 + CTX_CHUNKS - 1 - i), (lambda i: lat0 + LAT_CHUNKS - 1 - i)

    chains = [(b, d) for b in range(B) for d in range(2)]
    zero = jnp.zeros((1, S5_NSTATE), F32)
    init = tuple((zero, zero) for _ in chains)

    def phase(n_steps, which, carry):
        def body(i, hs):
            out = []
            for (b, d), h in zip(chains, hs):
                order = chain_order(b, d, 0)[which]
                out.append(step(b, d, order(i), h))
            return tuple(out)
        return lax.fori_loop(0, n_steps, body, carry)

    carry = phase(CTX_CHUNKS, 0, init)
    phase(LAT_CHUNKS, 1, carry)


def _s5_scan(s_rows, lam_p):
    return pl.pallas_call(
        _s5_scan_kernel,
        out_shape=jax.ShapeDtypeStruct((N_CHUNK, 4 * S5_NSTATE), F32),
        compiler_params=pltpu.CompilerParams(vmem_limit_bytes=48 << 20),
        name="s5_scan",
    )(s_rows, lam_p)


S5_TM = N_CHUNK // 2
S5_TPN = S5_TN // S5_W


def _s5_out_kernel(h_ref, w_ref, y_ref, *rest):
    o_refs, hb_ref = rest[:-1], rest[-1]
    j = pl.program_id(1)

    @pl.when(j == 0)
    def _():
        hb_ref[...] = h_ref[...].astype(BF16)

    acc = jnp.dot(hb_ref[...], w_ref[...], preferred_element_type=F32) + y_ref[...]
    for tt in range(S5_TPN):
        for h, o_ref in enumerate(o_refs):
            o_ref[pl.ds(j * S5_TPN + tt, S5_TM, stride=S5_CH), :] = (
                acc[:, tt * S5_W + h * LANE:tt * S5_W + (h + 1) * LANE])


def _s5_out(h_rows, wc, y_in, layer):
    nn = S5_ROWW // S5_TN
    n_out = S5_W // LANE
    return pl.pallas_call(
        _s5_out_kernel,
        grid=(N_CHUNK // S5_TM, nn),
        in_specs=[pl.BlockSpec((S5_TM, 4 * S5_NSTATE), lambda i, j: (i, 0), pipeline_mode=pl.Buffered(1)),
                  pl.BlockSpec((None, 4 * S5_NSTATE, S5_TN), lambda i, j: (layer, 0, j)),
                  pl.BlockSpec((S5_TM, S5_TN), lambda i, j: (i, j))],
        out_specs=[pl.BlockSpec((S5_TM * S5_CH, LANE), lambda i, j: (i, 0))] * n_out,
        out_shape=[jax.ShapeDtypeStruct((T_ALL, LANE), F32)] * n_out,
        scratch_shapes=[pltpu.VMEM((S5_TM, 4 * S5_NSTATE), BF16)],
        compiler_params=_cp(52 << 20, 2),
        name="s5_out",
    )(h_rows, wc, y_in)


def _rms(x):
    return x * lax.rsqrt(jnp.mean(x * x, axis=-1, keepdims=True) + LN_EPS)


def _layer_norm(x, g, b):
    mu = jnp.mean(x, axis=-1, keepdims=True)
    xc = x - mu
    var = jnp.mean(xc * xc, axis=-1, keepdims=True)
    return xc * lax.rsqrt(var + LN_EPS) * g + b


def _merge_kernel(x_ref, attn_ref, conv_ref, hu_ref, x0_ref, s5a_ref, s5b_ref, g1_ref, sh2_ref, sc2_ref, mixg_ref,
                  hyd_ref, wglu_ref, wout_ref, lng_ref, lnb_ref, o_ref, u2_ref, u2p_ref):
    hy = (conv_ref[...] + hu_ref[...] * hyd_ref[...]) * x0_ref[...]
    g = jax.nn.gelu(jnp.concatenate([s5a_ref[...], s5b_ref[...]], axis=-1))
    s5 = g * jax.nn.sigmoid(jnp.dot(g.astype(BF16), wglu_ref[...], preferred_element_type=F32))
    mixg = mixg_ref[...]
    parts = [_rms(attn_ref[...]) * mixg[:, 0:ATTN_W],
             _rms(hy) * mixg[:, ATTN_W:ATTN_W + HY_W],
             _rms(s5) * mixg[:, ATTN_W + HY_W:MIX_W]]
    mix = jnp.concatenate(parts, axis=-1).astype(BF16)
    o = jnp.dot(mix, wout_ref[...], preferred_element_type=F32)
    x1 = _layer_norm(ALPHA * x_ref[...] + g1_ref[...] * o, lng_ref[...], lnb_ref[...])
    o_ref[...] = x1
    u2 = x1 * (1.0 + sc2_ref[...]) + sh2_ref[...]
    u2_ref[...] = u2
    u2p_ref[...] = _pack_pairs(u2)


def _merge(n_rows, xall, attn, conv, hu, x0c, s5y, g1, sh2, sc2, mix_g, hy_d, wglu_bf, wout_bf, ln_g, ln_b):
    nt = n_rows // TM
    row = lambda i: (i, 0)
    full = lambda i: (0, 0)
    return pl.pallas_call(
        _merge_kernel,
        grid=(nt,),
        in_specs=[pl.BlockSpec((TM, D), row), pl.BlockSpec((TM, ATTN_W), row), pl.BlockSpec((TM, HY_W), row),
                  pl.BlockSpec((TM, HY_W), row), pl.BlockSpec((TM, HY_W), row),
                  pl.BlockSpec((TM, LANE), row), pl.BlockSpec((TM, LANE), row),
                  _vec_spec(TM), _vec_spec(TM), _vec_spec(TM), pl.BlockSpec((1, MIX_W), full),
                  pl.BlockSpec((1, HY_W), full), pl.BlockSpec((S5_W, S5_W), full), pl.BlockSpec((MIX_W, D), full),
                  pl.BlockSpec((1, D), full), pl.BlockSpec((1, D), full)],
        out_specs=[pl.BlockSpec((TM, D), row), pl.BlockSpec((TM, D), row), pl.BlockSpec((TM, HALF_D), row)],
        out_shape=[jax.ShapeDtypeStruct((n_rows, D), F32), jax.ShapeDtypeStruct((n_rows, D), F32),
                   jax.ShapeDtypeStruct((n_rows, HALF_D), jnp.int32)],
        compiler_params=_cp(48 << 20, 1),
        name="merge",
    )(xall, attn, conv, hu, x0c, s5y[0], s5y[1], g1, sh2, sc2, mix_g.reshape(1, MIX_W), hy_d.reshape(1, HY_W), wglu_bf,
      wout_bf, ln_g.reshape(1, D), ln_b.reshape(1, D))


def _router_kernel(u_ref, wt_ref, b_ref, e_ref, g_ref):
    logits = lax.dot_general(wt_ref[...], u_ref[...], (((1,), (1,)), ((), ())), precision=HIGHEST,
                             preferred_element_type=F32)
    scores = jax.nn.sigmoid(logits)
    biased = scores + b_ref[...]
    ninf = -jnp.inf
    grow = lax.broadcasted_iota(jnp.int32, (EGROUP, TM), 0)
    groups = [biased[gi * EGROUP:(gi + 1) * EGROUP] for gi in range(N_EGROUPS)]
    gscore = []
    for vals in groups:
        m1 = jnp.max(vals, axis=0, keepdims=True)
        i1 = jnp.min(jnp.where(vals == m1, grow, EGROUP), axis=0, keepdims=True)
        m2 = jnp.max(jnp.where(grow == i1, ninf, vals), axis=0, keepdims=True)
        gscore.append(m1 + m2)
    kept = []
    for gi in range(N_EGROUPS):
        rank = jnp.zeros((1, TM), jnp.int32)
        for gj in range(N_EGROUPS):
            if gj == gi:
                continue
            ahead = (gscore[gj] > gscore[gi]) | ((gscore[gj] == gscore[gi]) & (gj < gi))
            rank = rank + ahead.astype(jnp.int32)
        kept.append(jnp.where(rank < TOPK_GROUPS, groups[gi], ninf))
    masked = jnp.concatenate(kept, axis=0)
    row = lax.broadcasted_iota(jnp.int32, (N_EXPERTS, TM), 0)
    gates = []
    gsum = jnp.zeros((1, TM), F32)
    for kk in range(TOP_K):
        m = jnp.max(masked, axis=0, keepdims=True)
        idx = jnp.min(jnp.where(masked == m, row, N_EXPERTS), axis=0, keepdims=True)
        hit = row == idx
        gate = jnp.sum(jnp.where(hit, scores, 0.0), axis=0, keepdims=True)
        masked = jnp.where(hit, ninf, masked)
        e_ref[kk:kk + 1, :] = idx
        gates.append(gate)
        gsum = gsum + gate
    for kk in range(TOP_K):
        g_ref[kk:kk + 1, :] = gates[kk] / gsum * ROUTED_SCALE


def _router(n_rows, u2, w_router_t, router_bias):
    nt = n_rows // TM
    col = lambda i: (0, i)
    return pl.pallas_call(
        _router_kernel,
        grid=(nt,),
        in_specs=[pl.BlockSpec((TM, D), lambda i: (i, 0)),
                  pl.BlockSpec((N_EXPERTS, D), lambda i: (0, 0)), pl.BlockSpec((N_EXPERTS, 1), lambda i: (0, 0))],
        out_specs=[pl.BlockSpec((TOP_K, TM), col), pl.BlockSpec((TOP_K, TM), col)],
        out_shape=[jax.ShapeDtypeStruct((TOP_K, n_rows), jnp.int32), jax.ShapeDtypeStruct((TOP_K, n_rows), F32)],
        compiler_params=_cp(32 << 20, 1),
        name="router",
    )(u2, w_router_t, router_bias.reshape(N_EXPERTS, 1))


def _dispatch(top_e):
    t = top_e.shape[1]
    tk = t * TOP_K
    nblk = tk // MOE_BLOCK
    n_steps = nblk + N_EXPERTS
    flat_e = top_e.reshape(tk)
    pos = jnp.arange(tk, dtype=jnp.int32)
    pos_bits = (tk - 1).bit_length()
    assert pos_bits + (N_EXPERTS - 1).bit_length() < 32
    order = lax.sort(flat_e * (1 << pos_bits) + pos) % (1 << pos_bits)
    _, inv = lax.sort((order, pos), num_keys=1)
    experts = jnp.arange(N_EXPERTS, dtype=jnp.int32)
    counts = jnp.sum((flat_e[None, :] == experts[:, None]).astype(jnp.int32), axis=1)
    ends = jnp.cumsum(counts)
    starts = ends - counts
    fb = starts // MOE_BLOCK
    npairs = jnp.where(counts > 0, (ends - 1) // MOE_BLOCK - fb + 1, 0)
    pend = jnp.cumsum(npairs)
    poff = pend - npairs
    n_pairs = pend[-1]
    s = jnp.arange(n_steps, dtype=jnp.int32)
    sc = jnp.minimum(s, n_pairs - 1)
    pe = jnp.sum((pend[None, :] <= sc[:, None]).astype(jnp.int32), axis=1)
    pb = fb[pe] + (sc - poff[pe])
    lo = jnp.where(s < n_pairs, jnp.maximum(starts[pe] - pb * MOE_BLOCK, 0), 0)
    hi = jnp.where(s < n_pairs, jnp.minimum(ends[pe] - pb * MOE_BLOCK, MOE_BLOCK), 0)
    last_of_block = ((s == n_pairs - 1) | ((s + 1 < n_pairs) & (jnp.roll(pb, -1) != pb))).astype(jnp.int32)
    used = counts > 0
    ordinal = jnp.cumsum(used.astype(jnp.int32)) - 1
    later = jnp.where(used, experts, N_EXPERTS)
    next_used = lax.cummin(jnp.concatenate([later[1:], jnp.full((1,), N_EXPERTS, jnp.int32)]), reverse=True)
    nxt = next_used[pe]
    nxt = jnp.where(nxt < N_EXPERTS, nxt, pe)
    parity = ordinal[pe] % 2
    w_even = jnp.where(parity == 0, pe, nxt)
    w_odd = jnp.where(parity == 1, pe, nxt)
    return dict(pe=pe, pb=pb, lo=lo, hi=hi, last=last_of_block, parity=parity, w_even=w_even, w_odd=w_odd,
                tok=order % t, comb=inv)


def _expert_kernel(pe, pb, plo, phi, plast, ppar, pwe, pwo, x_ref, wg0_ref, wu0_ref, wd0_ref, wg1_ref, wu1_ref,
                   wd1_ref, out_hbm, obuf, osem, wgb, wub, wdb):
    del pwe, pwo
    s = pl.program_id(0)
    prev = jnp.maximum(s - 1, 0)
    n_blocks = out_hbm.shape[0] // MOE_BLOCK
    new_expert = (s == 0) | (pe[s] != pe[prev])

    for par, (wg_ref, wu_ref, wd_ref) in enumerate(((wg0_ref, wu0_ref, wd0_ref), (wg1_ref, wu1_ref, wd1_ref))):
        @pl.when(new_expert & (ppar[s] == par))
        def _():
            wgb[...] = wg_ref[...].astype(BF16)
            wub[...] = wu_ref[...].astype(BF16)
            wdb[...] = wd_ref[...].astype(BF16)

    lo = plo[s]
    hi = phi[s]
    blk = pb[s]
    slot = blk % 2
    first_of_block = (s == 0) | (blk != pb[prev])

    def out_copy(b, sl):
        rows = pl.ds(pl.multiple_of(b * MOE_BLOCK, MOE_BLOCK), MOE_BLOCK)
        return pltpu.make_async_copy(obuf.at[sl], out_hbm.at[rows], osem.at[sl])

    def ffn():
        x_lo, x_hi = _unpack_pairs(x_ref[...])
        x_lo = x_lo.astype(BF16)
        x_hi = x_hi.astype(BF16)
        hg = (jnp.dot(x_lo, wgb[0:HALF_D], preferred_element_type=F32)
              + jnp.dot(x_hi, wgb[HALF_D:D], preferred_element_type=F32))
        hu = (jnp.dot(x_lo, wub[0:HALF_D], preferred_element_type=F32)
              + jnp.dot(x_hi, wub[HALF_D:D], preferred_element_type=F32))
        h = (hg * jax.nn.sigmoid(hg)) * hu
        return _pack_pairs(jnp.dot(h.astype(BF16), wdb[...], preferred_element_type=F32))

    @pl.when((hi > lo) & first_of_block)
    def _():
        @pl.when(blk >= 2)
        def _():
            out_copy(blk - 2, slot).wait()
        obuf[slot] = ffn()

    @pl.when((hi > lo) & jnp.logical_not(first_of_block))
    def _():
        row = lax.broadcasted_iota(jnp.int32, (MOE_BLOCK, HALF_D), 0)
        obuf[slot] = jnp.where((row >= lo) & (row < hi), ffn(), obuf[slot])

    @pl.when(plast[s] == 1)
    def _():
        out_copy(blk, slot).start()

    @pl.when(s == pl.num_programs(0) - 1)
    def _():
        out_copy(n_blocks - 2, (n_blocks - 2) % 2).wait()
        out_copy(n_blocks - 1, (n_blocks - 1) % 2).wait()


def _experts(xs, disp, wg, wu, wd, layer):
    tk = xs.shape[0]
    n_steps = disp['pe'].shape[0]
    assert tk // MOE_BLOCK >= 2
    blk = lambda s, pe, pb, *_: (pb[s], 0)
    weven = lambda shape: pl.BlockSpec((None, None) + shape, lambda s, *p: (layer, p[6][s], 0, 0))
    wodd = lambda shape: pl.BlockSpec((None, None) + shape, lambda s, *p: (layer, p[7][s], 0, 0))
    grid_spec = pltpu.PrefetchScalarGridSpec(
        num_scalar_prefetch=8,
        grid=(n_steps,),
        in_specs=[pl.BlockSpec((MOE_BLOCK, HALF_D), blk),
                  weven((D, EXPERT_FF)), weven((D, EXPERT_FF)), weven((EXPERT_FF, D)),
                  wodd((D, EXPERT_FF)), wodd((D, EXPERT_FF)), wodd((EXPERT_FF, D))],
        out_specs=pl.BlockSpec(memory_space=pl.ANY),
        scratch_shapes=[pltpu.VMEM((2, MOE_BLOCK, HALF_D), jnp.int32), pltpu.SemaphoreType.DMA((2,)),
                        pltpu.VMEM((D, EXPERT_FF), BF16), pltpu.VMEM((D, EXPERT_FF), BF16),
                        pltpu.VMEM((EXPERT_FF, D), BF16)],
    )
    return pl.pallas_call(
        _expert_kernel,
        grid_spec=grid_spec,
        out_shape=jax.ShapeDtypeStruct((tk, HALF_D), jnp.int32),
        compiler_params=_cp(40 << 20, 1),
        name="experts",
    )(disp['pe'], disp['pb'], disp['lo'], disp['hi'], disp['last'], disp['parity'], disp['w_even'], disp['w_odd'],
      xs, wg, wu, wd, wg, wu, wd)


SC_ROWS = 64


def _row_gather(table, idx):
    n = idx.shape[0]
    d = table.shape[1]
    mesh = plsc.VectorSubcoreMesh(core_axis_name="c", subcore_axis_name="s")
    n_workers = mesh.num_cores * mesh.num_subcores
    per_worker = n // n_workers
    assert per_worker * n_workers == n and per_worker % SC_ROWS == 0

    @functools.partial(
        pl.kernel, mesh=mesh,
        out_type=jax.ShapeDtypeStruct((n, d), table.dtype),
        scratch_types=[pltpu.VMEM((SC_ROWS,), jnp.int32), pltpu.VMEM((SC_ROWS, d), table.dtype),
                       pltpu.SemaphoreType.DMA],
    )
    def gather(table_hbm, idx_hbm, out_hbm, idx_v, rows_v, sem):
        worker = lax.axis_index("s") * mesh.num_cores + lax.axis_index("c")
        base = worker * per_worker

        @pl.loop(0, per_worker // SC_ROWS)
        def _(c):
            off = pl.multiple_of(base + c * SC_ROWS, 8)
            pltpu.sync_copy(idx_hbm.at[pl.ds(off, SC_ROWS)], idx_v)
            pltpu.async_copy(table_hbm.at[idx_v], rows_v, sem).wait()
            pltpu.sync_copy(rows_v, out_hbm.at[pl.ds(off, SC_ROWS)])

    return gather(table, idx)


FM = 128


def _ffn_out_kernel(x_ref, u_ref, r_ref, gate_ref, g2_ref, wsg_ref, wsu_ref, wsd_ref, lng_ref, lnb_ref, o_ref):
    u_lo, u_hi = _unpack_pairs(u_ref[...])
    u_lo = u_lo.astype(BF16)
    u_hi = u_hi.astype(BF16)
    hg = (jnp.dot(u_lo, wsg_ref[0:HALF_D], preferred_element_type=F32)
          + jnp.dot(u_hi, wsg_ref[HALF_D:D], preferred_element_type=F32))
    hu = (jnp.dot(u_lo, wsu_ref[0:HALF_D], preferred_element_type=F32)
          + jnp.dot(u_hi, wsu_ref[HALF_D:D], preferred_element_type=F32))
    f = jnp.dot(((hg * jax.nn.sigmoid(hg)) * hu).astype(BF16), wsd_ref[...], preferred_element_type=F32)
    gates = jnp.transpose(jnp.concatenate([gate_ref[...], jnp.zeros((FM - TOP_K, FM), F32)], axis=0))
    r_lo = jnp.zeros((FM, HALF_D), F32)
    r_hi = jnp.zeros((FM, HALF_D), F32)
    for kk in range(TOP_K):
        k_lo, k_hi = _unpack_pairs(r_ref[kk])
        gk = gates[:, kk:kk + 1]
        r_lo = r_lo + k_lo * gk
        r_hi = r_hi + k_hi * gk
    f = jnp.concatenate([r_lo, r_hi], axis=1) + f
    o_ref[...] = _layer_norm(ALPHA * x_ref[...] + g2_ref[...] * f, lng_ref[...], lnb_ref[...])


def _ffn_out(n_rows, x1, u2p, routed, gate, g2, wsg_bf, wsu_bf, wsd_bf, ln_g, ln_b):
    row = lambda i: (i, 0)
    full = lambda i: (0, 0)
    return pl.pallas_call(
        _ffn_out_kernel,
        grid=(n_rows // FM,),
        in_specs=[pl.BlockSpec((FM, D), row), pl.BlockSpec((FM, HALF_D), row),
                  pl.BlockSpec((TOP_K, FM, HALF_D), lambda i: (0, i, 0)), pl.BlockSpec((TOP_K, FM), lambda i: (0, i)),
                  _vec_spec(FM), pl.BlockSpec((D, EXPERT_FF), full), pl.BlockSpec((D, EXPERT_FF), full),
                  pl.BlockSpec((EXPERT_FF, D), full), pl.BlockSpec((1, D), full), pl.BlockSpec((1, D), full)],
        out_specs=pl.BlockSpec((FM, D), row),
        out_shape=jax.ShapeDtypeStruct((n_rows, D), F32),
        compiler_params=_cp(40 << 20, 1),
        name="ffn_out",
    )(x1, u2p, routed, gate, g2, wsg_bf, wsu_bf, wsd_bf, ln_g.reshape(1, D), ln_b.reshape(1, D))


def _rope_tables():
    t = jnp.arange(L, dtype=jnp.int32)
    row = (t // GRID_W).astype(F32)
    col = (t % GRID_W).astype(F32)
    inv_freq = ROPE_BASE ** (-jnp.arange(0, AXIS_DIM, 2, dtype=F32) / AXIS_DIM)
    half = AXIS_DIM // 2

    def axis(pos):
        ang = pos[:, None] * inv_freq[None, :]
        c = jnp.cos(ang)
        s = jnp.sin(ang)
        return jnp.concatenate([c, c], axis=1), jnp.concatenate([-s, s], axis=1)

    cr, sr = axis(row)
    cc, sc = axis(col)
    cos_h = jnp.concatenate([cr, cc], axis=1)
    sin_h = jnp.concatenate([sr, sc], axis=1)
    cos_l = jnp.tile(cos_h, (B, LANE // HEAD_DIM))
    sin_l = jnp.tile(sin_h, (B, LANE // HEAD_DIM))
    cos_t = jnp.concatenate([cos_l, jnp.ones((T_CTX, LANE), F32)], axis=0)
    sin_t = jnp.concatenate([sin_l, jnp.zeros((T_CTX, LANE), F32)], axis=0)
    del half
    return cos_t, sin_t


def kernel(x, c, ctx, c_ctx, w_ada, b_ada, w_in, w_out, sink, mix_g, hy_short_w, hy_short_b, hy_w1, hy_b1, hy_freq,
           hy_w2, hy_b2, hy_w3, hy_d, s5_a_re, s5_a_im, s5_log_dt, s5_b_re, s5_b_im, s5_c_re, s5_c_im, s5_d, s5_w_glu,
           ln1_g, ln1_b, ln2_g, ln2_b, w_router, router_bias, w_exp_gate, w_exp_up, w_exp_down, w_sh_gate, w_sh_up,
           w_sh_down):
    xall = jnp.concatenate([x.reshape(T_LAT, D), ctx.reshape(T_CTX, D)], axis=0)
    cvec = jnp.concatenate([c, c_ctx[None, :], jnp.zeros((8 - B - 1, D), F32)], axis=0)
    mod = _ada(cvec, w_ada, b_ada)[:, 0:B + 1, :].reshape(DEPTH, B + 1, 6, 1, D)

    cos_t, sin_t = _rope_tables()
    g_cos, g_sin, f_cos, f_sin = _dft_tables()
    g_cos_bf, g_sin_bf = g_cos.astype(BF16), g_sin.astype(BF16)
    i_cos_bf = jnp.swapaxes(g_cos, 1, 2).astype(BF16)
    i_sin_bf = jnp.swapaxes(g_sin, 1, 2).astype(BF16)
    f_cos_bf, f_sin_bf = f_cos.astype(BF16), f_sin.astype(BF16)
    k1 = jnp.arange(K1P)
    spec_w = jnp.where((k1 == 0) | (k1 == FFT_R // 2), 1.0, 2.0) * (k1 < K1_USED) / N_FFT
    spec_w = spec_w.astype(F32)
    kt = jnp.arange(2 * C, dtype=jnp.int32)
    ang_c = ((kt[:, None] * kt[None, :]) % (2 * C)).astype(F32) * (2.0 * math.pi / (2 * C))
    d_cos, d_sin = jnp.cos(ang_c), jnp.sin(ang_c)

    filt_params = (hy_w1, hy_b1, hy_freq, hy_w2, hy_b2, hy_w3)
    filt_lat = jax.vmap(functools.partial(_hyena_filter, L))(*filt_params)
    filt_ctx = jax.vmap(functools.partial(_hyena_filter, C))(*(p[0:DEPTH - 1] for p in filt_params))
    k_r, k_i = jax.vmap(_hyena_spectrum, in_axes=(0, None, None, None, None, None))(
        filt_lat, g_cos, g_sin, f_cos, f_sin, spec_w)
    ws, e_all, wc, lam_p = jax.vmap(_s5_matrices)(s5_a_re, s5_a_im, s5_log_dt, s5_b_re, s5_b_im, s5_c_re, s5_c_im, s5_d)

    for l in range(DEPTH):
        last = l == DEPTH - 1
        n_rows = T_LAT if last else T_ALL
        sh1, sc1, g1, sh2, sc2, g2 = (mod[l, :, j] for j in range(6))

        q, k, v, hz, s5u = _inproj(xall, sh1, sc1, w_in[l].astype(BF16), cos_t, sin_t)
        attn = _attention(sink[l], q, k, v, with_ctx=not last)

        hu, x0c = _hyena_pre(hz, hy_short_w[l], hy_short_b[l])
        conv = _hyena_fft(hu, g_cos_bf, g_sin_bf, i_cos_bf, i_sin_bf, f_cos_bf, f_sin_bf, k_r, k_i, l)
        if not last:
            conv = jnp.concatenate([conv, _hyena_ctx(hu, filt_ctx[l], d_cos, d_sin)], axis=0)

        s_rows, y_in = _s5_in(s5u, ws, e_all, l)
        s5y = _s5_out(_s5_scan(s_rows, lam_p[l]), wc, y_in, l)

        x1, u2, u2p = _merge(n_rows, xall, attn, conv, hu, x0c, s5y, g1, sh2, sc2, mix_g[l], hy_d[l],
                             s5_w_glu[l].astype(BF16), w_out[l].astype(BF16), ln1_g[l], ln1_b[l])

        top_e, gate = _router(n_rows, u2, w_router[l].T, router_bias[l])
        disp = _dispatch(top_e)
        ys = _experts(_row_gather(u2p, disp['tok']), disp, w_exp_gate, w_exp_up, w_exp_down, l)
        routed = _row_gather(ys, disp['comb']).reshape(TOP_K, n_rows, HALF_D)
        xall = _ffn_out(n_rows, x1, u2p, routed, gate, g2, w_sh_gate[l].astype(BF16), w_sh_up[l].astype(BF16),
                        w_sh_down[l].astype(BF16), ln2_g[l], ln2_b[l])
    return xall.reshape(B, L, D)
```

```python
import functools
import math

import jax
import jax.numpy as jnp
from jax import lax
from jax.experimental import pallas as pl
from jax.experimental.pallas import tpu as pltpu
from jax.experimental.pallas import tpu_sc as plsc

F32 = jnp.float32
BF16 = jnp.bfloat16
HIGHEST = lax.Precision.HIGHEST

D = 1024
B = 2
L = 8192
DEPTH = 2
GRID_W = 64
C = 256
T_LAT = B * L
T_CTX = B * C
T_ALL = T_LAT + T_CTX

HEAD_DIM = 64
N_Q = 8
N_KV = 2
Q_GROUP = N_Q // N_KV
ATTN_W = N_Q * HEAD_DIM
KV_W = N_KV * HEAD_DIM
HY_W = 256
S5_W = 256
MIX_W = ATTN_W + HY_W + S5_W
K_OFF = ATTN_W
V_OFF = K_OFF + KV_W
HY_OFF = V_OFF + KV_W
S5_OFF = HY_OFF + 3 * HY_W
IN_W = S5_OFF + S5_W
WINDOW = 128
BLK = 128
NEG_INF = -1e30
ROPE_BASE = 10000.0
AXIS_DIM = HEAD_DIM // 2

SHORT_K = 3
FILTER_EMB = 33
DECAY_FAST = 0.3
DECAY_SLOW = 1.5
DECAY_TARGET = 1e-2

S5_GROUP = 16
S5_GROUPS = S5_W // S5_GROUP
S5_STATE = 64
S5_NSTATE = S5_GROUPS * S5_STATE
S5_CH = 16
S5_ROWW = S5_CH * S5_W
N_CHUNK = T_ALL // S5_CH
LAT_CHUNKS = L // S5_CH
CTX_CHUNKS = C // S5_CH

N_EXPERTS = 256
TOP_K = 8
N_EGROUPS = 8
EGROUP = N_EXPERTS // N_EGROUPS
TOPK_GROUPS = 4
EXPERT_FF = 256
ROUTED_SCALE = 2.5
MOE_BLOCK = 256

ALPHA = (2 * DEPTH) ** 0.25
LN_EPS = 1e-5

N_FFT = 2 * L
FFT_R = 128
FFT_T1 = L // FFT_R
K1_USED = FFT_R // 2 + 1
K1P = 80
K1H = K1P // 2
FFT_UNROLL = 4

TM = 256
LANE = 128
VMEM_CAP = 60000 * 1024


def _cp(vmem_bytes, n_axes):
    return pltpu.CompilerParams(
        dimension_semantics=("arbitrary",) * n_axes if n_axes else None,
        vmem_limit_bytes=min(int(vmem_bytes), VMEM_CAP),
    )


HALF_D = D // 2
HIGH_HALF_WORD = 0xFFFF0000


def _pack_pairs(x):
    bits = lax.bitcast_convert_type(x.astype(BF16).astype(F32), jnp.uint32)
    packed = (bits[:, 0:HALF_D] >> 16) | (bits[:, HALF_D:D] & jnp.uint32(HIGH_HALF_WORD))
    return lax.bitcast_convert_type(packed, jnp.int32)


def _unpack_pairs(p):
    bits = lax.bitcast_convert_type(p, jnp.uint32)
    low = lax.bitcast_convert_type(bits << 16, F32)
    high = lax.bitcast_convert_type(bits & jnp.uint32(HIGH_HALF_WORD), F32)
    return low, high


def _mod_sel(rows_per_tile):
    per_batch = L // rows_per_tile
    return lambda i: jnp.minimum(i // per_batch, 2)


def _vec_spec(rows_per_tile):
    sel = _mod_sel(rows_per_tile)
    return pl.BlockSpec((None, 1, D), lambda i: (sel(i), 0, 0))


ADA_TN = 1536


def _ada_kernel(c_ref, w_ref, b_ref, o_ref):
    c = c_ref[...]
    s = c * jax.nn.sigmoid(c)
    o_ref[...] = jnp.dot(s, w_ref[...], precision=HIGHEST, preferred_element_type=F32) + b_ref[...]


def _ada(cvec, w_ada, b_ada):
    return pl.pallas_call(
        _ada_kernel,
        grid=(DEPTH, 6 * D // ADA_TN),
        in_specs=[
            pl.BlockSpec((8, D), lambda l, j: (0, 0)),
            pl.BlockSpec((None, D, ADA_TN), lambda l, j: (l, 0, j)),
            pl.BlockSpec((None, 1, ADA_TN), lambda l, j: (l, 0, j)),
        ],
        out_specs=pl.BlockSpec((None, 8, ADA_TN), lambda l, j: (l, 0, j)),
        out_shape=jax.ShapeDtypeStruct((DEPTH, 8, 6 * D), F32),
        compiler_params=_cp(40 << 20, 2),
        name="ada",
    )(cvec, w_ada, b_ada.reshape(DEPTH, 1, 6 * D))


def _inproj_kernel(x_ref, sh_ref, sc_ref, w_ref, cos_ref, sin_ref, q_ref, k_ref, v_ref, hy_ref, s5_ref, *s5_tok_refs):
    u = x_ref[...] * (1.0 + sc_ref[...]) + sh_ref[...]
    proj = jnp.dot(u.astype(BF16), w_ref[...], preferred_element_type=F32)
    cos = cos_ref[...]
    sin = sin_ref[...]
    lane = lax.broadcasted_iota(jnp.int32, (TM, LANE), 1)
    first_half = (lane % AXIS_DIM) < (AXIS_DIM // 2)

    def rope(xc):
        partner = jnp.where(first_half, pltpu.roll(xc, LANE - AXIS_DIM // 2, 1), pltpu.roll(xc, AXIS_DIM // 2, 1))
        return xc * cos + partner * sin

    for j in range(ATTN_W // LANE):
        q_ref[:, j * LANE:(j + 1) * LANE] = rope(proj[:, j * LANE:(j + 1) * LANE]).astype(BF16)
    k_ref[...] = rope(proj[:, K_OFF:V_OFF]).astype(BF16)
    v_ref[...] = proj[:, V_OFF:HY_OFF].astype(BF16)
    hy_ref[...] = proj[:, HY_OFF:S5_OFF]
    for h, tok_ref in enumerate(s5_tok_refs):
        tok_ref[...] = proj[:, S5_OFF + h * LANE:S5_OFF + (h + 1) * LANE]
    for s in range(S5_CH):
        for h, tok_ref in enumerate(s5_tok_refs):
            s5_ref[:, s * S5_W + h * LANE:s * S5_W + (h + 1) * LANE] = (
                tok_ref[pl.ds(s, TM // S5_CH, stride=S5_CH), :].astype(BF16))


def _inproj(xall, sh, sc, w_in_bf, cos_t, sin_t):
    nt = T_ALL // TM
    row = lambda i: (i, 0)
    return pl.pallas_call(
        _inproj_kernel,
        grid=(nt,),
        in_specs=[
            pl.BlockSpec((TM, D), row),
            _vec_spec(TM),
            _vec_spec(TM),
            pl.BlockSpec((D, IN_W), lambda i: (0, 0)),
            pl.BlockSpec((TM, LANE), row),
            pl.BlockSpec((TM, LANE), row),
        ],
        out_specs=[
            pl.BlockSpec((TM, ATTN_W), row),
            pl.BlockSpec((TM, KV_W), row),
            pl.BlockSpec((TM, KV_W), row),
            pl.BlockSpec((TM, 3 * HY_W), row),
            pl.BlockSpec((TM // S5_CH, S5_ROWW), row),
        ],
        out_shape=[
            jax.ShapeDtypeStruct((T_ALL, ATTN_W), BF16),
            jax.ShapeDtypeStruct((T_ALL, KV_W), BF16),
            jax.ShapeDtypeStruct((T_ALL, KV_W), BF16),
            jax.ShapeDtypeStruct((T_ALL, 3 * HY_W), F32),
            jax.ShapeDtypeStruct((N_CHUNK, S5_ROWW), BF16),
        ],
        scratch_shapes=[pltpu.VMEM((TM, LANE), F32)] * (S5_W // LANE),
        compiler_params=_cp(40 << 20, 1),
        name="inproj",
    )(xall, sh, sc, w_in_bf, cos_t, sin_t)


NB_LAT = L // BLK
NB_CTX = C // BLK


def _nt_dot(a, b):
    return lax.dot_general(a, b, (((1,), (1,)), ((), ())), preferred_element_type=F32)


def _attn_kernel(sink_ref, q_ref, kp_ref, kc_ref, kn_ref, kx_ref, vp_ref, vc_ref, vn_ref, vx_ref, o_ref):
    n = pl.program_id(1)
    is_lat = n < NB_LAT
    rows = Q_GROUP * BLK
    r = lax.broadcasted_iota(jnp.int32, (rows, BLK), 0) % BLK
    j = lax.broadcasted_iota(jnp.int32, (rows, BLK), 1)
    ok_prev = (j >= r) & (n >= 1) & is_lat
    ok_next = (j <= r) & (n + 1 < NB_LAT) & is_lat
    head_of_row = lax.broadcasted_iota(jnp.int32, (rows, 1), 0) // BLK
    q = q_ref[...] * (HEAD_DIM ** -0.5)
    dot = lambda a, b: jnp.dot(a.astype(BF16), b, preferred_element_type=F32)
    for kh in range(N_KV):
        hs = slice(kh * HEAD_DIM, (kh + 1) * HEAD_DIM)
        heads = range(kh * Q_GROUP, (kh + 1) * Q_GROUP)
        qg = jnp.concatenate([q[:, h * HEAD_DIM:(h + 1) * HEAD_DIM] for h in heads], axis=0)
        sk = jnp.zeros((rows, 1), F32)
        for g, h in enumerate(heads):
            sk = jnp.where(head_of_row == g, sink_ref[h], sk)
        s_p = jnp.where(ok_prev, _nt_dot(qg, kp_ref[:, hs]), NEG_INF)
        s_c = jnp.where(is_lat, _nt_dot(qg, kc_ref[:, hs]), NEG_INF)
        s_n = jnp.where(ok_next, _nt_dot(qg, kn_ref[:, hs]), NEG_INF)
        s_x = _nt_dot(qg, kx_ref[:, hs])
        s_x0 = s_x[:, 0:BLK]
        s_x1 = s_x[:, BLK:2 * BLK]
        m = jnp.maximum(jnp.maximum(jnp.maximum(s_p, s_c), jnp.maximum(s_n, s_x0)), s_x1)
        m = jnp.maximum(jnp.max(m, axis=1, keepdims=True), sk)
        e_p = jnp.exp(s_p - m)
        e_c = jnp.exp(s_c - m)
        e_n = jnp.exp(s_n - m)
        e_x0 = jnp.exp(s_x0 - m)
        e_x1 = jnp.exp(s_x1 - m)
        den = jnp.sum((e_p + e_c) + (e_n + e_x0) + e_x1, axis=1, keepdims=True) + jnp.exp(sk - m)
        o = (dot(e_p, vp_ref[:, hs]) + dot(e_c, vc_ref[:, hs]) + dot(e_n, vn_ref[:, hs])
             + dot(e_x0, vx_ref[0:BLK, hs]) + dot(e_x1, vx_ref[BLK:2 * BLK, hs]))
        o = o / den
        for g, h in enumerate(heads):
            o_ref[:, h * HEAD_DIM:(h + 1) * HEAD_DIM] = o[g * BLK:(g + 1) * BLK]


def _attention(sink, q, k, v, with_ctx):
    nblk = NB_LAT + (NB_CTX if with_ctx else 0)

    def q_idx(b, n):
        return (jnp.where(n < NB_LAT, b * NB_LAT + n, B * NB_LAT + b * NB_CTX + (n - NB_LAT)), 0)

    def kv_idx(off):
        def idx(b, n):
            nn = jnp.clip(jnp.minimum(n, NB_LAT - 1) + off, 0, NB_LAT - 1)
            return (b * NB_LAT + nn, 0)
        return idx

    ctx_idx = lambda b, n: (T_LAT // C + b, 0)
    kv_specs = lambda: [pl.BlockSpec((BLK, KV_W), kv_idx(-1)), pl.BlockSpec((BLK, KV_W), kv_idx(0)),
                        pl.BlockSpec((BLK, KV_W), kv_idx(1)), pl.BlockSpec((C, KV_W), ctx_idx)]
    return pl.pallas_call(
        _attn_kernel,
        grid=(B, nblk),
        in_specs=[pl.BlockSpec(memory_space=pltpu.SMEM), pl.BlockSpec((BLK, ATTN_W), q_idx)] + kv_specs() + kv_specs(),
        out_specs=pl.BlockSpec((BLK, ATTN_W), q_idx),
        out_shape=jax.ShapeDtypeStruct((T_ALL if with_ctx else T_LAT, ATTN_W), F32),
        compiler_params=_cp(32 << 20, 2),
        name="attention",
    )(sink, q, k, k, k, k, v, v, v, v)


def _hyena_pre_kernel(z_ref, zp_ref, zn_ref, w_ref, b_ref, u_ref, x0_ref):
    i = pl.program_id(0)
    tiles_per_seq = L // TM
    is_ctx = i >= B * tiles_per_seq
    first = is_ctx | (i % tiles_per_seq == 0)
    last = is_ctx | (i % tiles_per_seq == tiles_per_seq - 1)
    z = z_ref[...]
    prev_row = jnp.where(first, 0.0, zp_ref[7:8, :])
    next_row = jnp.where(last, 0.0, zn_ref[0:1, :])
    row = lax.broadcasted_iota(jnp.int32, z.shape, 0)
    z_m1 = jnp.where(row == 0, prev_row, pltpu.roll(z, 1, 0))
    z_p1 = jnp.where(row == TM - 1, next_row, pltpu.roll(z, TM - 1, 0))
    zc = b_ref[...] + z_m1 * w_ref[0:1, :] + z * w_ref[1:2, :] + z_p1 * w_ref[2:3, :]
    u_ref[...] = zc[:, 0:HY_W] * zc[:, HY_W:2 * HY_W]
    x0_ref[...] = zc[:, 2 * HY_W:3 * HY_W]


def _hyena_pre(z, short_w, short_b):
    nt = T_ALL // TM
    sub = TM // 8
    n8 = T_ALL // 8
    return pl.pallas_call(
        _hyena_pre_kernel,
        grid=(nt,),
        in_specs=[
            pl.BlockSpec((TM, 3 * HY_W), lambda i: (i, 0)),
            pl.BlockSpec((8, 3 * HY_W), lambda i: (jnp.maximum(i * sub - 1, 0), 0)),
            pl.BlockSpec((8, 3 * HY_W), lambda i: (jnp.minimum((i + 1) * sub, n8 - 1), 0)),
            pl.BlockSpec((SHORT_K, 3 * HY_W), lambda i: (0, 0)),
            pl.BlockSpec((1, 3 * HY_W), lambda i: (0, 0)),
        ],
        out_specs=[pl.BlockSpec((TM, HY_W), lambda i: (i, 0)), pl.BlockSpec((TM, HY_W), lambda i: (i, 0))],
        out_shape=[jax.ShapeDtypeStruct((T_ALL, HY_W), F32), jax.ShapeDtypeStruct((T_ALL, HY_W), F32)],
        compiler_params=_cp(32 << 20, 1),
        name="hyena_pre",
    )(z, z, z, short_w, short_b.reshape(1, 3 * HY_W))


def _dft_tables():
    t0 = jnp.arange(FFT_R, dtype=jnp.int32)[:, None, None]
    k1 = jnp.arange(K1P, dtype=jnp.int32)[None, :, None]
    t1 = jnp.arange(FFT_T1, dtype=jnp.int32)[None, None, :]
    m = (k1 * (FFT_R * t1 + t0)) % N_FFT
    ang = m.astype(F32) * (2.0 * math.pi / N_FFT)
    used = (k1 < K1_USED).astype(F32)
    g_cos = jnp.cos(ang) * used
    g_sin = jnp.sin(ang) * used
    a = jnp.arange(FFT_R, dtype=jnp.int32)
    ang2 = ((a[:, None] * a[None, :]) % FFT_R).astype(F32) * (2.0 * math.pi / FFT_R)
    return g_cos, g_sin, jnp.cos(ang2), jnp.sin(ang2)


def _hyena_spec_kernel(k_ref, gc_ref, gs_ref, fc_ref, fs_ref, wt_ref, kr_ref, ki_ref, ar_ref, ai_ref):
    half = pl.program_id(1)
    kk = lax.broadcasted_iota(jnp.int32, (K1H, 1), 0) + half * K1H
    sign = jnp.where(kk % 2 == 0, 1.0, -1.0).astype(F32)

    dot = lambda a, b: jnp.dot(a, b.astype(BF16), preferred_element_type=F32)

    def stage1(i, carry):
        t0s = [i * FFT_UNROLL + u for u in range(FFT_UNROLL)]
        loaded = [(k_ref[pl.ds(t0, FFT_T1, stride=FFT_R), :], k_ref[pl.ds(L + t0, FFT_T1, stride=FFT_R), :],
                   gc_ref[t0].astype(BF16), gs_ref[t0].astype(BF16)) for t0 in t0s]
        res = [(dot(gc, x_lo) + sign * dot(gc, x_hi), -(dot(gs, x_lo) + sign * dot(gs, x_hi)))
               for x_lo, x_hi, gc, gs in loaded]
        for t0, (a_r, a_i) in zip(t0s, res):
            rows = pl.ds(pl.multiple_of(t0 * K1H, 8), K1H)
            ar_ref[rows, :] = a_r
            ai_ref[rows, :] = a_i
        return carry

    lax.fori_loop(0, FFT_R // FFT_UNROLL, stage1, 0)
    fc = fc_ref[...].astype(BF16)
    fs = fs_ref[...].astype(BF16)

    def stage2(i, carry):
        kls = [i * FFT_UNROLL + u for u in range(FFT_UNROLL)]
        loaded = [(ar_ref[pl.ds(kl, FFT_R, stride=K1H), :], ai_ref[pl.ds(kl, FFT_R, stride=K1H), :]) for kl in kls]
        for kl, (a_r, a_i) in zip(kls, loaded):
            w = wt_ref[half * K1H + kl]
            kr_ref[kl] = (dot(fc, a_r) + dot(fs, a_i)) * w
            ki_ref[kl] = (dot(fc, a_i) - dot(fs, a_r)) * w
        return carry

    lax.fori_loop(0, K1H // FFT_UNROLL, stage2, 0)


def _hyena_spectrum(kfilt, g_cos, g_sin, f_cos, f_sin, wts):
    nct = HY_W // LANE
    gspec = pl.BlockSpec((FFT_R, K1H, FFT_T1), lambda c, h: (0, h, 0))
    fspec = pl.BlockSpec((FFT_R, FFT_R), lambda c, h: (0, 0))
    ospec = pl.BlockSpec((K1H, FFT_R, LANE), lambda c, h: (h, 0, c))
    return pl.pallas_call(
        _hyena_spec_kernel,
        grid=(nct, 2),
        in_specs=[pl.BlockSpec((N_FFT, LANE), lambda c, h: (0, c)), gspec, gspec, fspec, fspec,
                  pl.BlockSpec(memory_space=pltpu.SMEM)],
        out_specs=[ospec, ospec],
        out_shape=[jax.ShapeDtypeStruct((K1P, FFT_R, HY_W), F32)] * 2,
        scratch_shapes=[pltpu.VMEM((FFT_R * K1H, LANE), F32)] * 2,
        compiler_params=_cp(56 << 20, 0),
        name="hyena_spectrum",
    )(kfilt, g_cos, g_sin, f_cos, f_sin, wts)


def _hyena_fft_kernel(u_ref, gc_ref, gs_ref, ic_ref, is_ref, fc_ref, fs_ref, kr_ref, ki_ref, o_ref, ar_ref, ai_ref):
    bdot = lambda a, b: jnp.dot(a, b.astype(BF16), preferred_element_type=F32)

    def stage1(i, carry):
        t0s = [i * FFT_UNROLL + u for u in range(FFT_UNROLL)]
        xs = [u_ref[pl.ds(t0, FFT_T1, stride=FFT_R), :] for t0 in t0s]
        res = [(bdot(gc_ref[t0], x), -bdot(gs_ref[t0], x)) for t0, x in zip(t0s, xs)]
        for t0, (a_r, a_i) in zip(t0s, res):
            rows = pl.ds(pl.multiple_of(t0 * K1P, 8), K1P)
            ar_ref[rows, :] = a_r
            ai_ref[rows, :] = a_i
        return carry

    lax.fori_loop(0, FFT_R // FFT_UNROLL, stage1, 0)
    fc = fc_ref[...]
    fs = fs_ref[...]

    def stage23(i, carry):
        k1s = [i * FFT_UNROLL + u for u in range(FFT_UNROLL)]
        loaded = [(ar_ref[pl.ds(k1, FFT_R, stride=K1P), :], ai_ref[pl.ds(k1, FFT_R, stride=K1P), :],
                   kr_ref[k1], ki_ref[k1]) for k1 in k1s]
        res = []
        for a_r, a_i, k_r, k_i in loaded:
            z_r = bdot(fc, a_r) + bdot(fs, a_i)
            z_i = bdot(fc, a_i) - bdot(fs, a_r)
            y_r = z_r * k_r - z_i * k_i
            y_i = z_r * k_i + z_i * k_r
            res.append((bdot(fc, y_r) - bdot(fs, y_i), bdot(fc, y_i) + bdot(fs, y_r)))
        for k1, (b_r, b_i) in zip(k1s, res):
            ar_ref[pl.ds(k1, FFT_R, stride=K1P), :] = b_r
            ai_ref[pl.ds(k1, FFT_R, stride=K1P), :] = b_i
        return carry

    lax.fori_loop(0, -(-K1_USED // FFT_UNROLL), stage23, 0)

    def stage4(i, carry):
        t0s = [i * FFT_UNROLL + u for u in range(FFT_UNROLL)]
        loaded = []
        for t0 in t0s:
            rows = pl.ds(pl.multiple_of(t0 * K1P, 8), K1P)
            loaded.append((ar_ref[rows, :], ai_ref[rows, :]))
        res = [bdot(ic_ref[t0], b_r) - bdot(is_ref[t0], b_i) for t0, (b_r, b_i) in zip(t0s, loaded)]
        for t0, y in zip(t0s, res):
            o_ref[pl.ds(t0, FFT_T1, stride=FFT_R), :] = y
        return carry

    lax.fori_loop(0, FFT_R // FFT_UNROLL, stage4, 0)


def _hyena_fft(u, g_cos_bf, g_sin_bf, i_cos_bf, i_sin_bf, f_cos_bf, f_sin_bf, k_r, k_i, layer):
    nct = HY_W // LANE
    one = pl.Buffered(1)
    gspec = pl.BlockSpec((FFT_R, K1P, FFT_T1), lambda c, b: (0, 0, 0), pipeline_mode=one)
    ispec = pl.BlockSpec((FFT_R, FFT_T1, K1P), lambda c, b: (0, 0, 0), pipeline_mode=one)
    fspec = pl.BlockSpec((FFT_R, FFT_R), lambda c, b: (0, 0), pipeline_mode=one)
    kspec = pl.BlockSpec((None, K1P, FFT_R, LANE), lambda c, b: (layer, 0, 0, c), pipeline_mode=one)
    return pl.pallas_call(
        _hyena_fft_kernel,
        grid=(nct, B),
        in_specs=[pl.BlockSpec((L, LANE), lambda c, b: (b, c)), gspec, gspec, ispec, ispec, fspec, fspec, kspec, kspec],
        out_specs=pl.BlockSpec((L, LANE), lambda c, b: (b, c)),
        out_shape=jax.ShapeDtypeStruct((T_LAT, HY_W), F32),
        scratch_shapes=[pltpu.VMEM((FFT_R * K1P, LANE), F32)] * 2,
        compiler_params=_cp(56 << 20, 2),
        name="hyena_fft",
    )(u, g_cos_bf, g_sin_bf, i_cos_bf, i_sin_bf, f_cos_bf, f_sin_bf, k_r, k_i)


def _hyena_ctx_kernel(u_ref, k_ref, dc_ref, ds_ref, o_ref):
    dot = lambda a, b: jnp.dot(a, b, precision=HIGHEST, preferred_element_type=F32)
    dc = dc_ref[...]
    ds = ds_ref[...]
    u = u_ref[...]
    kf = k_ref[...]
    u_r = dot(dc[:, 0:C], u)
    u_i = -dot(ds[:, 0:C], u)
    k_r = dot(dc, kf)
    k_i = -dot(ds, kf)
    y_r = u_r * k_r - u_i * k_i
    y_i = u_r * k_i + u_i * k_r
    o_ref[...] = (dot(dc[0:C, :], y_r) - dot(ds[0:C, :], y_i)) * (1.0 / (2 * C))


def _hyena_ctx(u, kfilt_ctx, d_cos, d_sin):
    full = lambda b: (0, 0)
    return pl.pallas_call(
        _hyena_ctx_kernel,
        grid=(B,),
        in_specs=[pl.BlockSpec((C, HY_W), lambda b: (T_LAT // C + b, 0)),
                  pl.BlockSpec((2 * C, HY_W), full), pl.BlockSpec((2 * C, 2 * C), full), pl.BlockSpec((2 * C, 2 * C), full)],
        out_specs=pl.BlockSpec((C, HY_W), lambda b: (b, 0)),
        out_shape=jax.ShapeDtypeStruct((T_CTX, HY_W), F32),
        compiler_params=_cp(32 << 20, 1),
        name="hyena_ctx",
    )(u, kfilt_ctx, d_cos, d_sin)


def _hyena_filter(n, w1, b1, freq, w2, b2, w3):
    t = jnp.linspace(0.0, 1.0, n, dtype=F32)[:, None]
    bands = (FILTER_EMB - 1) // 2
    w = 2.0 * math.pi * jnp.arange(n, dtype=F32)[:, None] / n
    f = jnp.linspace(1e-4, bands - 1, bands, dtype=F32)[None, :]
    z = jnp.concatenate([t, jnp.cos(f * w), -jnp.sin(f * w)], axis=-1)
    mm = functools.partial(jnp.matmul, precision=HIGHEST)
    h = jnp.sin(freq * (mm(z, w1) + b1))
    h = jnp.sin(freq * (mm(h, w2) + b2))
    deltas = jnp.abs(jnp.linspace(math.log(DECAY_TARGET) / DECAY_FAST, math.log(DECAY_TARGET) / DECAY_SLOW,
                                  HY_W, dtype=F32))
    decay = jnp.exp(-t * deltas[None, :])
    h_fwd = mm(h, w3[:, :HY_W]) * decay
    h_bwd_rev = mm(h[::-1], w3[:, HY_W:]) * decay[::-1]
    k = jnp.concatenate([h_fwd, jnp.zeros((1, HY_W), F32), h_bwd_rev[:-1]], axis=0)
    return k / jnp.sum(jnp.abs(k), axis=0, keepdims=True)


def _s5_matrices(a_re, a_im, log_dt, b_re, b_im, c_re, c_im, d_skip):
    dt = jnp.exp(log_dt)[:, :, None]
    lam_re = jnp.minimum(a_re, -1e-4)
    mag1 = jnp.exp(lam_re * dt)
    lbr = mag1 * jnp.cos(a_im * dt)
    lbi = mag1 * jnp.sin(a_im * dt)
    den = lam_re * lam_re + a_im * a_im
    fr = ((lbr - 1.0) * lam_re + lbi * a_im) / den
    fi = (lbi * lam_re - (lbr - 1.0) * a_im) / den
    bbr = fr[..., None] * b_re - fi[..., None] * b_im
    bbi = fr[..., None] * b_im + fi[..., None] * b_re
    j = jnp.arange(S5_CH + 1, dtype=F32)[:, None, None, None]
    magj = jnp.exp(j * (lam_re * dt)[None])
    pr = magj * jnp.cos(j * (a_im * dt)[None])
    pi = magj * jnp.sin(j * (a_im * dt)[None])
    hi = functools.partial(jnp.einsum, precision=HIGHEST)
    lbr_j = pr[..., None] * bbr[None] - pi[..., None] * bbi[None]
    lbi_j = pr[..., None] * bbi[None] + pi[..., None] * bbr[None]
    m = hi('dgop,jdgpi->jdgoi', c_re, lbr_j) - hi('dgop,jdgpi->jdgoi', c_im, lbi_j)
    eye_g = jnp.eye(S5_GROUPS, dtype=F32)
    s = jnp.arange(S5_CH)
    blocks = jnp.einsum('jdgoi,gh->djgiho', m[0:S5_CH], eye_g).reshape(2, S5_CH, S5_W, S5_W)
    lag0 = blocks[0, 0] + blocks[1, 0] + jnp.diag(d_skip)
    e_all = jnp.concatenate([blocks[1, S5_CH - 1:0:-1], lag0[None], blocks[0, 1:S5_CH]], axis=0).astype(BF16)
    sf_r = lbr_j[S5_CH - 1 - s, 0]
    sf_i = lbi_j[S5_CH - 1 - s, 0]
    sb_r = lbr_j[s, 1]
    sb_i = lbi_j[s, 1]
    st = jnp.stack([sf_r, sf_i, sb_r, sb_i], axis=0)
    rr = jnp.arange(S5_ROWW, dtype=jnp.int32)
    cc = jnp.arange(S5_W, dtype=jnp.int32)
    ws = _s5_expand(jnp.transpose(st, (1, 2, 4, 0, 3)).reshape(S5_ROWW, S5_W),
                    (cc[:, None] // S5_STATE == rr[None, :] // S5_NSTATE)
                    & (cc[:, None] % S5_STATE == rr[None, :] % S5_STATE),
                    (rr // S5_GROUP) % S5_GROUPS, (rr % S5_NSTATE) // S5_STATE)
    tt = jnp.arange(S5_CH)
    cf_r = c_re[0][None] * pr[tt + 1, 0][:, :, None, :] - c_im[0][None] * pi[tt + 1, 0][:, :, None, :]
    cf_i = c_re[0][None] * pi[tt + 1, 0][:, :, None, :] + c_im[0][None] * pr[tt + 1, 0][:, :, None, :]
    cb_r = c_re[1][None] * pr[S5_CH - tt, 1][:, :, None, :] - c_im[1][None] * pi[S5_CH - tt, 1][:, :, None, :]
    cb_i = c_re[1][None] * pi[S5_CH - tt, 1][:, :, None, :] + c_im[1][None] * pr[S5_CH - tt, 1][:, :, None, :]
    ct = jnp.stack([cf_r, -cf_i, cb_r, -cb_i], axis=0)
    wc = _s5_expand(jnp.transpose(ct, (0, 2, 4, 1, 3)).reshape(4 * S5_NSTATE, S5_W),
                    (cc[:, None] // S5_GROUP == rr[None, :] // S5_W)
                    & (cc[:, None] % S5_GROUP == rr[None, :] % S5_GROUP),
                    (rr % S5_NSTATE) // S5_STATE, (rr % S5_W) // S5_GROUP)
    lam_p = jnp.stack([jnp.stack([pr[S5_CH, 0], pi[S5_CH, 0]]), jnp.stack([pr[S5_CH, 1], pi[S5_CH, 1]])])
    return ws, e_all, wc, lam_p.reshape(2, 2, 1, S5_NSTATE)


S5_TN = 512
S5_NS = S5_ROWW // S5_W


def _s5_expand_kernel(a_ref, ex_ref, rg_ref, cg_ref, o_ref):
    v = jnp.dot(a_ref[...], ex_ref[...], preferred_element_type=F32)
    o_ref[...] = jnp.where(rg_ref[...] == cg_ref[...], v, 0.0).astype(BF16)


def _s5_expand(compact, placement, row_group, col_group):
    n = compact.shape[0]
    return pl.pallas_call(
        _s5_expand_kernel,
        grid=(S5_ROWW // S5_TN,),
        in_specs=[pl.BlockSpec((n, S5_W), lambda j: (0, 0)), pl.BlockSpec((S5_W, S5_TN), lambda j: (0, j)),
                  pl.BlockSpec((n, 1), lambda j: (0, 0)), pl.BlockSpec((1, S5_TN), lambda j: (0, j))],
        out_specs=pl.BlockSpec((n, S5_TN), lambda j: (0, j)),
        out_shape=jax.ShapeDtypeStruct((n, S5_ROWW), BF16),
        compiler_params=_cp(32 << 20, 0),
        name="s5_expand",
    )(compact.astype(BF16), placement.astype(BF16), row_group.reshape(n, 1), col_group.reshape(1, S5_ROWW))


def _s5_in_kernel(u_ref, ws_ref, e_ref, s_ref, o_ref):
    j = pl.program_id(0)

    @pl.when(j < S5_NS)
    def _():
        s_ref[...] = jnp.dot(u_ref[...], ws_ref[...], preferred_element_type=F32)

    @pl.when(j >= S5_NS)
    def _():
        t = j - S5_NS
        acc = jnp.dot(u_ref[:, 0:S5_W], e_ref[S5_CH - 1 + t], preferred_element_type=F32)
        for s in range(1, S5_CH):
            acc = acc + jnp.dot(u_ref[:, s * S5_W:(s + 1) * S5_W], e_ref[S5_CH - 1 + t - s],
                                preferred_element_type=F32)
        o_ref[...] = acc


def _s5_in(u_rows, ws, e_all, layer):
    return pl.pallas_call(
        _s5_in_kernel,
        grid=(2 * S5_NS,),
        in_specs=[pl.BlockSpec((N_CHUNK, S5_ROWW), lambda j: (0, 0)),
                  pl.BlockSpec((None, S5_ROWW, S5_W), lambda j: (layer, 0, jnp.minimum(j, S5_NS - 1))),
                  pl.BlockSpec((None, 2 * S5_CH - 1, S5_W, S5_W), lambda j: (layer, 0, 0, 0))],
        out_specs=[pl.BlockSpec((N_CHUNK, S5_W), lambda j: (0, jnp.minimum(j, S5_NS - 1))),
                   pl.BlockSpec((N_CHUNK, S5_W), lambda j: (0, jnp.maximum(j - S5_NS, 0)))],
        out_shape=[jax.ShapeDtypeStruct((N_CHUNK, 4 * S5_NSTATE), F32), jax.ShapeDtypeStruct((N_CHUNK, S5_ROWW), F32)],
        compiler_params=_cp(48 << 20, 1),
        name="s5_in",
    )(u_rows, ws, e_all)


def _s5_scan_kernel(s_ref, lam_ref, h_ref):
    lam = [[lam_ref[d, p] for p in range(2)] for d in range(2)]

    def step(b, d, chunk, h):
        row = pl.ds(chunk, 1)
        cols_r = pl.ds(d * 2 * S5_NSTATE, S5_NSTATE)
        cols_i = pl.ds(d * 2 * S5_NSTATE + S5_NSTATE, S5_NSTATE)
        h_ref[row, cols_r] = h[0]
        h_ref[row, cols_i] = h[1]
        s_r = s_ref[row, cols_r]
        s_i = s_ref[row, cols_i]
        lr, li = lam[d]
        return (lr * h[0] - li * h[1] + s_r, lr * h[1] + li * h[0] + s_i)

    def chain_order(b, d, n_ctx_done):
        ctx0 = B * LAT_CHUNKS + b * CTX_CHUNKS
        lat0 = b * LAT_CHUNKS
        if d == 0:
            return (lambda i: ctx0 + i), (lambda i: lat0 + i)
        return (lambda i: ctx0 + CTX_CHUNKS - 1 - i), (lambda i: lat0 + LAT_CHUNKS - 1 - i)

    chains = [(b, d) for b in range(B) for d in range(2)]
    zero = jnp.zeros((1, S5_NSTATE), F32)
    init = tuple((zero, zero) for _ in chains)

    def phase(n_steps, which, carry):
        def body(i, hs):
            out = []
            for (b, d), h in zip(chains, hs):
                order = chain_order(b, d, 0)[which]
                out.append(step(b, d, order(i), h))
            return tuple(out)
        return lax.fori_loop(0, n_steps, body, carry)

    carry = phase(CTX_CHUNKS, 0, init)
    phase(LAT_CHUNKS, 1, carry)


def _s5_scan(s_rows, lam_p):
    return pl.pallas_call(
        _s5_scan_kernel,
        out_shape=jax.ShapeDtypeStruct((N_CHUNK, 4 * S5_NSTATE), F32),
        compiler_params=pltpu.CompilerParams(vmem_limit_bytes=48 << 20),
        name="s5_scan",
    )(s_rows, lam_p)


S5_TM = N_CHUNK // 2
S5_TPN = S5_TN // S5_W


def _s5_out_kernel(h_ref, w_ref, y_ref, *rest):
    o_refs, hb_ref = rest[:-1], rest[-1]
    j = pl.program_id(1)

    @pl.when(j == 0)
    def _():
        hb_ref[...] = h_ref[...].astype(BF16)

    acc = jnp.dot(hb_ref[...], w_ref[...], preferred_element_type=F32) + y_ref[...]
    for tt in range(S5_TPN):
        for h, o_ref in enumerate(o_refs):
            o_ref[pl.ds(j * S5_TPN + tt, S5_TM, stride=S5_CH), :] = (
                acc[:, tt * S5_W + h * LANE:tt * S5_W + (h + 1) * LANE])


def _s5_out(h_rows, wc, y_in, layer):
    nn = S5_ROWW // S5_TN
    n_out = S5_W // LANE
    return pl.pallas_call(
        _s5_out_kernel,
        grid=(N_CHUNK // S5_TM, nn),
        in_specs=[pl.BlockSpec((S5_TM, 4 * S5_NSTATE), lambda i, j: (i, 0), pipeline_mode=pl.Buffered(1)),
                  pl.BlockSpec((None, 4 * S5_NSTATE, S5_TN), lambda i, j: (layer, 0, j)),
                  pl.BlockSpec((S5_TM, S5_TN), lambda i, j: (i, j))],
        out_specs=[pl.BlockSpec((S5_TM * S5_CH, LANE), lambda i, j: (i, 0))] * n_out,
        out_shape=[jax.ShapeDtypeStruct((T_ALL, LANE), F32)] * n_out,
        scratch_shapes=[pltpu.VMEM((S5_TM, 4 * S5_NSTATE), BF16)],
        compiler_params=_cp(52 << 20, 2),
        name="s5_out",
    )(h_rows, wc, y_in)


def _rms(x):
    return x * lax.rsqrt(jnp.mean(x * x, axis=-1, keepdims=True) + LN_EPS)


def _layer_norm(x, g, b):
    mu = jnp.mean(x, axis=-1, keepdims=True)
    xc = x - mu
    var = jnp.mean(xc * xc, axis=-1, keepdims=True)
    return xc * lax.rsqrt(var + LN_EPS) * g + b


def _merge_kernel(x_ref, attn_ref, conv_ref, hu_ref, x0_ref, s5a_ref, s5b_ref, g1_ref, sh2_ref, sc2_ref, mixg_ref,
                  hyd_ref, wglu_ref, wout_ref, lng_ref, lnb_ref, o_ref, u2_ref, u2p_ref):
    hy = (conv_ref[...] + hu_ref[...] * hyd_ref[...]) * x0_ref[...]
    g = jax.nn.gelu(jnp.concatenate([s5a_ref[...], s5b_ref[...]], axis=-1))
    s5 = g * jax.nn.sigmoid(jnp.dot(g.astype(BF16), wglu_ref[...], preferred_element_type=F32))
    mixg = mixg_ref[...]
    parts = [_rms(attn_ref[...]) * mixg[:, 0:ATTN_W],
             _rms(hy) * mixg[:, ATTN_W:ATTN_W + HY_W],
             _rms(s5) * mixg[:, ATTN_W + HY_W:MIX_W]]
    mix = jnp.concatenate(parts, axis=-1).astype(BF16)
    o = jnp.dot(mix, wout_ref[...], preferred_element_type=F32)
    x1 = _layer_norm(ALPHA * x_ref[...] + g1_ref[...] * o, lng_ref[...], lnb_ref[...])
    o_ref[...] = x1
    u2 = x1 * (1.0 + sc2_ref[...]) + sh2_ref[...]
    u2_ref[...] = u2
    u2p_ref[...] = _pack_pairs(u2)


def _merge(n_rows, xall, attn, conv, hu, x0c, s5y, g1, sh2, sc2, mix_g, hy_d, wglu_bf, wout_bf, ln_g, ln_b):
    nt = n_rows // TM
    row = lambda i: (i, 0)
    full = lambda i: (0, 0)
    return pl.pallas_call(
        _merge_kernel,
        grid=(nt,),
        in_specs=[pl.BlockSpec((TM, D), row), pl.BlockSpec((TM, ATTN_W), row), pl.BlockSpec((TM, HY_W), row),
                  pl.BlockSpec((TM, HY_W), row), pl.BlockSpec((TM, HY_W), row),
                  pl.BlockSpec((TM, LANE), row), pl.BlockSpec((TM, LANE), row),
                  _vec_spec(TM), _vec_spec(TM), _vec_spec(TM), pl.BlockSpec((1, MIX_W), full),
                  pl.BlockSpec((1, HY_W), full), pl.BlockSpec((S5_W, S5_W), full), pl.BlockSpec((MIX_W, D), full),
                  pl.BlockSpec((1, D), full), pl.BlockSpec((1, D), full)],
        out_specs=[pl.BlockSpec((TM, D), row), pl.BlockSpec((TM, D), row), pl.BlockSpec((TM, HALF_D), row)],
        out_shape=[jax.ShapeDtypeStruct((n_rows, D), F32), jax.ShapeDtypeStruct((n_rows, D), F32),
                   jax.ShapeDtypeStruct((n_rows, HALF_D), jnp.int32)],
        compiler_params=_cp(48 << 20, 1),
        name="merge",
    )(xall, attn, conv, hu, x0c, s5y[0], s5y[1], g1, sh2, sc2, mix_g.reshape(1, MIX_W), hy_d.reshape(1, HY_W), wglu_bf,
      wout_bf, ln_g.reshape(1, D), ln_b.reshape(1, D))


def _router_kernel(u_ref, wt_ref, b_ref, e_ref, g_ref):
    logits = lax.dot_general(wt_ref[...], u_ref[...], (((1,), (1,)), ((), ())), precision=HIGHEST,
                             preferred_element_type=F32)
    scores = jax.nn.sigmoid(logits)
    biased = scores + b_ref[...]
    ninf = -jnp.inf
    grow = lax.broadcasted_iota(jnp.int32, (EGROUP, TM), 0)
    groups = [biased[gi * EGROUP:(gi + 1) * EGROUP] for gi in range(N_EGROUPS)]
    gscore = []
    for vals in groups:
        m1 = jnp.max(vals, axis=0, keepdims=True)
        i1 = jnp.min(jnp.where(vals == m1, grow, EGROUP), axis=0, keepdims=True)
        m2 = jnp.max(jnp.where(grow == i1, ninf, vals), axis=0, keepdims=True)
        gscore.append(m1 + m2)
    kept = []
    for gi in range(N_EGROUPS):
        rank = jnp.zeros((1, TM), jnp.int32)
        for gj in range(N_EGROUPS):
            if gj == gi:
                continue
            ahead = (gscore[gj] > gscore[gi]) | ((gscore[gj] == gscore[gi]) & (gj < gi))
            rank = rank + ahead.astype(jnp.int32)
        kept.append(jnp.where(rank < TOPK_GROUPS, groups[gi], ninf))
    masked = jnp.concatenate(kept, axis=0)
    row = lax.broadcasted_iota(jnp.int32, (N_EXPERTS, TM), 0)
    gates = []
    gsum = jnp.zeros((1, TM), F32)
    for kk in range(TOP_K):
        m = jnp.max(masked, axis=0, keepdims=True)
        idx = jnp.min(jnp.where(masked == m, row, N_EXPERTS), axis=0, keepdims=True)
        hit = row == idx
        gate = jnp.sum(jnp.where(hit, scores, 0.0), axis=0, keepdims=True)
        masked = jnp.where(hit, ninf, masked)
        e_ref[kk:kk + 1, :] = idx
        gates.append(gate)
        gsum = gsum + gate
    for kk in range(TOP_K):
        g_ref[kk:kk + 1, :] = gates[kk] / gsum * ROUTED_SCALE


def _router(n_rows, u2, w_router_t, router_bias):
    nt = n_rows // TM
    col = lambda i: (0, i)
    return pl.pallas_call(
        _router_kernel,
        grid=(nt,),
        in_specs=[pl.BlockSpec((TM, D), lambda i: (i, 0)),
                  pl.BlockSpec((N_EXPERTS, D), lambda i: (0, 0)), pl.BlockSpec((N_EXPERTS, 1), lambda i: (0, 0))],
        out_specs=[pl.BlockSpec((TOP_K, TM), col), pl.BlockSpec((TOP_K, TM), col)],
        out_shape=[jax.ShapeDtypeStruct((TOP_K, n_rows), jnp.int32), jax.ShapeDtypeStruct((TOP_K, n_rows), F32)],
        compiler_params=_cp(32 << 20, 1),
        name="router",
    )(u2, w_router_t, router_bias.reshape(N_EXPERTS, 1))


def _dispatch(top_e):
    t = top_e.shape[1]
    tk = t * TOP_K
    nblk = tk // MOE_BLOCK
    n_steps = nblk + N_EXPERTS
    flat_e = top_e.reshape(tk)
    pos = jnp.arange(tk, dtype=jnp.int32)
    pos_bits = (tk - 1).bit_length()
    assert pos_bits + (N_EXPERTS - 1).bit_length() < 32
    order = lax.sort(flat_e * (1 << pos_bits) + pos) % (1 << pos_bits)
    _, inv = lax.sort((order, pos), num_keys=1)
    experts = jnp.arange(N_EXPERTS, dtype=jnp.int32)
    counts = jnp.sum((flat_e[None, :] == experts[:, None]).astype(jnp.int32), axis=1)
    ends = jnp.cumsum(counts)
    starts = ends - counts
    fb = starts // MOE_BLOCK
    npairs = jnp.where(counts > 0, (ends - 1) // MOE_BLOCK - fb + 1, 0)
    pend = jnp.cumsum(npairs)
    poff = pend - npairs
    n_pairs = pend[-1]
    s = jnp.arange(n_steps, dtype=jnp.int32)
    sc = jnp.minimum(s, n_pairs - 1)
    pe = jnp.sum((pend[None, :] <= sc[:, None]).astype(jnp.int32), axis=1)
    pb = fb[pe] + (sc - poff[pe])
    lo = jnp.where(s < n_pairs, jnp.maximum(starts[pe] - pb * MOE_BLOCK, 0), 0)
    hi = jnp.where(s < n_pairs, jnp.minimum(ends[pe] - pb * MOE_BLOCK, MOE_BLOCK), 0)
    last_of_block = ((s == n_pairs - 1) | ((s + 1 < n_pairs) & (jnp.roll(pb, -1) != pb))).astype(jnp.int32)
    used = counts > 0
    ordinal = jnp.cumsum(used.astype(jnp.int32)) - 1
    later = jnp.where(used, experts, N_EXPERTS)
    next_used = lax.cummin(jnp.concatenate([later[1:], jnp.full((1,), N_EXPERTS, jnp.int32)]), reverse=True)
    nxt = next_used[pe]
    nxt = jnp.where(nxt < N_EXPERTS, nxt, pe)
    parity = ordinal[pe] % 2
    w_even = jnp.where(parity == 0, pe, nxt)
    w_odd = jnp.where(parity == 1, pe, nxt)
    return dict(pe=pe, pb=pb, lo=lo, hi=hi, last=last_of_block, parity=parity, w_even=w_even, w_odd=w_odd,
                tok=order % t, comb=inv)


X_SLOTS = 4


def _expert_kernel(pe, pb, plo, phi, plast, ppar, pwe, pwo, xs_hbm, wg0_ref, wu0_ref, wd0_ref, wg1_ref, wu1_ref,
                   wd1_ref, out_hbm, xbuf, xsem, obuf, osem, wgb, wub, wdb):
    del pwe, pwo
    s = pl.program_id(0)
    prev = jnp.maximum(s - 1, 0)
    n_blocks = out_hbm.shape[0] // MOE_BLOCK
    new_expert = (s == 0) | (pe[s] != pe[prev])

    def x_copy(b, sl):
        rows = pl.ds(pl.multiple_of(b * MOE_BLOCK, MOE_BLOCK), MOE_BLOCK)
        return pltpu.make_async_copy(xs_hbm.at[rows], xbuf.at[sl], xsem.at[sl])

    @pl.when(s == 0)
    def _():
        for b in range(X_SLOTS - 1):
            x_copy(b, b).start()

    for par, (wg_ref, wu_ref, wd_ref) in enumerate(((wg0_ref, wu0_ref, wd0_ref), (wg1_ref, wu1_ref, wd1_ref))):
        @pl.when(new_expert & (ppar[s] == par))
        def _():
            wgb[...] = wg_ref[...].astype(BF16)
            wub[...] = wu_ref[...].astype(BF16)
            wdb[...] = wd_ref[...].astype(BF16)

    lo = plo[s]
    hi = phi[s]
    blk = pb[s]
    slot = blk % 2
    first_of_block = (s == 0) | (blk != pb[prev])

    def out_copy(b, sl):
        rows = pl.ds(pl.multiple_of(b * MOE_BLOCK, MOE_BLOCK), MOE_BLOCK)
        return pltpu.make_async_copy(obuf.at[sl], out_hbm.at[rows], osem.at[sl])

    @pl.when(first_of_block)
    def _():
        ahead = blk + (X_SLOTS - 1)

        @pl.when(ahead < n_blocks)
        def _():
            x_copy(ahead, ahead % X_SLOTS).start()
        x_copy(blk, blk % X_SLOTS).wait()

    def ffn():
        x_lo, x_hi = _unpack_pairs(xbuf[blk % X_SLOTS])
        x_lo = x_lo.astype(BF16)
        x_hi = x_hi.astype(BF16)
        hg = (jnp.dot(x_lo, wgb[0:HALF_D], preferred_element_type=F32)
              + jnp.dot(x_hi, wgb[HALF_D:D], preferred_element_type=F32))
        hu = (jnp.dot(x_lo, wub[0:HALF_D], preferred_element_type=F32)
              + jnp.dot(x_hi, wub[HALF_D:D], preferred_element_type=F32))
        h = (hg * jax.nn.sigmoid(hg)) * hu
        return _pack_pairs(jnp.dot(h.astype(BF16), wdb[...], preferred_element_type=F32))

    @pl.when((hi > lo) & first_of_block)
    def _():
        @pl.when(blk >= 2)
        def _():
            out_copy(blk - 2, slot).wait()
        obuf[slot] = ffn()

    @pl.when((hi > lo) & jnp.logical_not(first_of_block))
    def _():
        row = lax.broadcasted_iota(jnp.int32, (MOE_BLOCK, HALF_D), 0)
        obuf[slot] = jnp.where((row >= lo) & (row < hi), ffn(), obuf[slot])

    @pl.when(plast[s] == 1)
    def _():
        out_copy(blk, slot).start()

    @pl.when(s == pl.num_programs(0) - 1)
    def _():
        out_copy(n_blocks - 2, (n_blocks - 2) % 2).wait()
        out_copy(n_blocks - 1, (n_blocks - 1) % 2).wait()


def _experts(xs, disp, wg, wu, wd, layer):
    tk = xs.shape[0]
    n_steps = disp['pe'].shape[0]
    assert tk // MOE_BLOCK >= X_SLOTS
    weven = lambda shape: pl.BlockSpec((None, None) + shape, lambda s, *p: (layer, p[6][s], 0, 0))
    wodd = lambda shape: pl.BlockSpec((None, None) + shape, lambda s, *p: (layer, p[7][s], 0, 0))
    grid_spec = pltpu.PrefetchScalarGridSpec(
        num_scalar_prefetch=8,
        grid=(n_steps,),
        in_specs=[pl.BlockSpec(memory_space=pl.ANY),
                  weven((D, EXPERT_FF)), weven((D, EXPERT_FF)), weven((EXPERT_FF, D)),
                  wodd((D, EXPERT_FF)), wodd((D, EXPERT_FF)), wodd((EXPERT_FF, D))],
        out_specs=pl.BlockSpec(memory_space=pl.ANY),
        scratch_shapes=[pltpu.VMEM((X_SLOTS, MOE_BLOCK, HALF_D), jnp.int32), pltpu.SemaphoreType.DMA((X_SLOTS,)),
                        pltpu.VMEM((2, MOE_BLOCK, HALF_D), jnp.int32), pltpu.SemaphoreType.DMA((2,)),
                        pltpu.VMEM((D, EXPERT_FF), BF16), pltpu.VMEM((D, EXPERT_FF), BF16),
                        pltpu.VMEM((EXPERT_FF, D), BF16)],
    )
    return pl.pallas_call(
        _expert_kernel,
        grid_spec=grid_spec,
        out_shape=jax.ShapeDtypeStruct((tk, HALF_D), jnp.int32),
        compiler_params=_cp(40 << 20, 1),
        name="experts",
    )(disp['pe'], disp['pb'], disp['lo'], disp['hi'], disp['last'], disp['parity'], disp['w_even'], disp['w_odd'],
      xs, wg, wu, wd, wg, wu, wd)


SC_ROWS = 64


def _row_gather(table, idx):
    n = idx.shape[0]
    d = table.shape[1]
    mesh = plsc.VectorSubcoreMesh(core_axis_name="c", subcore_axis_name="s")
    n_workers = mesh.num_cores * mesh.num_subcores
    per_worker = n // n_workers
    assert per_worker * n_workers == n and per_worker % SC_ROWS == 0

    @functools.partial(
        pl.kernel, mesh=mesh,
        out_type=jax.ShapeDtypeStruct((n, d), table.dtype),
        scratch_types=[pltpu.VMEM((SC_ROWS,), jnp.int32), pltpu.VMEM((SC_ROWS, d), table.dtype),
                       pltpu.SemaphoreType.DMA],
    )
    def gather(table_hbm, idx_hbm, out_hbm, idx_v, rows_v, sem):
        worker = lax.axis_index("s") * mesh.num_cores + lax.axis_index("c")
        base = worker * per_worker

        @pl.loop(0, per_worker // SC_ROWS)
        def _(c):
            off = pl.multiple_of(base + c * SC_ROWS, 8)
            pltpu.sync_copy(idx_hbm.at[pl.ds(off, SC_ROWS)], idx_v)
            pltpu.async_copy(table_hbm.at[idx_v], rows_v, sem).wait()
            pltpu.sync_copy(rows_v, out_hbm.at[pl.ds(off, SC_ROWS)])

    return gather(table, idx)


FM = 128


def _ffn_out_kernel(x_ref, u_ref, r_ref, gate_ref, g2_ref, wsg_ref, wsu_ref, wsd_ref, lng_ref, lnb_ref, o_ref):
    u_lo, u_hi = _unpack_pairs(u_ref[...])
    u_lo = u_lo.astype(BF16)
    u_hi = u_hi.astype(BF16)
    hg = (jnp.dot(u_lo, wsg_ref[0:HALF_D], preferred_element_type=F32)
          + jnp.dot(u_hi, wsg_ref[HALF_D:D], preferred_element_type=F32))
    hu = (jnp.dot(u_lo, wsu_ref[0:HALF_D], preferred_element_type=F32)
          + jnp.dot(u_hi, wsu_ref[HALF_D:D], preferred_element_type=F32))
    f = jnp.dot(((hg * jax.nn.sigmoid(hg)) * hu).astype(BF16), wsd_ref[...], preferred_element_type=F32)
    gates = jnp.transpose(jnp.concatenate([gate_ref[...], jnp.zeros((FM - TOP_K, FM), F32)], axis=0))
    r_lo = jnp.zeros((FM, HALF_D), F32)
    r_hi = jnp.zeros((FM, HALF_D), F32)
    for kk in range(TOP_K):
        k_lo, k_hi = _unpack_pairs(r_ref[kk])
        gk = gates[:, kk:kk + 1]
        r_lo = r_lo + k_lo * gk
        r_hi = r_hi + k_hi * gk
    f = jnp.concatenate([r_lo, r_hi], axis=1) + f
    o_ref[...] = _layer_norm(ALPHA * x_ref[...] + g2_ref[...] * f, lng_ref[...], lnb_ref[...])


def _ffn_out(n_rows, x1, u2p, routed, gate, g2, wsg_bf, wsu_bf, wsd_bf, ln_g, ln_b):
    row = lambda i: (i, 0)
    full = lambda i: (0, 0)
    return pl.pallas_call(
        _ffn_out_kernel,
        grid=(n_rows // FM,),
        in_specs=[pl.BlockSpec((FM, D), row), pl.BlockSpec((FM, HALF_D), row),
                  pl.BlockSpec((TOP_K, FM, HALF_D), lambda i: (0, i, 0)), pl.BlockSpec((TOP_K, FM), lambda i: (0, i)),
                  _vec_spec(FM), pl.BlockSpec((D, EXPERT_FF), full), pl.BlockSpec((D, EXPERT_FF), full),
                  pl.BlockSpec((EXPERT_FF, D), full), pl.BlockSpec((1, D), full), pl.BlockSpec((1, D), full)],
        out_specs=pl.BlockSpec((FM, D), row),
        out_shape=jax.ShapeDtypeStruct((n_rows, D), F32),
        compiler_params=_cp(40 << 20, 1),
        name="ffn_out",
    )(x1, u2p, routed, gate, g2, wsg_bf, wsu_bf, wsd_bf, ln_g.reshape(1, D), ln_b.reshape(1, D))


def _rope_tables():
    t = jnp.arange(L, dtype=jnp.int32)
    row = (t // GRID_W).astype(F32)
    col = (t % GRID_W).astype(F32)
    inv_freq = ROPE_BASE ** (-jnp.arange(0, AXIS_DIM, 2, dtype=F32) / AXIS_DIM)
    half = AXIS_DIM // 2

    def axis(pos):
        ang = pos[:, None] * inv_freq[None, :]
        c = jnp.cos(ang)
        s = jnp.sin(ang)
        return jnp.concatenate([c, c], axis=1), jnp.concatenate([-s, s], axis=1)

    cr, sr = axis(row)
    cc, sc = axis(col)
    cos_h = jnp.concatenate([cr, cc], axis=1)
    sin_h = jnp.concatenate([sr, sc], axis=1)
    cos_l = jnp.tile(cos_h, (B, LANE // HEAD_DIM))
    sin_l = jnp.tile(sin_h, (B, LANE // HEAD_DIM))
    cos_t = jnp.concatenate([cos_l, jnp.ones((T_CTX, LANE), F32)], axis=0)
    sin_t = jnp.concatenate([sin_l, jnp.zeros((T_CTX, LANE), F32)], axis=0)
    del half
    return cos_t, sin_t


def kernel(x, c, ctx, c_ctx, w_ada, b_ada, w_in, w_out, sink, mix_g, hy_short_w, hy_short_b, hy_w1, hy_b1, hy_freq,
           hy_w2, hy_b2, hy_w3, hy_d, s5_a_re, s5_a_im, s5_log_dt, s5_b_re, s5_b_im, s5_c_re, s5_c_im, s5_d, s5_w_glu,
           ln1_g, ln1_b, ln2_g, ln2_b, w_router, router_bias, w_exp_gate, w_exp_up, w_exp_down, w_sh_gate, w_sh_up,
           w_sh_down):
    xall = jnp.concatenate([x.reshape(T_LAT, D), ctx.reshape(T_CTX, D)], axis=0)
    cvec = jnp.concatenate([c, c_ctx[None, :], jnp.zeros((8 - B - 1, D), F32)], axis=0)
    mod = _ada(cvec, w_ada, b_ada)[:, 0:B + 1, :].reshape(DEPTH, B + 1, 6, 1, D)

    cos_t, sin_t = _rope_tables()
    g_cos, g_sin, f_cos, f_sin = _dft_tables()
    g_cos_bf, g_sin_bf = g_cos.astype(BF16), g_sin.astype(BF16)
    i_cos_bf = jnp.swapaxes(g_cos, 1, 2).astype(BF16)
    i_sin_bf = jnp.swapaxes(g_sin, 1, 2).astype(BF16)
    f_cos_bf, f_sin_bf = f_cos.astype(BF16), f_sin.astype(BF16)
    k1 = jnp.arange(K1P)
    spec_w = jnp.where((k1 == 0) | (k1 == FFT_R // 2), 1.0, 2.0) * (k1 < K1_USED) / N_FFT
    spec_w = spec_w.astype(F32)
    kt = jnp.arange(2 * C, dtype=jnp.int32)
    ang_c = ((kt[:, None] * kt[None, :]) % (2 * C)).astype(F32) * (2.0 * math.pi / (2 * C))
    d_cos, d_sin = jnp.cos(ang_c), jnp.sin(ang_c)

    filt_params = (hy_w1, hy_b1, hy_freq, hy_w2, hy_b2, hy_w3)
    filt_lat = jax.vmap(functools.partial(_hyena_filter, L))(*filt_params)
    filt_ctx = jax.vmap(functools.partial(_hyena_filter, C))(*(p[0:DEPTH - 1] for p in filt_params))
    k_r, k_i = jax.vmap(_hyena_spectrum, in_axes=(0, None, None, None, None, None))(
        filt_lat, g_cos, g_sin, f_cos, f_sin, spec_w)
    ws, e_all, wc, lam_p = jax.vmap(_s5_matrices)(s5_a_re, s5_a_im, s5_log_dt, s5_b_re, s5_b_im, s5_c_re, s5_c_im, s5_d)

    for l in range(DEPTH):
        last = l == DEPTH - 1
        n_rows = T_LAT if last else T_ALL
        sh1, sc1, g1, sh2, sc2, g2 = (mod[l, :, j] for j in range(6))

        q, k, v, hz, s5u = _inproj(xall, sh1, sc1, w_in[l].astype(BF16), cos_t, sin_t)
        attn = _attention(sink[l], q, k, v, with_ctx=not last)

        hu, x0c = _hyena_pre(hz, hy_short_w[l], hy_short_b[l])
        conv = _hyena_fft(hu, g_cos_bf, g_sin_bf, i_cos_bf, i_sin_bf, f_cos_bf, f_sin_bf, k_r, k_i, l)
        if not last:
            conv = jnp.concatenate([conv, _hyena_ctx(hu, filt_ctx[l], d_cos, d_sin)], axis=0)

        s_rows, y_in = _s5_in(s5u, ws, e_all, l)
        s5y = _s5_out(_s5_scan(s_rows, lam_p[l]), wc, y_in, l)

        x1, u2, u2p = _merge(n_rows, xall, attn, conv, hu, x0c, s5y, g1, sh2, sc2, mix_g[l], hy_d[l],
                             s5_w_glu[l].astype(BF16), w_out[l].astype(BF16), ln1_g[l], ln1_b[l])

        top_e, gate = _router(n_rows, u2, w_router[l].T, router_bias[l])
        disp = _dispatch(top_e)
        ys = _experts(_row_gather(u2p, disp['tok']), disp, w_exp_gate, w_exp_up, w_exp_down, l)
        routed = _row_gather(ys, disp['comb']).reshape(TOP_K, n_rows, HALF_D)
        xall = _ffn_out(n_rows, x1, u2p, routed, gate, g2, w_sh_gate[l].astype(BF16), w_sh_up[l].astype(BF16),
                        w_sh_down[l].astype(BF16), ln2_g[l], ln2_b[l])
    return xall.reshape(B, L, D)
```

```python
import functools
import math

import jax
import jax.numpy as jnp
from jax import lax
from jax.experimental import pallas as pl
from jax.experimental.pallas import tpu as pltpu
from jax.experimental.pallas import tpu_sc as plsc

F32 = jnp.float32
BF16 = jnp.bfloat16
HIGHEST = lax.Precision.HIGHEST

D = 1024
B = 2
L = 8192
DEPTH = 2
GRID_W = 64
C = 256
T_LAT = B * L
T_CTX = B * C
T_ALL = T_LAT + T_CTX

HEAD_DIM = 64
N_Q = 8
N_KV = 2
Q_GROUP = N_Q // N_KV
ATTN_W = N_Q * HEAD_DIM
KV_W = N_KV * HEAD_DIM
HY_W = 256
S5_W = 256
MIX_W = ATTN_W + HY_W + S5_W
K_OFF = ATTN_W
V_OFF = K_OFF + KV_W
HY_OFF = V_OFF + KV_W
S5_OFF = HY_OFF + 3 * HY_W
IN_W = S5_OFF + S5_W
WINDOW = 128
BLK = 128
NEG_INF = -1e30
ROPE_BASE = 10000.0
AXIS_DIM = HEAD_DIM // 2

SHORT_K = 3
FILTER_EMB = 33
DECAY_FAST = 0.3
DECAY_SLOW = 1.5
DECAY_TARGET = 1e-2

S5_GROUP = 16
S5_GROUPS = S5_W // S5_GROUP
S5_STATE = 64
S5_NSTATE = S5_GROUPS * S5_STATE
S5_CH = 16
S5_ROWW = S5_CH * S5_W
N_CHUNK = T_ALL // S5_CH
LAT_CHUNKS = L // S5_CH
CTX_CHUNKS = C // S5_CH

N_EXPERTS = 256
TOP_K = 8
N_EGROUPS = 8
EGROUP = N_EXPERTS // N_EGROUPS
TOPK_GROUPS = 4
EXPERT_FF = 256
ROUTED_SCALE = 2.5
MOE_BLOCK = 256

ALPHA = (2 * DEPTH) ** 0.25
LN_EPS = 1e-5

N_FFT = 2 * L
FFT_R = 128
FFT_T1 = L // FFT_R
K1_USED = FFT_R // 2 + 1
K1P = 80
K1H = K1P // 2
FFT_UNROLL = 4

TM = 256
LANE = 128
VMEM_CAP = 60000 * 1024


def _cp(vmem_bytes, n_axes):
    return pltpu.CompilerParams(
        dimension_semantics=("arbitrary",) * n_axes if n_axes else None,
        vmem_limit_bytes=min(int(vmem_bytes), VMEM_CAP),
    )


HALF_D = D // 2
HIGH_HALF_WORD = 0xFFFF0000


def _pack_pairs(x):
    bits = lax.bitcast_convert_type(x.astype(BF16).astype(F32), jnp.uint32)
    packed = (bits[:, 0:HALF_D] >> 16) | (bits[:, HALF_D:D] & jnp.uint32(HIGH_HALF_WORD))
    return lax.bitcast_convert_type(packed, jnp.int32)


def _unpack_pairs(p):
    bits = lax.bitcast_convert_type(p, jnp.uint32)
    low = lax.bitcast_convert_type(bits << 16, F32)
    high = lax.bitcast_convert_type(bits & jnp.uint32(HIGH_HALF_WORD), F32)
    return low, high


def _mod_sel(rows_per_tile):
    per_batch = L // rows_per_tile
    return lambda i: jnp.minimum(i // per_batch, 2)


def _vec_spec(rows_per_tile):
    sel = _mod_sel(rows_per_tile)
    return pl.BlockSpec((None, 1, D), lambda i: (sel(i), 0, 0))


ADA_TN = 1536


def _ada_kernel(c_ref, w_ref, b_ref, o_ref):
    c = c_ref[...]
    s = c * jax.nn.sigmoid(c)
    o_ref[...] = jnp.dot(s, w_ref[...], precision=HIGHEST, preferred_element_type=F32) + b_ref[...]


def _ada(cvec, w_ada, b_ada):
    return pl.pallas_call(
        _ada_kernel,
        grid=(DEPTH, 6 * D // ADA_TN),
        in_specs=[
            pl.BlockSpec((8, D), lambda l, j: (0, 0)),
            pl.BlockSpec((None, D, ADA_TN), lambda l, j: (l, 0, j)),
            pl.BlockSpec((None, 1, ADA_TN), lambda l, j: (l, 0, j)),
        ],
        out_specs=pl.BlockSpec((None, 8, ADA_TN), lambda l, j: (l, 0, j)),
        out_shape=jax.ShapeDtypeStruct((DEPTH, 8, 6 * D), F32),
        compiler_params=_cp(40 << 20, 2),
        name="ada",
    )(cvec, w_ada, b_ada.reshape(DEPTH, 1, 6 * D))


def _inproj_kernel(x_ref, sh_ref, sc_ref, w_ref, cos_ref, sin_ref, q_ref, k_ref, v_ref, hy_ref, s5_ref, *s5_tok_refs):
    u = x_ref[...] * (1.0 + sc_ref[...]) + sh_ref[...]
    proj = jnp.dot(u.astype(BF16), w_ref[...], preferred_element_type=F32)
    cos = cos_ref[...]
    sin = sin_ref[...]
    lane = lax.broadcasted_iota(jnp.int32, (TM, LANE), 1)
    first_half = (lane % AXIS_DIM) < (AXIS_DIM // 2)

    def rope(xc):
        partner = jnp.where(first_half, pltpu.roll(xc, LANE - AXIS_DIM // 2, 1), pltpu.roll(xc, AXIS_DIM // 2, 1))
        return xc * cos + partner * sin

    for j in range(ATTN_W // LANE):
        q_ref[:, j * LANE:(j + 1) * LANE] = rope(proj[:, j * LANE:(j + 1) * LANE]).astype(BF16)
    k_ref[...] = rope(proj[:, K_OFF:V_OFF]).astype(BF16)
    v_ref[...] = proj[:, V_OFF:HY_OFF].astype(BF16)
    hy_ref[...] = proj[:, HY_OFF:S5_OFF]
    for h, tok_ref in enumerate(s5_tok_refs):
        tok_ref[...] = proj[:, S5_OFF + h * LANE:S5_OFF + (h + 1) * LANE]
    for s in range(S5_CH):
        for h, tok_ref in enumerate(s5_tok_refs):
            s5_ref[:, s * S5_W + h * LANE:s * S5_W + (h + 1) * LANE] = (
                tok_ref[pl.ds(s, TM // S5_CH, stride=S5_CH), :].astype(BF16))


def _inproj(xall, sh, sc, w_in_bf, cos_t, sin_t):
    nt = T_ALL // TM
    row = lambda i: (i, 0)
    return pl.pallas_call(
        _inproj_kernel,
        grid=(nt,),
        in_specs=[
            pl.BlockSpec((TM, D), row),
            _vec_spec(TM),
            _vec_spec(TM),
            pl.BlockSpec((D, IN_W), lambda i: (0, 0)),
            pl.BlockSpec((TM, LANE), row),
            pl.BlockSpec((TM, LANE), row),
        ],
        out_specs=[
            pl.BlockSpec((TM, ATTN_W), row),
            pl.BlockSpec((TM, KV_W), row),
            pl.BlockSpec((TM, KV_W), row),
            pl.BlockSpec((TM, 3 * HY_W), row),
            pl.BlockSpec((TM // S5_CH, S5_ROWW), row),
        ],
        out_shape=[
            jax.ShapeDtypeStruct((T_ALL, ATTN_W), BF16),
            jax.ShapeDtypeStruct((T_ALL, KV_W), BF16),
            jax.ShapeDtypeStruct((T_ALL, KV_W), BF16),
            jax.ShapeDtypeStruct((T_ALL, 3 * HY_W), F32),
            jax.ShapeDtypeStruct((N_CHUNK, S5_ROWW), BF16),
        ],
        scratch_shapes=[pltpu.VMEM((TM, LANE), F32)] * (S5_W // LANE),
        compiler_params=_cp(40 << 20, 1),
        name="inproj",
    )(xall, sh, sc, w_in_bf, cos_t, sin_t)


NB_LAT = L // BLK
NB_CTX = C // BLK


def _nt_dot(a, b):
    return lax.dot_general(a, b, (((1,), (1,)), ((), ())), preferred_element_type=F32)


def _attn_kernel(sink_ref, q_ref, kp_ref, kc_ref, kn_ref, kx_ref, vp_ref, vc_ref, vn_ref, vx_ref, o_ref):
    n = pl.program_id(1)
    is_lat = n < NB_LAT
    rows = Q_GROUP * BLK
    r = lax.broadcasted_iota(jnp.int32, (rows, BLK), 0) % BLK
    j = lax.broadcasted_iota(jnp.int32, (rows, BLK), 1)
    ok_prev = (j >= r) & (n >= 1) & is_lat
    ok_next = (j <= r) & (n + 1 < NB_LAT) & is_lat
    head_of_row = lax.broadcasted_iota(jnp.int32, (rows, 1), 0) // BLK
    q = q_ref[...] * (HEAD_DIM ** -0.5)
    dot = lambda a, b: jnp.dot(a.astype(BF16), b, preferred_element_type=F32)
    for kh in range(N_KV):
        hs = slice(kh * HEAD_DIM, (kh + 1) * HEAD_DIM)
        heads = range(kh * Q_GROUP, (kh + 1) * Q_GROUP)
        qg = jnp.concatenate([q[:, h * HEAD_DIM:(h + 1) * HEAD_DIM] for h in heads], axis=0)
        sk = jnp.zeros((rows, 1), F32)
        for g, h in enumerate(heads):
            sk = jnp.where(head_of_row == g, sink_ref[h], sk)
        s_p = jnp.where(ok_prev, _nt_dot(qg, kp_ref[:, hs]), NEG_INF)
        s_c = jnp.where(is_lat, _nt_dot(qg, kc_ref[:, hs]), NEG_INF)
        s_n = jnp.where(ok_next, _nt_dot(qg, kn_ref[:, hs]), NEG_INF)
        s_x = _nt_dot(qg, kx_ref[:, hs])
        s_x0 = s_x[:, 0:BLK]
        s_x1 = s_x[:, BLK:2 * BLK]
        m = jnp.maximum(jnp.maximum(jnp.maximum(s_p, s_c), jnp.maximum(s_n, s_x0)), s_x1)
        m = jnp.maximum(jnp.max(m, axis=1, keepdims=True), sk)
        e_p = jnp.exp(s_p - m)
        e_c = jnp.exp(s_c - m)
        e_n = jnp.exp(s_n - m)
        e_x0 = jnp.exp(s_x0 - m)
        e_x1 = jnp.exp(s_x1 - m)
        den = jnp.sum((e_p + e_c) + (e_n + e_x0) + e_x1, axis=1, keepdims=True) + jnp.exp(sk - m)
        o = (dot(e_p, vp_ref[:, hs]) + dot(e_c, vc_ref[:, hs]) + dot(e_n, vn_ref[:, hs])
             + dot(e_x0, vx_ref[0:BLK, hs]) + dot(e_x1, vx_ref[BLK:2 * BLK, hs]))
        o = o / den
        for g, h in enumerate(heads):
            o_ref[:, h * HEAD_DIM:(h + 1) * HEAD_DIM] = o[g * BLK:(g + 1) * BLK]


def _attention(sink, q, k, v, with_ctx):
    nblk = NB_LAT + (NB_CTX if with_ctx else 0)

    def q_idx(b, n):
        return (jnp.where(n < NB_LAT, b * NB_LAT + n, B * NB_LAT + b * NB_CTX + (n - NB_LAT)), 0)

    def kv_idx(off):
        def idx(b, n):
            nn = jnp.clip(jnp.minimum(n, NB_LAT - 1) + off, 0, NB_LAT - 1)
            return (b * NB_LAT + nn, 0)
        return idx

    ctx_idx = lambda b, n: (T_LAT // C + b, 0)
    kv_specs = lambda: [pl.BlockSpec((BLK, KV_W), kv_idx(-1)), pl.BlockSpec((BLK, KV_W), kv_idx(0)),
                        pl.BlockSpec((BLK, KV_W), kv_idx(1)), pl.BlockSpec((C, KV_W), ctx_idx)]
    return pl.pallas_call(
        _attn_kernel,
        grid=(B, nblk),
        in_specs=[pl.BlockSpec(memory_space=pltpu.SMEM), pl.BlockSpec((BLK, ATTN_W), q_idx)] + kv_specs() + kv_specs(),
        out_specs=pl.BlockSpec((BLK, ATTN_W), q_idx),
        out_shape=jax.ShapeDtypeStruct((T_ALL if with_ctx else T_LAT, ATTN_W), F32),
        compiler_params=_cp(32 << 20, 2),
        name="attention",
    )(sink, q, k, k, k, k, v, v, v, v)


def _hyena_pre_kernel(z_ref, zp_ref, zn_ref, w_ref, b_ref, u_ref, x0_ref):
    i = pl.program_id(0)
    tiles_per_seq = L // TM
    is_ctx = i >= B * tiles_per_seq
    first = is_ctx | (i % tiles_per_seq == 0)
    last = is_ctx | (i % tiles_per_seq == tiles_per_seq - 1)
    z = z_ref[...]
    prev_row = jnp.where(first, 0.0, zp_ref[7:8, :])
    next_row = jnp.where(last, 0.0, zn_ref[0:1, :])
    row = lax.broadcasted_iota(jnp.int32, z.shape, 0)
    z_m1 = jnp.where(row == 0, prev_row, pltpu.roll(z, 1, 0))
    z_p1 = jnp.where(row == TM - 1, next_row, pltpu.roll(z, TM - 1, 0))
    zc = b_ref[...] + z_m1 * w_ref[0:1, :] + z * w_ref[1:2, :] + z_p1 * w_ref[2:3, :]
    u_ref[...] = zc[:, 0:HY_W] * zc[:, HY_W:2 * HY_W]
    x0_ref[...] = zc[:, 2 * HY_W:3 * HY_W]


def _hyena_pre(z, short_w, short_b):
    nt = T_ALL // TM
    sub = TM // 8
    n8 = T_ALL // 8
    return pl.pallas_call(
        _hyena_pre_kernel,
        grid=(nt,),
        in_specs=[
            pl.BlockSpec((TM, 3 * HY_W), lambda i: (i, 0)),
            pl.BlockSpec((8, 3 * HY_W), lambda i: (jnp.maximum(i * sub - 1, 0), 0)),
            pl.BlockSpec((8, 3 * HY_W), lambda i: (jnp.minimum((i + 1) * sub, n8 - 1), 0)),
            pl.BlockSpec((SHORT_K, 3 * HY_W), lambda i: (0, 0)),
            pl.BlockSpec((1, 3 * HY_W), lambda i: (0, 0)),
        ],
        out_specs=[pl.BlockSpec((TM, HY_W), lambda i: (i, 0)), pl.BlockSpec((TM, HY_W), lambda i: (i, 0))],
        out_shape=[jax.ShapeDtypeStruct((T_ALL, HY_W), F32), jax.ShapeDtypeStruct((T_ALL, HY_W), F32)],
        compiler_params=_cp(32 << 20, 1),
        name="hyena_pre",
    )(z, z, z, short_w, short_b.reshape(1, 3 * HY_W))


def _dft_tables():
    t0 = jnp.arange(FFT_R, dtype=jnp.int32)[:, None, None]
    k1 = jnp.arange(K1P, dtype=jnp.int32)[None, :, None]
    t1 = jnp.arange(FFT_T1, dtype=jnp.int32)[None, None, :]
    m = (k1 * (FFT_R * t1 + t0)) % N_FFT
    ang = m.astype(F32) * (2.0 * math.pi / N_FFT)
    used = (k1 < K1_USED).astype(F32)
    g_cos = jnp.cos(ang) * used
    g_sin = jnp.sin(ang) * used
    a = jnp.arange(FFT_R, dtype=jnp.int32)
    ang2 = ((a[:, None] * a[None, :]) % FFT_R).astype(F32) * (2.0 * math.pi / FFT_R)
    return g_cos, g_sin, jnp.cos(ang2), jnp.sin(ang2)


def _hyena_spec_kernel(k_ref, gc_ref, gs_ref, fc_ref, fs_ref, wt_ref, kr_ref, ki_ref, ar_ref, ai_ref):
    half = pl.program_id(1)
    kk = lax.broadcasted_iota(jnp.int32, (K1H, 1), 0) + half * K1H
    sign = jnp.where(kk % 2 == 0, 1.0, -1.0).astype(F32)

    dot = lambda a, b: jnp.dot(a, b.astype(BF16), preferred_element_type=F32)

    def stage1(i, carry):
        t0s = [i * FFT_UNROLL + u for u in range(FFT_UNROLL)]
        loaded = [(k_ref[pl.ds(t0, FFT_T1, stride=FFT_R), :], k_ref[pl.ds(L + t0, FFT_T1, stride=FFT_R), :],
                   gc_ref[t0].astype(BF16), gs_ref[t0].astype(BF16)) for t0 in t0s]
        res = [(dot(gc, x_lo) + sign * dot(gc, x_hi), -(dot(gs, x_lo) + sign * dot(gs, x_hi)))
               for x_lo, x_hi, gc, gs in loaded]
        for t0, (a_r, a_i) in zip(t0s, res):
            rows = pl.ds(pl.multiple_of(t0 * K1H, 8), K1H)
            ar_ref[rows, :] = a_r
            ai_ref[rows, :] = a_i
        return carry

    lax.fori_loop(0, FFT_R // FFT_UNROLL, stage1, 0)
    fc = fc_ref[...].astype(BF16)
    fs = fs_ref[...].astype(BF16)

    def stage2(i, carry):
        kls = [i * FFT_UNROLL + u for u in range(FFT_UNROLL)]
        loaded = [(ar_ref[pl.ds(kl, FFT_R, stride=K1H), :], ai_ref[pl.ds(kl, FFT_R, stride=K1H), :]) for kl in kls]
        for kl, (a_r, a_i) in zip(kls, loaded):
            w = wt_ref[half * K1H + kl]
            kr_ref[kl] = (dot(fc, a_r) + dot(fs, a_i)) * w
            ki_ref[kl] = (dot(fc, a_i) - dot(fs, a_r)) * w
        return carry

    lax.fori_loop(0, K1H // FFT_UNROLL, stage2, 0)


def _hyena_spectrum(kfilt, g_cos, g_sin, f_cos, f_sin, wts):
    nct = HY_W // LANE
    gspec = pl.BlockSpec((FFT_R, K1H, FFT_T1), lambda c, h: (0, h, 0))
    fspec = pl.BlockSpec((FFT_R, FFT_R), lambda c, h: (0, 0))
    ospec = pl.BlockSpec((K1H, FFT_R, LANE), lambda c, h: (h, 0, c))
    return pl.pallas_call(
        _hyena_spec_kernel,
        grid=(nct, 2),
        in_specs=[pl.BlockSpec((N_FFT, LANE), lambda c, h: (0, c)), gspec, gspec, fspec, fspec,
                  pl.BlockSpec(memory_space=pltpu.SMEM)],
        out_specs=[ospec, ospec],
        out_shape=[jax.ShapeDtypeStruct((K1P, FFT_R, HY_W), F32)] * 2,
        scratch_shapes=[pltpu.VMEM((FFT_R * K1H, LANE), F32)] * 2,
        compiler_params=_cp(56 << 20, 0),
        name="hyena_spectrum",
    )(kfilt, g_cos, g_sin, f_cos, f_sin, wts)


def _hyena_fft_kernel(u_ref, gc_ref, gs_ref, ic_ref, is_ref, fc_ref, fs_ref, kr_ref, ki_ref, o_ref, ar_ref, ai_ref):
    bdot = lambda a, b: jnp.dot(a, b.astype(BF16), preferred_element_type=F32)

    def stage1(i, carry):
        t0s = [i * FFT_UNROLL + u for u in range(FFT_UNROLL)]
        xs = [u_ref[pl.ds(t0, FFT_T1, stride=FFT_R), :] for t0 in t0s]
        res = [(bdot(gc_ref[t0], x), -bdot(gs_ref[t0], x)) for t0, x in zip(t0s, xs)]
        for t0, (a_r, a_i) in zip(t0s, res):
            rows = pl.ds(pl.multiple_of(t0 * K1P, 8), K1P)
            ar_ref[rows, :] = a_r
            ai_ref[rows, :] = a_i
        return carry

    lax.fori_loop(0, FFT_R // FFT_UNROLL, stage1, 0)
    fc = fc_ref[...]
    fs = fs_ref[...]

    def stage23(i, carry):
        k1s = [i * FFT_UNROLL + u for u in range(FFT_UNROLL)]
        loaded = [(ar_ref[pl.ds(k1, FFT_R, stride=K1P), :], ai_ref[pl.ds(k1, FFT_R, stride=K1P), :],
                   kr_ref[k1], ki_ref[k1]) for k1 in k1s]
        res = []
        for a_r, a_i, k_r, k_i in loaded:
            z_r = bdot(fc, a_r) + bdot(fs, a_i)
            z_i = bdot(fc, a_i) - bdot(fs, a_r)
            y_r = z_r * k_r - z_i * k_i
            y_i = z_r * k_i + z_i * k_r
            res.append((bdot(fc, y_r) - bdot(fs, y_i), bdot(fc, y_i) + bdot(fs, y_r)))
        for k1, (b_r, b_i) in zip(k1s, res):
            ar_ref[pl.ds(k1, FFT_R, stride=K1P), :] = b_r
            ai_ref[pl.ds(k1, FFT_R, stride=K1P), :] = b_i
        return carry

    lax.fori_loop(0, -(-K1_USED // FFT_UNROLL), stage23, 0)

    def stage4(i, carry):
        t0s = [i * FFT_UNROLL + u for u in range(FFT_UNROLL)]
        loaded = []
        for t0 in t0s:
            rows = pl.ds(pl.multiple_of(t0 * K1P, 8), K1P)
            loaded.append((ar_ref[rows, :], ai_ref[rows, :]))
        res = [bdot(ic_ref[t0], b_r) - bdot(is_ref[t0], b_i) for t0, (b_r, b_i) in zip(t0s, loaded)]
        for t0, y in zip(t0s, res):
            o_ref[pl.ds(t0, FFT_T1, stride=FFT_R), :] = y
        return carry

    lax.fori_loop(0, FFT_R // FFT_UNROLL, stage4, 0)


def _hyena_fft(u, g_cos_bf, g_sin_bf, i_cos_bf, i_sin_bf, f_cos_bf, f_sin_bf, k_r, k_i, layer):
    nct = HY_W // LANE
    one = pl.Buffered(1)
    gspec = pl.BlockSpec((FFT_R, K1P, FFT_T1), lambda c, b: (0, 0, 0), pipeline_mode=one)
    ispec = pl.BlockSpec((FFT_R, FFT_T1, K1P), lambda c, b: (0, 0, 0), pipeline_mode=one)
    fspec = pl.BlockSpec((FFT_R, FFT_R), lambda c, b: (0, 0), pipeline_mode=one)
    kspec = pl.BlockSpec((None, K1P, FFT_R, LANE), lambda c, b: (layer, 0, 0, c), pipeline_mode=one)
    return pl.pallas_call(
        _hyena_fft_kernel,
        grid=(nct, B),
        in_specs=[pl.BlockSpec((L, LANE), lambda c, b: (b, c)), gspec, gspec, ispec, ispec, fspec, fspec, kspec, kspec],
        out_specs=pl.BlockSpec((L, LANE), lambda c, b: (b, c)),
        out_shape=jax.ShapeDtypeStruct((T_LAT, HY_W), F32),
        scratch_shapes=[pltpu.VMEM((FFT_R * K1P, LANE), F32)] * 2,
        compiler_params=_cp(56 << 20, 2),
        name="hyena_fft",
    )(u, g_cos_bf, g_sin_bf, i_cos_bf, i_sin_bf, f_cos_bf, f_sin_bf, k_r, k_i)


def _hyena_ctx_kernel(u_ref, k_ref, dc_ref, ds_ref, o_ref):
    dot = lambda a, b: jnp.dot(a, b, precision=HIGHEST, preferred_element_type=F32)
    dc = dc_ref[...]
    ds = ds_ref[...]
    u = u_ref[...]
    kf = k_ref[...]
    u_r = dot(dc[:, 0:C], u)
    u_i = -dot(ds[:, 0:C], u)
    k_r = dot(dc, kf)
    k_i = -dot(ds, kf)
    y_r = u_r * k_r - u_i * k_i
    y_i = u_r * k_i + u_i * k_r
    o_ref[...] = (dot(dc[0:C, :], y_r) - dot(ds[0:C, :], y_i)) * (1.0 / (2 * C))


def _hyena_ctx(u, kfilt_ctx, d_cos, d_sin):
    full = lambda b: (0, 0)
    return pl.pallas_call(
        _hyena_ctx_kernel,
        grid=(B,),
        in_specs=[pl.BlockSpec((C, HY_W), lambda b: (T_LAT // C + b, 0)),
                  pl.BlockSpec((2 * C, HY_W), full), pl.BlockSpec((2 * C, 2 * C), full), pl.BlockSpec((2 * C, 2 * C), full)],
        out_specs=pl.BlockSpec((C, HY_W), lambda b: (b, 0)),
        out_shape=jax.ShapeDtypeStruct((T_CTX, HY_W), F32),
        compiler_params=_cp(32 << 20, 1),
        name="hyena_ctx",
    )(u, kfilt_ctx, d_cos, d_sin)


def _hyena_filter(n, w1, b1, freq, w2, b2, w3):
    t = jnp.linspace(0.0, 1.0, n, dtype=F32)[:, None]
    bands = (FILTER_EMB - 1) // 2
    w = 2.0 * math.pi * jnp.arange(n, dtype=F32)[:, None] / n
    f = jnp.linspace(1e-4, bands - 1, bands, dtype=F32)[None, :]
    z = jnp.concatenate([t, jnp.cos(f * w), -jnp.sin(f * w)], axis=-1)
    mm = functools.partial(jnp.matmul, precision=HIGHEST)
    h = jnp.sin(freq * (mm(z, w1) + b1))
    h = jnp.sin(freq * (mm(h, w2) + b2))
    deltas = jnp.abs(jnp.linspace(math.log(DECAY_TARGET) / DECAY_FAST, math.log(DECAY_TARGET) / DECAY_SLOW,
                                  HY_W, dtype=F32))
    decay = jnp.exp(-t * deltas[None, :])
    h_fwd = mm(h, w3[:, :HY_W]) * decay
    h_bwd_rev = mm(h[::-1], w3[:, HY_W:]) * decay[::-1]
    k = jnp.concatenate([h_fwd, jnp.zeros((1, HY_W), F32), h_bwd_rev[:-1]], axis=0)
    return k / jnp.sum(jnp.abs(k), axis=0, keepdims=True)


def _s5_matrices(a_re, a_im, log_dt, b_re, b_im, c_re, c_im, d_skip):
    dt = jnp.exp(log_dt)[:, :, None]
    lam_re = jnp.minimum(a_re, -1e-4)
    mag1 = jnp.exp(lam_re * dt)
    lbr = mag1 * jnp.cos(a_im * dt)
    lbi = mag1 * jnp.sin(a_im * dt)
    den = lam_re * lam_re + a_im * a_im
    fr = ((lbr - 1.0) * lam_re + lbi * a_im) / den
    fi = (lbi * lam_re - (lbr - 1.0) * a_im) / den
    bbr = fr[..., None] * b_re - fi[..., None] * b_im
    bbi = fr[..., None] * b_im + fi[..., None] * b_re
    j = jnp.arange(S5_CH + 1, dtype=F32)[:, None, None, None]
    magj = jnp.exp(j * (lam_re * dt)[None])
    pr = magj * jnp.cos(j * (a_im * dt)[None])
    pi = magj * jnp.sin(j * (a_im * dt)[None])
    hi = functools.partial(jnp.einsum, precision=HIGHEST)
    lbr_j = pr[..., None] * bbr[None] - pi[..., None] * bbi[None]
    lbi_j = pr[..., None] * bbi[None] + pi[..., None] * bbr[None]
    m = hi('dgop,jdgpi->jdgoi', c_re, lbr_j) - hi('dgop,jdgpi->jdgoi', c_im, lbi_j)
    eye_g = jnp.eye(S5_GROUPS, dtype=F32)
    s = jnp.arange(S5_CH)
    blocks = jnp.einsum('jdgoi,gh->djgiho', m[0:S5_CH], eye_g).reshape(2, S5_CH, S5_W, S5_W)
    lag0 = blocks[0, 0] + blocks[1, 0] + jnp.diag(d_skip)
    e_all = jnp.concatenate([blocks[1, S5_CH - 1:0:-1], lag0[None], blocks[0, 1:S5_CH]], axis=0).astype(BF16)
    sf_r = lbr_j[S5_CH - 1 - s, 0]
    sf_i = lbi_j[S5_CH - 1 - s, 0]
    sb_r = lbr_j[s, 1]
    sb_i = lbi_j[s, 1]
    st = jnp.stack([sf_r, sf_i, sb_r, sb_i], axis=0)
    rr = jnp.arange(S5_ROWW, dtype=jnp.int32)
    cc = jnp.arange(S5_W, dtype=jnp.int32)
    ws = _s5_expand(jnp.transpose(st, (1, 2, 4, 0, 3)).reshape(S5_ROWW, S5_W),
                    (cc[:, None] // S5_STATE == rr[None, :] // S5_NSTATE)
                    & (cc[:, None] % S5_STATE == rr[None, :] % S5_STATE),
                    (rr // S5_GROUP) % S5_GROUPS, (rr % S5_NSTATE) // S5_STATE)
    tt = jnp.arange(S5_CH)
    cf_r = c_re[0][None] * pr[tt + 1, 0][:, :, None, :] - c_im[0][None] * pi[tt + 1, 0][:, :, None, :]
    cf_i = c_re[0][None] * pi[tt + 1, 0][:, :, None, :] + c_im[0][None] * pr[tt + 1, 0][:, :, None, :]
    cb_r = c_re[1][None] * pr[S5_CH - tt, 1][:, :, None, :] - c_im[1][None] * pi[S5_CH - tt, 1][:, :, None, :]
    cb_i = c_re[1][None] * pi[S5_CH - tt, 1][:, :, None, :] + c_im[1][None] * pr[S5_CH - tt, 1][:, :, None, :]
    ct = jnp.stack([cf_r, -cf_i, cb_r, -cb_i], axis=0)
    wc = _s5_expand(jnp.transpose(ct, (0, 2, 4, 1, 3)).reshape(4 * S5_NSTATE, S5_W),
                    (cc[:, None] // S5_GROUP == rr[None, :] // S5_W)
                    & (cc[:, None] % S5_GROUP == rr[None, :] % S5_GROUP),
                    (rr % S5_NSTATE) // S5_STATE, (rr % S5_W) // S5_GROUP)
    lam_p = jnp.stack([jnp.stack([pr[S5_CH, 0], pi[S5_CH, 0]]), jnp.stack([pr[S5_CH, 1], pi[S5_CH, 1]])])
    return ws, e_all, wc, lam_p.reshape(2, 2, 1, S5_NSTATE)


S5_TN = 512
S5_NS = S5_ROWW // S5_W


def _s5_expand_kernel(a_ref, ex_ref, rg_ref, cg_ref, o_ref):
    v = jnp.dot(a_ref[...], ex_ref[...], preferred_element_type=F32)
    o_ref[...] = jnp.where(rg_ref[...] == cg_ref[...], v, 0.0).astype(BF16)


def _s5_expand(compact, placement, row_group, col_group):
    n = compact.shape[0]
    return pl.pallas_call(
        _s5_expand_kernel,
        grid=(S5_ROWW // S5_TN,),
        in_specs=[pl.BlockSpec((n, S5_W), lambda j: (0, 0)), pl.BlockSpec((S5_W, S5_TN), lambda j: (0, j)),
                  pl.BlockSpec((n, 1), lambda j: (0, 0)), pl.BlockSpec((1, S5_TN), lambda j: (0, j))],
        out_specs=pl.BlockSpec((n, S5_TN), lambda j: (0, j)),
        out_shape=jax.ShapeDtypeStruct((n, S5_ROWW), BF16),
        compiler_params=_cp(32 << 20, 0),
        name="s5_expand",
    )(compact.astype(BF16), placement.astype(BF16), row_group.reshape(n, 1), col_group.reshape(1, S5_ROWW))


def _s5_in_kernel(u_ref, ws_ref, e_ref, s_ref, o_ref):
    j = pl.program_id(0)

    @pl.when(j < S5_NS)
    def _():
        s_ref[...] = jnp.dot(u_ref[...], ws_ref[...], preferred_element_type=F32)

    @pl.when(j >= S5_NS)
    def _():
        t = j - S5_NS
        acc = jnp.dot(u_ref[:, 0:S5_W], e_ref[S5_CH - 1 + t], preferred_element_type=F32)
        for s in range(1, S5_CH):
            acc = acc + jnp.dot(u_ref[:, s * S5_W:(s + 1) * S5_W], e_ref[S5_CH - 1 + t - s],
                                preferred_element_type=F32)
        o_ref[...] = acc


def _s5_in(u_rows, ws, e_all, layer):
    return pl.pallas_call(
        _s5_in_kernel,
        grid=(2 * S5_NS,),
        in_specs=[pl.BlockSpec((N_CHUNK, S5_ROWW), lambda j: (0, 0)),
                  pl.BlockSpec((None, S5_ROWW, S5_W), lambda j: (layer, 0, jnp.minimum(j, S5_NS - 1))),
                  pl.BlockSpec((None, 2 * S5_CH - 1, S5_W, S5_W), lambda j: (layer, 0, 0, 0))],
        out_specs=[pl.BlockSpec((N_CHUNK, S5_W), lambda j: (0, jnp.minimum(j, S5_NS - 1))),
                   pl.BlockSpec((N_CHUNK, S5_W), lambda j: (0, jnp.maximum(j - S5_NS, 0)))],
        out_shape=[jax.ShapeDtypeStruct((N_CHUNK, 4 * S5_NSTATE), F32), jax.ShapeDtypeStruct((N_CHUNK, S5_ROWW), F32)],
        compiler_params=_cp(48 << 20, 1),
        name="s5_in",
    )(u_rows, ws, e_all)


def _s5_scan_kernel(s_ref, lam_ref, h_ref):
    lam = [[lam_ref[d, p] for p in range(2)] for d in range(2)]

    def step(b, d, chunk, h):
        row = pl.ds(chunk, 1)
        cols_r = pl.ds(d * 2 * S5_NSTATE, S5_NSTATE)
        cols_i = pl.ds(d * 2 * S5_NSTATE + S5_NSTATE, S5_NSTATE)
        h_ref[row, cols_r] = h[0]
        h_ref[row, cols_i] = h[1]
        s_r = s_ref[row, cols_r]
        s_i = s_ref[row, cols_i]
        lr, li = lam[d]
        return (lr * h[0] - li * h[1] + s_r, lr * h[1] + li * h[0] + s_i)

    def chain_order(b, d, n_ctx_done):
        ctx0 = B * LAT_CHUNKS + b * CTX_CHUNKS
        lat0 = b * LAT_CHUNKS
        if d == 0:
            return (lambda i: ctx0 + i), (lambda i: lat0 + i)
        return (lambda i: ctx0 + CTX_CHUNKS - 1 - i), (lambda i: lat0 + LAT_CHUNKS - 1 - i)

    chains = [(b, d) for b in range(B) for d in range(2)]
    zero = jnp.zeros((1, S5_NSTATE), F32)
    init = tuple((zero, zero) for _ in chains)

    def phase(n_steps, which, carry):
        def body(i, hs):
            out = []
            for (b, d), h in zip(chains, hs):
                order = chain_order(b, d, 0)[which]
                out.append(step(b, d, order(i), h))
            return tuple(out)
        return lax.fori_loop(0, n_steps, body, carry)

    carry = phase(CTX_CHUNKS, 0, init)
    phase(LAT_CHUNKS, 1, carry)


def _s5_scan(s_rows, lam_p):
    return pl.pallas_call(
        _s5_scan_kernel,
        out_shape=jax.ShapeDtypeStruct((N_CHUNK, 4 * S5_NSTATE), F32),
        compiler_params=pltpu.CompilerParams(vmem_limit_bytes=48 << 20),
        name="s5_scan",
    )(s_rows, lam_p)


S5_TM = N_CHUNK // 2
S5_TPN = S5_TN // S5_W


def _s5_out_kernel(h_ref, w_ref, y_ref, *rest):
    o_refs, hb_ref = rest[:-1], rest[-1]
    j = pl.program_id(1)

    @pl.when(j == 0)
    def _():
        hb_ref[...] = h_ref[...].astype(BF16)

    acc = jnp.dot(hb_ref[...], w_ref[...], preferred_element_type=F32) + y_ref[...]
    for tt in range(S5_TPN):
        for h, o_ref in enumerate(o_refs):
            o_ref[pl.ds(j * S5_TPN + tt, S5_TM, stride=S5_CH), :] = (
                acc[:, tt * S5_W + h * LANE:tt * S5_W + (h + 1) * LANE])


def _s5_out(h_rows, wc, y_in, layer):
    nn = S5_ROWW // S5_TN
    n_out = S5_W // LANE
    return pl.pallas_call(
        _s5_out_kernel,
        grid=(N_CHUNK // S5_TM, nn),
        in_specs=[pl.BlockSpec((S5_TM, 4 * S5_NSTATE), lambda i, j: (i, 0), pipeline_mode=pl.Buffered(1)),
                  pl.BlockSpec((None, 4 * S5_NSTATE, S5_TN), lambda i, j: (layer, 0, j)),
                  pl.BlockSpec((S5_TM, S5_TN), lambda i, j: (i, j))],
        out_specs=[pl.BlockSpec((S5_TM * S5_CH, LANE), lambda i, j: (i, 0))] * n_out,
        out_shape=[jax.ShapeDtypeStruct((T_ALL, LANE), F32)] * n_out,
        scratch_shapes=[pltpu.VMEM((S5_TM, 4 * S5_NSTATE), BF16)],
        compiler_params=_cp(52 << 20, 2),
        name="s5_out",
    )(h_rows, wc, y_in)


def _rms(x):
    return x * lax.rsqrt(jnp.mean(x * x, axis=-1, keepdims=True) + LN_EPS)


def _layer_norm(x, g, b):
    mu = jnp.mean(x, axis=-1, keepdims=True)
    xc = x - mu
    var = jnp.mean(xc * xc, axis=-1, keepdims=True)
    return xc * lax.rsqrt(var + LN_EPS) * g + b


def _merge_kernel(x_ref, attn_ref, conv_ref, hu_ref, x0_ref, s5a_ref, s5b_ref, g1_ref, sh2_ref, sc2_ref, mixg_ref,
                  hyd_ref, wglu_ref, wout_ref, lng_ref, lnb_ref, o_ref, u2_ref, u2p_ref):
    hy = (conv_ref[...] + hu_ref[...] * hyd_ref[...]) * x0_ref[...]
    g = jax.nn.gelu(jnp.concatenate([s5a_ref[...], s5b_ref[...]], axis=-1))
    s5 = g * jax.nn.sigmoid(jnp.dot(g.astype(BF16), wglu_ref[...], preferred_element_type=F32))
    mixg = mixg_ref[...]
    parts = [_rms(attn_ref[...]) * mixg[:, 0:ATTN_W],
             _rms(hy) * mixg[:, ATTN_W:ATTN_W + HY_W],
             _rms(s5) * mixg[:, ATTN_W + HY_W:MIX_W]]
    mix = jnp.concatenate(parts, axis=-1).astype(BF16)
    o = jnp.dot(mix, wout_ref[...], preferred_element_type=F32)
    x1 = _layer_norm(ALPHA * x_ref[...] + g1_ref[...] * o, lng_ref[...], lnb_ref[...])
    o_ref[...] = x1
    u2 = x1 * (1.0 + sc2_ref[...]) + sh2_ref[...]
    u2_ref[...] = u2
    u2p_ref[...] = _pack_pairs(u2)


def _merge(n_rows, xall, attn, conv, hu, x0c, s5y, g1, sh2, sc2, mix_g, hy_d, wglu_bf, wout_bf, ln_g, ln_b):
    nt = n_rows // TM
    row = lambda i: (i, 0)
    full = lambda i: (0, 0)
    return pl.pallas_call(
        _merge_kernel,
        grid=(nt,),
        in_specs=[pl.BlockSpec((TM, D), row), pl.BlockSpec((TM, ATTN_W), row), pl.BlockSpec((TM, HY_W), row),
                  pl.BlockSpec((TM, HY_W), row), pl.BlockSpec((TM, HY_W), row),
                  pl.BlockSpec((TM, LANE), row), pl.BlockSpec((TM, LANE), row),
                  _vec_spec(TM), _vec_spec(TM), _vec_spec(TM), pl.BlockSpec((1, MIX_W), full),
                  pl.BlockSpec((1, HY_W), full), pl.BlockSpec((S5_W, S5_W), full), pl.BlockSpec((MIX_W, D), full),
                  pl.BlockSpec((1, D), full), pl.BlockSpec((1, D), full)],
        out_specs=[pl.BlockSpec((TM, D), row), pl.BlockSpec((TM, D), row), pl.BlockSpec((TM, HALF_D), row)],
        out_shape=[jax.ShapeDtypeStruct((n_rows, D), F32), jax.ShapeDtypeStruct((n_rows, D), F32),
                   jax.ShapeDtypeStruct((n_rows, HALF_D), jnp.int32)],
        compiler_params=_cp(48 << 20, 1),
        name="merge",
    )(xall, attn, conv, hu, x0c, s5y[0], s5y[1], g1, sh2, sc2, mix_g.reshape(1, MIX_W), hy_d.reshape(1, HY_W), wglu_bf,
      wout_bf, ln_g.reshape(1, D), ln_b.reshape(1, D))


def _router_kernel(u_ref, wt_ref, b_ref, e_ref, g_ref):
    logits = lax.dot_general(wt_ref[...], u_ref[...], (((1,), (1,)), ((), ())), precision=HIGHEST,
                             preferred_element_type=F32)
    scores = jax.nn.sigmoid(logits)
    biased = scores + b_ref[...]
    ninf = -jnp.inf
    grow = lax.broadcasted_iota(jnp.int32, (EGROUP, TM), 0)
    groups = [biased[gi * EGROUP:(gi + 1) * EGROUP] for gi in range(N_EGROUPS)]
    gscore = []
    for vals in groups:
        m1 = jnp.max(vals, axis=0, keepdims=True)
        i1 = jnp.min(jnp.where(vals == m1, grow, EGROUP), axis=0, keepdims=True)
        m2 = jnp.max(jnp.where(grow == i1, ninf, vals), axis=0, keepdims=True)
        gscore.append(m1 + m2)
    kept = []
    for gi in range(N_EGROUPS):
        rank = jnp.zeros((1, TM), jnp.int32)
        for gj in range(N_EGROUPS):
            if gj == gi:
                continue
            ahead = (gscore[gj] > gscore[gi]) | ((gscore[gj] == gscore[gi]) & (gj < gi))
            rank = rank + ahead.astype(jnp.int32)
        kept.append(jnp.where(rank < TOPK_GROUPS, groups[gi], ninf))
    masked = jnp.concatenate(kept, axis=0)
    row = lax.broadcasted_iota(jnp.int32, (N_EXPERTS, TM), 0)
    gates = []
    gsum = jnp.zeros((1, TM), F32)
    for kk in range(TOP_K):
        m = jnp.max(masked, axis=0, keepdims=True)
        idx = jnp.min(jnp.where(masked == m, row, N_EXPERTS), axis=0, keepdims=True)
        hit = row == idx
        gate = jnp.sum(jnp.where(hit, scores, 0.0), axis=0, keepdims=True)
        masked = jnp.where(hit, ninf, masked)
        e_ref[kk:kk + 1, :] = idx
        gates.append(gate)
        gsum = gsum + gate
    for kk in range(TOP_K):
        g_ref[kk:kk + 1, :] = gates[kk] / gsum * ROUTED_SCALE


def _router(n_rows, u2, w_router_t, router_bias):
    nt = n_rows // TM
    col = lambda i: (0, i)
    return pl.pallas_call(
        _router_kernel,
        grid=(nt,),
        in_specs=[pl.BlockSpec((TM, D), lambda i: (i, 0)),
                  pl.BlockSpec((N_EXPERTS, D), lambda i: (0, 0)), pl.BlockSpec((N_EXPERTS, 1), lambda i: (0, 0))],
        out_specs=[pl.BlockSpec((TOP_K, TM), col), pl.BlockSpec((TOP_K, TM), col)],
        out_shape=[jax.ShapeDtypeStruct((TOP_K, n_rows), jnp.int32), jax.ShapeDtypeStruct((TOP_K, n_rows), F32)],
        compiler_params=_cp(32 << 20, 1),
        name="router",
    )(u2, w_router_t, router_bias.reshape(N_EXPERTS, 1))


def _dispatch(top_e):
    t = top_e.shape[1]
    tk = t * TOP_K
    nblk = tk // MOE_BLOCK
    n_steps = nblk + N_EXPERTS
    flat_e = top_e.reshape(tk)
    pos = jnp.arange(tk, dtype=jnp.int32)
    pos_bits = (tk - 1).bit_length()
    assert pos_bits + (N_EXPERTS - 1).bit_length() < 32
    order = lax.sort(flat_e * (1 << pos_bits) + pos) % (1 << pos_bits)
    _, inv = lax.sort((order, pos), num_keys=1)
    experts = jnp.arange(N_EXPERTS, dtype=jnp.int32)
    counts = jnp.sum((flat_e[None, :] == experts[:, None]).astype(jnp.int32), axis=1)
    ends = jnp.cumsum(counts)
    starts = ends - counts
    fb = starts // MOE_BLOCK
    npairs = jnp.where(counts > 0, (ends - 1) // MOE_BLOCK - fb + 1, 0)
    pend = jnp.cumsum(npairs)
    poff = pend - npairs
    n_pairs = pend[-1]
    s = jnp.arange(n_steps, dtype=jnp.int32)
    sc = jnp.minimum(s, n_pairs - 1)
    pe = jnp.sum((pend[None, :] <= sc[:, None]).astype(jnp.int32), axis=1)
    pb = fb[pe] + (sc - poff[pe])
    lo = jnp.where(s < n_pairs, jnp.maximum(starts[pe] - pb * MOE_BLOCK, 0), 0)
    hi = jnp.where(s < n_pairs, jnp.minimum(ends[pe] - pb * MOE_BLOCK, MOE_BLOCK), 0)
    last_of_block = ((s == n_pairs - 1) | ((s + 1 < n_pairs) & (jnp.roll(pb, -1) != pb))).astype(jnp.int32)
    used = counts > 0
    ordinal = jnp.cumsum(used.astype(jnp.int32)) - 1
    later = jnp.where(used, experts, N_EXPERTS)
    next_used = lax.cummin(jnp.concatenate([later[1:], jnp.full((1,), N_EXPERTS, jnp.int32)]), reverse=True)
    nxt = next_used[pe]
    nxt = jnp.where(nxt < N_EXPERTS, nxt, pe)
    parity = ordinal[pe] % 2
    w_even = jnp.where(parity == 0, pe, nxt)
    w_odd = jnp.where(parity == 1, pe, nxt)
    return dict(pe=pe, pb=pb, lo=lo, hi=hi, last=last_of_block, parity=parity, w_even=w_even, w_odd=w_odd,
                tok=order % t, comb=inv)


X_SLOTS = 4


def _expert_kernel(pe, pb, plo, phi, plast, ppar, pwe, pwo, xs_hbm, wg0_ref, wu0_ref, wd0_ref, wg1_ref, wu1_ref,
                   wd1_ref, out_hbm, xbuf, xsem, obuf, osem, wgb, wub, wdb):
    del pwe, pwo
    s = pl.program_id(0)
    prev = jnp.maximum(s - 1, 0)
    n_blocks = out_hbm.shape[0] // MOE_BLOCK
    new_expert = (s == 0) | (pe[s] != pe[prev])

    def x_copy(b, sl):
        rows = pl.ds(pl.multiple_of(b * MOE_BLOCK, MOE_BLOCK), MOE_BLOCK)
        return pltpu.make_async_copy(xs_hbm.at[rows], xbuf.at[sl], xsem.at[sl])

    @pl.when(s == 0)
    def _():
        for b in range(X_SLOTS - 1):
            x_copy(b, b).start()
        obuf[...] = jnp.zeros(obuf.shape, obuf.dtype)

    for par, (wg_ref, wu_ref, wd_ref) in enumerate(((wg0_ref, wu0_ref, wd0_ref), (wg1_ref, wu1_ref, wd1_ref))):
        @pl.when(new_expert & (ppar[s] == par))
        def _():
            wgb[...] = wg_ref[...].astype(BF16)
            wub[...] = wu_ref[...].astype(BF16)
            wdb[...] = wd_ref[...].astype(BF16)

    lo = plo[s]
    hi = phi[s]
    blk = pb[s]
    slot = blk % 2
    first_of_block = (s == 0) | (blk != pb[prev])

    def out_copy(b, sl):
        rows = pl.ds(pl.multiple_of(b * MOE_BLOCK, MOE_BLOCK), MOE_BLOCK)
        return pltpu.make_async_copy(obuf.at[sl], out_hbm.at[rows], osem.at[sl])

    @pl.when(first_of_block)
    def _():
        ahead = blk + (X_SLOTS - 1)

        @pl.when(ahead < n_blocks)
        def _():
            x_copy(ahead, ahead % X_SLOTS).start()
        x_copy(blk, blk % X_SLOTS).wait()

    def ffn():
        x_lo, x_hi = _unpack_pairs(xbuf[blk % X_SLOTS])
        x_lo = x_lo.astype(BF16)
        x_hi = x_hi.astype(BF16)
        hg = (jnp.dot(x_lo, wgb[0:HALF_D], preferred_element_type=F32)
              + jnp.dot(x_hi, wgb[HALF_D:D], preferred_element_type=F32))
        hu = (jnp.dot(x_lo, wub[0:HALF_D], preferred_element_type=F32)
              + jnp.dot(x_hi, wub[HALF_D:D], preferred_element_type=F32))
        h = (hg * jax.nn.sigmoid(hg)) * hu
        return _pack_pairs(jnp.dot(h.astype(BF16), wdb[...], preferred_element_type=F32))

    @pl.when(first_of_block & (blk >= 2))
    def _():
        out_copy(blk - 2, slot).wait()

    @pl.when(hi > lo)
    def _():
        row = lax.broadcasted_iota(jnp.int32, (MOE_BLOCK, HALF_D), 0)
        mine = first_of_block | ((row >= lo) & (row < hi))
        obuf[slot] = jnp.where(mine, ffn(), obuf[slot])

    @pl.when(plast[s] == 1)
    def _():
        out_copy(blk, slot).start()

    @pl.when(s == pl.num_programs(0) - 1)
    def _():
        out_copy(n_blocks - 2, (n_blocks - 2) % 2).wait()
        out_copy(n_blocks - 1, (n_blocks - 1) % 2).wait()


def _experts(xs, disp, wg, wu, wd, layer):
    tk = xs.shape[0]
    n_steps = disp['pe'].shape[0]
    assert tk // MOE_BLOCK >= X_SLOTS
    weven = lambda shape: pl.BlockSpec((None, None) + shape, lambda s, *p: (layer, p[6][s], 0, 0))
    wodd = lambda shape: pl.BlockSpec((None, None) + shape, lambda s, *p: (layer, p[7][s], 0, 0))
    grid_spec = pltpu.PrefetchScalarGridSpec(
        num_scalar_prefetch=8,
        grid=(n_steps,),
        in_specs=[pl.BlockSpec(memory_space=pl.ANY),
                  weven((D, EXPERT_FF)), weven((D, EXPERT_FF)), weven((EXPERT_FF, D)),
                  wodd((D, EXPERT_FF)), wodd((D, EXPERT_FF)), wodd((EXPERT_FF, D))],
        out_specs=pl.BlockSpec(memory_space=pl.ANY),
        scratch_shapes=[pltpu.VMEM((X_SLOTS, MOE_BLOCK, HALF_D), jnp.int32), pltpu.SemaphoreType.DMA((X_SLOTS,)),
                        pltpu.VMEM((2, MOE_BLOCK, HALF_D), jnp.int32), pltpu.SemaphoreType.DMA((2,)),
                        pltpu.VMEM((D, EXPERT_FF), BF16), pltpu.VMEM((D, EXPERT_FF), BF16),
                        pltpu.VMEM((EXPERT_FF, D), BF16)],
    )
    return pl.pallas_call(
        _expert_kernel,
        grid_spec=grid_spec,
        out_shape=jax.ShapeDtypeStruct((tk, HALF_D), jnp.int32),
        compiler_params=_cp(40 << 20, 1),
        name="experts",
    )(disp['pe'], disp['pb'], disp['lo'], disp['hi'], disp['last'], disp['parity'], disp['w_even'], disp['w_odd'],
      xs, wg, wu, wd, wg, wu, wd)


SC_ROWS = 64


def _row_gather(table, idx):
    n = idx.shape[0]
    d = table.shape[1]
    mesh = plsc.VectorSubcoreMesh(core_axis_name="c", subcore_axis_name="s")
    n_workers = mesh.num_cores * mesh.num_subcores
    per_worker = n // n_workers
    assert per_worker * n_workers == n and per_worker % SC_ROWS == 0

    @functools.partial(
        pl.kernel, mesh=mesh,
        out_type=jax.ShapeDtypeStruct((n, d), table.dtype),
        scratch_types=[pltpu.VMEM((SC_ROWS,), jnp.int32), pltpu.VMEM((SC_ROWS, d), table.dtype),
                       pltpu.SemaphoreType.DMA],
    )
    def gather(table_hbm, idx_hbm, out_hbm, idx_v, rows_v, sem):
        worker = lax.axis_index("s") * mesh.num_cores + lax.axis_index("c")
        base = worker * per_worker

        @pl.loop(0, per_worker // SC_ROWS)
        def _(c):
            off = pl.multiple_of(base + c * SC_ROWS, 8)
            pltpu.sync_copy(idx_hbm.at[pl.ds(off, SC_ROWS)], idx_v)
            pltpu.async_copy(table_hbm.at[idx_v], rows_v, sem).wait()
            pltpu.sync_copy(rows_v, out_hbm.at[pl.ds(off, SC_ROWS)])

    return gather(table, idx)


FM = 128


def _ffn_out_kernel(x_ref, u_ref, r_ref, gate_ref, g2_ref, wsg_ref, wsu_ref, wsd_ref, lng_ref, lnb_ref, o_ref):
    u_lo, u_hi = _unpack_pairs(u_ref[...])
    u_lo = u_lo.astype(BF16)
    u_hi = u_hi.astype(BF16)
    hg = (jnp.dot(u_lo, wsg_ref[0:HALF_D], preferred_element_type=F32)
          + jnp.dot(u_hi, wsg_ref[HALF_D:D], preferred_element_type=F32))
    hu = (jnp.dot(u_lo, wsu_ref[0:HALF_D], preferred_element_type=F32)
          + jnp.dot(u_hi, wsu_ref[HALF_D:D], preferred_element_type=F32))
    f = jnp.dot(((hg * jax.nn.sigmoid(hg)) * hu).astype(BF16), wsd_ref[...], preferred_element_type=F32)
    gates = jnp.transpose(jnp.concatenate([gate_ref[...], jnp.zeros((FM - TOP_K, FM), F32)], axis=0))
    r_lo = jnp.zeros((FM, HALF_D), F32)
    r_hi = jnp.zeros((FM, HALF_D), F32)
    for kk in range(TOP_K):
        k_lo, k_hi = _unpack_pairs(r_ref[kk])
        gk = gates[:, kk:kk + 1]
        r_lo = r_lo + k_lo * gk
        r_hi = r_hi + k_hi * gk
    f = jnp.concatenate([r_lo, r_hi], axis=1) + f
    o_ref[...] = _layer_norm(ALPHA * x_ref[...] + g2_ref[...] * f, lng_ref[...], lnb_ref[...])


def _ffn_out(n_rows, x1, u2p, routed, gate, g2, wsg_bf, wsu_bf, wsd_bf, ln_g, ln_b):
    row = lambda i: (i, 0)
    full = lambda i: (0, 0)
    return pl.pallas_call(
        _ffn_out_kernel,
        grid=(n_rows // FM,),
        in_specs=[pl.BlockSpec((FM, D), row), pl.BlockSpec((FM, HALF_D), row),
                  pl.BlockSpec((TOP_K, FM, HALF_D), lambda i: (0, i, 0)), pl.BlockSpec((TOP_K, FM), lambda i: (0, i)),
                  _vec_spec(FM), pl.BlockSpec((D, EXPERT_FF), full), pl.BlockSpec((D, EXPERT_FF), full),
                  pl.BlockSpec((EXPERT_FF, D), full), pl.BlockSpec((1, D), full), pl.BlockSpec((1, D), full)],
        out_specs=pl.BlockSpec((FM, D), row),
        out_shape=jax.ShapeDtypeStruct((n_rows, D), F32),
        compiler_params=_cp(40 << 20, 1),
        name="ffn_out",
    )(x1, u2p, routed, gate, g2, wsg_bf, wsu_bf, wsd_bf, ln_g.reshape(1, D), ln_b.reshape(1, D))


def _rope_tables():
    t = jnp.arange(L, dtype=jnp.int32)
    row = (t // GRID_W).astype(F32)
    col = (t % GRID_W).astype(F32)
    inv_freq = ROPE_BASE ** (-jnp.arange(0, AXIS_DIM, 2, dtype=F32) / AXIS_DIM)
    half = AXIS_DIM // 2

    def axis(pos):
        ang = pos[:, None] * inv_freq[None, :]
        c = jnp.cos(ang)
        s = jnp.sin(ang)
        return jnp.concatenate([c, c], axis=1), jnp.concatenate([-s, s], axis=1)

    cr, sr = axis(row)
    cc, sc = axis(col)
    cos_h = jnp.concatenate([cr, cc], axis=1)
    sin_h = jnp.concatenate([sr, sc], axis=1)
    cos_l = jnp.tile(cos_h, (B, LANE // HEAD_DIM))
    sin_l = jnp.tile(sin_h, (B, LANE // HEAD_DIM))
    cos_t = jnp.concatenate([cos_l, jnp.ones((T_CTX, LANE), F32)], axis=0)
    sin_t = jnp.concatenate([sin_l, jnp.zeros((T_CTX, LANE), F32)], axis=0)
    del half
    return cos_t, sin_t


def kernel(x, c, ctx, c_ctx, w_ada, b_ada, w_in, w_out, sink, mix_g, hy_short_w, hy_short_b, hy_w1, hy_b1, hy_freq,
           hy_w2, hy_b2, hy_w3, hy_d, s5_a_re, s5_a_im, s5_log_dt, s5_b_re, s5_b_im, s5_c_re, s5_c_im, s5_d, s5_w_glu,
           ln1_g, ln1_b, ln2_g, ln2_b, w_router, router_bias, w_exp_gate, w_exp_up, w_exp_down, w_sh_gate, w_sh_up,
           w_sh_down):
    xall = jnp.concatenate([x.reshape(T_LAT, D), ctx.reshape(T_CTX, D)], axis=0)
    cvec = jnp.concatenate([c, c_ctx[None, :], jnp.zeros((8 - B - 1, D), F32)], axis=0)
    mod = _ada(cvec, w_ada, b_ada)[:, 0:B + 1, :].reshape(DEPTH, B + 1, 6, 1, D)

    cos_t, sin_t = _rope_tables()
    g_cos, g_sin, f_cos, f_sin = _dft_tables()
    g_cos_bf, g_sin_bf = g_cos.astype(BF16), g_sin.astype(BF16)
    i_cos_bf = jnp.swapaxes(g_cos, 1, 2).astype(BF16)
    i_sin_bf = jnp.swapaxes(g_sin, 1, 2).astype(BF16)
    f_cos_bf, f_sin_bf = f_cos.astype(BF16), f_sin.astype(BF16)
    k1 = jnp.arange(K1P)
    spec_w = jnp.where((k1 == 0) | (k1 == FFT_R // 2), 1.0, 2.0) * (k1 < K1_USED) / N_FFT
    spec_w = spec_w.astype(F32)
    kt = jnp.arange(2 * C, dtype=jnp.int32)
    ang_c = ((kt[:, None] * kt[None, :]) % (2 * C)).astype(F32) * (2.0 * math.pi / (2 * C))
    d_cos, d_sin = jnp.cos(ang_c), jnp.sin(ang_c)

    filt_params = (hy_w1, hy_b1, hy_freq, hy_w2, hy_b2, hy_w3)
    filt_lat = jax.vmap(functools.partial(_hyena_filter, L))(*filt_params)
    filt_ctx = jax.vmap(functools.partial(_hyena_filter, C))(*(p[0:DEPTH - 1] for p in filt_params))
    k_r, k_i = jax.vmap(_hyena_spectrum, in_axes=(0, None, None, None, None, None))(
        filt_lat, g_cos, g_sin, f_cos, f_sin, spec_w)
    ws, e_all, wc, lam_p = jax.vmap(_s5_matrices)(s5_a_re, s5_a_im, s5_log_dt, s5_b_re, s5_b_im, s5_c_re, s5_c_im, s5_d)

    for l in range(DEPTH):
        last = l == DEPTH - 1
        n_rows = T_LAT if last else T_ALL
        sh1, sc1, g1, sh2, sc2, g2 = (mod[l, :, j] for j in range(6))

        q, k, v, hz, s5u = _inproj(xall, sh1, sc1, w_in[l].astype(BF16), cos_t, sin_t)
        attn = _attention(sink[l], q, k, v, with_ctx=not last)

        hu, x0c = _hyena_pre(hz, hy_short_w[l], hy_short_b[l])
        conv = _hyena_fft(hu, g_cos_bf, g_sin_bf, i_cos_bf, i_sin_bf, f_cos_bf, f_sin_bf, k_r, k_i, l)
        if not last:
            conv = jnp.concatenate([conv, _hyena_ctx(hu, filt_ctx[l], d_cos, d_sin)], axis=0)

        s_rows, y_in = _s5_in(s5u, ws, e_all, l)
        s5y = _s5_out(_s5_scan(s_rows, lam_p[l]), wc, y_in, l)

        x1, u2, u2p = _merge(n_rows, xall, attn, conv, hu, x0c, s5y, g1, sh2, sc2, mix_g[l], hy_d[l],
                             s5_w_glu[l].astype(BF16), w_out[l].astype(BF16), ln1_g[l], ln1_b[l])

        top_e, gate = _router(n_rows, u2, w_router[l].T, router_bias[l])
        disp = _dispatch(top_e)
        ys = _experts(_row_gather(u2p, disp['tok']), disp, w_exp_gate, w_exp_up, w_exp_down, l)
        routed = _row_gather(ys, disp['comb']).reshape(TOP_K, n_rows, HALF_D)
        xall = _ffn_out(n_rows, x1, u2p, routed, gate, g2, w_sh_gate[l].astype(BF16), w_sh_up[l].astype(BF16),
                        w_sh_down[l].astype(BF16), ln2_g[l], ln2_b[l])
    return xall.reshape(B, L, D)
```

```python
import functools
import math

import jax
import jax.numpy as jnp
from jax import lax
from jax.experimental import pallas as pl
from jax.experimental.pallas import tpu as pltpu
from jax.experimental.pallas import tpu_sc as plsc

F32 = jnp.float32
BF16 = jnp.bfloat16
HIGHEST = lax.Precision.HIGHEST

D = 1024
B = 2
L = 8192
DEPTH = 2
GRID_W = 64
C = 256
T_LAT = B * L
T_CTX = B * C
T_ALL = T_LAT + T_CTX

HEAD_DIM = 64
N_Q = 8
N_KV = 2
Q_GROUP = N_Q // N_KV
ATTN_W = N_Q * HEAD_DIM
KV_W = N_KV * HEAD_DIM
HY_W = 256
S5_W = 256
MIX_W = ATTN_W + HY_W + S5_W
K_OFF = ATTN_W
V_OFF = K_OFF + KV_W
HY_OFF = V_OFF + KV_W
S5_OFF = HY_OFF + 3 * HY_W
IN_W = S5_OFF + S5_W
WINDOW = 128
BLK = 128
NEG_INF = -1e30
ROPE_BASE = 10000.0
AXIS_DIM = HEAD_DIM // 2

SHORT_K = 3
FILTER_EMB = 33
DECAY_FAST = 0.3
DECAY_SLOW = 1.5
DECAY_TARGET = 1e-2

S5_GROUP = 16
S5_GROUPS = S5_W // S5_GROUP
S5_STATE = 64
S5_NSTATE = S5_GROUPS * S5_STATE
S5_CH = 16
S5_ROWW = S5_CH * S5_W
N_CHUNK = T_ALL // S5_CH
LAT_CHUNKS = L // S5_CH
CTX_CHUNKS = C // S5_CH

N_EXPERTS = 256
TOP_K = 8
N_EGROUPS = 8
EGROUP = N_EXPERTS // N_EGROUPS
TOPK_GROUPS = 4
EXPERT_FF = 256
ROUTED_SCALE = 2.5
MOE_BLOCK = 256

ALPHA = (2 * DEPTH) ** 0.25
LN_EPS = 1e-5

N_FFT = 2 * L
FFT_R = 128
FFT_T1 = L // FFT_R
K1_USED = FFT_R // 2 + 1
K1P = 80
K1H = K1P // 2
FFT_UNROLL = 4

TM = 256
LANE = 128
VMEM_CAP = 60000 * 1024


def _cp(vmem_bytes, n_axes):
    return pltpu.CompilerParams(
        dimension_semantics=("arbitrary",) * n_axes if n_axes else None,
        vmem_limit_bytes=min(int(vmem_bytes), VMEM_CAP),
    )


HALF_D = D // 2
HIGH_HALF_WORD = 0xFFFF0000


def _pack_pairs(x):
    bits = lax.bitcast_convert_type(x.astype(BF16).astype(F32), jnp.uint32)
    packed = (bits[:, 0:HALF_D] >> 16) | (bits[:, HALF_D:D] & jnp.uint32(HIGH_HALF_WORD))
    return lax.bitcast_convert_type(packed, jnp.int32)


def _unpack_pairs(p):
    bits = lax.bitcast_convert_type(p, jnp.uint32)
    low = lax.bitcast_convert_type(bits << 16, F32)
    high = lax.bitcast_convert_type(bits & jnp.uint32(HIGH_HALF_WORD), F32)
    return low, high


def _mod_sel(rows_per_tile):
    per_batch = L // rows_per_tile
    return lambda i: jnp.minimum(i // per_batch, 2)


def _vec_spec(rows_per_tile):
    sel = _mod_sel(rows_per_tile)
    return pl.BlockSpec((None, 1, D), lambda i: (sel(i), 0, 0))


ADA_TN = 1536


def _ada_kernel(c_ref, w_ref, b_ref, o_ref):
    c = c_ref[...]
    s = c * jax.nn.sigmoid(c)
    o_ref[...] = jnp.dot(s, w_ref[...], precision=HIGHEST, preferred_element_type=F32) + b_ref[...]


def _ada(cvec, w_ada, b_ada):
    return pl.pallas_call(
        _ada_kernel,
        grid=(DEPTH, 6 * D // ADA_TN),
        in_specs=[
            pl.BlockSpec((8, D), lambda l, j: (0, 0)),
            pl.BlockSpec((None, D, ADA_TN), lambda l, j: (l, 0, j)),
            pl.BlockSpec((None, 1, ADA_TN), lambda l, j: (l, 0, j)),
        ],
        out_specs=pl.BlockSpec((None, 8, ADA_TN), lambda l, j: (l, 0, j)),
        out_shape=jax.ShapeDtypeStruct((DEPTH, 8, 6 * D), F32),
        compiler_params=_cp(40 << 20, 2),
        name="ada",
    )(cvec, w_ada, b_ada.reshape(DEPTH, 1, 6 * D))


def _inproj_kernel(x_ref, sh_ref, sc_ref, w_ref, cos_ref, sin_ref, q_ref, k_ref, v_ref, hy_ref, s5_ref, *s5_tok_refs):
    u = x_ref[...] * (1.0 + sc_ref[...]) + sh_ref[...]
    proj = jnp.dot(u.astype(BF16), w_ref[...], preferred_element_type=F32)
    cos = cos_ref[...]
    sin = sin_ref[...]
    lane = lax.broadcasted_iota(jnp.int32, (TM, LANE), 1)
    first_half = (lane % AXIS_DIM) < (AXIS_DIM // 2)

    def rope(xc):
        partner = jnp.where(first_half, pltpu.roll(xc, LANE - AXIS_DIM // 2, 1), pltpu.roll(xc, AXIS_DIM // 2, 1))
        return xc * cos + partner * sin

    for j in range(ATTN_W // LANE):
        q_ref[:, j * LANE:(j + 1) * LANE] = rope(proj[:, j * LANE:(j + 1) * LANE]).astype(BF16)
    k_ref[...] = rope(proj[:, K_OFF:V_OFF]).astype(BF16)
    v_ref[...] = proj[:, V_OFF:HY_OFF].astype(BF16)
    hy_ref[...] = proj[:, HY_OFF:S5_OFF]
    for h, tok_ref in enumerate(s5_tok_refs):
        tok_ref[...] = proj[:, S5_OFF + h * LANE:S5_OFF + (h + 1) * LANE]
    for s in range(S5_CH):
        for h, tok_ref in enumerate(s5_tok_refs):
            s5_ref[:, s * S5_W + h * LANE:s * S5_W + (h + 1) * LANE] = (
                tok_ref[pl.ds(s, TM // S5_CH, stride=S5_CH), :].astype(BF16))


def _inproj(xall, sh, sc, w_in_bf, cos_t, sin_t):
    nt = T_ALL // TM
    row = lambda i: (i, 0)
    return pl.pallas_call(
        _inproj_kernel,
        grid=(nt,),
        in_specs=[
            pl.BlockSpec((TM, D), row),
            _vec_spec(TM),
            _vec_spec(TM),
            pl.BlockSpec((D, IN_W), lambda i: (0, 0)),
            pl.BlockSpec((TM, LANE), row),
            pl.BlockSpec((TM, LANE), row),
        ],
        out_specs=[
            pl.BlockSpec((TM, ATTN_W), row),
            pl.BlockSpec((TM, KV_W), row),
            pl.BlockSpec((TM, KV_W), row),
            pl.BlockSpec((TM, 3 * HY_W), row),
            pl.BlockSpec((TM // S5_CH, S5_ROWW), row),
        ],
        out_shape=[
            jax.ShapeDtypeStruct((T_ALL, ATTN_W), BF16),
            jax.ShapeDtypeStruct((T_ALL, KV_W), BF16),
            jax.ShapeDtypeStruct((T_ALL, KV_W), BF16),
            jax.ShapeDtypeStruct((T_ALL, 3 * HY_W), F32),
            jax.ShapeDtypeStruct((N_CHUNK, S5_ROWW), BF16),
        ],
        scratch_shapes=[pltpu.VMEM((TM, LANE), F32)] * (S5_W // LANE),
        compiler_params=_cp(40 << 20, 1),
        name="inproj",
    )(xall, sh, sc, w_in_bf, cos_t, sin_t)


NB_LAT = L // BLK
NB_CTX = C // BLK


def _nt_dot(a, b):
    return lax.dot_general(a, b, (((1,), (1,)), ((), ())), preferred_element_type=F32)


def _attn_kernel(sink_ref, q_ref, kp_ref, kc_ref, kn_ref, kx_ref, vp_ref, vc_ref, vn_ref, vx_ref, o_ref):
    n = pl.program_id(1)
    is_lat = n < NB_LAT
    rows = Q_GROUP * BLK
    r = lax.broadcasted_iota(jnp.int32, (rows, BLK), 0) % BLK
    j = lax.broadcasted_iota(jnp.int32, (rows, BLK), 1)
    ok_prev = (j >= r) & (n >= 1) & is_lat
    ok_next = (j <= r) & (n + 1 < NB_LAT) & is_lat
    head_of_row = lax.broadcasted_iota(jnp.int32, (rows, 1), 0) // BLK
    q = q_ref[...] * (HEAD_DIM ** -0.5)
    dot = lambda a, b: jnp.dot(a.astype(BF16), b, preferred_element_type=F32)
    for kh in range(N_KV):
        hs = slice(kh * HEAD_DIM, (kh + 1) * HEAD_DIM)
        heads = range(kh * Q_GROUP, (kh + 1) * Q_GROUP)
        qg = jnp.concatenate([q[:, h * HEAD_DIM:(h + 1) * HEAD_DIM] for h in heads], axis=0)
        sk = jnp.zeros((rows, 1), F32)
        for g, h in enumerate(heads):
            sk = jnp.where(head_of_row == g, sink_ref[h], sk)
        s_p = jnp.where(ok_prev, _nt_dot(qg, kp_ref[:, hs]), NEG_INF)
        s_c = jnp.where(is_lat, _nt_dot(qg, kc_ref[:, hs]), NEG_INF)
        s_n = jnp.where(ok_next, _nt_dot(qg, kn_ref[:, hs]), NEG_INF)
        s_x = _nt_dot(qg, kx_ref[:, hs])
        s_x0 = s_x[:, 0:BLK]
        s_x1 = s_x[:, BLK:2 * BLK]
        m = jnp.maximum(jnp.maximum(jnp.maximum(s_p, s_c), jnp.maximum(s_n, s_x0)), s_x1)
        m = jnp.maximum(jnp.max(m, axis=1, keepdims=True), sk)
        e_p = jnp.exp(s_p - m)
        e_c = jnp.exp(s_c - m)
        e_n = jnp.exp(s_n - m)
        e_x0 = jnp.exp(s_x0 - m)
        e_x1 = jnp.exp(s_x1 - m)
        den = jnp.sum((e_p + e_c) + (e_n + e_x0) + e_x1, axis=1, keepdims=True) + jnp.exp(sk - m)
        o = (dot(e_p, vp_ref[:, hs]) + dot(e_c, vc_ref[:, hs]) + dot(e_n, vn_ref[:, hs])
             + dot(e_x0, vx_ref[0:BLK, hs]) + dot(e_x1, vx_ref[BLK:2 * BLK, hs]))
        o = o / den
        for g, h in enumerate(heads):
            o_ref[:, h * HEAD_DIM:(h + 1) * HEAD_DIM] = o[g * BLK:(g + 1) * BLK]


def _attention(sink, q, k, v, with_ctx):
    nblk = NB_LAT + (NB_CTX if with_ctx else 0)

    def q_idx(b, n):
        return (jnp.where(n < NB_LAT, b * NB_LAT + n, B * NB_LAT + b * NB_CTX + (n - NB_LAT)), 0)

    def kv_idx(off):
        def idx(b, n):
            nn = jnp.clip(jnp.minimum(n, NB_LAT - 1) + off, 0, NB_LAT - 1)
            return (b * NB_LAT + nn, 0)
        return idx

    ctx_idx = lambda b, n: (T_LAT // C + b, 0)
    kv_specs = lambda: [pl.BlockSpec((BLK, KV_W), kv_idx(-1)), pl.BlockSpec((BLK, KV_W), kv_idx(0)),
                        pl.BlockSpec((BLK, KV_W), kv_idx(1)), pl.BlockSpec((C, KV_W), ctx_idx)]
    return pl.pallas_call(
        _attn_kernel,
        grid=(B, nblk),
        in_specs=[pl.BlockSpec(memory_space=pltpu.SMEM), pl.BlockSpec((BLK, ATTN_W), q_idx)] + kv_specs() + kv_specs(),
        out_specs=pl.BlockSpec((BLK, ATTN_W), q_idx),
        out_shape=jax.ShapeDtypeStruct((T_ALL if with_ctx else T_LAT, ATTN_W), F32),
        compiler_params=_cp(32 << 20, 2),
        name="attention",
    )(sink, q, k, k, k, k, v, v, v, v)


def _hyena_pre_kernel(z_ref, zp_ref, zn_ref, w_ref, b_ref, u_ref, x0_ref):
    i = pl.program_id(0)
    tiles_per_seq = L // TM
    is_ctx = i >= B * tiles_per_seq
    first = is_ctx | (i % tiles_per_seq == 0)
    last = is_ctx | (i % tiles_per_seq == tiles_per_seq - 1)
    z = z_ref[...]
    prev_row = jnp.where(first, 0.0, zp_ref[7:8, :])
    next_row = jnp.where(last, 0.0, zn_ref[0:1, :])
    row = lax.broadcasted_iota(jnp.int32, z.shape, 0)
    z_m1 = jnp.where(row == 0, prev_row, pltpu.roll(z, 1, 0))
    z_p1 = jnp.where(row == TM - 1, next_row, pltpu.roll(z, TM - 1, 0))
    zc = b_ref[...] + z_m1 * w_ref[0:1, :] + z * w_ref[1:2, :] + z_p1 * w_ref[2:3, :]
    u_ref[...] = zc[:, 0:HY_W] * zc[:, HY_W:2 * HY_W]
    x0_ref[...] = zc[:, 2 * HY_W:3 * HY_W]


def _hyena_pre(z, short_w, short_b):
    nt = T_ALL // TM
    sub = TM // 8
    n8 = T_ALL // 8
    return pl.pallas_call(
        _hyena_pre_kernel,
        grid=(nt,),
        in_specs=[
            pl.BlockSpec((TM, 3 * HY_W), lambda i: (i, 0)),
            pl.BlockSpec((8, 3 * HY_W), lambda i: (jnp.maximum(i * sub - 1, 0), 0)),
            pl.BlockSpec((8, 3 * HY_W), lambda i: (jnp.minimum((i + 1) * sub, n8 - 1), 0)),
            pl.BlockSpec((SHORT_K, 3 * HY_W), lambda i: (0, 0)),
            pl.BlockSpec((1, 3 * HY_W), lambda i: (0, 0)),
        ],
        out_specs=[pl.BlockSpec((TM, HY_W), lambda i: (i, 0)), pl.BlockSpec((TM, HY_W), lambda i: (i, 0))],
        out_shape=[jax.ShapeDtypeStruct((T_ALL, HY_W), F32), jax.ShapeDtypeStruct((T_ALL, HY_W), F32)],
        compiler_params=_cp(32 << 20, 1),
        name="hyena_pre",
    )(z, z, z, short_w, short_b.reshape(1, 3 * HY_W))


def _dft_tables():
    t0 = jnp.arange(FFT_R, dtype=jnp.int32)[:, None, None]
    k1 = jnp.arange(K1P, dtype=jnp.int32)[None, :, None]
    t1 = jnp.arange(FFT_T1, dtype=jnp.int32)[None, None, :]
    m = (k1 * (FFT_R * t1 + t0)) % N_FFT
    ang = m.astype(F32) * (2.0 * math.pi / N_FFT)
    used = (k1 < K1_USED).astype(F32)
    g_cos = jnp.cos(ang) * used
    g_sin = jnp.sin(ang) * used
    a = jnp.arange(FFT_R, dtype=jnp.int32)
    ang2 = ((a[:, None] * a[None, :]) % FFT_R).astype(F32) * (2.0 * math.pi / FFT_R)
    return g_cos, g_sin, jnp.cos(ang2), jnp.sin(ang2)


def _hyena_spec_kernel(k_ref, gc_ref, gs_ref, fc_ref, fs_ref, wt_ref, kr_ref, ki_ref, ar_ref, ai_ref):
    half = pl.program_id(1)
    kk = lax.broadcasted_iota(jnp.int32, (K1H, 1), 0) + half * K1H
    sign = jnp.where(kk % 2 == 0, 1.0, -1.0).astype(F32)

    dot = lambda a, b: jnp.dot(a, b.astype(BF16), preferred_element_type=F32)

    def stage1(i, carry):
        t0s = [i * FFT_UNROLL + u for u in range(FFT_UNROLL)]
        loaded = [(k_ref[pl.ds(t0, FFT_T1, stride=FFT_R), :], k_ref[pl.ds(L + t0, FFT_T1, stride=FFT_R), :],
                   gc_ref[t0].astype(BF16), gs_ref[t0].astype(BF16)) for t0 in t0s]
        res = [(dot(gc, x_lo) + sign * dot(gc, x_hi), -(dot(gs, x_lo) + sign * dot(gs, x_hi)))
               for x_lo, x_hi, gc, gs in loaded]
        for t0, (a_r, a_i) in zip(t0s, res):
            rows = pl.ds(pl.multiple_of(t0 * K1H, 8), K1H)
            ar_ref[rows, :] = a_r
            ai_ref[rows, :] = a_i
        return carry

    lax.fori_loop(0, FFT_R // FFT_UNROLL, stage1, 0)
    fc = fc_ref[...].astype(BF16)
    fs = fs_ref[...].astype(BF16)

    def stage2(i, carry):
        kls = [i * FFT_UNROLL + u for u in range(FFT_UNROLL)]
        loaded = [(ar_ref[pl.ds(kl, FFT_R, stride=K1H), :], ai_ref[pl.ds(kl, FFT_R, stride=K1H), :]) for kl in kls]
        for kl, (a_r, a_i) in zip(kls, loaded):
            w = wt_ref[half * K1H + kl]
            kr_ref[kl] = (dot(fc, a_r) + dot(fs, a_i)) * w
            ki_ref[kl] = (dot(fc, a_i) - dot(fs, a_r)) * w
        return carry

    lax.fori_loop(0, K1H // FFT_UNROLL, stage2, 0)


def _hyena_spectrum(kfilt, g_cos, g_sin, f_cos, f_sin, wts):
    nct = HY_W // LANE
    gspec = pl.BlockSpec((FFT_R, K1H, FFT_T1), lambda c, h: (0, h, 0))
    fspec = pl.BlockSpec((FFT_R, FFT_R), lambda c, h: (0, 0))
    ospec = pl.BlockSpec((K1H, FFT_R, LANE), lambda c, h: (h, 0, c))
    return pl.pallas_call(
        _hyena_spec_kernel,
        grid=(nct, 2),
        in_specs=[pl.BlockSpec((N_FFT, LANE), lambda c, h: (0, c)), gspec, gspec, fspec, fspec,
                  pl.BlockSpec(memory_space=pltpu.SMEM)],
        out_specs=[ospec, ospec],
        out_shape=[jax.ShapeDtypeStruct((K1P, FFT_R, HY_W), F32)] * 2,
        scratch_shapes=[pltpu.VMEM((FFT_R * K1H, LANE), F32)] * 2,
        compiler_params=_cp(56 << 20, 0),
        name="hyena_spectrum",
    )(kfilt, g_cos, g_sin, f_cos, f_sin, wts)


def _hyena_fft_kernel(u_ref, gc_ref, gs_ref, ic_ref, is_ref, fc_ref, fs_ref, kr_ref, ki_ref, o_ref, ar_ref, ai_ref):
    bdot = lambda a, b: jnp.dot(a, b.astype(BF16), preferred_element_type=F32)

    def stage1(i, carry):
        t0s = [i * FFT_UNROLL + u for u in range(FFT_UNROLL)]
        xs = [u_ref[pl.ds(t0, FFT_T1, stride=FFT_R), :] for t0 in t0s]
        res = [(bdot(gc_ref[t0], x), -bdot(gs_ref[t0], x)) for t0, x in zip(t0s, xs)]
        for t0, (a_r, a_i) in zip(t0s, res):
            rows = pl.ds(pl.multiple_of(t0 * K1P, 8), K1P)
            ar_ref[rows, :] = a_r
            ai_ref[rows, :] = a_i
        return carry

    lax.fori_loop(0, FFT_R // FFT_UNROLL, stage1, 0)
    fc = fc_ref[...]
    fs = fs_ref[...]

    def stage23(i, carry):
        k1s = [i * FFT_UNROLL + u for u in range(FFT_UNROLL)]
        loaded = [(ar_ref[pl.ds(k1, FFT_R, stride=K1P), :], ai_ref[pl.ds(k1, FFT_R, stride=K1P), :],
                   kr_ref[k1], ki_ref[k1]) for k1 in k1s]
        res = []
        for a_r, a_i, k_r, k_i in loaded:
            z_r = bdot(fc, a_r) + bdot(fs, a_i)
            z_i = bdot(fc, a_i) - bdot(fs, a_r)
            y_r = z_r * k_r - z_i * k_i
            y_i = z_r * k_i + z_i * k_r
            res.append((bdot(fc, y_r) - bdot(fs, y_i), bdot(fc, y_i) + bdot(fs, y_r)))
        for k1, (b_r, b_i) in zip(k1s, res):
            ar_ref[pl.ds(k1, FFT_R, stride=K1P), :] = b_r
            ai_ref[pl.ds(k1, FFT_R, stride=K1P), :] = b_i
        return carry

    lax.fori_loop(0, -(-K1_USED // FFT_UNROLL), stage23, 0)

    def stage4(i, carry):
        t0s = [i * FFT_UNROLL + u for u in range(FFT_UNROLL)]
        loaded = []
        for t0 in t0s:
            rows = pl.ds(pl.multiple_of(t0 * K1P, 8), K1P)
            loaded.append((ar_ref[rows, :], ai_ref[rows, :]))
        res = [bdot(ic_ref[t0], b_r) - bdot(is_ref[t0], b_i) for t0, (b_r, b_i) in zip(t0s, loaded)]
        for t0, y in zip(t0s, res):
            o_ref[pl.ds(t0, FFT_T1, stride=FFT_R), :] = y
        return carry

    lax.fori_loop(0, FFT_R // FFT_UNROLL, stage4, 0)


def _hyena_fft(u, g_cos_bf, g_sin_bf, i_cos_bf, i_sin_bf, f_cos_bf, f_sin_bf, k_r, k_i, layer):
    nct = HY_W // LANE
    one = pl.Buffered(1)
    gspec = pl.BlockSpec((FFT_R, K1P, FFT_T1), lambda c, b: (0, 0, 0), pipeline_mode=one)
    ispec = pl.BlockSpec((FFT_R, FFT_T1, K1P), lambda c, b: (0, 0, 0), pipeline_mode=one)
    fspec = pl.BlockSpec((FFT_R, FFT_R), lambda c, b: (0, 0), pipeline_mode=one)
    kspec = pl.BlockSpec((None, K1P, FFT_R, LANE), lambda c, b: (layer, 0, 0, c), pipeline_mode=one)
    return pl.pallas_call(
        _hyena_fft_kernel,
        grid=(nct, B),
        in_specs=[pl.BlockSpec((L, LANE), lambda c, b: (b, c)), gspec, gspec, ispec, ispec, fspec, fspec, kspec, kspec],
        out_specs=pl.BlockSpec((L, LANE), lambda c, b: (b, c)),
        out_shape=jax.ShapeDtypeStruct((T_LAT, HY_W), F32),
        scratch_shapes=[pltpu.VMEM((FFT_R * K1P, LANE), F32)] * 2,
        compiler_params=_cp(56 << 20, 2),
        name="hyena_fft",
    )(u, g_cos_bf, g_sin_bf, i_cos_bf, i_sin_bf, f_cos_bf, f_sin_bf, k_r, k_i)


def _hyena_ctx_kernel(u_ref, k_ref, dc_ref, ds_ref, o_ref):
    dot = lambda a, b: jnp.dot(a, b, precision=HIGHEST, preferred_element_type=F32)
    dc = dc_ref[...]
    ds = ds_ref[...]
    u = u_ref[...]
    kf = k_ref[...]
    u_r = dot(dc[:, 0:C], u)
    u_i = -dot(ds[:, 0:C], u)
    k_r = dot(dc, kf)
    k_i = -dot(ds, kf)
    y_r = u_r * k_r - u_i * k_i
    y_i = u_r * k_i + u_i * k_r
    o_ref[...] = (dot(dc[0:C, :], y_r) - dot(ds[0:C, :], y_i)) * (1.0 / (2 * C))


def _hyena_ctx(u, kfilt_ctx, d_cos, d_sin):
    full = lambda b: (0, 0)
    return pl.pallas_call(
        _hyena_ctx_kernel,
        grid=(B,),
        in_specs=[pl.BlockSpec((C, HY_W), lambda b: (T_LAT // C + b, 0)),
                  pl.BlockSpec((2 * C, HY_W), full), pl.BlockSpec((2 * C, 2 * C), full), pl.BlockSpec((2 * C, 2 * C), full)],
        out_specs=pl.BlockSpec((C, HY_W), lambda b: (b, 0)),
        out_shape=jax.ShapeDtypeStruct((T_CTX, HY_W), F32),
        compiler_params=_cp(32 << 20, 1),
        name="hyena_ctx",
    )(u, kfilt_ctx, d_cos, d_sin)


def _hyena_filter(n, w1, b1, freq, w2, b2, w3):
    t = jnp.linspace(0.0, 1.0, n, dtype=F32)[:, None]
    bands = (FILTER_EMB - 1) // 2
    w = 2.0 * math.pi * jnp.arange(n, dtype=F32)[:, None] / n
    f = jnp.linspace(1e-4, bands - 1, bands, dtype=F32)[None, :]
    z = jnp.concatenate([t, jnp.cos(f * w), -jnp.sin(f * w)], axis=-1)
    mm = functools.partial(jnp.matmul, precision=HIGHEST)
    h = jnp.sin(freq * (mm(z, w1) + b1))
    h = jnp.sin(freq * (mm(h, w2) + b2))
    deltas = jnp.abs(jnp.linspace(math.log(DECAY_TARGET) / DECAY_FAST, math.log(DECAY_TARGET) / DECAY_SLOW,
                                  HY_W, dtype=F32))
    decay = jnp.exp(-t * deltas[None, :])
    h_fwd = mm(h, w3[:, :HY_W]) * decay
    h_bwd_rev = mm(h[::-1], w3[:, HY_W:]) * decay[::-1]
    k = jnp.concatenate([h_fwd, jnp.zeros((1, HY_W), F32), h_bwd_rev[:-1]], axis=0)
    return k / jnp.sum(jnp.abs(k), axis=0, keepdims=True)


def _s5_matrices(a_re, a_im, log_dt, b_re, b_im, c_re, c_im, d_skip):
    dt = jnp.exp(log_dt)[:, :, None]
    lam_re = jnp.minimum(a_re, -1e-4)
    mag1 = jnp.exp(lam_re * dt)
    lbr = mag1 * jnp.cos(a_im * dt)
    lbi = mag1 * jnp.sin(a_im * dt)
    den = lam_re * lam_re + a_im * a_im
    fr = ((lbr - 1.0) * lam_re + lbi * a_im) / den
    fi = (lbi * lam_re - (lbr - 1.0) * a_im) / den
    bbr = fr[..., None] * b_re - fi[..., None] * b_im
    bbi = fr[..., None] * b_im + fi[..., None] * b_re
    j = jnp.arange(S5_CH + 1, dtype=F32)[:, None, None, None]
    magj = jnp.exp(j * (lam_re * dt)[None])
    pr = magj * jnp.cos(j * (a_im * dt)[None])
    pi = magj * jnp.sin(j * (a_im * dt)[None])
    hi = functools.partial(jnp.einsum, precision=HIGHEST)
    lbr_j = pr[..., None] * bbr[None] - pi[..., None] * bbi[None]
    lbi_j = pr[..., None] * bbi[None] + pi[..., None] * bbr[None]
    m = hi('dgop,jdgpi->jdgoi', c_re, lbr_j) - hi('dgop,jdgpi->jdgoi', c_im, lbi_j)
    eye_g = jnp.eye(S5_GROUPS, dtype=F32)
    s = jnp.arange(S5_CH)
    blocks = jnp.einsum('jdgoi,gh->djgiho', m[0:S5_CH], eye_g).reshape(2, S5_CH, S5_W, S5_W)
    lag0 = blocks[0, 0] + blocks[1, 0] + jnp.diag(d_skip)
    e_all = jnp.concatenate([blocks[1, S5_CH - 1:0:-1], lag0[None], blocks[0, 1:S5_CH]], axis=0).astype(BF16)
    sf_r = lbr_j[S5_CH - 1 - s, 0]
    sf_i = lbi_j[S5_CH - 1 - s, 0]
    sb_r = lbr_j[s, 1]
    sb_i = lbi_j[s, 1]
    st = jnp.stack([sf_r, sf_i, sb_r, sb_i], axis=0)
    rr = jnp.arange(S5_ROWW, dtype=jnp.int32)
    cc = jnp.arange(S5_W, dtype=jnp.int32)
    ws = _s5_expand(jnp.transpose(st, (1, 2, 4, 0, 3)).reshape(S5_ROWW, S5_W),
                    (cc[:, None] // S5_STATE == rr[None, :] // S5_NSTATE)
                    & (cc[:, None] % S5_STATE == rr[None, :] % S5_STATE),
                    (rr // S5_GROUP) % S5_GROUPS, (rr % S5_NSTATE) // S5_STATE)
    tt = jnp.arange(S5_CH)
    cf_r = c_re[0][None] * pr[tt + 1, 0][:, :, None, :] - c_im[0][None] * pi[tt + 1, 0][:, :, None, :]
    cf_i = c_re[0][None] * pi[tt + 1, 0][:, :, None, :] + c_im[0][None] * pr[tt + 1, 0][:, :, None, :]
    cb_r = c_re[1][None] * pr[S5_CH - tt, 1][:, :, None, :] - c_im[1][None] * pi[S5_CH - tt, 1][:, :, None, :]
    cb_i = c_re[1][None] * pi[S5_CH - tt, 1][:, :, None, :] + c_im[1][None] * pr[S5_CH - tt, 1][:, :, None, :]
    ct = jnp.stack([cf_r, -cf_i, cb_r, -cb_i], axis=0)
    wc = _s5_expand(jnp.transpose(ct, (0, 2, 4, 1, 3)).reshape(4 * S5_NSTATE, S5_W),
                    (cc[:, None] // S5_GROUP == rr[None, :] // S5_W)
                    & (cc[:, None] % S5_GROUP == rr[None, :] % S5_GROUP),
                    (rr % S5_NSTATE) // S5_STATE, (rr % S5_W) // S5_GROUP)
    lam_p = jnp.stack([jnp.stack([pr[S5_CH, 0], pi[S5_CH, 0]]), jnp.stack([pr[S5_CH, 1], pi[S5_CH, 1]])])
    return ws, e_all, wc, lam_p.reshape(2, 2, 1, S5_NSTATE)


S5_TN = 512
S5_NS = S5_ROWW // S5_W


def _s5_expand_kernel(a_ref, ex_ref, rg_ref, cg_ref, o_ref):
    v = jnp.dot(a_ref[...], ex_ref[...], preferred_element_type=F32)
    o_ref[...] = jnp.where(rg_ref[...] == cg_ref[...], v, 0.0).astype(BF16)


def _s5_expand(compact, placement, row_group, col_group):
    n = compact.shape[0]
    return pl.pallas_call(
        _s5_expand_kernel,
        grid=(S5_ROWW // S5_TN,),
        in_specs=[pl.BlockSpec((n, S5_W), lambda j: (0, 0)), pl.BlockSpec((S5_W, S5_TN), lambda j: (0, j)),
                  pl.BlockSpec((n, 1), lambda j: (0, 0)), pl.BlockSpec((1, S5_TN), lambda j: (0, j))],
        out_specs=pl.BlockSpec((n, S5_TN), lambda j: (0, j)),
        out_shape=jax.ShapeDtypeStruct((n, S5_ROWW), BF16),
        compiler_params=_cp(32 << 20, 0),
        name="s5_expand",
    )(compact.astype(BF16), placement.astype(BF16), row_group.reshape(n, 1), col_group.reshape(1, S5_ROWW))


def _s5_in_kernel(u_ref, ws_ref, e_ref, s_ref, o_ref):
    j = pl.program_id(0)

    @pl.when(j < S5_NS)
    def _():
        s_ref[...] = jnp.dot(u_ref[...], ws_ref[...], preferred_element_type=F32)

    @pl.when(j >= S5_NS)
    def _():
        t = j - S5_NS
        acc = jnp.dot(u_ref[:, 0:S5_W], e_ref[S5_CH - 1 + t], preferred_element_type=F32)
        for s in range(1, S5_CH):
            acc = acc + jnp.dot(u_ref[:, s * S5_W:(s + 1) * S5_W], e_ref[S5_CH - 1 + t - s],
                                preferred_element_type=F32)
        o_ref[...] = acc


def _s5_in(u_rows, ws, e_all, layer):
    return pl.pallas_call(
        _s5_in_kernel,
        grid=(2 * S5_NS,),
        in_specs=[pl.BlockSpec((N_CHUNK, S5_ROWW), lambda j: (0, 0)),
                  pl.BlockSpec((None, S5_ROWW, S5_W), lambda j: (layer, 0, jnp.minimum(j, S5_NS - 1))),
                  pl.BlockSpec((None, 2 * S5_CH - 1, S5_W, S5_W), lambda j: (layer, 0, 0, 0))],
        out_specs=[pl.BlockSpec((N_CHUNK, S5_W), lambda j: (0, jnp.minimum(j, S5_NS - 1))),
                   pl.BlockSpec((N_CHUNK, S5_W), lambda j: (0, jnp.maximum(j - S5_NS, 0)))],
        out_shape=[jax.ShapeDtypeStruct((N_CHUNK, 4 * S5_NSTATE), F32), jax.ShapeDtypeStruct((N_CHUNK, S5_ROWW), F32)],
        compiler_params=_cp(48 << 20, 1),
        name="s5_in",
    )(u_rows, ws, e_all)


def _s5_scan_kernel(s_ref, lam_ref, h_ref):
    lam = [[lam_ref[d, p] for p in range(2)] for d in range(2)]

    def step(b, d, chunk, h):
        row = pl.ds(chunk, 1)
        cols_r = pl.ds(d * 2 * S5_NSTATE, S5_NSTATE)
        cols_i = pl.ds(d * 2 * S5_NSTATE + S5_NSTATE, S5_NSTATE)
        h_ref[row, cols_r] = h[0]
        h_ref[row, cols_i] = h[1]
        s_r = s_ref[row, cols_r]
        s_i = s_ref[row, cols_i]
        lr, li = lam[d]
        return (lr * h[0] - li * h[1] + s_r, lr * h[1] + li * h[0] + s_i)

    def chain_order(b, d, n_ctx_done):
        ctx0 = B * LAT_CHUNKS + b * CTX_CHUNKS
        lat0 = b * LAT_CHUNKS
        if d == 0:
            return (lambda i: ctx0 + i), (lambda i: lat0 + i)
        return (lambda i: ctx0 + CTX_CHUNKS - 1 - i), (lambda i: lat0 + LAT_CHUNKS - 1 - i)

    chains = [(b, d) for b in range(B) for d in range(2)]
    zero = jnp.zeros((1, S5_NSTATE), F32)
    init = tuple((zero, zero) for _ in chains)

    def phase(n_steps, which, carry):
        def body(i, hs):
            out = []
            for (b, d), h in zip(chains, hs):
                order = chain_order(b, d, 0)[which]
                out.append(step(b, d, order(i), h))
            return tuple(out)
        return lax.fori_loop(0, n_steps, body, carry)

    carry = phase(CTX_CHUNKS, 0, init)
    phase(LAT_CHUNKS, 1, carry)


def _s5_scan(s_rows, lam_p):
    return pl.pallas_call(
        _s5_scan_kernel,
        out_shape=jax.ShapeDtypeStruct((N_CHUNK, 4 * S5_NSTATE), F32),
        compiler_params=pltpu.CompilerParams(vmem_limit_bytes=48 << 20),
        name="s5_scan",
    )(s_rows, lam_p)


S5_TM = N_CHUNK // 2
S5_TPN = S5_TN // S5_W


def _s5_out_kernel(h_ref, w_ref, y_ref, *rest):
    o_refs, hb_ref = rest[:-1], rest[-1]
    j = pl.program_id(1)

    @pl.when(j == 0)
    def _():
        hb_ref[...] = h_ref[...].astype(BF16)

    acc = jnp.dot(hb_ref[...], w_ref[...], preferred_element_type=F32) + y_ref[...]
    for tt in range(S5_TPN):
        for h, o_ref in enumerate(o_refs):
            o_ref[pl.ds(j * S5_TPN + tt, S5_TM, stride=S5_CH), :] = (
                acc[:, tt * S5_W + h * LANE:tt * S5_W + (h + 1) * LANE])


def _s5_out(h_rows, wc, y_in, layer):
    nn = S5_ROWW // S5_TN
    n_out = S5_W // LANE
    return pl.pallas_call(
        _s5_out_kernel,
        grid=(N_CHUNK // S5_TM, nn),
        in_specs=[pl.BlockSpec((S5_TM, 4 * S5_NSTATE), lambda i, j: (i, 0), pipeline_mode=pl.Buffered(1)),
                  pl.BlockSpec((None, 4 * S5_NSTATE, S5_TN), lambda i, j: (layer, 0, j)),
                  pl.BlockSpec((S5_TM, S5_TN), lambda i, j: (i, j))],
        out_specs=[pl.BlockSpec((S5_TM * S5_CH, LANE), lambda i, j: (i, 0))] * n_out,
        out_shape=[jax.ShapeDtypeStruct((T_ALL, LANE), F32)] * n_out,
        scratch_shapes=[pltpu.VMEM((S5_TM, 4 * S5_NSTATE), BF16)],
        compiler_params=_cp(52 << 20, 2),
        name="s5_out",
    )(h_rows, wc, y_in)


def _rms(x):
    return x * lax.rsqrt(jnp.mean(x * x, axis=-1, keepdims=True) + LN_EPS)


def _layer_norm(x, g, b):
    mu = jnp.mean(x, axis=-1, keepdims=True)
    xc = x - mu
    var = jnp.mean(xc * xc, axis=-1, keepdims=True)
    return xc * lax.rsqrt(var + LN_EPS) * g + b


def _merge_kernel(x_ref, attn_ref, conv_ref, hu_ref, x0_ref, s5a_ref, s5b_ref, g1_ref, sh2_ref, sc2_ref, mixg_ref,
                  hyd_ref, wglu_ref, wout_ref, lng_ref, lnb_ref, o_ref, u2_ref, u2p_ref):
    hy = (conv_ref[...] + hu_ref[...] * hyd_ref[...]) * x0_ref[...]
    g = jax.nn.gelu(jnp.concatenate([s5a_ref[...], s5b_ref[...]], axis=-1))
    s5 = g * jax.nn.sigmoid(jnp.dot(g.astype(BF16), wglu_ref[...], preferred_element_type=F32))
    mixg = mixg_ref[...]
    parts = [_rms(attn_ref[...]) * mixg[:, 0:ATTN_W],
             _rms(hy) * mixg[:, ATTN_W:ATTN_W + HY_W],
             _rms(s5) * mixg[:, ATTN_W + HY_W:MIX_W]]
    mix = jnp.concatenate(parts, axis=-1).astype(BF16)
    o = jnp.dot(mix, wout_ref[...], preferred_element_type=F32)
    x1 = _layer_norm(ALPHA * x_ref[...] + g1_ref[...] * o, lng_ref[...], lnb_ref[...])
    o_ref[...] = x1
    u2 = x1 * (1.0 + sc2_ref[...]) + sh2_ref[...]
    u2_ref[...] = u2
    u2p_ref[...] = _pack_pairs(u2)


def _merge(n_rows, xall, attn, conv, hu, x0c, s5y, g1, sh2, sc2, mix_g, hy_d, wglu_bf, wout_bf, ln_g, ln_b):
    nt = n_rows // TM
    row = lambda i: (i, 0)
    full = lambda i: (0, 0)
    return pl.pallas_call(
        _merge_kernel,
        grid=(nt,),
        in_specs=[pl.BlockSpec((TM, D), row), pl.BlockSpec((TM, ATTN_W), row), pl.BlockSpec((TM, HY_W), row),
                  pl.BlockSpec((TM, HY_W), row), pl.BlockSpec((TM, HY_W), row),
                  pl.BlockSpec((TM, LANE), row), pl.BlockSpec((TM, LANE), row),
                  _vec_spec(TM), _vec_spec(TM), _vec_spec(TM), pl.BlockSpec((1, MIX_W), full),
                  pl.BlockSpec((1, HY_W), full), pl.BlockSpec((S5_W, S5_W), full), pl.BlockSpec((MIX_W, D), full),
                  pl.BlockSpec((1, D), full), pl.BlockSpec((1, D), full)],
        out_specs=[pl.BlockSpec((TM, D), row), pl.BlockSpec((TM, D), row), pl.BlockSpec((TM, HALF_D), row)],
        out_shape=[jax.ShapeDtypeStruct((n_rows, D), F32), jax.ShapeDtypeStruct((n_rows, D), F32),
                   jax.ShapeDtypeStruct((n_rows, HALF_D), jnp.int32)],
        compiler_params=_cp(48 << 20, 1),
        name="merge",
    )(xall, attn, conv, hu, x0c, s5y[0], s5y[1], g1, sh2, sc2, mix_g.reshape(1, MIX_W), hy_d.reshape(1, HY_W), wglu_bf,
      wout_bf, ln_g.reshape(1, D), ln_b.reshape(1, D))


def _router_kernel(u_ref, wt_ref, b_ref, e_ref, g_ref):
    logits = lax.dot_general(wt_ref[...], u_ref[...], (((1,), (1,)), ((), ())), precision=HIGHEST,
                             preferred_element_type=F32)
    scores = jax.nn.sigmoid(logits)
    biased = scores + b_ref[...]
    ninf = -jnp.inf
    grow = lax.broadcasted_iota(jnp.int32, (EGROUP, TM), 0)
    groups = [biased[gi * EGROUP:(gi + 1) * EGROUP] for gi in range(N_EGROUPS)]
    gscore = []
    for vals in groups:
        m1 = jnp.max(vals, axis=0, keepdims=True)
        i1 = jnp.min(jnp.where(vals == m1, grow, EGROUP), axis=0, keepdims=True)
        m2 = jnp.max(jnp.where(grow == i1, ninf, vals), axis=0, keepdims=True)
        gscore.append(m1 + m2)
    kept = []
    for gi in range(N_EGROUPS):
        rank = jnp.zeros((1, TM), jnp.int32)
        for gj in range(N_EGROUPS):
            if gj == gi:
                continue
            ahead = (gscore[gj] > gscore[gi]) | ((gscore[gj] == gscore[gi]) & (gj < gi))
            rank = rank + ahead.astype(jnp.int32)
        kept.append(jnp.where(rank < TOPK_GROUPS, groups[gi], ninf))
    masked = jnp.concatenate(kept, axis=0)
    row = lax.broadcasted_iota(jnp.int32, (N_EXPERTS, TM), 0)
    gates = []
    gsum = jnp.zeros((1, TM), F32)
    for kk in range(TOP_K):
        m = jnp.max(masked, axis=0, keepdims=True)
        idx = jnp.min(jnp.where(masked == m, row, N_EXPERTS), axis=0, keepdims=True)
        hit = row == idx
        gate = jnp.sum(jnp.where(hit, scores, 0.0), axis=0, keepdims=True)
        masked = jnp.where(hit, ninf, masked)
        e_ref[kk:kk + 1, :] = idx
        gates.append(gate)
        gsum = gsum + gate
    for kk in range(TOP_K):
        g_ref[kk:kk + 1, :] = gates[kk] / gsum * ROUTED_SCALE


def _router(n_rows, u2, w_router_t, router_bias):
    nt = n_rows // TM
    col = lambda i: (0, i)
    return pl.pallas_call(
        _router_kernel,
        grid=(nt,),
        in_specs=[pl.BlockSpec((TM, D), lambda i: (i, 0)),
                  pl.BlockSpec((N_EXPERTS, D), lambda i: (0, 0)), pl.BlockSpec((N_EXPERTS, 1), lambda i: (0, 0))],
        out_specs=[pl.BlockSpec((TOP_K, TM), col), pl.BlockSpec((TOP_K, TM), col)],
        out_shape=[jax.ShapeDtypeStruct((TOP_K, n_rows), jnp.int32), jax.ShapeDtypeStruct((TOP_K, n_rows), F32)],
        compiler_params=_cp(32 << 20, 1),
        name="router",
    )(u2, w_router_t, router_bias.reshape(N_EXPERTS, 1))


def _dispatch(top_e):
    t = top_e.shape[1]
    tk = t * TOP_K
    nblk = tk // MOE_BLOCK
    n_steps = nblk + N_EXPERTS
    flat_e = top_e.reshape(tk)
    pos = jnp.arange(tk, dtype=jnp.int32)
    pos_bits = (tk - 1).bit_length()
    assert pos_bits + (N_EXPERTS - 1).bit_length() < 32
    order = lax.sort(flat_e * (1 << pos_bits) + pos) % (1 << pos_bits)
    experts = jnp.arange(N_EXPERTS, dtype=jnp.int32)
    counts = jnp.sum((flat_e[None, :] == experts[:, None]).astype(jnp.int32), axis=1)
    ends = jnp.cumsum(counts)
    starts = ends - counts
    fb = starts // MOE_BLOCK
    npairs = jnp.where(counts > 0, (ends - 1) // MOE_BLOCK - fb + 1, 0)
    pend = jnp.cumsum(npairs)
    poff = pend - npairs
    n_pairs = pend[-1]
    s = jnp.arange(n_steps, dtype=jnp.int32)
    sc = jnp.minimum(s, n_pairs - 1)
    pe = jnp.sum((pend[None, :] <= sc[:, None]).astype(jnp.int32), axis=1)
    pb = fb[pe] + (sc - poff[pe])
    lo = jnp.where(s < n_pairs, jnp.maximum(starts[pe] - pb * MOE_BLOCK, 0), 0)
    hi = jnp.where(s < n_pairs, jnp.minimum(ends[pe] - pb * MOE_BLOCK, MOE_BLOCK), 0)
    last_of_block = ((s == n_pairs - 1) | ((s + 1 < n_pairs) & (jnp.roll(pb, -1) != pb))).astype(jnp.int32)
    used = counts > 0
    ordinal = jnp.cumsum(used.astype(jnp.int32)) - 1
    later = jnp.where(used, experts, N_EXPERTS)
    next_used = lax.cummin(jnp.concatenate([later[1:], jnp.full((1,), N_EXPERTS, jnp.int32)]), reverse=True)
    nxt = next_used[pe]
    nxt = jnp.where(nxt < N_EXPERTS, nxt, pe)
    parity = ordinal[pe] % 2
    w_even = jnp.where(parity == 0, pe, nxt)
    w_odd = jnp.where(parity == 1, pe, nxt)
    return dict(pe=pe, pb=pb, lo=lo, hi=hi, last=last_of_block, parity=parity, w_even=w_even, w_odd=w_odd,
                tok=order % t, order=order)


X_SLOTS = 4


def _expert_kernel(pe, pb, plo, phi, plast, ppar, pwe, pwo, xs_hbm, wg0_ref, wu0_ref, wd0_ref, wg1_ref, wu1_ref,
                   wd1_ref, out_hbm, xbuf, xsem, obuf, osem, wgb, wub, wdb):
    del pwe, pwo
    s = pl.program_id(0)
    prev = jnp.maximum(s - 1, 0)
    n_blocks = out_hbm.shape[0] // MOE_BLOCK
    new_expert = (s == 0) | (pe[s] != pe[prev])

    def x_copy(b, sl):
        rows = pl.ds(pl.multiple_of(b * MOE_BLOCK, MOE_BLOCK), MOE_BLOCK)
        return pltpu.make_async_copy(xs_hbm.at[rows], xbuf.at[sl], xsem.at[sl])

    @pl.when(s == 0)
    def _():
        for b in range(X_SLOTS - 1):
            x_copy(b, b).start()
        obuf[...] = jnp.zeros(obuf.shape, obuf.dtype)

    for par, (wg_ref, wu_ref, wd_ref) in enumerate(((wg0_ref, wu0_ref, wd0_ref), (wg1_ref, wu1_ref, wd1_ref))):
        @pl.when(new_expert & (ppar[s] == par))
        def _():
            wgb[...] = wg_ref[...].astype(BF16)
            wub[...] = wu_ref[...].astype(BF16)
            wdb[...] = wd_ref[...].astype(BF16)

    lo = plo[s]
    hi = phi[s]
    blk = pb[s]
    slot = blk % 2
    first_of_block = (s == 0) | (blk != pb[prev])

    def out_copy(b, sl):
        rows = pl.ds(pl.multiple_of(b * MOE_BLOCK, MOE_BLOCK), MOE_BLOCK)
        return pltpu.make_async_copy(obuf.at[sl], out_hbm.at[rows], osem.at[sl])

    @pl.when(first_of_block)
    def _():
        ahead = blk + (X_SLOTS - 1)

        @pl.when(ahead < n_blocks)
        def _():
            x_copy(ahead, ahead % X_SLOTS).start()
        x_copy(blk, blk % X_SLOTS).wait()

    def ffn():
        x_lo, x_hi = _unpack_pairs(xbuf[blk % X_SLOTS])
        x_lo = x_lo.astype(BF16)
        x_hi = x_hi.astype(BF16)
        hg = (jnp.dot(x_lo, wgb[0:HALF_D], preferred_element_type=F32)
              + jnp.dot(x_hi, wgb[HALF_D:D], preferred_element_type=F32))
        hu = (jnp.dot(x_lo, wub[0:HALF_D], preferred_element_type=F32)
              + jnp.dot(x_hi, wub[HALF_D:D], preferred_element_type=F32))
        h = (hg * jax.nn.sigmoid(hg)) * hu
        return _pack_pairs(jnp.dot(h.astype(BF16), wdb[...], preferred_element_type=F32))

    @pl.when(first_of_block & (blk >= 2))
    def _():
        out_copy(blk - 2, slot).wait()

    @pl.when(hi > lo)
    def _():
        row = lax.broadcasted_iota(jnp.int32, (MOE_BLOCK, HALF_D), 0)
        mine = first_of_block | ((row >= lo) & (row < hi))
        obuf[slot] = jnp.where(mine, ffn(), obuf[slot])

    @pl.when(plast[s] == 1)
    def _():
        out_copy(blk, slot).start()

    @pl.when(s == pl.num_programs(0) - 1)
    def _():
        out_copy(n_blocks - 2, (n_blocks - 2) % 2).wait()
        out_copy(n_blocks - 1, (n_blocks - 1) % 2).wait()


def _experts(xs, disp, wg, wu, wd, layer):
    tk = xs.shape[0]
    n_steps = disp['pe'].shape[0]
    assert tk // MOE_BLOCK >= X_SLOTS
    weven = lambda shape: pl.BlockSpec((None, None) + shape, lambda s, *p: (layer, p[6][s], 0, 0))
    wodd = lambda shape: pl.BlockSpec((None, None) + shape, lambda s, *p: (layer, p[7][s], 0, 0))
    grid_spec = pltpu.PrefetchScalarGridSpec(
        num_scalar_prefetch=8,
        grid=(n_steps,),
        in_specs=[pl.BlockSpec(memory_space=pl.ANY),
                  weven((D, EXPERT_FF)), weven((D, EXPERT_FF)), weven((EXPERT_FF, D)),
                  wodd((D, EXPERT_FF)), wodd((D, EXPERT_FF)), wodd((EXPERT_FF, D))],
        out_specs=pl.BlockSpec(memory_space=pl.ANY),
        scratch_shapes=[pltpu.VMEM((X_SLOTS, MOE_BLOCK, HALF_D), jnp.int32), pltpu.SemaphoreType.DMA((X_SLOTS,)),
                        pltpu.VMEM((2, MOE_BLOCK, HALF_D), jnp.int32), pltpu.SemaphoreType.DMA((2,)),
                        pltpu.VMEM((D, EXPERT_FF), BF16), pltpu.VMEM((D, EXPERT_FF), BF16),
                        pltpu.VMEM((EXPERT_FF, D), BF16)],
    )
    return pl.pallas_call(
        _expert_kernel,
        grid_spec=grid_spec,
        out_shape=jax.ShapeDtypeStruct((tk, HALF_D), jnp.int32),
        compiler_params=_cp(40 << 20, 1),
        name="experts",
    )(disp['pe'], disp['pb'], disp['lo'], disp['hi'], disp['last'], disp['parity'], disp['w_even'], disp['w_odd'],
      xs, wg, wu, wd, wg, wu, wd)


SC_ROWS = 64


def _row_gather(table, idx):
    n = idx.shape[0]
    d = table.shape[1]
    mesh = plsc.VectorSubcoreMesh(core_axis_name="c", subcore_axis_name="s")
    n_workers = mesh.num_cores * mesh.num_subcores
    per_worker = n // n_workers
    assert per_worker * n_workers == n and per_worker % SC_ROWS == 0

    @functools.partial(
        pl.kernel, mesh=mesh,
        out_type=jax.ShapeDtypeStruct((n, d), table.dtype),
        scratch_types=[pltpu.VMEM((SC_ROWS,), jnp.int32), pltpu.VMEM((SC_ROWS, d), table.dtype),
                       pltpu.SemaphoreType.DMA],
    )
    def gather(table_hbm, idx_hbm, out_hbm, idx_v, rows_v, sem):
        worker = lax.axis_index("s") * mesh.num_cores + lax.axis_index("c")
        base = worker * per_worker

        @pl.loop(0, per_worker // SC_ROWS)
        def _(c):
            off = pl.multiple_of(base + c * SC_ROWS, 8)
            pltpu.sync_copy(idx_hbm.at[pl.ds(off, SC_ROWS)], idx_v)
            pltpu.async_copy(table_hbm.at[idx_v], rows_v, sem).wait()
            pltpu.sync_copy(rows_v, out_hbm.at[pl.ds(off, SC_ROWS)])

    return gather(table, idx)


def _row_scatter(rows, idx):
    n, d = rows.shape
    mesh = plsc.VectorSubcoreMesh(core_axis_name="c", subcore_axis_name="s")
    n_workers = mesh.num_cores * mesh.num_subcores
    per_worker = n // n_workers
    assert per_worker * n_workers == n and per_worker % SC_ROWS == 0

    @functools.partial(
        pl.kernel, mesh=mesh,
        out_type=jax.ShapeDtypeStruct((n, d), rows.dtype),
        scratch_types=[pltpu.VMEM((SC_ROWS,), jnp.int32), pltpu.VMEM((SC_ROWS, d), rows.dtype),
                       pltpu.SemaphoreType.DMA],
    )
    def scatter(rows_hbm, idx_hbm, out_hbm, idx_v, rows_v, sem):
        worker = lax.axis_index("s") * mesh.num_cores + lax.axis_index("c")
        base = worker * per_worker

        @pl.loop(0, per_worker // SC_ROWS)
        def _(c):
            off = pl.multiple_of(base + c * SC_ROWS, 8)
            pltpu.sync_copy(idx_hbm.at[pl.ds(off, SC_ROWS)], idx_v)
            pltpu.sync_copy(rows_hbm.at[pl.ds(off, SC_ROWS)], rows_v)
            pltpu.async_copy(rows_v, out_hbm.at[idx_v], sem).wait()

    return scatter(rows, idx)


FM = 128


def _ffn_out_kernel(x_ref, u_ref, r_ref, gate_ref, g2_ref, wsg_ref, wsu_ref, wsd_ref, lng_ref, lnb_ref, o_ref):
    u_lo, u_hi = _unpack_pairs(u_ref[...])
    u_lo = u_lo.astype(BF16)
    u_hi = u_hi.astype(BF16)
    hg = (jnp.dot(u_lo, wsg_ref[0:HALF_D], preferred_element_type=F32)
          + jnp.dot(u_hi, wsg_ref[HALF_D:D], preferred_element_type=F32))
    hu = (jnp.dot(u_lo, wsu_ref[0:HALF_D], preferred_element_type=F32)
          + jnp.dot(u_hi, wsu_ref[HALF_D:D], preferred_element_type=F32))
    f = jnp.dot(((hg * jax.nn.sigmoid(hg)) * hu).astype(BF16), wsd_ref[...], preferred_element_type=F32)
    gates = jnp.transpose(jnp.concatenate([gate_ref[...], jnp.zeros((FM - TOP_K, FM), F32)], axis=0))
    r_lo = jnp.zeros((FM, HALF_D), F32)
    r_hi = jnp.zeros((FM, HALF_D), F32)
    for kk in range(TOP_K):
        k_lo, k_hi = _unpack_pairs(r_ref[kk])
        gk = gates[:, kk:kk + 1]
        r_lo = r_lo + k_lo * gk
        r_hi = r_hi + k_hi * gk
    f = jnp.concatenate([r_lo, r_hi], axis=1) + f
    o_ref[...] = _layer_norm(ALPHA * x_ref[...] + g2_ref[...] * f, lng_ref[...], lnb_ref[...])


def _ffn_out(n_rows, x1, u2p, routed, gate, g2, wsg_bf, wsu_bf, wsd_bf, ln_g, ln_b):
    row = lambda i: (i, 0)
    full = lambda i: (0, 0)
    return pl.pallas_call(
        _ffn_out_kernel,
        grid=(n_rows // FM,),
        in_specs=[pl.BlockSpec((FM, D), row), pl.BlockSpec((FM, HALF_D), row),
                  pl.BlockSpec((TOP_K, FM, HALF_D), lambda i: (0, i, 0)), pl.BlockSpec((TOP_K, FM), lambda i: (0, i)),
                  _vec_spec(FM), pl.BlockSpec((D, EXPERT_FF), full), pl.BlockSpec((D, EXPERT_FF), full),
                  pl.BlockSpec((EXPERT_FF, D), full), pl.BlockSpec((1, D), full), pl.BlockSpec((1, D), full)],
        out_specs=pl.BlockSpec((FM, D), row),
        out_shape=jax.ShapeDtypeStruct((n_rows, D), F32),
        compiler_params=_cp(40 << 20, 1),
        name="ffn_out",
    )(x1, u2p, routed, gate, g2, wsg_bf, wsu_bf, wsd_bf, ln_g.reshape(1, D), ln_b.reshape(1, D))


def _rope_tables():
    t = jnp.arange(L, dtype=jnp.int32)
    row = (t // GRID_W).astype(F32)
    col = (t % GRID_W).astype(F32)
    inv_freq = ROPE_BASE ** (-jnp.arange(0, AXIS_DIM, 2, dtype=F32) / AXIS_DIM)
    half = AXIS_DIM // 2

    def axis(pos):
        ang = pos[:, None] * inv_freq[None, :]
        c = jnp.cos(ang)
        s = jnp.sin(ang)
        return jnp.concatenate([c, c], axis=1), jnp.concatenate([-s, s], axis=1)

    cr, sr = axis(row)
    cc, sc = axis(col)
    cos_h = jnp.concatenate([cr, cc], axis=1)
    sin_h = jnp.concatenate([sr, sc], axis=1)
    cos_l = jnp.tile(cos_h, (B, LANE // HEAD_DIM))
    sin_l = jnp.tile(sin_h, (B, LANE // HEAD_DIM))
    cos_t = jnp.concatenate([cos_l, jnp.ones((T_CTX, LANE), F32)], axis=0)
    sin_t = jnp.concatenate([sin_l, jnp.zeros((T_CTX, LANE), F32)], axis=0)
    del half
    return cos_t, sin_t


def kernel(x, c, ctx, c_ctx, w_ada, b_ada, w_in, w_out, sink, mix_g, hy_short_w, hy_short_b, hy_w1, hy_b1, hy_freq,
           hy_w2, hy_b2, hy_w3, hy_d, s5_a_re, s5_a_im, s5_log_dt, s5_b_re, s5_b_im, s5_c_re, s5_c_im, s5_d, s5_w_glu,
           ln1_g, ln1_b, ln2_g, ln2_b, w_router, router_bias, w_exp_gate, w_exp_up, w_exp_down, w_sh_gate, w_sh_up,
           w_sh_down):
    xall = jnp.concatenate([x.reshape(T_LAT, D), ctx.reshape(T_CTX, D)], axis=0)
    cvec = jnp.concatenate([c, c_ctx[None, :], jnp.zeros((8 - B - 1, D), F32)], axis=0)
    mod = _ada(cvec, w_ada, b_ada)[:, 0:B + 1, :].reshape(DEPTH, B + 1, 6, 1, D)

    cos_t, sin_t = _rope_tables()
    g_cos, g_sin, f_cos, f_sin = _dft_tables()
    g_cos_bf, g_sin_bf = g_cos.astype(BF16), g_sin.astype(BF16)
    i_cos_bf = jnp.swapaxes(g_cos, 1, 2).astype(BF16)
    i_sin_bf = jnp.swapaxes(g_sin, 1, 2).astype(BF16)
    f_cos_bf, f_sin_bf = f_cos.astype(BF16), f_sin.astype(BF16)
    k1 = jnp.arange(K1P)
    spec_w = jnp.where((k1 == 0) | (k1 == FFT_R // 2), 1.0, 2.0) * (k1 < K1_USED) / N_FFT
    spec_w = spec_w.astype(F32)
    kt = jnp.arange(2 * C, dtype=jnp.int32)
    ang_c = ((kt[:, None] * kt[None, :]) % (2 * C)).astype(F32) * (2.0 * math.pi / (2 * C))
    d_cos, d_sin = jnp.cos(ang_c), jnp.sin(ang_c)

    filt_params = (hy_w1, hy_b1, hy_freq, hy_w2, hy_b2, hy_w3)
    filt_lat = jax.vmap(functools.partial(_hyena_filter, L))(*filt_params)
    filt_ctx = jax.vmap(functools.partial(_hyena_filter, C))(*(p[0:DEPTH - 1] for p in filt_params))
    k_r, k_i = jax.vmap(_hyena_spectrum, in_axes=(0, None, None, None, None, None))(
        filt_lat, g_cos, g_sin, f_cos, f_sin, spec_w)
    ws, e_all, wc, lam_p = jax.vmap(_s5_matrices)(s5_a_re, s5_a_im, s5_log_dt, s5_b_re, s5_b_im, s5_c_re, s5_c_im, s5_d)

    for l in range(DEPTH):
        last = l == DEPTH - 1
        n_rows = T_LAT if last else T_ALL
        sh1, sc1, g1, sh2, sc2, g2 = (mod[l, :, j] for j in range(6))

        q, k, v, hz, s5u = _inproj(xall, sh1, sc1, w_in[l].astype(BF16), cos_t, sin_t)
        attn = _attention(sink[l], q, k, v, with_ctx=not last)

        hu, x0c = _hyena_pre(hz, hy_short_w[l], hy_short_b[l])
        conv = _hyena_fft(hu, g_cos_bf, g_sin_bf, i_cos_bf, i_sin_bf, f_cos_bf, f_sin_bf, k_r, k_i, l)
        if not last:
            conv = jnp.concatenate([conv, _hyena_ctx(hu, filt_ctx[l], d_cos, d_sin)], axis=0)

        s_rows, y_in = _s5_in(s5u, ws, e_all, l)
        s5y = _s5_out(_s5_scan(s_rows, lam_p[l]), wc, y_in, l)

        x1, u2, u2p = _merge(n_rows, xall, attn, conv, hu, x0c, s5y, g1, sh2, sc2, mix_g[l], hy_d[l],
                             s5_w_glu[l].astype(BF16), w_out[l].astype(BF16), ln1_g[l], ln1_b[l])

        top_e, gate = _router(n_rows, u2, w_router[l].T, router_bias[l])
        disp = _dispatch(top_e)
        ys = _experts(_row_gather(u2p, disp['tok']), disp, w_exp_gate, w_exp_up, w_exp_down, l)
        routed = _row_scatter(ys, disp['order']).reshape(TOP_K, n_rows, HALF_D)
        xall = _ffn_out(n_rows, x1, u2p, routed, gate, g2, w_sh_gate[l].astype(BF16), w_sh_up[l].astype(BF16),
                        w_sh_down[l].astype(BF16), ln2_g[l], ln2_b[l])
    return xall.reshape(B, L, D)
```

```python
import functools
import math

import jax
import jax.numpy as jnp
from jax import lax
from jax.experimental import pallas as pl
from jax.experimental.pallas import tpu as pltpu
from jax.experimental.pallas import tpu_sc as plsc

F32 = jnp.float32
BF16 = jnp.bfloat16
HIGHEST = lax.Precision.HIGHEST

D = 1024
B = 2
L = 8192
DEPTH = 2
GRID_W = 64
C = 256
T_LAT = B * L
T_CTX = B * C
T_ALL = T_LAT + T_CTX

HEAD_DIM = 64
N_Q = 8
N_KV = 2
Q_GROUP = N_Q // N_KV
ATTN_W = N_Q * HEAD_DIM
KV_W = N_KV * HEAD_DIM
HY_W = 256
S5_W = 256
MIX_W = ATTN_W + HY_W + S5_W
K_OFF = ATTN_W
V_OFF = K_OFF + KV_W
HY_OFF = V_OFF + KV_W
S5_OFF = HY_OFF + 3 * HY_W
IN_W = S5_OFF + S5_W
WINDOW = 128
BLK = 128
NEG_INF = -1e30
ROPE_BASE = 10000.0
AXIS_DIM = HEAD_DIM // 2

SHORT_K = 3
FILTER_EMB = 33
DECAY_FAST = 0.3
DECAY_SLOW = 1.5
DECAY_TARGET = 1e-2

S5_GROUP = 16
S5_GROUPS = S5_W // S5_GROUP
S5_STATE = 64
S5_NSTATE = S5_GROUPS * S5_STATE
S5_CH = 16
S5_ROWW = S5_CH * S5_W
N_CHUNK = T_ALL // S5_CH
LAT_CHUNKS = L // S5_CH
CTX_CHUNKS = C // S5_CH

N_EXPERTS = 256
TOP_K = 8
N_EGROUPS = 8
EGROUP = N_EXPERTS // N_EGROUPS
TOPK_GROUPS = 4
EXPERT_FF = 256
ROUTED_SCALE = 2.5
MOE_BLOCK = 256

ALPHA = (2 * DEPTH) ** 0.25
LN_EPS = 1e-5

N_FFT = 2 * L
FFT_R = 128
FFT_T1 = L // FFT_R
K1_USED = FFT_R // 2 + 1
K1P = 80
K1H = K1P // 2
FFT_UNROLL = 4

TM = 256
LANE = 128
VMEM_CAP = 60000 * 1024


def _cp(vmem_bytes, n_axes):
    return pltpu.CompilerParams(
        dimension_semantics=("arbitrary",) * n_axes if n_axes else None,
        vmem_limit_bytes=min(int(vmem_bytes), VMEM_CAP),
    )


HALF_D = D // 2
HIGH_HALF_WORD = 0xFFFF0000


def _pack_pairs(x):
    bits = lax.bitcast_convert_type(x.astype(BF16).astype(F32), jnp.uint32)
    packed = (bits[:, 0:HALF_D] >> 16) | (bits[:, HALF_D:D] & jnp.uint32(HIGH_HALF_WORD))
    return lax.bitcast_convert_type(packed, jnp.int32)


def _unpack_pairs(p):
    bits = lax.bitcast_convert_type(p, jnp.uint32)
    low = lax.bitcast_convert_type(bits << 16, F32)
    high = lax.bitcast_convert_type(bits & jnp.uint32(HIGH_HALF_WORD), F32)
    return low, high


def _mod_sel(rows_per_tile):
    per_batch = L // rows_per_tile
    return lambda i: jnp.minimum(i // per_batch, 2)


def _vec_spec(rows_per_tile):
    sel = _mod_sel(rows_per_tile)
    return pl.BlockSpec((None, 1, D), lambda i: (sel(i), 0, 0))


ADA_TN = 1536


def _ada_kernel(c_ref, w_ref, b_ref, o_ref):
    c = c_ref[...]
    s = c * jax.nn.sigmoid(c)
    o_ref[...] = jnp.dot(s, w_ref[...], precision=HIGHEST, preferred_element_type=F32) + b_ref[...]


def _ada(cvec, w_ada, b_ada):
    return pl.pallas_call(
        _ada_kernel,
        grid=(DEPTH, 6 * D // ADA_TN),
        in_specs=[
            pl.BlockSpec((8, D), lambda l, j: (0, 0)),
            pl.BlockSpec((None, D, ADA_TN), lambda l, j: (l, 0, j)),
            pl.BlockSpec((None, 1, ADA_TN), lambda l, j: (l, 0, j)),
        ],
        out_specs=pl.BlockSpec((None, 8, ADA_TN), lambda l, j: (l, 0, j)),
        out_shape=jax.ShapeDtypeStruct((DEPTH, 8, 6 * D), F32),
        compiler_params=_cp(40 << 20, 2),
        name="ada",
    )(cvec, w_ada, b_ada.reshape(DEPTH, 1, 6 * D))


def _inproj_kernel(x_ref, sh_ref, sc_ref, w_ref, cos_ref, sin_ref, q_ref, k_ref, v_ref, hy_ref, s5_ref, *s5_tok_refs):
    u = x_ref[...] * (1.0 + sc_ref[...]) + sh_ref[...]
    proj = jnp.dot(u.astype(BF16), w_ref[...], preferred_element_type=F32)
    cos = cos_ref[...]
    sin = sin_ref[...]
    lane = lax.broadcasted_iota(jnp.int32, (TM, LANE), 1)
    first_half = (lane % AXIS_DIM) < (AXIS_DIM // 2)

    def rope(xc):
        partner = jnp.where(first_half, pltpu.roll(xc, LANE - AXIS_DIM // 2, 1), pltpu.roll(xc, AXIS_DIM // 2, 1))
        return xc * cos + partner * sin

    for j in range(ATTN_W // LANE):
        q_ref[:, j * LANE:(j + 1) * LANE] = rope(proj[:, j * LANE:(j + 1) * LANE]).astype(BF16)
    k_ref[...] = rope(proj[:, K_OFF:V_OFF]).astype(BF16)
    v_ref[...] = proj[:, V_OFF:HY_OFF].astype(BF16)
    hy_ref[...] = proj[:, HY_OFF:S5_OFF]
    for h, tok_ref in enumerate(s5_tok_refs):
        tok_ref[...] = proj[:, S5_OFF + h * LANE:S5_OFF + (h + 1) * LANE]
    for s in range(S5_CH):
        for h, tok_ref in enumerate(s5_tok_refs):
            s5_ref[:, s * S5_W + h * LANE:s * S5_W + (h + 1) * LANE] = (
                tok_ref[pl.ds(s, TM // S5_CH, stride=S5_CH), :].astype(BF16))


def _inproj(xall, sh, sc, w_in_bf, cos_t, sin_t):
    nt = T_ALL // TM
    row = lambda i: (i, 0)
    return pl.pallas_call(
        _inproj_kernel,
        grid=(nt,),
        in_specs=[
            pl.BlockSpec((TM, D), row),
            _vec_spec(TM),
            _vec_spec(TM),
            pl.BlockSpec((D, IN_W), lambda i: (0, 0)),
            pl.BlockSpec((TM, LANE), row),
            pl.BlockSpec((TM, LANE), row),
        ],
        out_specs=[
            pl.BlockSpec((TM, ATTN_W), row),
            pl.BlockSpec((TM, KV_W), row),
            pl.BlockSpec((TM, KV_W), row),
            pl.BlockSpec((TM, 3 * HY_W), row),
            pl.BlockSpec((TM // S5_CH, S5_ROWW), row),
        ],
        out_shape=[
            jax.ShapeDtypeStruct((T_ALL, ATTN_W), BF16),
            jax.ShapeDtypeStruct((T_ALL, KV_W), BF16),
            jax.ShapeDtypeStruct((T_ALL, KV_W), BF16),
            jax.ShapeDtypeStruct((T_ALL, 3 * HY_W), F32),
            jax.ShapeDtypeStruct((N_CHUNK, S5_ROWW), BF16),
        ],
        scratch_shapes=[pltpu.VMEM((TM, LANE), F32)] * (S5_W // LANE),
        compiler_params=_cp(40 << 20, 1),
        name="inproj",
    )(xall, sh, sc, w_in_bf, cos_t, sin_t)


NB_LAT = L // BLK
NB_CTX = C // BLK


def _nt_dot(a, b):
    return lax.dot_general(a, b, (((1,), (1,)), ((), ())), preferred_element_type=F32)


def _attn_kernel(sink_ref, q_ref, kp_ref, kc_ref, kn_ref, kx_ref, vp_ref, vc_ref, vn_ref, vx_ref, o_ref):
    n = pl.program_id(1)
    is_lat = n < NB_LAT
    rows = Q_GROUP * BLK
    r = lax.broadcasted_iota(jnp.int32, (rows, BLK), 0) % BLK
    j = lax.broadcasted_iota(jnp.int32, (rows, BLK), 1)
    ok_prev = (j >= r) & (n >= 1) & is_lat
    ok_next = (j <= r) & (n + 1 < NB_LAT) & is_lat
    head_of_row = lax.broadcasted_iota(jnp.int32, (rows, 1), 0) // BLK
    q = q_ref[...] * (HEAD_DIM ** -0.5)
    dot = lambda a, b: jnp.dot(a.astype(BF16), b, preferred_element_type=F32)
    for kh in range(N_KV):
        hs = slice(kh * HEAD_DIM, (kh + 1) * HEAD_DIM)
        heads = range(kh * Q_GROUP, (kh + 1) * Q_GROUP)
        qg = jnp.concatenate([q[:, h * HEAD_DIM:(h + 1) * HEAD_DIM] for h in heads], axis=0)
        sk = jnp.zeros((rows, 1), F32)
        for g, h in enumerate(heads):
            sk = jnp.where(head_of_row == g, sink_ref[h], sk)
        s_p = jnp.where(ok_prev, _nt_dot(qg, kp_ref[:, hs]), NEG_INF)
        s_c = jnp.where(is_lat, _nt_dot(qg, kc_ref[:, hs]), NEG_INF)
        s_n = jnp.where(ok_next, _nt_dot(qg, kn_ref[:, hs]), NEG_INF)
        s_x = _nt_dot(qg, kx_ref[:, hs])
        s_x0 = s_x[:, 0:BLK]
        s_x1 = s_x[:, BLK:2 * BLK]
        m = jnp.maximum(jnp.maximum(jnp.maximum(s_p, s_c), jnp.maximum(s_n, s_x0)), s_x1)
        m = jnp.maximum(jnp.max(m, axis=1, keepdims=True), sk)
        e_p = jnp.exp(s_p - m)
        e_c = jnp.exp(s_c - m)
        e_n = jnp.exp(s_n - m)
        e_x0 = jnp.exp(s_x0 - m)
        e_x1 = jnp.exp(s_x1 - m)
        den = jnp.sum((e_p + e_c) + (e_n + e_x0) + e_x1, axis=1, keepdims=True) + jnp.exp(sk - m)
        o = (dot(e_p, vp_ref[:, hs]) + dot(e_c, vc_ref[:, hs]) + dot(e_n, vn_ref[:, hs])
             + dot(e_x0, vx_ref[0:BLK, hs]) + dot(e_x1, vx_ref[BLK:2 * BLK, hs]))
        o = o / den
        for g, h in enumerate(heads):
            o_ref[:, h * HEAD_DIM:(h + 1) * HEAD_DIM] = o[g * BLK:(g + 1) * BLK]


def _attention(sink, q, k, v, with_ctx):
    nblk = NB_LAT + (NB_CTX if with_ctx else 0)

    def q_idx(b, n):
        return (jnp.where(n < NB_LAT, b * NB_LAT + n, B * NB_LAT + b * NB_CTX + (n - NB_LAT)), 0)

    def kv_idx(off):
        def idx(b, n):
            nn = jnp.clip(jnp.minimum(n, NB_LAT - 1) + off, 0, NB_LAT - 1)
            return (b * NB_LAT + nn, 0)
        return idx

    ctx_idx = lambda b, n: (T_LAT // C + b, 0)
    kv_specs = lambda: [pl.BlockSpec((BLK, KV_W), kv_idx(-1)), pl.BlockSpec((BLK, KV_W), kv_idx(0)),
                        pl.BlockSpec((BLK, KV_W), kv_idx(1)), pl.BlockSpec((C, KV_W), ctx_idx)]
    return pl.pallas_call(
        _attn_kernel,
        grid=(B, nblk),
        in_specs=[pl.BlockSpec(memory_space=pltpu.SMEM), pl.BlockSpec((BLK, ATTN_W), q_idx)] + kv_specs() + kv_specs(),
        out_specs=pl.BlockSpec((BLK, ATTN_W), q_idx),
        out_shape=jax.ShapeDtypeStruct((T_ALL if with_ctx else T_LAT, ATTN_W), F32),
        compiler_params=_cp(32 << 20, 2),
        name="attention",
    )(sink, q, k, k, k, k, v, v, v, v)


def _hyena_pre_kernel(z_ref, zp_ref, zn_ref, w_ref, b_ref, u_ref, x0_ref):
    i = pl.program_id(0)
    tiles_per_seq = L // TM
    is_ctx = i >= B * tiles_per_seq
    first = is_ctx | (i % tiles_per_seq == 0)
    last = is_ctx | (i % tiles_per_seq == tiles_per_seq - 1)
    z = z_ref[...]
    prev_row = jnp.where(first, 0.0, zp_ref[7:8, :])
    next_row = jnp.where(last, 0.0, zn_ref[0:1, :])
    row = lax.broadcasted_iota(jnp.int32, z.shape, 0)
    z_m1 = jnp.where(row == 0, prev_row, pltpu.roll(z, 1, 0))
    z_p1 = jnp.where(row == TM - 1, next_row, pltpu.roll(z, TM - 1, 0))
    zc = b_ref[...] + z_m1 * w_ref[0:1, :] + z * w_ref[1:2, :] + z_p1 * w_ref[2:3, :]
    u_ref[...] = zc[:, 0:HY_W] * zc[:, HY_W:2 * HY_W]
    x0_ref[...] = zc[:, 2 * HY_W:3 * HY_W]


def _hyena_pre(z, short_w, short_b):
    nt = T_ALL // TM
    sub = TM // 8
    n8 = T_ALL // 8
    return pl.pallas_call(
        _hyena_pre_kernel,
        grid=(nt,),
        in_specs=[
            pl.BlockSpec((TM, 3 * HY_W), lambda i: (i, 0)),
            pl.BlockSpec((8, 3 * HY_W), lambda i: (jnp.maximum(i * sub - 1, 0), 0)),
            pl.BlockSpec((8, 3 * HY_W), lambda i: (jnp.minimum((i + 1) * sub, n8 - 1), 0)),
            pl.BlockSpec((SHORT_K, 3 * HY_W), lambda i: (0, 0)),
            pl.BlockSpec((1, 3 * HY_W), lambda i: (0, 0)),
        ],
        out_specs=[pl.BlockSpec((TM, HY_W), lambda i: (i, 0)), pl.BlockSpec((TM, HY_W), lambda i: (i, 0))],
        out_shape=[jax.ShapeDtypeStruct((T_ALL, HY_W), F32), jax.ShapeDtypeStruct((T_ALL, HY_W), F32)],
        compiler_params=_cp(32 << 20, 1),
        name="hyena_pre",
    )(z, z, z, short_w, short_b.reshape(1, 3 * HY_W))


def _dft_tables():
    t0 = jnp.arange(FFT_R, dtype=jnp.int32)[:, None, None]
    k1 = jnp.arange(K1P, dtype=jnp.int32)[None, :, None]
    t1 = jnp.arange(FFT_T1, dtype=jnp.int32)[None, None, :]
    m = (k1 * (FFT_R * t1 + t0)) % N_FFT
    ang = m.astype(F32) * (2.0 * math.pi / N_FFT)
    used = (k1 < K1_USED).astype(F32)
    g_cos = jnp.cos(ang) * used
    g_sin = jnp.sin(ang) * used
    a = jnp.arange(FFT_R, dtype=jnp.int32)
    ang2 = ((a[:, None] * a[None, :]) % FFT_R).astype(F32) * (2.0 * math.pi / FFT_R)
    return g_cos, g_sin, jnp.cos(ang2), jnp.sin(ang2)


def _hyena_spec_kernel(k_ref, gc_ref, gs_ref, fc_ref, fs_ref, wt_ref, kr_ref, ki_ref, ar_ref, ai_ref):
    half = pl.program_id(1)
    kk = lax.broadcasted_iota(jnp.int32, (K1H, 1), 0) + half * K1H
    sign = jnp.where(kk % 2 == 0, 1.0, -1.0).astype(F32)

    dot = lambda a, b: jnp.dot(a, b.astype(BF16), preferred_element_type=F32)

    def stage1(i, carry):
        t0s = [i * FFT_UNROLL + u for u in range(FFT_UNROLL)]
        loaded = [(k_ref[pl.ds(t0, FFT_T1, stride=FFT_R), :], k_ref[pl.ds(L + t0, FFT_T1, stride=FFT_R), :],
                   gc_ref[t0].astype(BF16), gs_ref[t0].astype(BF16)) for t0 in t0s]
        res = [(dot(gc, x_lo) + sign * dot(gc, x_hi), -(dot(gs, x_lo) + sign * dot(gs, x_hi)))
               for x_lo, x_hi, gc, gs in loaded]
        for t0, (a_r, a_i) in zip(t0s, res):
            rows = pl.ds(pl.multiple_of(t0 * K1H, 8), K1H)
            ar_ref[rows, :] = a_r
            ai_ref[rows, :] = a_i
        return carry

    lax.fori_loop(0, FFT_R // FFT_UNROLL, stage1, 0)
    fc = fc_ref[...].astype(BF16)
    fs = fs_ref[...].astype(BF16)

    def stage2(i, carry):
        kls = [i * FFT_UNROLL + u for u in range(FFT_UNROLL)]
        loaded = [(ar_ref[pl.ds(kl, FFT_R, stride=K1H), :], ai_ref[pl.ds(kl, FFT_R, stride=K1H), :]) for kl in kls]
        for kl, (a_r, a_i) in zip(kls, loaded):
            w = wt_ref[half * K1H + kl]
            kr_ref[kl] = (dot(fc, a_r) + dot(fs, a_i)) * w
            ki_ref[kl] = (dot(fc, a_i) - dot(fs, a_r)) * w
        return carry

    lax.fori_loop(0, K1H // FFT_UNROLL, stage2, 0)


def _hyena_spectrum(kfilt, g_cos, g_sin, f_cos, f_sin, wts):
    nct = HY_W // LANE
    gspec = pl.BlockSpec((FFT_R, K1H, FFT_T1), lambda c, h: (0, h, 0))
    fspec = pl.BlockSpec((FFT_R, FFT_R), lambda c, h: (0, 0))
    ospec = pl.BlockSpec((K1H, FFT_R, LANE), lambda c, h: (h, 0, c))
    return pl.pallas_call(
        _hyena_spec_kernel,
        grid=(nct, 2),
        in_specs=[pl.BlockSpec((N_FFT, LANE), lambda c, h: (0, c)), gspec, gspec, fspec, fspec,
                  pl.BlockSpec(memory_space=pltpu.SMEM)],
        out_specs=[ospec, ospec],
        out_shape=[jax.ShapeDtypeStruct((K1P, FFT_R, HY_W), F32)] * 2,
        scratch_shapes=[pltpu.VMEM((FFT_R * K1H, LANE), F32)] * 2,
        compiler_params=_cp(56 << 20, 0),
        name="hyena_spectrum",
    )(kfilt, g_cos, g_sin, f_cos, f_sin, wts)


def _hyena_fft_kernel(u_ref, gc_ref, gs_ref, ic_ref, is_ref, fc_ref, fs_ref, kr_ref, ki_ref, o_ref, ar_ref, ai_ref):
    bdot = lambda a, b: jnp.dot(a, b.astype(BF16), preferred_element_type=F32)

    def stage1(i, carry):
        t0s = [i * FFT_UNROLL + u for u in range(FFT_UNROLL)]
        xs = [u_ref[pl.ds(t0, FFT_T1, stride=FFT_R), :] for t0 in t0s]
        res = [(bdot(gc_ref[t0], x), -bdot(gs_ref[t0], x)) for t0, x in zip(t0s, xs)]
        for t0, (a_r, a_i) in zip(t0s, res):
            rows = pl.ds(pl.multiple_of(t0 * K1P, 8), K1P)
            ar_ref[rows, :] = a_r
            ai_ref[rows, :] = a_i
        return carry

    lax.fori_loop(0, FFT_R // FFT_UNROLL, stage1, 0)
    fc = fc_ref[...]
    fs = fs_ref[...]

    def stage23(i, carry):
        k1s = [i * FFT_UNROLL + u for u in range(FFT_UNROLL)]
        loaded = [(ar_ref[pl.ds(k1, FFT_R, stride=K1P), :], ai_ref[pl.ds(k1, FFT_R, stride=K1P), :],
                   kr_ref[k1], ki_ref[k1]) for k1 in k1s]
        res = []
        for a_r, a_i, k_r, k_i in loaded:
            z_r = bdot(fc, a_r) + bdot(fs, a_i)
            z_i = bdot(fc, a_i) - bdot(fs, a_r)
            y_r = z_r * k_r - z_i * k_i
            y_i = z_r * k_i + z_i * k_r
            res.append((bdot(fc, y_r) - bdot(fs, y_i), bdot(fc, y_i) + bdot(fs, y_r)))
        for k1, (b_r, b_i) in zip(k1s, res):
            ar_ref[pl.ds(k1, FFT_R, stride=K1P), :] = b_r
            ai_ref[pl.ds(k1, FFT_R, stride=K1P), :] = b_i
        return carry

    lax.fori_loop(0, -(-K1_USED // FFT_UNROLL), stage23, 0)

    def stage4(i, carry):
        t0s = [i * FFT_UNROLL + u for u in range(FFT_UNROLL)]
        loaded = []
        for t0 in t0s:
            rows = pl.ds(pl.multiple_of(t0 * K1P, 8), K1P)
            loaded.append((ar_ref[rows, :], ai_ref[rows, :]))
        res = [bdot(ic_ref[t0], b_r) - bdot(is_ref[t0], b_i) for t0, (b_r, b_i) in zip(t0s, loaded)]
        for t0, y in zip(t0s, res):
            o_ref[pl.ds(t0, FFT_T1, stride=FFT_R), :] = y
        return carry

    lax.fori_loop(0, FFT_R // FFT_UNROLL, stage4, 0)


def _hyena_fft(u, g_cos_bf, g_sin_bf, i_cos_bf, i_sin_bf, f_cos_bf, f_sin_bf, k_r, k_i, layer):
    nct = HY_W // LANE
    one = pl.Buffered(1)
    gspec = pl.BlockSpec((FFT_R, K1P, FFT_T1), lambda c, b: (0, 0, 0), pipeline_mode=one)
    ispec = pl.BlockSpec((FFT_R, FFT_T1, K1P), lambda c, b: (0, 0, 0), pipeline_mode=one)
    fspec = pl.BlockSpec((FFT_R, FFT_R), lambda c, b: (0, 0), pipeline_mode=one)
    kspec = pl.BlockSpec((None, K1P, FFT_R, LANE), lambda c, b: (layer, 0, 0, c), pipeline_mode=one)
    return pl.pallas_call(
        _hyena_fft_kernel,
        grid=(nct, B),
        in_specs=[pl.BlockSpec((L, LANE), lambda c, b: (b, c)), gspec, gspec, ispec, ispec, fspec, fspec, kspec, kspec],
        out_specs=pl.BlockSpec((L, LANE), lambda c, b: (b, c)),
        out_shape=jax.ShapeDtypeStruct((T_LAT, HY_W), F32),
        scratch_shapes=[pltpu.VMEM((FFT_R * K1P, LANE), F32)] * 2,
        compiler_params=_cp(56 << 20, 2),
        name="hyena_fft",
    )(u, g_cos_bf, g_sin_bf, i_cos_bf, i_sin_bf, f_cos_bf, f_sin_bf, k_r, k_i)


def _hyena_ctx_kernel(u_ref, k_ref, dc_ref, ds_ref, o_ref):
    dot = lambda a, b: jnp.dot(a, b, precision=HIGHEST, preferred_element_type=F32)
    dc = dc_ref[...]
    ds = ds_ref[...]
    u = u_ref[...]
    kf = k_ref[...]
    u_r = dot(dc[:, 0:C], u)
    u_i = -dot(ds[:, 0:C], u)
    k_r = dot(dc, kf)
    k_i = -dot(ds, kf)
    y_r = u_r * k_r - u_i * k_i
    y_i = u_r * k_i + u_i * k_r
    o_ref[...] = (dot(dc[0:C, :], y_r) - dot(ds[0:C, :], y_i)) * (1.0 / (2 * C))


def _hyena_ctx(u, kfilt_ctx, d_cos, d_sin):
    full = lambda b: (0, 0)
    return pl.pallas_call(
        _hyena_ctx_kernel,
        grid=(B,),
        in_specs=[pl.BlockSpec((C, HY_W), lambda b: (T_LAT // C + b, 0)),
                  pl.BlockSpec((2 * C, HY_W), full), pl.BlockSpec((2 * C, 2 * C), full), pl.BlockSpec((2 * C, 2 * C), full)],
        out_specs=pl.BlockSpec((C, HY_W), lambda b: (b, 0)),
        out_shape=jax.ShapeDtypeStruct((T_CTX, HY_W), F32),
        compiler_params=_cp(32 << 20, 1),
        name="hyena_ctx",
    )(u, kfilt_ctx, d_cos, d_sin)


def _hyena_filter(n, w1, b1, freq, w2, b2, w3):
    t = jnp.linspace(0.0, 1.0, n, dtype=F32)[:, None]
    bands = (FILTER_EMB - 1) // 2
    w = 2.0 * math.pi * jnp.arange(n, dtype=F32)[:, None] / n
    f = jnp.linspace(1e-4, bands - 1, bands, dtype=F32)[None, :]
    z = jnp.concatenate([t, jnp.cos(f * w), -jnp.sin(f * w)], axis=-1)
    mm = functools.partial(jnp.matmul, precision=HIGHEST)
    h = jnp.sin(freq * (mm(z, w1) + b1))
    h = jnp.sin(freq * (mm(h, w2) + b2))
    deltas = jnp.abs(jnp.linspace(math.log(DECAY_TARGET) / DECAY_FAST, math.log(DECAY_TARGET) / DECAY_SLOW,
                                  HY_W, dtype=F32))
    decay = jnp.exp(-t * deltas[None, :])
    h_fwd = mm(h, w3[:, :HY_W]) * decay
    h_bwd_rev = mm(h[::-1], w3[:, HY_W:]) * decay[::-1]
    k = jnp.concatenate([h_fwd, jnp.zeros((1, HY_W), F32), h_bwd_rev[:-1]], axis=0)
    return k / jnp.sum(jnp.abs(k), axis=0, keepdims=True)


def _s5_matrices(a_re, a_im, log_dt, b_re, b_im, c_re, c_im, d_skip):
    dt = jnp.exp(log_dt)[:, :, None]
    lam_re = jnp.minimum(a_re, -1e-4)
    mag1 = jnp.exp(lam_re * dt)
    lbr = mag1 * jnp.cos(a_im * dt)
    lbi = mag1 * jnp.sin(a_im * dt)
    den = lam_re * lam_re + a_im * a_im
    fr = ((lbr - 1.0) * lam_re + lbi * a_im) / den
    fi = (lbi * lam_re - (lbr - 1.0) * a_im) / den
    bbr = fr[..., None] * b_re - fi[..., None] * b_im
    bbi = fr[..., None] * b_im + fi[..., None] * b_re
    j = jnp.arange(S5_CH + 1, dtype=F32)[:, None, None, None]
    magj = jnp.exp(j * (lam_re * dt)[None])
    pr = magj * jnp.cos(j * (a_im * dt)[None])
    pi = magj * jnp.sin(j * (a_im * dt)[None])
    hi = functools.partial(jnp.einsum, precision=HIGHEST)
    lbr_j = pr[..., None] * bbr[None] - pi[..., None] * bbi[None]
    lbi_j = pr[..., None] * bbi[None] + pi[..., None] * bbr[None]
    m = hi('dgop,jdgpi->jdgoi', c_re, lbr_j) - hi('dgop,jdgpi->jdgoi', c_im, lbi_j)
    eye_g = jnp.eye(S5_GROUPS, dtype=F32)
    s = jnp.arange(S5_CH)
    blocks = jnp.einsum('jdgoi,gh->djgiho', m[0:S5_CH], eye_g).reshape(2, S5_CH, S5_W, S5_W)
    lag0 = blocks[0, 0] + blocks[1, 0] + jnp.diag(d_skip)
    e_all = jnp.concatenate([blocks[1, S5_CH - 1:0:-1], lag0[None], blocks[0, 1:S5_CH]], axis=0).astype(BF16)
    sf_r = lbr_j[S5_CH - 1 - s, 0]
    sf_i = lbi_j[S5_CH - 1 - s, 0]
    sb_r = lbr_j[s, 1]
    sb_i = lbi_j[s, 1]
    st = jnp.stack([sf_r, sf_i, sb_r, sb_i], axis=0)
    rr = jnp.arange(S5_ROWW, dtype=jnp.int32)
    cc = jnp.arange(S5_W, dtype=jnp.int32)
    ws = _s5_expand(jnp.transpose(st, (1, 2, 4, 0, 3)).reshape(S5_ROWW, S5_W),
                    (cc[:, None] // S5_STATE == rr[None, :] // S5_NSTATE)
                    & (cc[:, None] % S5_STATE == rr[None, :] % S5_STATE),
                    (rr // S5_GROUP) % S5_GROUPS, (rr % S5_NSTATE) // S5_STATE)
    tt = jnp.arange(S5_CH)
    cf_r = c_re[0][None] * pr[tt + 1, 0][:, :, None, :] - c_im[0][None] * pi[tt + 1, 0][:, :, None, :]
    cf_i = c_re[0][None] * pi[tt + 1, 0][:, :, None, :] + c_im[0][None] * pr[tt + 1, 0][:, :, None, :]
    cb_r = c_re[1][None] * pr[S5_CH - tt, 1][:, :, None, :] - c_im[1][None] * pi[S5_CH - tt, 1][:, :, None, :]
    cb_i = c_re[1][None] * pi[S5_CH - tt, 1][:, :, None, :] + c_im[1][None] * pr[S5_CH - tt, 1][:, :, None, :]
    ct = jnp.stack([cf_r, -cf_i, cb_r, -cb_i], axis=0)
    wc = _s5_expand(jnp.transpose(ct, (0, 2, 4, 1, 3)).reshape(4 * S5_NSTATE, S5_W),
                    (cc[:, None] // S5_GROUP == rr[None, :] // S5_W)
                    & (cc[:, None] % S5_GROUP == rr[None, :] % S5_GROUP),
                    (rr % S5_NSTATE) // S5_STATE, (rr % S5_W) // S5_GROUP)
    lam_p = jnp.stack([jnp.stack([pr[S5_CH, 0], pi[S5_CH, 0]]), jnp.stack([pr[S5_CH, 1], pi[S5_CH, 1]])])
    return ws, e_all, wc, lam_p.reshape(2, 2, 1, S5_NSTATE)


S5_TN = 512
S5_NS = S5_ROWW // S5_W


def _s5_expand_kernel(a_ref, ex_ref, rg_ref, cg_ref, o_ref):
    v = jnp.dot(a_ref[...], ex_ref[...], preferred_element_type=F32)
    o_ref[...] = jnp.where(rg_ref[...] == cg_ref[...], v, 0.0).astype(BF16)


def _s5_expand(compact, placement, row_group, col_group):
    n = compact.shape[0]
    return pl.pallas_call(
        _s5_expand_kernel,
        grid=(S5_ROWW // S5_TN,),
        in_specs=[pl.BlockSpec((n, S5_W), lambda j: (0, 0)), pl.BlockSpec((S5_W, S5_TN), lambda j: (0, j)),
                  pl.BlockSpec((n, 1), lambda j: (0, 0)), pl.BlockSpec((1, S5_TN), lambda j: (0, j))],
        out_specs=pl.BlockSpec((n, S5_TN), lambda j: (0, j)),
        out_shape=jax.ShapeDtypeStruct((n, S5_ROWW), BF16),
        compiler_params=_cp(32 << 20, 0),
        name="s5_expand",
    )(compact.astype(BF16), placement.astype(BF16), row_group.reshape(n, 1), col_group.reshape(1, S5_ROWW))


def _s5_in_kernel(u_ref, ws_ref, e_ref, s_ref, o_ref):
    j = pl.program_id(0)

    @pl.when(j < S5_NS)
    def _():
        s_ref[...] = jnp.dot(u_ref[...], ws_ref[...], preferred_element_type=F32)

    @pl.when(j >= S5_NS)
    def _():
        t = j - S5_NS
        acc = jnp.dot(u_ref[:, 0:S5_W], e_ref[S5_CH - 1 + t], preferred_element_type=F32)
        for s in range(1, S5_CH):
            acc = acc + jnp.dot(u_ref[:, s * S5_W:(s + 1) * S5_W], e_ref[S5_CH - 1 + t - s],
                                preferred_element_type=F32)
        o_ref[...] = acc


def _s5_in(u_rows, ws, e_all, layer):
    return pl.pallas_call(
        _s5_in_kernel,
        grid=(2 * S5_NS,),
        in_specs=[pl.BlockSpec((N_CHUNK, S5_ROWW), lambda j: (0, 0)),
                  pl.BlockSpec((None, S5_ROWW, S5_W), lambda j: (layer, 0, jnp.minimum(j, S5_NS - 1))),
                  pl.BlockSpec((None, 2 * S5_CH - 1, S5_W, S5_W), lambda j: (layer, 0, 0, 0))],
        out_specs=[pl.BlockSpec((N_CHUNK, S5_W), lambda j: (0, jnp.minimum(j, S5_NS - 1))),
                   pl.BlockSpec((N_CHUNK, S5_W), lambda j: (0, jnp.maximum(j - S5_NS, 0)))],
        out_shape=[jax.ShapeDtypeStruct((N_CHUNK, 4 * S5_NSTATE), F32), jax.ShapeDtypeStruct((N_CHUNK, S5_ROWW), F32)],
        compiler_params=_cp(48 << 20, 1),
        name="s5_in",
    )(u_rows, ws, e_all)


def _s5_scan_kernel(s_ref, lam_ref, h_ref):
    lam = [[lam_ref[d, p] for p in range(2)] for d in range(2)]

    def step(b, d, chunk, h):
        row = pl.ds(chunk, 1)
        cols_r = pl.ds(d * 2 * S5_NSTATE, S5_NSTATE)
        cols_i = pl.ds(d * 2 * S5_NSTATE + S5_NSTATE, S5_NSTATE)
        h_ref[row, cols_r] = h[0]
        h_ref[row, cols_i] = h[1]
        s_r = s_ref[row, cols_r]
        s_i = s_ref[row, cols_i]
        lr, li = lam[d]
        return (lr * h[0] - li * h[1] + s_r, lr * h[1] + li * h[0] + s_i)

    def chain_order(b, d, n_ctx_done):
        ctx0 = B * LAT_CHUNKS + b * CTX_CHUNKS
        lat0 = b * LAT_CHUNKS
        if d == 0:
            return (lambda i: ctx0 + i), (lambda i: lat0 + i)
        return (lambda i: ctx0 + CTX_CHUNKS - 1 - i), (lambda i: lat0 + LAT_CHUNKS - 1 - i)

    chains = [(b, d) for b in range(B) for d in range(2)]
    zero = jnp.zeros((1, S5_NSTATE), F32)
    init = tuple((zero, zero) for _ in chains)

    def phase(n_steps, which, carry):
        def body(i, hs):
            out = []
            for (b, d), h in zip(chains, hs):
                order = chain_order(b, d, 0)[which]
                out.append(step(b, d, order(i), h))
            return tuple(out)
        return lax.fori_loop(0, n_steps, body, carry)

    carry = phase(CTX_CHUNKS, 0, init)
    phase(LAT_CHUNKS, 1, carry)


def _s5_scan(s_rows, lam_p):
    return pl.pallas_call(
        _s5_scan_kernel,
        out_shape=jax.ShapeDtypeStruct((N_CHUNK, 4 * S5_NSTATE), F32),
        compiler_params=pltpu.CompilerParams(vmem_limit_bytes=48 << 20),
        name="s5_scan",
    )(s_rows, lam_p)


S5_TM = N_CHUNK // 2
S5_TPN = S5_TN // S5_W


def _s5_out_kernel(h_ref, w_ref, y_ref, *rest):
    o_refs, hb_ref = rest[:-1], rest[-1]
    j = pl.program_id(1)

    @pl.when(j == 0)
    def _():
        hb_ref[...] = h_ref[...].astype(BF16)

    acc = jnp.dot(hb_ref[...], w_ref[...], preferred_element_type=F32) + y_ref[...]
    for tt in range(S5_TPN):
        for h, o_ref in enumerate(o_refs):
            o_ref[pl.ds(j * S5_TPN + tt, S5_TM, stride=S5_CH), :] = (
                acc[:, tt * S5_W + h * LANE:tt * S5_W + (h + 1) * LANE])


def _s5_out(h_rows, wc, y_in, layer):
    nn = S5_ROWW // S5_TN
    n_out = S5_W // LANE
    return pl.pallas_call(
        _s5_out_kernel,
        grid=(N_CHUNK // S5_TM, nn),
        in_specs=[pl.BlockSpec((S5_TM, 4 * S5_NSTATE), lambda i, j: (i, 0), pipeline_mode=pl.Buffered(1)),
                  pl.BlockSpec((None, 4 * S5_NSTATE, S5_TN), lambda i, j: (layer, 0, j)),
                  pl.BlockSpec((S5_TM, S5_TN), lambda i, j: (i, j))],
        out_specs=[pl.BlockSpec((S5_TM * S5_CH, LANE), lambda i, j: (i, 0))] * n_out,
        out_shape=[jax.ShapeDtypeStruct((T_ALL, LANE), F32)] * n_out,
        scratch_shapes=[pltpu.VMEM((S5_TM, 4 * S5_NSTATE), BF16)],
        compiler_params=_cp(52 << 20, 2),
        name="s5_out",
    )(h_rows, wc, y_in)


def _rms(x):
    return x * lax.rsqrt(jnp.mean(x * x, axis=-1, keepdims=True) + LN_EPS)


def _layer_norm(x, g, b):
    mu = jnp.mean(x, axis=-1, keepdims=True)
    xc = x - mu
    var = jnp.mean(xc * xc, axis=-1, keepdims=True)
    return xc * lax.rsqrt(var + LN_EPS) * g + b


def _merge_kernel(x_ref, attn_ref, conv_ref, hu_ref, x0_ref, s5a_ref, s5b_ref, g1_ref, sh2_ref, sc2_ref, mixg_ref,
                  hyd_ref, wglu_ref, wout_ref, lng_ref, lnb_ref, o_ref, u2_ref, u2p_ref):
    hy = (conv_ref[...] + hu_ref[...] * hyd_ref[...]) * x0_ref[...]
    g = jax.nn.gelu(jnp.concatenate([s5a_ref[...], s5b_ref[...]], axis=-1))
    s5 = g * jax.nn.sigmoid(jnp.dot(g.astype(BF16), wglu_ref[...], preferred_element_type=F32))
    mixg = mixg_ref[...]
    parts = [_rms(attn_ref[...]) * mixg[:, 0:ATTN_W],
             _rms(hy) * mixg[:, ATTN_W:ATTN_W + HY_W],
             _rms(s5) * mixg[:, ATTN_W + HY_W:MIX_W]]
    mix = jnp.concatenate(parts, axis=-1).astype(BF16)
    o = jnp.dot(mix, wout_ref[...], preferred_element_type=F32)
    x1 = _layer_norm(ALPHA * x_ref[...] + g1_ref[...] * o, lng_ref[...], lnb_ref[...])
    o_ref[...] = x1
    u2 = x1 * (1.0 + sc2_ref[...]) + sh2_ref[...]
    u2_ref[...] = u2
    u2p_ref[...] = _pack_pairs(u2)


def _merge(n_rows, xall, attn, conv, hu, x0c, s5y, g1, sh2, sc2, mix_g, hy_d, wglu_bf, wout_bf, ln_g, ln_b):
    nt = n_rows // TM
    row = lambda i: (i, 0)
    full = lambda i: (0, 0)
    return pl.pallas_call(
        _merge_kernel,
        grid=(nt,),
        in_specs=[pl.BlockSpec((TM, D), row), pl.BlockSpec((TM, ATTN_W), row), pl.BlockSpec((TM, HY_W), row),
                  pl.BlockSpec((TM, HY_W), row), pl.BlockSpec((TM, HY_W), row),
                  pl.BlockSpec((TM, LANE), row), pl.BlockSpec((TM, LANE), row),
                  _vec_spec(TM), _vec_spec(TM), _vec_spec(TM), pl.BlockSpec((1, MIX_W), full),
                  pl.BlockSpec((1, HY_W), full), pl.BlockSpec((S5_W, S5_W), full), pl.BlockSpec((MIX_W, D), full),
                  pl.BlockSpec((1, D), full), pl.BlockSpec((1, D), full)],
        out_specs=[pl.BlockSpec((TM, D), row), pl.BlockSpec((TM, D), row), pl.BlockSpec((TM, HALF_D), row)],
        out_shape=[jax.ShapeDtypeStruct((n_rows, D), F32), jax.ShapeDtypeStruct((n_rows, D), F32),
                   jax.ShapeDtypeStruct((n_rows, HALF_D), jnp.int32)],
        compiler_params=_cp(48 << 20, 1),
        name="merge",
    )(xall, attn, conv, hu, x0c, s5y[0], s5y[1], g1, sh2, sc2, mix_g.reshape(1, MIX_W), hy_d.reshape(1, HY_W), wglu_bf,
      wout_bf, ln_g.reshape(1, D), ln_b.reshape(1, D))


def _router_kernel(u_ref, wt_ref, b_ref, e_ref, g_ref):
    logits = lax.dot_general(wt_ref[...], u_ref[...], (((1,), (1,)), ((), ())), precision=HIGHEST,
                             preferred_element_type=F32)
    scores = jax.nn.sigmoid(logits)
    biased = scores + b_ref[...]
    ninf = -jnp.inf
    grow = lax.broadcasted_iota(jnp.int32, (EGROUP, TM), 0)
    groups = [biased[gi * EGROUP:(gi + 1) * EGROUP] for gi in range(N_EGROUPS)]
    gscore = []
    for vals in groups:
        m1 = jnp.max(vals, axis=0, keepdims=True)
        i1 = jnp.min(jnp.where(vals == m1, grow, EGROUP), axis=0, keepdims=True)
        m2 = jnp.max(jnp.where(grow == i1, ninf, vals), axis=0, keepdims=True)
        gscore.append(m1 + m2)
    kept = []
    for gi in range(N_EGROUPS):
        rank = jnp.zeros((1, TM), jnp.int32)
        for gj in range(N_EGROUPS):
            if gj == gi:
                continue
            ahead = (gscore[gj] > gscore[gi]) | ((gscore[gj] == gscore[gi]) & (gj < gi))
            rank = rank + ahead.astype(jnp.int32)
        kept.append(jnp.where(rank < TOPK_GROUPS, groups[gi], ninf))
    masked = jnp.concatenate(kept, axis=0)
    row = lax.broadcasted_iota(jnp.int32, (N_EXPERTS, TM), 0)
    gates = []
    gsum = jnp.zeros((1, TM), F32)
    for kk in range(TOP_K):
        m = jnp.max(masked, axis=0, keepdims=True)
        idx = jnp.min(jnp.where(masked == m, row, N_EXPERTS), axis=0, keepdims=True)
        hit = row == idx
        gate = jnp.sum(jnp.where(hit, scores, 0.0), axis=0, keepdims=True)
        masked = jnp.where(hit, ninf, masked)
        e_ref[kk:kk + 1, :] = idx
        gates.append(gate)
        gsum = gsum + gate
    for kk in range(TOP_K):
        g_ref[kk:kk + 1, :] = gates[kk] / gsum * ROUTED_SCALE


def _router(n_rows, u2, w_router_t, router_bias):
    nt = n_rows // TM
    col = lambda i: (0, i)
    return pl.pallas_call(
        _router_kernel,
        grid=(nt,),
        in_specs=[pl.BlockSpec((TM, D), lambda i: (i, 0)),
                  pl.BlockSpec((N_EXPERTS, D), lambda i: (0, 0)), pl.BlockSpec((N_EXPERTS, 1), lambda i: (0, 0))],
        out_specs=[pl.BlockSpec((TOP_K, TM), col), pl.BlockSpec((TOP_K, TM), col)],
        out_shape=[jax.ShapeDtypeStruct((TOP_K, n_rows), jnp.int32), jax.ShapeDtypeStruct((TOP_K, n_rows), F32)],
        compiler_params=_cp(32 << 20, 1),
        name="router",
    )(u2, w_router_t, router_bias.reshape(N_EXPERTS, 1))


def _dispatch(top_e):
    t = top_e.shape[1]
    tk = t * TOP_K
    nblk = tk // MOE_BLOCK
    n_steps = nblk + N_EXPERTS
    flat_e = top_e.reshape(tk)
    pos = jnp.arange(tk, dtype=jnp.int32)
    pos_bits = (tk - 1).bit_length()
    assert pos_bits + (N_EXPERTS - 1).bit_length() < 32
    order = lax.sort(flat_e * (1 << pos_bits) + pos) % (1 << pos_bits)
    experts = jnp.arange(N_EXPERTS, dtype=jnp.int32)
    counts = jnp.sum((flat_e[None, :] == experts[:, None]).astype(jnp.int32), axis=1)
    ends = jnp.cumsum(counts)
    starts = ends - counts
    fb = starts // MOE_BLOCK
    npairs = jnp.where(counts > 0, (ends - 1) // MOE_BLOCK - fb + 1, 0)
    pend = jnp.cumsum(npairs)
    poff = pend - npairs
    n_pairs = pend[-1]
    s = jnp.arange(n_steps, dtype=jnp.int32)
    sc = jnp.minimum(s, n_pairs - 1)
    pe = jnp.sum((pend[None, :] <= sc[:, None]).astype(jnp.int32), axis=1)
    pb = fb[pe] + (sc - poff[pe])
    lo = jnp.where(s < n_pairs, jnp.maximum(starts[pe] - pb * MOE_BLOCK, 0), 0)
    hi = jnp.where(s < n_pairs, jnp.minimum(ends[pe] - pb * MOE_BLOCK, MOE_BLOCK), 0)
    last_of_block = ((s == n_pairs - 1) | ((s + 1 < n_pairs) & (jnp.roll(pb, -1) != pb))).astype(jnp.int32)
    used = counts > 0
    ordinal = jnp.cumsum(used.astype(jnp.int32)) - 1
    later = jnp.where(used, experts, N_EXPERTS)
    next_used = lax.cummin(jnp.concatenate([later[1:], jnp.full((1,), N_EXPERTS, jnp.int32)]), reverse=True)
    nxt = next_used[pe]
    nxt = jnp.where(nxt < N_EXPERTS, nxt, pe)
    parity = ordinal[pe] % 2
    w_even = jnp.where(parity == 0, pe, nxt)
    w_odd = jnp.where(parity == 1, pe, nxt)
    return dict(pe=pe, pb=pb, lo=lo, hi=hi, last=last_of_block, parity=parity, w_even=w_even, w_odd=w_odd,
                tok=order % t, order=order)


X_SLOTS = 4


def _expert_kernel(pe, pb, plo, phi, plast, ppar, pwe, pwo, xs_hbm, wg0_ref, wu0_ref, wd0_ref, wg1_ref, wu1_ref,
                   wd1_ref, out_hbm, xbuf, xsem, obuf, osem, wgb, wub, wdb):
    del pwe, pwo
    s = pl.program_id(0)
    prev = jnp.maximum(s - 1, 0)
    n_blocks = out_hbm.shape[0] // MOE_BLOCK
    new_expert = (s == 0) | (pe[s] != pe[prev])

    def x_copy(b, sl):
        rows = pl.ds(pl.multiple_of(b * MOE_BLOCK, MOE_BLOCK), MOE_BLOCK)
        return pltpu.make_async_copy(xs_hbm.at[rows], xbuf.at[sl], xsem.at[sl])

    @pl.when(s == 0)
    def _():
        for b in range(X_SLOTS - 1):
            x_copy(b, b).start()
        obuf[...] = jnp.zeros(obuf.shape, obuf.dtype)

    for par, (wg_ref, wu_ref, wd_ref) in enumerate(((wg0_ref, wu0_ref, wd0_ref), (wg1_ref, wu1_ref, wd1_ref))):
        @pl.when(new_expert & (ppar[s] == par))
        def _():
            wgb[...] = wg_ref[...].astype(BF16)
            wub[...] = wu_ref[...].astype(BF16)
            wdb[...] = wd_ref[...].astype(BF16)

    lo = plo[s]
    hi = phi[s]
    blk = pb[s]
    slot = blk % 2
    first_of_block = (s == 0) | (blk != pb[prev])

    def out_copy(b, sl):
        rows = pl.ds(pl.multiple_of(b * MOE_BLOCK, MOE_BLOCK), MOE_BLOCK)
        return pltpu.make_async_copy(obuf.at[sl], out_hbm.at[rows], osem.at[sl])

    @pl.when(first_of_block)
    def _():
        ahead = blk + (X_SLOTS - 1)

        @pl.when(ahead < n_blocks)
        def _():
            x_copy(ahead, ahead % X_SLOTS).start()
        x_copy(blk, blk % X_SLOTS).wait()

    def ffn():
        x_lo, x_hi = _unpack_pairs(xbuf[blk % X_SLOTS])
        x_lo = x_lo.astype(BF16)
        x_hi = x_hi.astype(BF16)
        hg = (jnp.dot(x_lo, wgb[0:HALF_D], preferred_element_type=F32)
              + jnp.dot(x_hi, wgb[HALF_D:D], preferred_element_type=F32))
        hu = (jnp.dot(x_lo, wub[0:HALF_D], preferred_element_type=F32)
              + jnp.dot(x_hi, wub[HALF_D:D], preferred_element_type=F32))
        h = (hg * jax.nn.sigmoid(hg)) * hu
        return _pack_pairs(jnp.dot(h.astype(BF16), wdb[...], preferred_element_type=F32))

    @pl.when(first_of_block & (blk >= 2))
    def _():
        out_copy(blk - 2, slot).wait()

    @pl.when(hi > lo)
    def _():
        row = lax.broadcasted_iota(jnp.int32, (MOE_BLOCK, HALF_D), 0)
        mine = first_of_block | ((row >= lo) & (row < hi))
        obuf[slot] = jnp.where(mine, ffn(), obuf[slot])

    @pl.when(plast[s] == 1)
    def _():
        out_copy(blk, slot).start()

    @pl.when(s == pl.num_programs(0) - 1)
    def _():
        out_copy(n_blocks - 2, (n_blocks - 2) % 2).wait()
        out_copy(n_blocks - 1, (n_blocks - 1) % 2).wait()


def _experts(xs, disp, wg, wu, wd, layer):
    tk = xs.shape[0]
    n_steps = disp['pe'].shape[0]
    assert tk // MOE_BLOCK >= X_SLOTS
    weven = lambda shape: pl.BlockSpec((None, None) + shape, lambda s, *p: (layer, p[6][s], 0, 0))
    wodd = lambda shape: pl.BlockSpec((None, None) + shape, lambda s, *p: (layer, p[7][s], 0, 0))
    grid_spec = pltpu.PrefetchScalarGridSpec(
        num_scalar_prefetch=8,
        grid=(n_steps,),
        in_specs=[pl.BlockSpec(memory_space=pl.ANY),
                  weven((D, EXPERT_FF)), weven((D, EXPERT_FF)), weven((EXPERT_FF, D)),
                  wodd((D, EXPERT_FF)), wodd((D, EXPERT_FF)), wodd((EXPERT_FF, D))],
        out_specs=pl.BlockSpec(memory_space=pl.ANY),
        scratch_shapes=[pltpu.VMEM((X_SLOTS, MOE_BLOCK, HALF_D), jnp.int32), pltpu.SemaphoreType.DMA((X_SLOTS,)),
                        pltpu.VMEM((2, MOE_BLOCK, HALF_D), jnp.int32), pltpu.SemaphoreType.DMA((2,)),
                        pltpu.VMEM((D, EXPERT_FF), BF16), pltpu.VMEM((D, EXPERT_FF), BF16),
                        pltpu.VMEM((EXPERT_FF, D), BF16)],
    )
    return pl.pallas_call(
        _expert_kernel,
        grid_spec=grid_spec,
        out_shape=jax.ShapeDtypeStruct((tk, HALF_D), jnp.int32),
        compiler_params=_cp(40 << 20, 1),
        name="experts",
    )(disp['pe'], disp['pb'], disp['lo'], disp['hi'], disp['last'], disp['parity'], disp['w_even'], disp['w_odd'],
      xs, wg, wu, wd, wg, wu, wd)


SC_ROWS = 128


def _row_gather(table, idx):
    n = idx.shape[0]
    d = table.shape[1]
    mesh = plsc.VectorSubcoreMesh(core_axis_name="c", subcore_axis_name="s")
    n_workers = mesh.num_cores * mesh.num_subcores
    per_worker = n // n_workers
    assert per_worker * n_workers == n and per_worker % SC_ROWS == 0

    @functools.partial(
        pl.kernel, mesh=mesh,
        out_type=jax.ShapeDtypeStruct((n, d), table.dtype),
        scratch_types=[pltpu.VMEM((SC_ROWS,), jnp.int32), pltpu.VMEM((SC_ROWS, d), table.dtype),
                       pltpu.SemaphoreType.DMA],
    )
    def gather(table_hbm, idx_hbm, out_hbm, idx_v, rows_v, sem):
        worker = lax.axis_index("s") * mesh.num_cores + lax.axis_index("c")
        base = worker * per_worker

        @pl.loop(0, per_worker // SC_ROWS)
        def _(c):
            off = pl.multiple_of(base + c * SC_ROWS, 8)
            pltpu.sync_copy(idx_hbm.at[pl.ds(off, SC_ROWS)], idx_v)
            pltpu.async_copy(table_hbm.at[idx_v], rows_v, sem).wait()
            pltpu.sync_copy(rows_v, out_hbm.at[pl.ds(off, SC_ROWS)])

    return gather(table, idx)


def _row_scatter(rows, idx):
    n, d = rows.shape
    mesh = plsc.VectorSubcoreMesh(core_axis_name="c", subcore_axis_name="s")
    n_workers = mesh.num_cores * mesh.num_subcores
    per_worker = n // n_workers
    assert per_worker * n_workers == n and per_worker % SC_ROWS == 0

    @functools.partial(
        pl.kernel, mesh=mesh,
        out_type=jax.ShapeDtypeStruct((n, d), rows.dtype),
        scratch_types=[pltpu.VMEM((SC_ROWS,), jnp.int32), pltpu.VMEM((SC_ROWS, d), rows.dtype),
                       pltpu.SemaphoreType.DMA],
    )
    def scatter(rows_hbm, idx_hbm, out_hbm, idx_v, rows_v, sem):
        worker = lax.axis_index("s") * mesh.num_cores + lax.axis_index("c")
        base = worker * per_worker

        @pl.loop(0, per_worker // SC_ROWS)
        def _(c):
            off = pl.multiple_of(base + c * SC_ROWS, 8)
            pltpu.sync_copy(idx_hbm.at[pl.ds(off, SC_ROWS)], idx_v)
            pltpu.sync_copy(rows_hbm.at[pl.ds(off, SC_ROWS)], rows_v)
            pltpu.async_copy(rows_v, out_hbm.at[idx_v], sem).wait()

    return scatter(rows, idx)


FM = 128


def _ffn_out_kernel(x_ref, u_ref, r_ref, gate_ref, g2_ref, wsg_ref, wsu_ref, wsd_ref, lng_ref, lnb_ref, o_ref):
    u_lo, u_hi = _unpack_pairs(u_ref[...])
    u_lo = u_lo.astype(BF16)
    u_hi = u_hi.astype(BF16)
    hg = (jnp.dot(u_lo, wsg_ref[0:HALF_D], preferred_element_type=F32)
          + jnp.dot(u_hi, wsg_ref[HALF_D:D], preferred_element_type=F32))
    hu = (jnp.dot(u_lo, wsu_ref[0:HALF_D], preferred_element_type=F32)
          + jnp.dot(u_hi, wsu_ref[HALF_D:D], preferred_element_type=F32))
    f = jnp.dot(((hg * jax.nn.sigmoid(hg)) * hu).astype(BF16), wsd_ref[...], preferred_element_type=F32)
    gates = jnp.transpose(jnp.concatenate([gate_ref[...], jnp.zeros((FM - TOP_K, FM), F32)], axis=0))
    r_lo = jnp.zeros((FM, HALF_D), F32)
    r_hi = jnp.zeros((FM, HALF_D), F32)
    for kk in range(TOP_K):
        k_lo, k_hi = _unpack_pairs(r_ref[kk])
        gk = gates[:, kk:kk + 1]
        r_lo = r_lo + k_lo * gk
        r_hi = r_hi + k_hi * gk
    f = jnp.concatenate([r_lo, r_hi], axis=1) + f
    o_ref[...] = _layer_norm(ALPHA * x_ref[...] + g2_ref[...] * f, lng_ref[...], lnb_ref[...])


def _ffn_out(n_rows, x1, u2p, routed, gate, g2, wsg_bf, wsu_bf, wsd_bf, ln_g, ln_b):
    row = lambda i: (i, 0)
    full = lambda i: (0, 0)
    return pl.pallas_call(
        _ffn_out_kernel,
        grid=(n_rows // FM,),
        in_specs=[pl.BlockSpec((FM, D), row), pl.BlockSpec((FM, HALF_D), row),
                  pl.BlockSpec((TOP_K, FM, HALF_D), lambda i: (0, i, 0)), pl.BlockSpec((TOP_K, FM), lambda i: (0, i)),
                  _vec_spec(FM), pl.BlockSpec((D, EXPERT_FF), full), pl.BlockSpec((D, EXPERT_FF), full),
                  pl.BlockSpec((EXPERT_FF, D), full), pl.BlockSpec((1, D), full), pl.BlockSpec((1, D), full)],
        out_specs=pl.BlockSpec((FM, D), row),
        out_shape=jax.ShapeDtypeStruct((n_rows, D), F32),
        compiler_params=_cp(40 << 20, 1),
        name="ffn_out",
    )(x1, u2p, routed, gate, g2, wsg_bf, wsu_bf, wsd_bf, ln_g.reshape(1, D), ln_b.reshape(1, D))


def _rope_tables():
    t = jnp.arange(L, dtype=jnp.int32)
    row = (t // GRID_W).astype(F32)
    col = (t % GRID_W).astype(F32)
    inv_freq = ROPE_BASE ** (-jnp.arange(0, AXIS_DIM, 2, dtype=F32) / AXIS_DIM)
    half = AXIS_DIM // 2

    def axis(pos):
        ang = pos[:, None] * inv_freq[None, :]
        c = jnp.cos(ang)
        s = jnp.sin(ang)
        return jnp.concatenate([c, c], axis=1), jnp.concatenate([-s, s], axis=1)

    cr, sr = axis(row)
    cc, sc = axis(col)
    cos_h = jnp.concatenate([cr, cc], axis=1)
    sin_h = jnp.concatenate([sr, sc], axis=1)
    cos_l = jnp.tile(cos_h, (B, LANE // HEAD_DIM))
    sin_l = jnp.tile(sin_h, (B, LANE // HEAD_DIM))
    cos_t = jnp.concatenate([cos_l, jnp.ones((T_CTX, LANE), F32)], axis=0)
    sin_t = jnp.concatenate([sin_l, jnp.zeros((T_CTX, LANE), F32)], axis=0)
    del half
    return cos_t, sin_t


def kernel(x, c, ctx, c_ctx, w_ada, b_ada, w_in, w_out, sink, mix_g, hy_short_w, hy_short_b, hy_w1, hy_b1, hy_freq,
           hy_w2, hy_b2, hy_w3, hy_d, s5_a_re, s5_a_im, s5_log_dt, s5_b_re, s5_b_im, s5_c_re, s5_c_im, s5_d, s5_w_glu,
           ln1_g, ln1_b, ln2_g, ln2_b, w_router, router_bias, w_exp_gate, w_exp_up, w_exp_down, w_sh_gate, w_sh_up,
           w_sh_down):
    xall = jnp.concatenate([x.reshape(T_LAT, D), ctx.reshape(T_CTX, D)], axis=0)
    cvec = jnp.concatenate([c, c_ctx[None, :], jnp.zeros((8 - B - 1, D), F32)], axis=0)
    mod = _ada(cvec, w_ada, b_ada)[:, 0:B + 1, :].reshape(DEPTH, B + 1, 6, 1, D)

    cos_t, sin_t = _rope_tables()
    g_cos, g_sin, f_cos, f_sin = _dft_tables()
    g_cos_bf, g_sin_bf = g_cos.astype(BF16), g_sin.astype(BF16)
    i_cos_bf = jnp.swapaxes(g_cos, 1, 2).astype(BF16)
    i_sin_bf = jnp.swapaxes(g_sin, 1, 2).astype(BF16)
    f_cos_bf, f_sin_bf = f_cos.astype(BF16), f_sin.astype(BF16)
    k1 = jnp.arange(K1P)
    spec_w = jnp.where((k1 == 0) | (k1 == FFT_R // 2), 1.0, 2.0) * (k1 < K1_USED) / N_FFT
    spec_w = spec_w.astype(F32)
    kt = jnp.arange(2 * C, dtype=jnp.int32)
    ang_c = ((kt[:, None] * kt[None, :]) % (2 * C)).astype(F32) * (2.0 * math.pi / (2 * C))
    d_cos, d_sin = jnp.cos(ang_c), jnp.sin(ang_c)

    filt_params = (hy_w1, hy_b1, hy_freq, hy_w2, hy_b2, hy_w3)
    filt_lat = jax.vmap(functools.partial(_hyena_filter, L))(*filt_params)
    filt_ctx = jax.vmap(functools.partial(_hyena_filter, C))(*(p[0:DEPTH - 1] for p in filt_params))
    k_r, k_i = jax.vmap(_hyena_spectrum, in_axes=(0, None, None, None, None, None))(
        filt_lat, g_cos, g_sin, f_cos, f_sin, spec_w)
    ws, e_all, wc, lam_p = jax.vmap(_s5_matrices)(s5_a_re, s5_a_im, s5_log_dt, s5_b_re, s5_b_im, s5_c_re, s5_c_im, s5_d)

    for l in range(DEPTH):
        last = l == DEPTH - 1
        n_rows = T_LAT if last else T_ALL
        sh1, sc1, g1, sh2, sc2, g2 = (mod[l, :, j] for j in range(6))

        q, k, v, hz, s5u = _inproj(xall, sh1, sc1, w_in[l].astype(BF16), cos_t, sin_t)
        attn = _attention(sink[l], q, k, v, with_ctx=not last)

        hu, x0c = _hyena_pre(hz, hy_short_w[l], hy_short_b[l])
        conv = _hyena_fft(hu, g_cos_bf, g_sin_bf, i_cos_bf, i_sin_bf, f_cos_bf, f_sin_bf, k_r, k_i, l)
        if not last:
            conv = jnp.concatenate([conv, _hyena_ctx(hu, filt_ctx[l], d_cos, d_sin)], axis=0)

        s_rows, y_in = _s5_in(s5u, ws, e_all, l)
        s5y = _s5_out(_s5_scan(s_rows, lam_p[l]), wc, y_in, l)

        x1, u2, u2p = _merge(n_rows, xall, attn, conv, hu, x0c, s5y, g1, sh2, sc2, mix_g[l], hy_d[l],
                             s5_w_glu[l].astype(BF16), w_out[l].astype(BF16), ln1_g[l], ln1_b[l])

        top_e, gate = _router(n_rows, u2, w_router[l].T, router_bias[l])
        disp = _dispatch(top_e)
        ys = _experts(_row_gather(u2p, disp['tok']), disp, w_exp_gate, w_exp_up, w_exp_down, l)
        routed = _row_scatter(ys, disp['order']).reshape(TOP_K, n_rows, HALF_D)
        xall = _ffn_out(n_rows, x1, u2p, routed, gate, g2, w_sh_gate[l].astype(BF16), w_sh_up[l].astype(BF16),
                        w_sh_down[l].astype(BF16), ln2_g[l], ln2_b[l])
    return xall.reshape(B, L, D)
```

```python
import functools
import math

import jax
import jax.numpy as jnp
from jax import lax
from jax.experimental import pallas as pl
from jax.experimental.pallas import tpu as pltpu
from jax.experimental.pallas import tpu_sc as plsc

F32 = jnp.float32
BF16 = jnp.bfloat16
HIGHEST = lax.Precision.HIGHEST

D = 1024
B = 2
L = 8192
DEPTH = 2
GRID_W = 64
C = 256
T_LAT = B * L
T_CTX = B * C
T_ALL = T_LAT + T_CTX

HEAD_DIM = 64
N_Q = 8
N_KV = 2
Q_GROUP = N_Q // N_KV
ATTN_W = N_Q * HEAD_DIM
KV_W = N_KV * HEAD_DIM
HY_W = 256
S5_W = 256
MIX_W = ATTN_W + HY_W + S5_W
K_OFF = ATTN_W
V_OFF = K_OFF + KV_W
HY_OFF = V_OFF + KV_W
S5_OFF = HY_OFF + 3 * HY_W
IN_W = S5_OFF + S5_W
WINDOW = 128
BLK = 128
NEG_INF = -1e30
ROPE_BASE = 10000.0
AXIS_DIM = HEAD_DIM // 2

SHORT_K = 3
FILTER_EMB = 33
DECAY_FAST = 0.3
DECAY_SLOW = 1.5
DECAY_TARGET = 1e-2

S5_GROUP = 16
S5_GROUPS = S5_W // S5_GROUP
S5_STATE = 64
S5_NSTATE = S5_GROUPS * S5_STATE
S5_CH = 16
S5_ROWW = S5_CH * S5_W
N_CHUNK = T_ALL // S5_CH
LAT_CHUNKS = L // S5_CH
CTX_CHUNKS = C // S5_CH

N_EXPERTS = 256
TOP_K = 8
N_EGROUPS = 8
EGROUP = N_EXPERTS // N_EGROUPS
TOPK_GROUPS = 4
EXPERT_FF = 256
ROUTED_SCALE = 2.5
MOE_BLOCK = 256

ALPHA = (2 * DEPTH) ** 0.25
LN_EPS = 1e-5

N_FFT = 2 * L
FFT_R = 128
FFT_T1 = L // FFT_R
K1_USED = FFT_R // 2 + 1
K1P = 80
K1H = K1P // 2
FFT_UNROLL = 4

TM = 256
LANE = 128
VMEM_CAP = 60000 * 1024


def _cp(vmem_bytes, n_axes):
    return pltpu.CompilerParams(
        dimension_semantics=("arbitrary",) * n_axes if n_axes else None,
        vmem_limit_bytes=min(int(vmem_bytes), VMEM_CAP),
    )


HALF_D = D // 2
HIGH_HALF_WORD = 0xFFFF0000


def _pack_pairs(x):
    bits = lax.bitcast_convert_type(x.astype(BF16).astype(F32), jnp.uint32)
    packed = (bits[:, 0:HALF_D] >> 16) | (bits[:, HALF_D:D] & jnp.uint32(HIGH_HALF_WORD))
    return lax.bitcast_convert_type(packed, jnp.int32)


def _unpack_pairs(p):
    bits = lax.bitcast_convert_type(p, jnp.uint32)
    low = lax.bitcast_convert_type(bits << 16, F32)
    high = lax.bitcast_convert_type(bits & jnp.uint32(HIGH_HALF_WORD), F32)
    return low, high


def _mod_sel(rows_per_tile):
    per_batch = L // rows_per_tile
    return lambda i: jnp.minimum(i // per_batch, 2)


def _vec_spec(rows_per_tile):
    sel = _mod_sel(rows_per_tile)
    return pl.BlockSpec((None, 1, D), lambda i: (sel(i), 0, 0))


ADA_TN = 1536


def _ada_kernel(c_ref, w_ref, b_ref, o_ref):
    c = c_ref[...]
    s = c * jax.nn.sigmoid(c)
    o_ref[...] = jnp.dot(s, w_ref[...], precision=HIGHEST, preferred_element_type=F32) + b_ref[...]


def _ada(cvec, w_ada, b_ada):
    return pl.pallas_call(
        _ada_kernel,
        grid=(DEPTH, 6 * D // ADA_TN),
        in_specs=[
            pl.BlockSpec((8, D), lambda l, j: (0, 0)),
            pl.BlockSpec((None, D, ADA_TN), lambda l, j: (l, 0, j)),
            pl.BlockSpec((None, 1, ADA_TN), lambda l, j: (l, 0, j)),
        ],
        out_specs=pl.BlockSpec((None, 8, ADA_TN), lambda l, j: (l, 0, j)),
        out_shape=jax.ShapeDtypeStruct((DEPTH, 8, 6 * D), F32),
        compiler_params=_cp(40 << 20, 2),
        name="ada",
    )(cvec, w_ada, b_ada.reshape(DEPTH, 1, 6 * D))


def _inproj_kernel(x_ref, sh_ref, sc_ref, w_ref, cos_ref, sin_ref, q_ref, k_ref, v_ref, hy_ref, s5_ref, *s5_tok_refs):
    u = x_ref[...] * (1.0 + sc_ref[...]) + sh_ref[...]
    proj = jnp.dot(u.astype(BF16), w_ref[...], preferred_element_type=F32)
    cos = cos_ref[...]
    sin = sin_ref[...]
    lane = lax.broadcasted_iota(jnp.int32, (TM, LANE), 1)
    first_half = (lane % AXIS_DIM) < (AXIS_DIM // 2)

    def rope(xc):
        partner = jnp.where(first_half, pltpu.roll(xc, LANE - AXIS_DIM // 2, 1), pltpu.roll(xc, AXIS_DIM // 2, 1))
        return xc * cos + partner * sin

    for j in range(ATTN_W // LANE):
        q_ref[:, j * LANE:(j + 1) * LANE] = rope(proj[:, j * LANE:(j + 1) * LANE]).astype(BF16)
    k_ref[...] = rope(proj[:, K_OFF:V_OFF]).astype(BF16)
    v_ref[...] = proj[:, V_OFF:HY_OFF].astype(BF16)
    hy_ref[...] = proj[:, HY_OFF:S5_OFF]
    for h, tok_ref in enumerate(s5_tok_refs):
        tok_ref[...] = proj[:, S5_OFF + h * LANE:S5_OFF + (h + 1) * LANE]
    for s in range(S5_CH):
        for h, tok_ref in enumerate(s5_tok_refs):
            s5_ref[:, s * S5_W + h * LANE:s * S5_W + (h + 1) * LANE] = (
                tok_ref[pl.ds(s, TM // S5_CH, stride=S5_CH), :].astype(BF16))


def _inproj(xall, sh, sc, w_in_bf, cos_t, sin_t):
    nt = T_ALL // TM
    row = lambda i: (i, 0)
    return pl.pallas_call(
        _inproj_kernel,
        grid=(nt,),
        in_specs=[
            pl.BlockSpec((TM, D), row),
            _vec_spec(TM),
            _vec_spec(TM),
            pl.BlockSpec((D, IN_W), lambda i: (0, 0)),
            pl.BlockSpec((TM, LANE), row),
            pl.BlockSpec((TM, LANE), row),
        ],
        out_specs=[
            pl.BlockSpec((TM, ATTN_W), row),
            pl.BlockSpec((TM, KV_W), row),
            pl.BlockSpec((TM, KV_W), row),
            pl.BlockSpec((TM, 3 * HY_W), row),
            pl.BlockSpec((TM // S5_CH, S5_ROWW), row),
        ],
        out_shape=[
            jax.ShapeDtypeStruct((T_ALL, ATTN_W), BF16),
            jax.ShapeDtypeStruct((T_ALL, KV_W), BF16),
            jax.ShapeDtypeStruct((T_ALL, KV_W), BF16),
            jax.ShapeDtypeStruct((T_ALL, 3 * HY_W), F32),
            jax.ShapeDtypeStruct((N_CHUNK, S5_ROWW), BF16),
        ],
        scratch_shapes=[pltpu.VMEM((TM, LANE), F32)] * (S5_W // LANE),
        compiler_params=_cp(40 << 20, 1),
        name="inproj",
    )(xall, sh, sc, w_in_bf, cos_t, sin_t)


NB_LAT = L // BLK
NB_CTX = C // BLK


def _nt_dot(a, b):
    return lax.dot_general(a, b, (((1,), (1,)), ((), ())), preferred_element_type=F32)


def _attn_kernel(sink_ref, q_ref, kp_ref, kc_ref, kn_ref, kx_ref, vp_ref, vc_ref, vn_ref, vx_ref, o_ref):
    n = pl.program_id(1)
    is_lat = n < NB_LAT
    rows = Q_GROUP * BLK
    r = lax.broadcasted_iota(jnp.int32, (rows, BLK), 0) % BLK
    j = lax.broadcasted_iota(jnp.int32, (rows, BLK), 1)
    ok_prev = (j >= r) & (n >= 1) & is_lat
    ok_next = (j <= r) & (n + 1 < NB_LAT) & is_lat
    head_of_row = lax.broadcasted_iota(jnp.int32, (rows, 1), 0) // BLK
    q = q_ref[...] * (HEAD_DIM ** -0.5)
    dot = lambda a, b: jnp.dot(a.astype(BF16), b, preferred_element_type=F32)
    for kh in range(N_KV):
        hs = slice(kh * HEAD_DIM, (kh + 1) * HEAD_DIM)
        heads = range(kh * Q_GROUP, (kh + 1) * Q_GROUP)
        qg = jnp.concatenate([q[:, h * HEAD_DIM:(h + 1) * HEAD_DIM] for h in heads], axis=0)
        sk = jnp.zeros((rows, 1), F32)
        for g, h in enumerate(heads):
            sk = jnp.where(head_of_row == g, sink_ref[h], sk)
        s_p = jnp.where(ok_prev, _nt_dot(qg, kp_ref[:, hs]), NEG_INF)
        s_c = jnp.where(is_lat, _nt_dot(qg, kc_ref[:, hs]), NEG_INF)
        s_n = jnp.where(ok_next, _nt_dot(qg, kn_ref[:, hs]), NEG_INF)
        s_x = _nt_dot(qg, kx_ref[:, hs])
        s_x0 = s_x[:, 0:BLK]
        s_x1 = s_x[:, BLK:2 * BLK]
        m = jnp.maximum(jnp.maximum(jnp.maximum(s_p, s_c), jnp.maximum(s_n, s_x0)), s_x1)
        m = jnp.maximum(jnp.max(m, axis=1, keepdims=True), sk)
        e_p = jnp.exp(s_p - m)
        e_c = jnp.exp(s_c - m)
        e_n = jnp.exp(s_n - m)
        e_x0 = jnp.exp(s_x0 - m)
        e_x1 = jnp.exp(s_x1 - m)
        den = jnp.sum((e_p + e_c) + (e_n + e_x0) + e_x1, axis=1, keepdims=True) + jnp.exp(sk - m)
        o = (dot(e_p, vp_ref[:, hs]) + dot(e_c, vc_ref[:, hs]) + dot(e_n, vn_ref[:, hs])
             + dot(e_x0, vx_ref[0:BLK, hs]) + dot(e_x1, vx_ref[BLK:2 * BLK, hs]))
        o = o / den
        for g, h in enumerate(heads):
            o_ref[:, h * HEAD_DIM:(h + 1) * HEAD_DIM] = o[g * BLK:(g + 1) * BLK]


def _attention(sink, q, k, v, with_ctx):
    nblk = NB_LAT + (NB_CTX if with_ctx else 0)

    def q_idx(b, n):
        return (jnp.where(n < NB_LAT, b * NB_LAT + n, B * NB_LAT + b * NB_CTX + (n - NB_LAT)), 0)

    def kv_idx(off):
        def idx(b, n):
            nn = jnp.clip(jnp.minimum(n, NB_LAT - 1) + off, 0, NB_LAT - 1)
            return (b * NB_LAT + nn, 0)
        return idx

    ctx_idx = lambda b, n: (T_LAT // C + b, 0)
    kv_specs = lambda: [pl.BlockSpec((BLK, KV_W), kv_idx(-1)), pl.BlockSpec((BLK, KV_W), kv_idx(0)),
                        pl.BlockSpec((BLK, KV_W), kv_idx(1)), pl.BlockSpec((C, KV_W), ctx_idx)]
    return pl.pallas_call(
        _attn_kernel,
        grid=(B, nblk),
        in_specs=[pl.BlockSpec(memory_space=pltpu.SMEM), pl.BlockSpec((BLK, ATTN_W), q_idx)] + kv_specs() + kv_specs(),
        out_specs=pl.BlockSpec((BLK, ATTN_W), q_idx),
        out_shape=jax.ShapeDtypeStruct((T_ALL if with_ctx else T_LAT, ATTN_W), F32),
        compiler_params=_cp(32 << 20, 2),
        name="attention",
    )(sink, q, k, k, k, k, v, v, v, v)


def _hyena_pre_kernel(z_ref, zp_ref, zn_ref, w_ref, b_ref, u_ref, x0_ref):
    i = pl.program_id(0)
    tiles_per_seq = L // TM
    is_ctx = i >= B * tiles_per_seq
    first = is_ctx | (i % tiles_per_seq == 0)
    last = is_ctx | (i % tiles_per_seq == tiles_per_seq - 1)
    z = z_ref[...]
    prev_row = jnp.where(first, 0.0, zp_ref[7:8, :])
    next_row = jnp.where(last, 0.0, zn_ref[0:1, :])
    row = lax.broadcasted_iota(jnp.int32, z.shape, 0)
    z_m1 = jnp.where(row == 0, prev_row, pltpu.roll(z, 1, 0))
    z_p1 = jnp.where(row == TM - 1, next_row, pltpu.roll(z, TM - 1, 0))
    zc = b_ref[...] + z_m1 * w_ref[0:1, :] + z * w_ref[1:2, :] + z_p1 * w_ref[2:3, :]
    u_ref[...] = zc[:, 0:HY_W] * zc[:, HY_W:2 * HY_W]
    x0_ref[...] = zc[:, 2 * HY_W:3 * HY_W]


def _hyena_pre(z, short_w, short_b):
    nt = T_ALL // TM
    sub = TM // 8
    n8 = T_ALL // 8
    return pl.pallas_call(
        _hyena_pre_kernel,
        grid=(nt,),
        in_specs=[
            pl.BlockSpec((TM, 3 * HY_W), lambda i: (i, 0)),
            pl.BlockSpec((8, 3 * HY_W), lambda i: (jnp.maximum(i * sub - 1, 0), 0)),
            pl.BlockSpec((8, 3 * HY_W), lambda i: (jnp.minimum((i + 1) * sub, n8 - 1), 0)),
            pl.BlockSpec((SHORT_K, 3 * HY_W), lambda i: (0, 0)),
            pl.BlockSpec((1, 3 * HY_W), lambda i: (0, 0)),
        ],
        out_specs=[pl.BlockSpec((TM, HY_W), lambda i: (i, 0)), pl.BlockSpec((TM, HY_W), lambda i: (i, 0))],
        out_shape=[jax.ShapeDtypeStruct((T_ALL, HY_W), F32), jax.ShapeDtypeStruct((T_ALL, HY_W), F32)],
        compiler_params=_cp(32 << 20, 1),
        name="hyena_pre",
    )(z, z, z, short_w, short_b.reshape(1, 3 * HY_W))


def _dft_tables():
    t0 = jnp.arange(FFT_R, dtype=jnp.int32)[:, None, None]
    k1 = jnp.arange(K1P, dtype=jnp.int32)[None, :, None]
    t1 = jnp.arange(FFT_T1, dtype=jnp.int32)[None, None, :]
    m = (k1 * (FFT_R * t1 + t0)) % N_FFT
    ang = m.astype(F32) * (2.0 * math.pi / N_FFT)
    used = (k1 < K1_USED).astype(F32)
    g_cos = jnp.cos(ang) * used
    g_sin = jnp.sin(ang) * used
    a = jnp.arange(FFT_R, dtype=jnp.int32)
    ang2 = ((a[:, None] * a[None, :]) % FFT_R).astype(F32) * (2.0 * math.pi / FFT_R)
    return g_cos, g_sin, jnp.cos(ang2), jnp.sin(ang2)


def _hyena_spec_kernel(k_ref, gc_ref, gs_ref, fc_ref, fs_ref, wt_ref, kr_ref, ki_ref, ar_ref, ai_ref):
    half = pl.program_id(1)
    kk = lax.broadcasted_iota(jnp.int32, (K1H, 1), 0) + half * K1H
    sign = jnp.where(kk % 2 == 0, 1.0, -1.0).astype(F32)

    dot = lambda a, b: jnp.dot(a, b.astype(BF16), preferred_element_type=F32)

    def stage1(i, carry):
        t0s = [i * FFT_UNROLL + u for u in range(FFT_UNROLL)]
        loaded = [(k_ref[pl.ds(t0, FFT_T1, stride=FFT_R), :], k_ref[pl.ds(L + t0, FFT_T1, stride=FFT_R), :],
                   gc_ref[t0].astype(BF16), gs_ref[t0].astype(BF16)) for t0 in t0s]
        res = [(dot(gc, x_lo) + sign * dot(gc, x_hi), -(dot(gs, x_lo) + sign * dot(gs, x_hi)))
               for x_lo, x_hi, gc, gs in loaded]
        for t0, (a_r, a_i) in zip(t0s, res):
            rows = pl.ds(pl.multiple_of(t0 * K1H, 8), K1H)
            ar_ref[rows, :] = a_r
            ai_ref[rows, :] = a_i
        return carry

    lax.fori_loop(0, FFT_R // FFT_UNROLL, stage1, 0)
    fc = fc_ref[...].astype(BF16)
    fs = fs_ref[...].astype(BF16)

    def stage2(i, carry):
        kls = [i * FFT_UNROLL + u for u in range(FFT_UNROLL)]
        loaded = [(ar_ref[pl.ds(kl, FFT_R, stride=K1H), :], ai_ref[pl.ds(kl, FFT_R, stride=K1H), :]) for kl in kls]
        for kl, (a_r, a_i) in zip(kls, loaded):
            w = wt_ref[half * K1H + kl]
            kr_ref[kl] = (dot(fc, a_r) + dot(fs, a_i)) * w
            ki_ref[kl] = (dot(fc, a_i) - dot(fs, a_r)) * w
        return carry

    lax.fori_loop(0, K1H // FFT_UNROLL, stage2, 0)


def _hyena_spectrum(kfilt, g_cos, g_sin, f_cos, f_sin, wts):
    nct = HY_W // LANE
    gspec = pl.BlockSpec((FFT_R, K1H, FFT_T1), lambda c, h: (0, h, 0))
    fspec = pl.BlockSpec((FFT_R, FFT_R), lambda c, h: (0, 0))
    ospec = pl.BlockSpec((K1H, FFT_R, LANE), lambda c, h: (h, 0, c))
    return pl.pallas_call(
        _hyena_spec_kernel,
        grid=(nct, 2),
        in_specs=[pl.BlockSpec((N_FFT, LANE), lambda c, h: (0, c)), gspec, gspec, fspec, fspec,
                  pl.BlockSpec(memory_space=pltpu.SMEM)],
        out_specs=[ospec, ospec],
        out_shape=[jax.ShapeDtypeStruct((K1P, FFT_R, HY_W), F32)] * 2,
        scratch_shapes=[pltpu.VMEM((FFT_R * K1H, LANE), F32)] * 2,
        compiler_params=_cp(56 << 20, 0),
        name="hyena_spectrum",
    )(kfilt, g_cos, g_sin, f_cos, f_sin, wts)


def _hyena_fft_kernel(u_ref, gc_ref, gs_ref, ic_ref, is_ref, fc_ref, fs_ref, kr_ref, ki_ref, o_ref, ar_ref, ai_ref):
    bdot = lambda a, b: jnp.dot(a, b.astype(BF16), preferred_element_type=F32)

    def stage1(i, carry):
        t0s = [i * FFT_UNROLL + u for u in range(FFT_UNROLL)]
        xs = [u_ref[pl.ds(t0, FFT_T1, stride=FFT_R), :] for t0 in t0s]
        res = [(bdot(gc_ref[t0], x), -bdot(gs_ref[t0], x)) for t0, x in zip(t0s, xs)]
        for t0, (a_r, a_i) in zip(t0s, res):
            rows = pl.ds(pl.multiple_of(t0 * K1P, 8), K1P)
            ar_ref[rows, :] = a_r
            ai_ref[rows, :] = a_i
        return carry

    lax.fori_loop(0, FFT_R // FFT_UNROLL, stage1, 0)
    fc = fc_ref[...]
    fs = fs_ref[...]

    def stage23(i, carry):
        k1s = [i * FFT_UNROLL + u for u in range(FFT_UNROLL)]
        loaded = [(ar_ref[pl.ds(k1, FFT_R, stride=K1P), :], ai_ref[pl.ds(k1, FFT_R, stride=K1P), :],
                   kr_ref[k1], ki_ref[k1]) for k1 in k1s]
        res = []
        for a_r, a_i, k_r, k_i in loaded:
            z_r = bdot(fc, a_r) + bdot(fs, a_i)
            z_i = bdot(fc, a_i) - bdot(fs, a_r)
            y_r = z_r * k_r - z_i * k_i
            y_i = z_r * k_i + z_i * k_r
            res.append((bdot(fc, y_r) - bdot(fs, y_i), bdot(fc, y_i) + bdot(fs, y_r)))
        for k1, (b_r, b_i) in zip(k1s, res):
            ar_ref[pl.ds(k1, FFT_R, stride=K1P), :] = b_r
            ai_ref[pl.ds(k1, FFT_R, stride=K1P), :] = b_i
        return carry

    lax.fori_loop(0, -(-K1_USED // FFT_UNROLL), stage23, 0)

    def stage4(i, carry):
        t0s = [i * FFT_UNROLL + u for u in range(FFT_UNROLL)]
        loaded = []
        for t0 in t0s:
            rows = pl.ds(pl.multiple_of(t0 * K1P, 8), K1P)
            loaded.append((ar_ref[rows, :], ai_ref[rows, :]))
        res = [bdot(ic_ref[t0], b_r) - bdot(is_ref[t0], b_i) for t0, (b_r, b_i) in zip(t0s, loaded)]
        for t0, y in zip(t0s, res):
            o_ref[pl.ds(t0, FFT_T1, stride=FFT_R), :] = y
        return carry

    lax.fori_loop(0, FFT_R // FFT_UNROLL, stage4, 0)


def _hyena_fft(u, g_cos_bf, g_sin_bf, i_cos_bf, i_sin_bf, f_cos_bf, f_sin_bf, k_r, k_i, layer):
    nct = HY_W // LANE
    one = pl.Buffered(1)
    gspec = pl.BlockSpec((FFT_R, K1P, FFT_T1), lambda c, b: (0, 0, 0), pipeline_mode=one)
    ispec = pl.BlockSpec((FFT_R, FFT_T1, K1P), lambda c, b: (0, 0, 0), pipeline_mode=one)
    fspec = pl.BlockSpec((FFT_R, FFT_R), lambda c, b: (0, 0), pipeline_mode=one)
    kspec = pl.BlockSpec((None, K1P, FFT_R, LANE), lambda c, b: (layer, 0, 0, c), pipeline_mode=one)
    return pl.pallas_call(
        _hyena_fft_kernel,
        grid=(nct, B),
        in_specs=[pl.BlockSpec((L, LANE), lambda c, b: (b, c)), gspec, gspec, ispec, ispec, fspec, fspec, kspec, kspec],
        out_specs=pl.BlockSpec((L, LANE), lambda c, b: (b, c)),
        out_shape=jax.ShapeDtypeStruct((T_LAT, HY_W), F32),
        scratch_shapes=[pltpu.VMEM((FFT_R * K1P, LANE), F32)] * 2,
        compiler_params=_cp(56 << 20, 2),
        name="hyena_fft",
    )(u, g_cos_bf, g_sin_bf, i_cos_bf, i_sin_bf, f_cos_bf, f_sin_bf, k_r, k_i)


def _hyena_ctx_kernel(u_ref, k_ref, dc_ref, ds_ref, o_ref):
    dot = lambda a, b: jnp.dot(a, b, precision=HIGHEST, preferred_element_type=F32)
    dc = dc_ref[...]
    ds = ds_ref[...]
    u = u_ref[...]
    kf = k_ref[...]
    u_r = dot(dc[:, 0:C], u)
    u_i = -dot(ds[:, 0:C], u)
    k_r = dot(dc, kf)
    k_i = -dot(ds, kf)
    y_r = u_r * k_r - u_i * k_i
    y_i = u_r * k_i + u_i * k_r
    o_ref[...] = (dot(dc[0:C, :], y_r) - dot(ds[0:C, :], y_i)) * (1.0 / (2 * C))


def _hyena_ctx(u, kfilt_ctx, d_cos, d_sin):
    full = lambda b: (0, 0)
    return pl.pallas_call(
        _hyena_ctx_kernel,
        grid=(B,),
        in_specs=[pl.BlockSpec((C, HY_W), lambda b: (T_LAT // C + b, 0)),
                  pl.BlockSpec((2 * C, HY_W), full), pl.BlockSpec((2 * C, 2 * C), full), pl.BlockSpec((2 * C, 2 * C), full)],
        out_specs=pl.BlockSpec((C, HY_W), lambda b: (b, 0)),
        out_shape=jax.ShapeDtypeStruct((T_CTX, HY_W), F32),
        compiler_params=_cp(32 << 20, 1),
        name="hyena_ctx",
    )(u, kfilt_ctx, d_cos, d_sin)


def _hyena_filter(n, w1, b1, freq, w2, b2, w3):
    t = jnp.linspace(0.0, 1.0, n, dtype=F32)[:, None]
    bands = (FILTER_EMB - 1) // 2
    w = 2.0 * math.pi * jnp.arange(n, dtype=F32)[:, None] / n
    f = jnp.linspace(1e-4, bands - 1, bands, dtype=F32)[None, :]
    z = jnp.concatenate([t, jnp.cos(f * w), -jnp.sin(f * w)], axis=-1)
    mm = functools.partial(jnp.matmul, precision=HIGHEST)
    h = jnp.sin(freq * (mm(z, w1) + b1))
    h = jnp.sin(freq * (mm(h, w2) + b2))
    deltas = jnp.abs(jnp.linspace(math.log(DECAY_TARGET) / DECAY_FAST, math.log(DECAY_TARGET) / DECAY_SLOW,
                                  HY_W, dtype=F32))
    decay = jnp.exp(-t * deltas[None, :])
    h_fwd = mm(h, w3[:, :HY_W]) * decay
    h_bwd_rev = mm(h[::-1], w3[:, HY_W:]) * decay[::-1]
    k = jnp.concatenate([h_fwd, jnp.zeros((1, HY_W), F32), h_bwd_rev[:-1]], axis=0)
    return k / jnp.sum(jnp.abs(k), axis=0, keepdims=True)


def _s5_matrices(a_re, a_im, log_dt, b_re, b_im, c_re, c_im, d_skip):
    dt = jnp.exp(log_dt)[:, :, None]
    lam_re = jnp.minimum(a_re, -1e-4)
    mag1 = jnp.exp(lam_re * dt)
    lbr = mag1 * jnp.cos(a_im * dt)
    lbi = mag1 * jnp.sin(a_im * dt)
    den = lam_re * lam_re + a_im * a_im
    fr = ((lbr - 1.0) * lam_re + lbi * a_im) / den
    fi = (lbi * lam_re - (lbr - 1.0) * a_im) / den
    bbr = fr[..., None] * b_re - fi[..., None] * b_im
    bbi = fr[..., None] * b_im + fi[..., None] * b_re
    j = jnp.arange(S5_CH + 1, dtype=F32)[:, None, None, None]
    magj = jnp.exp(j * (lam_re * dt)[None])
    pr = magj * jnp.cos(j * (a_im * dt)[None])
    pi = magj * jnp.sin(j * (a_im * dt)[None])
    hi = functools.partial(jnp.einsum, precision=HIGHEST)
    lbr_j = pr[..., None] * bbr[None] - pi[..., None] * bbi[None]
    lbi_j = pr[..., None] * bbi[None] + pi[..., None] * bbr[None]
    m = hi('dgop,jdgpi->jdgoi', c_re, lbr_j) - hi('dgop,jdgpi->jdgoi', c_im, lbi_j)
    eye_g = jnp.eye(S5_GROUPS, dtype=F32)
    s = jnp.arange(S5_CH)
    blocks = jnp.einsum('jdgoi,gh->djgiho', m[0:S5_CH], eye_g).reshape(2, S5_CH, S5_W, S5_W)
    lag0 = blocks[0, 0] + blocks[1, 0] + jnp.diag(d_skip)
    e_all = jnp.concatenate([blocks[1, S5_CH - 1:0:-1], lag0[None], blocks[0, 1:S5_CH]], axis=0).astype(BF16)
    sf_r = lbr_j[S5_CH - 1 - s, 0]
    sf_i = lbi_j[S5_CH - 1 - s, 0]
    sb_r = lbr_j[s, 1]
    sb_i = lbi_j[s, 1]
    st = jnp.stack([sf_r, sf_i, sb_r, sb_i], axis=0)
    rr = jnp.arange(S5_ROWW, dtype=jnp.int32)
    cc = jnp.arange(S5_W, dtype=jnp.int32)
    ws = _s5_expand(jnp.transpose(st, (1, 2, 4, 0, 3)).reshape(S5_ROWW, S5_W),
                    (cc[:, None] // S5_STATE == rr[None, :] // S5_NSTATE)
                    & (cc[:, None] % S5_STATE == rr[None, :] % S5_STATE),
                    (rr // S5_GROUP) % S5_GROUPS, (rr % S5_NSTATE) // S5_STATE)
    tt = jnp.arange(S5_CH)
    cf_r = c_re[0][None] * pr[tt + 1, 0][:, :, None, :] - c_im[0][None] * pi[tt + 1, 0][:, :, None, :]
    cf_i = c_re[0][None] * pi[tt + 1, 0][:, :, None, :] + c_im[0][None] * pr[tt + 1, 0][:, :, None, :]
    cb_r = c_re[1][None] * pr[S5_CH - tt, 1][:, :, None, :] - c_im[1][None] * pi[S5_CH - tt, 1][:, :, None, :]
    cb_i = c_re[1][None] * pi[S5_CH - tt, 1][:, :, None, :] + c_im[1][None] * pr[S5_CH - tt, 1][:, :, None, :]
    ct = jnp.stack([cf_r, -cf_i, cb_r, -cb_i], axis=0)
    wc = _s5_expand(jnp.transpose(ct, (0, 2, 4, 1, 3)).reshape(4 * S5_NSTATE, S5_W),
                    (cc[:, None] // S5_GROUP == rr[None, :] // S5_W)
                    & (cc[:, None] % S5_GROUP == rr[None, :] % S5_GROUP),
                    (rr % S5_NSTATE) // S5_STATE, (rr % S5_W) // S5_GROUP)
    lam_p = jnp.stack([jnp.stack([pr[S5_CH, 0], pi[S5_CH, 0]]), jnp.stack([pr[S5_CH, 1], pi[S5_CH, 1]])])
    return ws, e_all, wc, lam_p.reshape(2, 2, 1, S5_NSTATE)


S5_TN = 512
S5_NS = S5_ROWW // S5_W


def _s5_expand_kernel(a_ref, ex_ref, rg_ref, cg_ref, o_ref):
    v = jnp.dot(a_ref[...], ex_ref[...], preferred_element_type=F32)
    o_ref[...] = jnp.where(rg_ref[...] == cg_ref[...], v, 0.0).astype(BF16)


def _s5_expand(compact, placement, row_group, col_group):
    n = compact.shape[0]
    return pl.pallas_call(
        _s5_expand_kernel,
        grid=(S5_ROWW // S5_TN,),
        in_specs=[pl.BlockSpec((n, S5_W), lambda j: (0, 0)), pl.BlockSpec((S5_W, S5_TN), lambda j: (0, j)),
                  pl.BlockSpec((n, 1), lambda j: (0, 0)), pl.BlockSpec((1, S5_TN), lambda j: (0, j))],
        out_specs=pl.BlockSpec((n, S5_TN), lambda j: (0, j)),
        out_shape=jax.ShapeDtypeStruct((n, S5_ROWW), BF16),
        compiler_params=_cp(32 << 20, 0),
        name="s5_expand",
    )(compact.astype(BF16), placement.astype(BF16), row_group.reshape(n, 1), col_group.reshape(1, S5_ROWW))


def _s5_in_kernel(u_ref, ws_ref, e_ref, s_ref, o_ref):
    j = pl.program_id(0)

    @pl.when(j < S5_NS)
    def _():
        s_ref[...] = jnp.dot(u_ref[...], ws_ref[...], preferred_element_type=F32)

    @pl.when(j >= S5_NS)
    def _():
        t = j - S5_NS
        acc = jnp.dot(u_ref[:, 0:S5_W], e_ref[S5_CH - 1 + t], preferred_element_type=F32)
        for s in range(1, S5_CH):
            acc = acc + jnp.dot(u_ref[:, s * S5_W:(s + 1) * S5_W], e_ref[S5_CH - 1 + t - s],
                                preferred_element_type=F32)
        o_ref[...] = acc


def _s5_in(u_rows, ws, e_all, layer):
    return pl.pallas_call(
        _s5_in_kernel,
        grid=(2 * S5_NS,),
        in_specs=[pl.BlockSpec((N_CHUNK, S5_ROWW), lambda j: (0, 0)),
                  pl.BlockSpec((None, S5_ROWW, S5_W), lambda j: (layer, 0, jnp.minimum(j, S5_NS - 1))),
                  pl.BlockSpec((None, 2 * S5_CH - 1, S5_W, S5_W), lambda j: (layer, 0, 0, 0))],
        out_specs=[pl.BlockSpec((N_CHUNK, S5_W), lambda j: (0, jnp.minimum(j, S5_NS - 1))),
                   pl.BlockSpec((N_CHUNK, S5_W), lambda j: (0, jnp.maximum(j - S5_NS, 0)))],
        out_shape=[jax.ShapeDtypeStruct((N_CHUNK, 4 * S5_NSTATE), F32), jax.ShapeDtypeStruct((N_CHUNK, S5_ROWW), F32)],
        compiler_params=_cp(48 << 20, 1),
        name="s5_in",
    )(u_rows, ws, e_all)


def _s5_scan_kernel(s_ref, lam_ref, h_ref):
    lam = [[lam_ref[d, p] for p in range(2)] for d in range(2)]

    def step(b, d, chunk, h):
        row = pl.ds(chunk, 1)
        cols_r = pl.ds(d * 2 * S5_NSTATE, S5_NSTATE)
        cols_i = pl.ds(d * 2 * S5_NSTATE + S5_NSTATE, S5_NSTATE)
        h_ref[row, cols_r] = h[0]
        h_ref[row, cols_i] = h[1]
        s_r = s_ref[row, cols_r]
        s_i = s_ref[row, cols_i]
        lr, li = lam[d]
        return (lr * h[0] - li * h[1] + s_r, lr * h[1] + li * h[0] + s_i)

    def chain_order(b, d, n_ctx_done):
        ctx0 = B * LAT_CHUNKS + b * CTX_CHUNKS
        lat0 = b * LAT_CHUNKS
        if d == 0:
            return (lambda i: ctx0 + i), (lambda i: lat0 + i)
        return (lambda i: ctx0 + CTX_CHUNKS - 1 - i), (lambda i: lat0 + LAT_CHUNKS - 1 - i)

    chains = [(b, d) for b in range(B) for d in range(2)]
    zero = jnp.zeros((1, S5_NSTATE), F32)
    init = tuple((zero, zero) for _ in chains)

    def phase(n_steps, which, carry):
        def body(i, hs):
            out = []
            for (b, d), h in zip(chains, hs):
                order = chain_order(b, d, 0)[which]
                out.append(step(b, d, order(i), h))
            return tuple(out)
        return lax.fori_loop(0, n_steps, body, carry)

    carry = phase(CTX_CHUNKS, 0, init)
    phase(LAT_CHUNKS, 1, carry)


def _s5_scan(s_rows, lam_p):
    return pl.pallas_call(
        _s5_scan_kernel,
        out_shape=jax.ShapeDtypeStruct((N_CHUNK, 4 * S5_NSTATE), F32),
        compiler_params=pltpu.CompilerParams(vmem_limit_bytes=48 << 20),
        name="s5_scan",
    )(s_rows, lam_p)


S5_TM = N_CHUNK // 2
S5_TPN = S5_TN // S5_W


def _s5_out_kernel(h_ref, w_ref, y_ref, *rest):
    o_refs, hb_ref = rest[:-1], rest[-1]
    j = pl.program_id(1)

    @pl.when(j == 0)
    def _():
        hb_ref[...] = h_ref[...].astype(BF16)

    acc = jnp.dot(hb_ref[...], w_ref[...], preferred_element_type=F32) + y_ref[...]
    for tt in range(S5_TPN):
        for h, o_ref in enumerate(o_refs):
            o_ref[pl.ds(j * S5_TPN + tt, S5_TM, stride=S5_CH), :] = (
                acc[:, tt * S5_W + h * LANE:tt * S5_W + (h + 1) * LANE])


def _s5_out(h_rows, wc, y_in, layer):
    nn = S5_ROWW // S5_TN
    n_out = S5_W // LANE
    return pl.pallas_call(
        _s5_out_kernel,
        grid=(N_CHUNK // S5_TM, nn),
        in_specs=[pl.BlockSpec((S5_TM, 4 * S5_NSTATE), lambda i, j: (i, 0), pipeline_mode=pl.Buffered(1)),
                  pl.BlockSpec((None, 4 * S5_NSTATE, S5_TN), lambda i, j: (layer, 0, j)),
                  pl.BlockSpec((S5_TM, S5_TN), lambda i, j: (i, j))],
        out_specs=[pl.BlockSpec((S5_TM * S5_CH, LANE), lambda i, j: (i, 0))] * n_out,
        out_shape=[jax.ShapeDtypeStruct((T_ALL, LANE), F32)] * n_out,
        scratch_shapes=[pltpu.VMEM((S5_TM, 4 * S5_NSTATE), BF16)],
        compiler_params=_cp(52 << 20, 2),
        name="s5_out",
    )(h_rows, wc, y_in)


def _rms(x):
    return x * lax.rsqrt(jnp.mean(x * x, axis=-1, keepdims=True) + LN_EPS)


def _layer_norm(x, g, b):
    mu = jnp.mean(x, axis=-1, keepdims=True)
    xc = x - mu
    var = jnp.mean(xc * xc, axis=-1, keepdims=True)
    return xc * lax.rsqrt(var + LN_EPS) * g + b


def _merge_kernel(x_ref, attn_ref, conv_ref, hu_ref, x0_ref, s5a_ref, s5b_ref, g1_ref, sh2_ref, sc2_ref, mixg_ref,
                  hyd_ref, wglu_ref, wout_ref, lng_ref, lnb_ref, o_ref, u2_ref, u2p_ref):
    hy = (conv_ref[...] + hu_ref[...] * hyd_ref[...]) * x0_ref[...]
    g = jax.nn.gelu(jnp.concatenate([s5a_ref[...], s5b_ref[...]], axis=-1))
    s5 = g * jax.nn.sigmoid(jnp.dot(g.astype(BF16), wglu_ref[...], preferred_element_type=F32))
    mixg = mixg_ref[...]
    parts = [_rms(attn_ref[...]) * mixg[:, 0:ATTN_W],
             _rms(hy) * mixg[:, ATTN_W:ATTN_W + HY_W],
             _rms(s5) * mixg[:, ATTN_W + HY_W:MIX_W]]
    mix = jnp.concatenate(parts, axis=-1).astype(BF16)
    o = jnp.dot(mix, wout_ref[...], preferred_element_type=F32)
    x1 = _layer_norm(ALPHA * x_ref[...] + g1_ref[...] * o, lng_ref[...], lnb_ref[...])
    o_ref[...] = x1
    u2 = x1 * (1.0 + sc2_ref[...]) + sh2_ref[...]
    u2_ref[...] = u2
    u2p_ref[...] = _pack_pairs(u2)


def _merge(n_rows, xall, attn, conv, hu, x0c, s5y, g1, sh2, sc2, mix_g, hy_d, wglu_bf, wout_bf, ln_g, ln_b):
    nt = n_rows // TM
    row = lambda i: (i, 0)
    full = lambda i: (0, 0)
    return pl.pallas_call(
        _merge_kernel,
        grid=(nt,),
        in_specs=[pl.BlockSpec((TM, D), row), pl.BlockSpec((TM, ATTN_W), row), pl.BlockSpec((TM, HY_W), row),
                  pl.BlockSpec((TM, HY_W), row), pl.BlockSpec((TM, HY_W), row),
                  pl.BlockSpec((TM, LANE), row), pl.BlockSpec((TM, LANE), row),
                  _vec_spec(TM), _vec_spec(TM), _vec_spec(TM), pl.BlockSpec((1, MIX_W), full),
                  pl.BlockSpec((1, HY_W), full), pl.BlockSpec((S5_W, S5_W), full), pl.BlockSpec((MIX_W, D), full),
                  pl.BlockSpec((1, D), full), pl.BlockSpec((1, D), full)],
        out_specs=[pl.BlockSpec((TM, D), row), pl.BlockSpec((TM, D), row), pl.BlockSpec((TM, HALF_D), row)],
        out_shape=[jax.ShapeDtypeStruct((n_rows, D), F32), jax.ShapeDtypeStruct((n_rows, D), F32),
                   jax.ShapeDtypeStruct((n_rows, HALF_D), jnp.int32)],
        compiler_params=_cp(48 << 20, 1),
        name="merge",
    )(xall, attn, conv, hu, x0c, s5y[0], s5y[1], g1, sh2, sc2, mix_g.reshape(1, MIX_W), hy_d.reshape(1, HY_W), wglu_bf,
      wout_bf, ln_g.reshape(1, D), ln_b.reshape(1, D))


def _router_kernel(u_ref, wt_ref, b_ref, e_ref, g_ref):
    logits = lax.dot_general(wt_ref[...], u_ref[...], (((1,), (1,)), ((), ())), precision=HIGHEST,
                             preferred_element_type=F32)
    scores = jax.nn.sigmoid(logits)
    biased = scores + b_ref[...]
    ninf = -jnp.inf
    grow = lax.broadcasted_iota(jnp.int32, (EGROUP, TM), 0)
    groups = [biased[gi * EGROUP:(gi + 1) * EGROUP] for gi in range(N_EGROUPS)]
    gscore = []
    for vals in groups:
        m1 = jnp.max(vals, axis=0, keepdims=True)
        i1 = jnp.min(jnp.where(vals == m1, grow, EGROUP), axis=0, keepdims=True)
        m2 = jnp.max(jnp.where(grow == i1, ninf, vals), axis=0, keepdims=True)
        gscore.append(m1 + m2)
    kept = []
    for gi in range(N_EGROUPS):
        rank = jnp.zeros((1, TM), jnp.int32)
        for gj in range(N_EGROUPS):
            if gj == gi:
                continue
            ahead = (gscore[gj] > gscore[gi]) | ((gscore[gj] == gscore[gi]) & (gj < gi))
            rank = rank + ahead.astype(jnp.int32)
        kept.append(jnp.where(rank < TOPK_GROUPS, groups[gi], ninf))
    masked = jnp.concatenate(kept, axis=0)
    row = lax.broadcasted_iota(jnp.int32, (N_EXPERTS, TM), 0)
    gates = []
    gsum = jnp.zeros((1, TM), F32)
    for kk in range(TOP_K):
        m = jnp.max(masked, axis=0, keepdims=True)
        idx = jnp.min(jnp.where(masked == m, row, N_EXPERTS), axis=0, keepdims=True)
        hit = row == idx
        gate = jnp.sum(jnp.where(hit, scores, 0.0), axis=0, keepdims=True)
        masked = jnp.where(hit, ninf, masked)
        e_ref[kk:kk + 1, :] = idx
        gates.append(gate)
        gsum = gsum + gate
    for kk in range(TOP_K):
        g_ref[kk:kk + 1, :] = gates[kk] / gsum * ROUTED_SCALE


def _router(n_rows, u2, w_router_t, router_bias):
    nt = n_rows // TM
    col = lambda i: (0, i)
    return pl.pallas_call(
        _router_kernel,
        grid=(nt,),
        in_specs=[pl.BlockSpec((TM, D), lambda i: (i, 0)),
                  pl.BlockSpec((N_EXPERTS, D), lambda i: (0, 0)), pl.BlockSpec((N_EXPERTS, 1), lambda i: (0, 0))],
        out_specs=[pl.BlockSpec((TOP_K, TM), col), pl.BlockSpec((TOP_K, TM), col)],
        out_shape=[jax.ShapeDtypeStruct((TOP_K, n_rows), jnp.int32), jax.ShapeDtypeStruct((TOP_K, n_rows), F32)],
        compiler_params=_cp(32 << 20, 1),
        name="router",
    )(u2, w_router_t, router_bias.reshape(N_EXPERTS, 1))


def _dispatch(top_e):
    t = top_e.shape[1]
    tk = t * TOP_K
    nblk = tk // MOE_BLOCK
    n_steps = nblk + N_EXPERTS
    flat_e = top_e.reshape(tk)
    pos = jnp.arange(tk, dtype=jnp.int32)
    pos_bits = (tk - 1).bit_length()
    assert pos_bits + (N_EXPERTS - 1).bit_length() < 32
    order = lax.sort(flat_e * (1 << pos_bits) + pos) % (1 << pos_bits)
    experts = jnp.arange(N_EXPERTS, dtype=jnp.int32)
    counts = jnp.sum((flat_e[None, :] == experts[:, None]).astype(jnp.int32), axis=1)
    ends = jnp.cumsum(counts)
    starts = ends - counts
    fb = starts // MOE_BLOCK
    npairs = jnp.where(counts > 0, (ends - 1) // MOE_BLOCK - fb + 1, 0)
    pend = jnp.cumsum(npairs)
    poff = pend - npairs
    n_pairs = pend[-1]
    s = jnp.arange(n_steps, dtype=jnp.int32)
    sc = jnp.minimum(s, n_pairs - 1)
    pe = jnp.sum((pend[None, :] <= sc[:, None]).astype(jnp.int32), axis=1)
    pb = fb[pe] + (sc - poff[pe])
    lo = jnp.where(s < n_pairs, jnp.maximum(starts[pe] - pb * MOE_BLOCK, 0), 0)
    hi = jnp.where(s < n_pairs, jnp.minimum(ends[pe] - pb * MOE_BLOCK, MOE_BLOCK), 0)
    last_of_block = ((s == n_pairs - 1) | ((s + 1 < n_pairs) & (jnp.roll(pb, -1) != pb))).astype(jnp.int32)
    used = counts > 0
    ordinal = jnp.cumsum(used.astype(jnp.int32)) - 1
    later = jnp.where(used, experts, N_EXPERTS)
    next_used = lax.cummin(jnp.concatenate([later[1:], jnp.full((1,), N_EXPERTS, jnp.int32)]), reverse=True)
    nxt = next_used[pe]
    nxt = jnp.where(nxt < N_EXPERTS, nxt, pe)
    parity = ordinal[pe] % 2
    w_even = jnp.where(parity == 0, pe, nxt)
    w_odd = jnp.where(parity == 1, pe, nxt)
    return dict(pe=pe, pb=pb, lo=lo, hi=hi, last=last_of_block, parity=parity, w_even=w_even, w_odd=w_odd,
                tok=order % t, order=order)


X_SLOTS = 4


def _expert_kernel(pe, pb, plo, phi, plast, ppar, pwe, pwo, xs_hbm, wg0_ref, wu0_ref, wd0_ref, wg1_ref, wu1_ref,
                   wd1_ref, out_hbm, xbuf, xsem, obuf, osem, wgb, wub, wdb):
    del pwe, pwo
    s = pl.program_id(0)
    prev = jnp.maximum(s - 1, 0)
    n_blocks = out_hbm.shape[0] // MOE_BLOCK
    new_expert = (s == 0) | (pe[s] != pe[prev])

    def x_copy(b, sl):
        rows = pl.ds(pl.multiple_of(b * MOE_BLOCK, MOE_BLOCK), MOE_BLOCK)
        return pltpu.make_async_copy(xs_hbm.at[rows], xbuf.at[sl], xsem.at[sl])

    @pl.when(s == 0)
    def _():
        for b in range(X_SLOTS - 1):
            x_copy(b, b).start()
        obuf[...] = jnp.zeros(obuf.shape, obuf.dtype)

    for par, (wg_ref, wu_ref, wd_ref) in enumerate(((wg0_ref, wu0_ref, wd0_ref), (wg1_ref, wu1_ref, wd1_ref))):
        @pl.when(new_expert & (ppar[s] == par))
        def _():
            wgb[...] = wg_ref[...].astype(BF16)
            wub[...] = wu_ref[...].astype(BF16)
            wdb[...] = wd_ref[...].astype(BF16)

    lo = plo[s]
    hi = phi[s]
    blk = pb[s]
    slot = blk % 2
    first_of_block = (s == 0) | (blk != pb[prev])

    def out_copy(b, sl):
        rows = pl.ds(pl.multiple_of(b * MOE_BLOCK, MOE_BLOCK), MOE_BLOCK)
        return pltpu.make_async_copy(obuf.at[sl], out_hbm.at[rows], osem.at[sl])

    @pl.when(first_of_block)
    def _():
        ahead = blk + (X_SLOTS - 1)

        @pl.when(ahead < n_blocks)
        def _():
            x_copy(ahead, ahead % X_SLOTS).start()
        x_copy(blk, blk % X_SLOTS).wait()

    def ffn():
        x_lo, x_hi = _unpack_pairs(xbuf[blk % X_SLOTS])
        x_lo = x_lo.astype(BF16)
        x_hi = x_hi.astype(BF16)
        hg = (jnp.dot(x_lo, wgb[0:HALF_D], preferred_element_type=F32)
              + jnp.dot(x_hi, wgb[HALF_D:D], preferred_element_type=F32))
        hu = (jnp.dot(x_lo, wub[0:HALF_D], preferred_element_type=F32)
              + jnp.dot(x_hi, wub[HALF_D:D], preferred_element_type=F32))
        h = (hg * jax.nn.sigmoid(hg)) * hu
        return _pack_pairs(jnp.dot(h.astype(BF16), wdb[...], preferred_element_type=F32))

    @pl.when(first_of_block & (blk >= 2))
    def _():
        out_copy(blk - 2, slot).wait()

    @pl.when(hi > lo)
    def _():
        row = lax.broadcasted_iota(jnp.int32, (MOE_BLOCK, HALF_D), 0)
        mine = first_of_block | ((row >= lo) & (row < hi))
        obuf[slot] = jnp.where(mine, ffn(), obuf[slot])

    @pl.when(plast[s] == 1)
    def _():
        out_copy(blk, slot).start()

    @pl.when(s == pl.num_programs(0) - 1)
    def _():
        out_copy(n_blocks - 2, (n_blocks - 2) % 2).wait()
        out_copy(n_blocks - 1, (n_blocks - 1) % 2).wait()


def _experts(xs, disp, wg, wu, wd, layer):
    tk = xs.shape[0]
    n_steps = disp['pe'].shape[0]
    assert tk // MOE_BLOCK >= X_SLOTS
    weven = lambda shape: pl.BlockSpec((None, None) + shape, lambda s, *p: (layer, p[6][s], 0, 0))
    wodd = lambda shape: pl.BlockSpec((None, None) + shape, lambda s, *p: (layer, p[7][s], 0, 0))
    grid_spec = pltpu.PrefetchScalarGridSpec(
        num_scalar_prefetch=8,
        grid=(n_steps,),
        in_specs=[pl.BlockSpec(memory_space=pl.ANY),
                  weven((D, EXPERT_FF)), weven((D, EXPERT_FF)), weven((EXPERT_FF, D)),
                  wodd((D, EXPERT_FF)), wodd((D, EXPERT_FF)), wodd((EXPERT_FF, D))],
        out_specs=pl.BlockSpec(memory_space=pl.ANY),
        scratch_shapes=[pltpu.VMEM((X_SLOTS, MOE_BLOCK, HALF_D), jnp.int32), pltpu.SemaphoreType.DMA((X_SLOTS,)),
                        pltpu.VMEM((2, MOE_BLOCK, HALF_D), jnp.int32), pltpu.SemaphoreType.DMA((2,)),
                        pltpu.VMEM((D, EXPERT_FF), BF16), pltpu.VMEM((D, EXPERT_FF), BF16),
                        pltpu.VMEM((EXPERT_FF, D), BF16)],
    )
    return pl.pallas_call(
        _expert_kernel,
        grid_spec=grid_spec,
        out_shape=jax.ShapeDtypeStruct((tk, HALF_D), jnp.int32),
        compiler_params=_cp(40 << 20, 1),
        name="experts",
    )(disp['pe'], disp['pb'], disp['lo'], disp['hi'], disp['last'], disp['parity'], disp['w_even'], disp['w_odd'],
      xs, wg, wu, wd, wg, wu, wd)


SC_ROWS = 128


def _row_gather(table, idx):
    n = idx.shape[0]
    d = table.shape[1]
    mesh = plsc.VectorSubcoreMesh(core_axis_name="c", subcore_axis_name="s")
    n_workers = mesh.num_cores * mesh.num_subcores
    per_worker = n // n_workers
    assert per_worker * n_workers == n and per_worker % SC_ROWS == 0

    @functools.partial(
        pl.kernel, mesh=mesh,
        out_type=jax.ShapeDtypeStruct((n, d), table.dtype),
        scratch_types=[pltpu.VMEM((per_worker,), jnp.int32), pltpu.VMEM((SC_ROWS, d), table.dtype),
                       pltpu.SemaphoreType.DMA],
    )
    def gather(table_hbm, idx_hbm, out_hbm, idx_v, rows_v, sem):
        worker = lax.axis_index("s") * mesh.num_cores + lax.axis_index("c")
        base = pl.multiple_of(worker * per_worker, 8)
        pltpu.sync_copy(idx_hbm.at[pl.ds(base, per_worker)], idx_v)

        @pl.loop(0, per_worker // SC_ROWS)
        def _(c):
            lo = pl.multiple_of(c * SC_ROWS, 8)
            pltpu.async_copy(table_hbm.at[idx_v.at[pl.ds(lo, SC_ROWS)]], rows_v, sem).wait()
            pltpu.sync_copy(rows_v, out_hbm.at[pl.ds(base + lo, SC_ROWS)])

    return gather(table, idx)


def _row_scatter(rows, idx):
    n, d = rows.shape
    mesh = plsc.VectorSubcoreMesh(core_axis_name="c", subcore_axis_name="s")
    n_workers = mesh.num_cores * mesh.num_subcores
    per_worker = n // n_workers
    assert per_worker * n_workers == n and per_worker % SC_ROWS == 0

    @functools.partial(
        pl.kernel, mesh=mesh,
        out_type=jax.ShapeDtypeStruct((n, d), rows.dtype),
        scratch_types=[pltpu.VMEM((SC_ROWS,), jnp.int32), pltpu.VMEM((SC_ROWS, d), rows.dtype),
                       pltpu.SemaphoreType.DMA],
    )
    def scatter(rows_hbm, idx_hbm, out_hbm, idx_v, rows_v, sem):
        worker = lax.axis_index("s") * mesh.num_cores + lax.axis_index("c")
        base = worker * per_worker

        @pl.loop(0, per_worker // SC_ROWS)
        def _(c):
            off = pl.multiple_of(base + c * SC_ROWS, 8)
            pltpu.sync_copy(idx_hbm.at[pl.ds(off, SC_ROWS)], idx_v)
            pltpu.sync_copy(rows_hbm.at[pl.ds(off, SC_ROWS)], rows_v)
            pltpu.async_copy(rows_v, out_hbm.at[idx_v], sem).wait()

    return scatter(rows, idx)


FM = 128


def _ffn_out_kernel(x_ref, u_ref, r_ref, gate_ref, g2_ref, wsg_ref, wsu_ref, wsd_ref, lng_ref, lnb_ref, o_ref):
    u_lo, u_hi = _unpack_pairs(u_ref[...])
    u_lo = u_lo.astype(BF16)
    u_hi = u_hi.astype(BF16)
    hg = (jnp.dot(u_lo, wsg_ref[0:HALF_D], preferred_element_type=F32)
          + jnp.dot(u_hi, wsg_ref[HALF_D:D], preferred_element_type=F32))
    hu = (jnp.dot(u_lo, wsu_ref[0:HALF_D], preferred_element_type=F32)
          + jnp.dot(u_hi, wsu_ref[HALF_D:D], preferred_element_type=F32))
    f = jnp.dot(((hg * jax.nn.sigmoid(hg)) * hu).astype(BF16), wsd_ref[...], preferred_element_type=F32)
    gates = jnp.transpose(jnp.concatenate([gate_ref[...], jnp.zeros((FM - TOP_K, FM), F32)], axis=0))
    r_lo = jnp.zeros((FM, HALF_D), F32)
    r_hi = jnp.zeros((FM, HALF_D), F32)
    for kk in range(TOP_K):
        k_lo, k_hi = _unpack_pairs(r_ref[kk])
        gk = gates[:, kk:kk + 1]
        r_lo = r_lo + k_lo * gk
        r_hi = r_hi + k_hi * gk
    f = jnp.concatenate([r_lo, r_hi], axis=1) + f
    o_ref[...] = _layer_norm(ALPHA * x_ref[...] + g2_ref[...] * f, lng_ref[...], lnb_ref[...])


def _ffn_out(n_rows, x1, u2p, routed, gate, g2, wsg_bf, wsu_bf, wsd_bf, ln_g, ln_b):
    row = lambda i: (i, 0)
    full = lambda i: (0, 0)
    return pl.pallas_call(
        _ffn_out_kernel,
        grid=(n_rows // FM,),
        in_specs=[pl.BlockSpec((FM, D), row), pl.BlockSpec((FM, HALF_D), row),
                  pl.BlockSpec((TOP_K, FM, HALF_D), lambda i: (0, i, 0)), pl.BlockSpec((TOP_K, FM), lambda i: (0, i)),
                  _vec_spec(FM), pl.BlockSpec((D, EXPERT_FF), full), pl.BlockSpec((D, EXPERT_FF), full),
                  pl.BlockSpec((EXPERT_FF, D), full), pl.BlockSpec((1, D), full), pl.BlockSpec((1, D), full)],
        out_specs=pl.BlockSpec((FM, D), row),
        out_shape=jax.ShapeDtypeStruct((n_rows, D), F32),
        compiler_params=_cp(40 << 20, 1),
        name="ffn_out",
    )(x1, u2p, routed, gate, g2, wsg_bf, wsu_bf, wsd_bf, ln_g.reshape(1, D), ln_b.reshape(1, D))


def _rope_tables():
    t = jnp.arange(L, dtype=jnp.int32)
    row = (t // GRID_W).astype(F32)
    col = (t % GRID_W).astype(F32)
    inv_freq = ROPE_BASE ** (-jnp.arange(0, AXIS_DIM, 2, dtype=F32) / AXIS_DIM)
    half = AXIS_DIM // 2

    def axis(pos):
        ang = pos[:, None] * inv_freq[None, :]
        c = jnp.cos(ang)
        s = jnp.sin(ang)
        return jnp.concatenate([c, c], axis=1), jnp.concatenate([-s, s], axis=1)

    cr, sr = axis(row)
    cc, sc = axis(col)
    cos_h = jnp.concatenate([cr, cc], axis=1)
    sin_h = jnp.concatenate([sr, sc], axis=1)
    cos_l = jnp.tile(cos_h, (B, LANE // HEAD_DIM))
    sin_l = jnp.tile(sin_h, (B, LANE // HEAD_DIM))
    cos_t = jnp.concatenate([cos_l, jnp.ones((T_CTX, LANE), F32)], axis=0)
    sin_t = jnp.concatenate([sin_l, jnp.zeros((T_CTX, LANE), F32)], axis=0)
    del half
    return cos_t, sin_t


def kernel(x, c, ctx, c_ctx, w_ada, b_ada, w_in, w_out, sink, mix_g, hy_short_w, hy_short_b, hy_w1, hy_b1, hy_freq,
           hy_w2, hy_b2, hy_w3, hy_d, s5_a_re, s5_a_im, s5_log_dt, s5_b_re, s5_b_im, s5_c_re, s5_c_im, s5_d, s5_w_glu,
           ln1_g, ln1_b, ln2_g, ln2_b, w_router, router_bias, w_exp_gate, w_exp_up, w_exp_down, w_sh_gate, w_sh_up,
           w_sh_down):
    xall = jnp.concatenate([x.reshape(T_LAT, D), ctx.reshape(T_CTX, D)], axis=0)
    cvec = jnp.concatenate([c, c_ctx[None, :], jnp.zeros((8 - B - 1, D), F32)], axis=0)
    mod = _ada(cvec, w_ada, b_ada)[:, 0:B + 1, :].reshape(DEPTH, B + 1, 6, 1, D)

    cos_t, sin_t = _rope_tables()
    g_cos, g_sin, f_cos, f_sin = _dft_tables()
    g_cos_bf, g_sin_bf = g_cos.astype(BF16), g_sin.astype(BF16)
    i_cos_bf = jnp.swapaxes(g_cos, 1, 2).astype(BF16)
    i_sin_bf = jnp.swapaxes(g_sin, 1, 2).astype(BF16)
    f_cos_bf, f_sin_bf = f_cos.astype(BF16), f_sin.astype(BF16)
    k1 = jnp.arange(K1P)
    spec_w = jnp.where((k1 == 0) | (k1 == FFT_R // 2), 1.0, 2.0) * (k1 < K1_USED) / N_FFT
    spec_w = spec_w.astype(F32)
    kt = jnp.arange(2 * C, dtype=jnp.int32)
    ang_c = ((kt[:, None] * kt[None, :]) % (2 * C)).astype(F32) * (2.0 * math.pi / (2 * C))
    d_cos, d_sin = jnp.cos(ang_c), jnp.sin(ang_c)

    filt_params = (hy_w1, hy_b1, hy_freq, hy_w2, hy_b2, hy_w3)
    filt_lat = jax.vmap(functools.partial(_hyena_filter, L))(*filt_params)
    filt_ctx = jax.vmap(functools.partial(_hyena_filter, C))(*(p[0:DEPTH - 1] for p in filt_params))
    k_r, k_i = jax.vmap(_hyena_spectrum, in_axes=(0, None, None, None, None, None))(
        filt_lat, g_cos, g_sin, f_cos, f_sin, spec_w)
    ws, e_all, wc, lam_p = jax.vmap(_s5_matrices)(s5_a_re, s5_a_im, s5_log_dt, s5_b_re, s5_b_im, s5_c_re, s5_c_im, s5_d)

    for l in range(DEPTH):
        last = l == DEPTH - 1
        n_rows = T_LAT if last else T_ALL
        sh1, sc1, g1, sh2, sc2, g2 = (mod[l, :, j] for j in range(6))

        q, k, v, hz, s5u = _inproj(xall, sh1, sc1, w_in[l].astype(BF16), cos_t, sin_t)
        attn = _attention(sink[l], q, k, v, with_ctx=not last)

        hu, x0c = _hyena_pre(hz, hy_short_w[l], hy_short_b[l])
        conv = _hyena_fft(hu, g_cos_bf, g_sin_bf, i_cos_bf, i_sin_bf, f_cos_bf, f_sin_bf, k_r, k_i, l)
        if not last:
            conv = jnp.concatenate([conv, _hyena_ctx(hu, filt_ctx[l], d_cos, d_sin)], axis=0)

        s_rows, y_in = _s5_in(s5u, ws, e_all, l)
        s5y = _s5_out(_s5_scan(s_rows, lam_p[l]), wc, y_in, l)

        x1, u2, u2p = _merge(n_rows, xall, attn, conv, hu, x0c, s5y, g1, sh2, sc2, mix_g[l], hy_d[l],
                             s5_w_glu[l].astype(BF16), w_out[l].astype(BF16), ln1_g[l], ln1_b[l])

        top_e, gate = _router(n_rows, u2, w_router[l].T, router_bias[l])
        disp = _dispatch(top_e)
        ys = _experts(_row_gather(u2p, disp['tok']), disp, w_exp_gate, w_exp_up, w_exp_down, l)
        routed = _row_scatter(ys, disp['order']).reshape(TOP_K, n_rows, HALF_D)
        xall = _ffn_out(n_rows, x1, u2p, routed, gate, g2, w_sh_gate[l].astype(BF16), w_sh_up[l].astype(BF16),
                        w_sh_down[l].astype(BF16), ln2_g[l], ln2_b[l])
    return xall.reshape(B, L, D)
```

```python
import functools
import math

import jax
import jax.numpy as jnp
from jax import lax
from jax.experimental import pallas as pl
from jax.experimental.pallas import tpu as pltpu
from jax.experimental.pallas import tpu_sc as plsc

F32 = jnp.float32
BF16 = jnp.bfloat16
HIGHEST = lax.Precision.HIGHEST

D = 1024
B = 2
L = 8192
DEPTH = 2
GRID_W = 64
C = 256
T_LAT = B * L
T_CTX = B * C
T_ALL = T_LAT + T_CTX

HEAD_DIM = 64
N_Q = 8
N_KV = 2
Q_GROUP = N_Q // N_KV
ATTN_W = N_Q * HEAD_DIM
KV_W = N_KV * HEAD_DIM
HY_W = 256
S5_W = 256
MIX_W = ATTN_W + HY_W + S5_W
K_OFF = ATTN_W
V_OFF = K_OFF + KV_W
HY_OFF = V_OFF + KV_W
S5_OFF = HY_OFF + 3 * HY_W
IN_W = S5_OFF + S5_W
WINDOW = 128
BLK = 128
NEG_INF = -1e30
ROPE_BASE = 10000.0
AXIS_DIM = HEAD_DIM // 2

SHORT_K = 3
FILTER_EMB = 33
DECAY_FAST = 0.3
DECAY_SLOW = 1.5
DECAY_TARGET = 1e-2

S5_GROUP = 16
S5_GROUPS = S5_W // S5_GROUP
S5_STATE = 64
S5_NSTATE = S5_GROUPS * S5_STATE
S5_CH = 16
S5_ROWW = S5_CH * S5_W
N_CHUNK = T_ALL // S5_CH
LAT_CHUNKS = L // S5_CH
CTX_CHUNKS = C // S5_CH

N_EXPERTS = 256
TOP_K = 8
N_EGROUPS = 8
EGROUP = N_EXPERTS // N_EGROUPS
TOPK_GROUPS = 4
EXPERT_FF = 256
ROUTED_SCALE = 2.5
MOE_BLOCK = 256

ALPHA = (2 * DEPTH) ** 0.25
LN_EPS = 1e-5

N_FFT = 2 * L
FFT_R = 128
FFT_T1 = L // FFT_R
K1_USED = FFT_R // 2 + 1
K1P = 80
K1H = K1P // 2
FFT_UNROLL = 4

TM = 256
LANE = 128
VMEM_CAP = 60000 * 1024


def _cp(vmem_bytes, n_axes):
    return pltpu.CompilerParams(
        dimension_semantics=("arbitrary",) * n_axes if n_axes else None,
        vmem_limit_bytes=min(int(vmem_bytes), VMEM_CAP),
    )


HALF_D = D // 2
HIGH_HALF_WORD = 0xFFFF0000


def _pack_pairs(x):
    bits = lax.bitcast_convert_type(x.astype(BF16).astype(F32), jnp.uint32)
    packed = (bits[:, 0:HALF_D] >> 16) | (bits[:, HALF_D:D] & jnp.uint32(HIGH_HALF_WORD))
    return lax.bitcast_convert_type(packed, jnp.int32)


def _unpack_pairs(p):
    bits = lax.bitcast_convert_type(p, jnp.uint32)
    low = lax.bitcast_convert_type(bits << 16, F32)
    high = lax.bitcast_convert_type(bits & jnp.uint32(HIGH_HALF_WORD), F32)
    return low, high


def _mod_sel(rows_per_tile):
    per_batch = L // rows_per_tile
    return lambda i: jnp.minimum(i // per_batch, 2)


def _vec_spec(rows_per_tile):
    sel = _mod_sel(rows_per_tile)
    return pl.BlockSpec((None, 1, D), lambda i: (sel(i), 0, 0))


ADA_TN = 1536


def _ada_kernel(c_ref, w_ref, b_ref, o_ref):
    c = c_ref[...]
    s = c * jax.nn.sigmoid(c)
    o_ref[...] = jnp.dot(s, w_ref[...], precision=HIGHEST, preferred_element_type=F32) + b_ref[...]


def _ada(cvec, w_ada, b_ada):
    return pl.pallas_call(
        _ada_kernel,
        grid=(DEPTH, 6 * D // ADA_TN),
        in_specs=[
            pl.BlockSpec((8, D), lambda l, j: (0, 0)),
            pl.BlockSpec((None, D, ADA_TN), lambda l, j: (l, 0, j)),
            pl.BlockSpec((None, 1, ADA_TN), lambda l, j: (l, 0, j)),
        ],
        out_specs=pl.BlockSpec((None, 8, ADA_TN), lambda l, j: (l, 0, j)),
        out_shape=jax.ShapeDtypeStruct((DEPTH, 8, 6 * D), F32),
        compiler_params=_cp(40 << 20, 2),
        name="ada",
    )(cvec, w_ada, b_ada.reshape(DEPTH, 1, 6 * D))


def _inproj_kernel(x_ref, sh_ref, sc_ref, w_ref, cos_ref, sin_ref, q_ref, k_ref, v_ref, hy_ref, s5_ref, *s5_tok_refs):
    u = x_ref[...] * (1.0 + sc_ref[...]) + sh_ref[...]
    proj = jnp.dot(u.astype(BF16), w_ref[...], preferred_element_type=F32)
    cos = cos_ref[...]
    sin = sin_ref[...]
    lane = lax.broadcasted_iota(jnp.int32, (TM, LANE), 1)
    first_half = (lane % AXIS_DIM) < (AXIS_DIM // 2)

    def rope(xc):
        partner = jnp.where(first_half, pltpu.roll(xc, LANE - AXIS_DIM // 2, 1), pltpu.roll(xc, AXIS_DIM // 2, 1))
        return xc * cos + partner * sin

    for j in range(ATTN_W // LANE):
        q_ref[:, j * LANE:(j + 1) * LANE] = rope(proj[:, j * LANE:(j + 1) * LANE]).astype(BF16)
    k_ref[...] = rope(proj[:, K_OFF:V_OFF]).astype(BF16)
    v_ref[...] = proj[:, V_OFF:HY_OFF].astype(BF16)
    hy_ref[...] = proj[:, HY_OFF:S5_OFF]
    for h, tok_ref in enumerate(s5_tok_refs):
        tok_ref[...] = proj[:, S5_OFF + h * LANE:S5_OFF + (h + 1) * LANE]
    for s in range(S5_CH):
        for h, tok_ref in enumerate(s5_tok_refs):
            s5_ref[:, s * S5_W + h * LANE:s * S5_W + (h + 1) * LANE] = (
                tok_ref[pl.ds(s, TM // S5_CH, stride=S5_CH), :].astype(BF16))


def _inproj(xall, sh, sc, w_in_bf, cos_t, sin_t):
    nt = T_ALL // TM
    row = lambda i: (i, 0)
    return pl.pallas_call(
        _inproj_kernel,
        grid=(nt,),
        in_specs=[
            pl.BlockSpec((TM, D), row),
            _vec_spec(TM),
            _vec_spec(TM),
            pl.BlockSpec((D, IN_W), lambda i: (0, 0)),
            pl.BlockSpec((TM, LANE), row),
            pl.BlockSpec((TM, LANE), row),
        ],
        out_specs=[
            pl.BlockSpec((TM, ATTN_W), row),
            pl.BlockSpec((TM, KV_W), row),
            pl.BlockSpec((TM, KV_W), row),
            pl.BlockSpec((TM, 3 * HY_W), row),
            pl.BlockSpec((TM // S5_CH, S5_ROWW), row),
        ],
        out_shape=[
            jax.ShapeDtypeStruct((T_ALL, ATTN_W), BF16),
            jax.ShapeDtypeStruct((T_ALL, KV_W), BF16),
            jax.ShapeDtypeStruct((T_ALL, KV_W), BF16),
            jax.ShapeDtypeStruct((T_ALL, 3 * HY_W), F32),
            jax.ShapeDtypeStruct((N_CHUNK, S5_ROWW), BF16),
        ],
        scratch_shapes=[pltpu.VMEM((TM, LANE), F32)] * (S5_W // LANE),
        compiler_params=_cp(40 << 20, 1),
        name="inproj",
    )(xall, sh, sc, w_in_bf, cos_t, sin_t)


NB_LAT = L // BLK
NB_CTX = C // BLK


def _nt_dot(a, b):
    return lax.dot_general(a, b, (((1,), (1,)), ((), ())), preferred_element_type=F32)


def _attn_kernel(sink_ref, q_ref, kp_ref, kc_ref, kn_ref, kx_ref, vp_ref, vc_ref, vn_ref, vx_ref, o_ref):
    n = pl.program_id(1)
    is_lat = n < NB_LAT
    rows = Q_GROUP * BLK
    r = lax.broadcasted_iota(jnp.int32, (rows, BLK), 0) % BLK
    j = lax.broadcasted_iota(jnp.int32, (rows, BLK), 1)
    ok_prev = (j >= r) & (n >= 1) & is_lat
    ok_next = (j <= r) & (n + 1 < NB_LAT) & is_lat
    head_of_row = lax.broadcasted_iota(jnp.int32, (rows, 1), 0) // BLK
    q = q_ref[...] * (HEAD_DIM ** -0.5)
    dot = lambda a, b: jnp.dot(a.astype(BF16), b, preferred_element_type=F32)
    for kh in range(N_KV):
        hs = slice(kh * HEAD_DIM, (kh + 1) * HEAD_DIM)
        heads = range(kh * Q_GROUP, (kh + 1) * Q_GROUP)
        qg = jnp.concatenate([q[:, h * HEAD_DIM:(h + 1) * HEAD_DIM] for h in heads], axis=0)
        sk = jnp.zeros((rows, 1), F32)
        for g, h in enumerate(heads):
            sk = jnp.where(head_of_row == g, sink_ref[h], sk)
        s_p = jnp.where(ok_prev, _nt_dot(qg, kp_ref[:, hs]), NEG_INF)
        s_c = jnp.where(is_lat, _nt_dot(qg, kc_ref[:, hs]), NEG_INF)
        s_n = jnp.where(ok_next, _nt_dot(qg, kn_ref[:, hs]), NEG_INF)
        s_x = _nt_dot(qg, kx_ref[:, hs])
        s_x0 = s_x[:, 0:BLK]
        s_x1 = s_x[:, BLK:2 * BLK]
        m = jnp.maximum(jnp.maximum(jnp.maximum(s_p, s_c), jnp.maximum(s_n, s_x0)), s_x1)
        m = jnp.maximum(jnp.max(m, axis=1, keepdims=True), sk)
        e_p = jnp.exp(s_p - m)
        e_c = jnp.exp(s_c - m)
        e_n = jnp.exp(s_n - m)
        e_x0 = jnp.exp(s_x0 - m)
        e_x1 = jnp.exp(s_x1 - m)
        den = jnp.sum((e_p + e_c) + (e_n + e_x0) + e_x1, axis=1, keepdims=True) + jnp.exp(sk - m)
        o = (dot(e_p, vp_ref[:, hs]) + dot(e_c, vc_ref[:, hs]) + dot(e_n, vn_ref[:, hs])
             + dot(e_x0, vx_ref[0:BLK, hs]) + dot(e_x1, vx_ref[BLK:2 * BLK, hs]))
        o = o / den
        for g, h in enumerate(heads):
            o_ref[:, h * HEAD_DIM:(h + 1) * HEAD_DIM] = o[g * BLK:(g + 1) * BLK]


def _attention(sink, q, k, v, with_ctx):
    nblk = NB_LAT + (NB_CTX if with_ctx else 0)

    def q_idx(b, n):
        return (jnp.where(n < NB_LAT, b * NB_LAT + n, B * NB_LAT + b * NB_CTX + (n - NB_LAT)), 0)

    def kv_idx(off):
        def idx(b, n):
            nn = jnp.clip(jnp.minimum(n, NB_LAT - 1) + off, 0, NB_LAT - 1)
            return (b * NB_LAT + nn, 0)
        return idx

    ctx_idx = lambda b, n: (T_LAT // C + b, 0)
    kv_specs = lambda: [pl.BlockSpec((BLK, KV_W), kv_idx(-1)), pl.BlockSpec((BLK, KV_W), kv_idx(0)),
                        pl.BlockSpec((BLK, KV_W), kv_idx(1)), pl.BlockSpec((C, KV_W), ctx_idx)]
    return pl.pallas_call(
        _attn_kernel,
        grid=(B, nblk),
        in_specs=[pl.BlockSpec(memory_space=pltpu.SMEM), pl.BlockSpec((BLK, ATTN_W), q_idx)] + kv_specs() + kv_specs(),
        out_specs=pl.BlockSpec((BLK, ATTN_W), q_idx),
        out_shape=jax.ShapeDtypeStruct((T_ALL if with_ctx else T_LAT, ATTN_W), F32),
        compiler_params=_cp(32 << 20, 2),
        name="attention",
    )(sink, q, k, k, k, k, v, v, v, v)


def _hyena_pre_kernel(z_ref, zp_ref, zn_ref, w_ref, b_ref, u_ref, x0_ref):
    i = pl.program_id(0)
    tiles_per_seq = L // TM
    is_ctx = i >= B * tiles_per_seq
    first = is_ctx | (i % tiles_per_seq == 0)
    last = is_ctx | (i % tiles_per_seq == tiles_per_seq - 1)
    z = z_ref[...]
    prev_row = jnp.where(first, 0.0, zp_ref[7:8, :])
    next_row = jnp.where(last, 0.0, zn_ref[0:1, :])
    row = lax.broadcasted_iota(jnp.int32, z.shape, 0)
    z_m1 = jnp.where(row == 0, prev_row, pltpu.roll(z, 1, 0))
    z_p1 = jnp.where(row == TM - 1, next_row, pltpu.roll(z, TM - 1, 0))
    zc = b_ref[...] + z_m1 * w_ref[0:1, :] + z * w_ref[1:2, :] + z_p1 * w_ref[2:3, :]
    u_ref[...] = zc[:, 0:HY_W] * zc[:, HY_W:2 * HY_W]
    x0_ref[...] = zc[:, 2 * HY_W:3 * HY_W]


def _hyena_pre(z, short_w, short_b):
    nt = T_ALL // TM
    sub = TM // 8
    n8 = T_ALL // 8
    return pl.pallas_call(
        _hyena_pre_kernel,
        grid=(nt,),
        in_specs=[
            pl.BlockSpec((TM, 3 * HY_W), lambda i: (i, 0)),
            pl.BlockSpec((8, 3 * HY_W), lambda i: (jnp.maximum(i * sub - 1, 0), 0)),
            pl.BlockSpec((8, 3 * HY_W), lambda i: (jnp.minimum((i + 1) * sub, n8 - 1), 0)),
            pl.BlockSpec((SHORT_K, 3 * HY_W), lambda i: (0, 0)),
            pl.BlockSpec((1, 3 * HY_W), lambda i: (0, 0)),
        ],
        out_specs=[pl.BlockSpec((TM, HY_W), lambda i: (i, 0)), pl.BlockSpec((TM, HY_W), lambda i: (i, 0))],
        out_shape=[jax.ShapeDtypeStruct((T_ALL, HY_W), F32), jax.ShapeDtypeStruct((T_ALL, HY_W), F32)],
        compiler_params=_cp(32 << 20, 1),
        name="hyena_pre",
    )(z, z, z, short_w, short_b.reshape(1, 3 * HY_W))


def _dft_tables():
    t0 = jnp.arange(FFT_R, dtype=jnp.int32)[:, None, None]
    k1 = jnp.arange(K1P, dtype=jnp.int32)[None, :, None]
    t1 = jnp.arange(FFT_T1, dtype=jnp.int32)[None, None, :]
    m = (k1 * (FFT_R * t1 + t0)) % N_FFT
    ang = m.astype(F32) * (2.0 * math.pi / N_FFT)
    used = (k1 < K1_USED).astype(F32)
    g_cos = jnp.cos(ang) * used
    g_sin = jnp.sin(ang) * used
    a = jnp.arange(FFT_R, dtype=jnp.int32)
    ang2 = ((a[:, None] * a[None, :]) % FFT_R).astype(F32) * (2.0 * math.pi / FFT_R)
    return g_cos, g_sin, jnp.cos(ang2), jnp.sin(ang2)


def _hyena_spec_kernel(k_ref, gc_ref, gs_ref, fc_ref, fs_ref, wt_ref, kr_ref, ki_ref, ar_ref, ai_ref):
    half = pl.program_id(1)
    kk = lax.broadcasted_iota(jnp.int32, (K1H, 1), 0) + half * K1H
    sign = jnp.where(kk % 2 == 0, 1.0, -1.0).astype(F32)

    dot = lambda a, b: jnp.dot(a, b.astype(BF16), preferred_element_type=F32)

    def stage1(i, carry):
        t0s = [i * FFT_UNROLL + u for u in range(FFT_UNROLL)]
        loaded = [(k_ref[pl.ds(t0, FFT_T1, stride=FFT_R), :], k_ref[pl.ds(L + t0, FFT_T1, stride=FFT_R), :],
                   gc_ref[t0].astype(BF16), gs_ref[t0].astype(BF16)) for t0 in t0s]
        res = [(dot(gc, x_lo) + sign * dot(gc, x_hi), -(dot(gs, x_lo) + sign * dot(gs, x_hi)))
               for x_lo, x_hi, gc, gs in loaded]
        for t0, (a_r, a_i) in zip(t0s, res):
            rows = pl.ds(pl.multiple_of(t0 * K1H, 8), K1H)
            ar_ref[rows, :] = a_r
            ai_ref[rows, :] = a_i
        return carry

    lax.fori_loop(0, FFT_R // FFT_UNROLL, stage1, 0)
    fc = fc_ref[...].astype(BF16)
    fs = fs_ref[...].astype(BF16)

    def stage2(i, carry):
        kls = [i * FFT_UNROLL + u for u in range(FFT_UNROLL)]
        loaded = [(ar_ref[pl.ds(kl, FFT_R, stride=K1H), :], ai_ref[pl.ds(kl, FFT_R, stride=K1H), :]) for kl in kls]
        for kl, (a_r, a_i) in zip(kls, loaded):
            w = wt_ref[half * K1H + kl]
            kr_ref[kl] = (dot(fc, a_r) + dot(fs, a_i)) * w
            ki_ref[kl] = (dot(fc, a_i) - dot(fs, a_r)) * w
        return carry

    lax.fori_loop(0, K1H // FFT_UNROLL, stage2, 0)


def _hyena_spectrum(kfilt, g_cos, g_sin, f_cos, f_sin, wts):
    nct = HY_W // LANE
    gspec = pl.BlockSpec((FFT_R, K1H, FFT_T1), lambda c, h: (0, h, 0))
    fspec = pl.BlockSpec((FFT_R, FFT_R), lambda c, h: (0, 0))
    ospec = pl.BlockSpec((K1H, FFT_R, LANE), lambda c, h: (h, 0, c))
    return pl.pallas_call(
        _hyena_spec_kernel,
        grid=(nct, 2),
        in_specs=[pl.BlockSpec((N_FFT, LANE), lambda c, h: (0, c)), gspec, gspec, fspec, fspec,
                  pl.BlockSpec(memory_space=pltpu.SMEM)],
        out_specs=[ospec, ospec],
        out_shape=[jax.ShapeDtypeStruct((K1P, FFT_R, HY_W), F32)] * 2,
        scratch_shapes=[pltpu.VMEM((FFT_R * K1H, LANE), F32)] * 2,
        compiler_params=_cp(56 << 20, 0),
        name="hyena_spectrum",
    )(kfilt, g_cos, g_sin, f_cos, f_sin, wts)


def _hyena_fft_kernel(u_ref, gc_ref, gs_ref, ic_ref, is_ref, fc_ref, fs_ref, kr_ref, ki_ref, o_ref, ar_ref, ai_ref):
    bdot = lambda a, b: jnp.dot(a, b.astype(BF16), preferred_element_type=F32)

    def stage1(i, carry):
        t0s = [i * FFT_UNROLL + u for u in range(FFT_UNROLL)]
        xs = [u_ref[pl.ds(t0, FFT_T1, stride=FFT_R), :] for t0 in t0s]
        res = [(bdot(gc_ref[t0], x), -bdot(gs_ref[t0], x)) for t0, x in zip(t0s, xs)]
        for t0, (a_r, a_i) in zip(t0s, res):
            rows = pl.ds(pl.multiple_of(t0 * K1P, 8), K1P)
            ar_ref[rows, :] = a_r
            ai_ref[rows, :] = a_i
        return carry

    lax.fori_loop(0, FFT_R // FFT_UNROLL, stage1, 0)
    fc = fc_ref[...]
    fs = fs_ref[...]

    def stage23(i, carry):
        k1s = [i * FFT_UNROLL + u for u in range(FFT_UNROLL)]
        loaded = [(ar_ref[pl.ds(k1, FFT_R, stride=K1P), :], ai_ref[pl.ds(k1, FFT_R, stride=K1P), :],
                   kr_ref[k1], ki_ref[k1]) for k1 in k1s]
        res = []
        for a_r, a_i, k_r, k_i in loaded:
            z_r = bdot(fc, a_r) + bdot(fs, a_i)
            z_i = bdot(fc, a_i) - bdot(fs, a_r)
            y_r = z_r * k_r - z_i * k_i
            y_i = z_r * k_i + z_i * k_r
            res.append((bdot(fc, y_r) - bdot(fs, y_i), bdot(fc, y_i) + bdot(fs, y_r)))
        for k1, (b_r, b_i) in zip(k1s, res):
            ar_ref[pl.ds(k1, FFT_R, stride=K1P), :] = b_r
            ai_ref[pl.ds(k1, FFT_R, stride=K1P), :] = b_i
        return carry

    lax.fori_loop(0, -(-K1_USED // FFT_UNROLL), stage23, 0)

    def stage4(i, carry):
        t0s = [i * FFT_UNROLL + u for u in range(FFT_UNROLL)]
        loaded = []
        for t0 in t0s:
            rows = pl.ds(pl.multiple_of(t0 * K1P, 8), K1P)
            loaded.append((ar_ref[rows, :], ai_ref[rows, :]))
        res = [bdot(ic_ref[t0], b_r) - bdot(is_ref[t0], b_i) for t0, (b_r, b_i) in zip(t0s, loaded)]
        for t0, y in zip(t0s, res):
            o_ref[pl.ds(t0, FFT_T1, stride=FFT_R), :] = y
        return carry

    lax.fori_loop(0, FFT_R // FFT_UNROLL, stage4, 0)


def _hyena_fft(u, g_cos_bf, g_sin_bf, i_cos_bf, i_sin_bf, f_cos_bf, f_sin_bf, k_r, k_i, layer):
    nct = HY_W // LANE
    one = pl.Buffered(1)
    gspec = pl.BlockSpec((FFT_R, K1P, FFT_T1), lambda c, b: (0, 0, 0), pipeline_mode=one)
    ispec = pl.BlockSpec((FFT_R, FFT_T1, K1P), lambda c, b: (0, 0, 0), pipeline_mode=one)
    fspec = pl.BlockSpec((FFT_R, FFT_R), lambda c, b: (0, 0), pipeline_mode=one)
    kspec = pl.BlockSpec((None, K1P, FFT_R, LANE), lambda c, b: (layer, 0, 0, c), pipeline_mode=one)
    return pl.pallas_call(
        _hyena_fft_kernel,
        grid=(nct, B),
        in_specs=[pl.BlockSpec((L, LANE), lambda c, b: (b, c)), gspec, gspec, ispec, ispec, fspec, fspec, kspec, kspec],
        out_specs=pl.BlockSpec((L, LANE), lambda c, b: (b, c)),
        out_shape=jax.ShapeDtypeStruct((T_LAT, HY_W), F32),
        scratch_shapes=[pltpu.VMEM((FFT_R * K1P, LANE), F32)] * 2,
        compiler_params=_cp(56 << 20, 2),
        name="hyena_fft",
    )(u, g_cos_bf, g_sin_bf, i_cos_bf, i_sin_bf, f_cos_bf, f_sin_bf, k_r, k_i)


def _hyena_ctx_kernel(u_ref, k_ref, dc_ref, ds_ref, o_ref):
    dot = lambda a, b: jnp.dot(a, b, precision=HIGHEST, preferred_element_type=F32)
    dc = dc_ref[...]
    ds = ds_ref[...]
    u = u_ref[...]
    kf = k_ref[...]
    u_r = dot(dc[:, 0:C], u)
    u_i = -dot(ds[:, 0:C], u)
    k_r = dot(dc, kf)
    k_i = -dot(ds, kf)
    y_r = u_r * k_r - u_i * k_i
    y_i = u_r * k_i + u_i * k_r
    o_ref[...] = (dot(dc[0:C, :], y_r) - dot(ds[0:C, :], y_i)) * (1.0 / (2 * C))


def _hyena_ctx(u, kfilt_ctx, d_cos, d_sin):
    full = lambda b: (0, 0)
    return pl.pallas_call(
        _hyena_ctx_kernel,
        grid=(B,),
        in_specs=[pl.BlockSpec((C, HY_W), lambda b: (T_LAT // C + b, 0)),
                  pl.BlockSpec((2 * C, HY_W), full), pl.BlockSpec((2 * C, 2 * C), full), pl.BlockSpec((2 * C, 2 * C), full)],
        out_specs=pl.BlockSpec((C, HY_W), lambda b: (b, 0)),
        out_shape=jax.ShapeDtypeStruct((T_CTX, HY_W), F32),
        compiler_params=_cp(32 << 20, 1),
        name="hyena_ctx",
    )(u, kfilt_ctx, d_cos, d_sin)


def _hyena_filter(n, w1, b1, freq, w2, b2, w3):
    t = jnp.linspace(0.0, 1.0, n, dtype=F32)[:, None]
    bands = (FILTER_EMB - 1) // 2
    w = 2.0 * math.pi * jnp.arange(n, dtype=F32)[:, None] / n
    f = jnp.linspace(1e-4, bands - 1, bands, dtype=F32)[None, :]
    z = jnp.concatenate([t, jnp.cos(f * w), -jnp.sin(f * w)], axis=-1)
    mm = functools.partial(jnp.matmul, precision=HIGHEST)
    h = jnp.sin(freq * (mm(z, w1) + b1))
    h = jnp.sin(freq * (mm(h, w2) + b2))
    deltas = jnp.abs(jnp.linspace(math.log(DECAY_TARGET) / DECAY_FAST, math.log(DECAY_TARGET) / DECAY_SLOW,
                                  HY_W, dtype=F32))
    decay = jnp.exp(-t * deltas[None, :])
    h_fwd = mm(h, w3[:, :HY_W]) * decay
    h_bwd_rev = mm(h[::-1], w3[:, HY_W:]) * decay[::-1]
    k = jnp.concatenate([h_fwd, jnp.zeros((1, HY_W), F32), h_bwd_rev[:-1]], axis=0)
    return k / jnp.sum(jnp.abs(k), axis=0, keepdims=True)


def _s5_matrices(a_re, a_im, log_dt, b_re, b_im, c_re, c_im, d_skip):
    dt = jnp.exp(log_dt)[:, :, None]
    lam_re = jnp.minimum(a_re, -1e-4)
    mag1 = jnp.exp(lam_re * dt)
    lbr = mag1 * jnp.cos(a_im * dt)
    lbi = mag1 * jnp.sin(a_im * dt)
    den = lam_re * lam_re + a_im * a_im
    fr = ((lbr - 1.0) * lam_re + lbi * a_im) / den
    fi = (lbi * lam_re - (lbr - 1.0) * a_im) / den
    bbr = fr[..., None] * b_re - fi[..., None] * b_im
    bbi = fr[..., None] * b_im + fi[..., None] * b_re
    j = jnp.arange(S5_CH + 1, dtype=F32)[:, None, None, None]
    magj = jnp.exp(j * (lam_re * dt)[None])
    pr = magj * jnp.cos(j * (a_im * dt)[None])
    pi = magj * jnp.sin(j * (a_im * dt)[None])
    hi = functools.partial(jnp.einsum, precision=HIGHEST)
    lbr_j = pr[..., None] * bbr[None] - pi[..., None] * bbi[None]
    lbi_j = pr[..., None] * bbi[None] + pi[..., None] * bbr[None]
    m = hi('dgop,jdgpi->jdgoi', c_re, lbr_j) - hi('dgop,jdgpi->jdgoi', c_im, lbi_j)
    eye_g = jnp.eye(S5_GROUPS, dtype=F32)
    s = jnp.arange(S5_CH)
    blocks = jnp.einsum('jdgoi,gh->djgiho', m[0:S5_CH], eye_g).reshape(2, S5_CH, S5_W, S5_W)
    lag0 = blocks[0, 0] + blocks[1, 0] + jnp.diag(d_skip)
    e_all = jnp.concatenate([blocks[1, S5_CH - 1:0:-1], lag0[None], blocks[0, 1:S5_CH]], axis=0).astype(BF16)
    sf_r = lbr_j[S5_CH - 1 - s, 0]
    sf_i = lbi_j[S5_CH - 1 - s, 0]
    sb_r = lbr_j[s, 1]
    sb_i = lbi_j[s, 1]
    st = jnp.stack([sf_r, sf_i, sb_r, sb_i], axis=0)
    rr = jnp.arange(S5_ROWW, dtype=jnp.int32)
    cc = jnp.arange(S5_W, dtype=jnp.int32)
    ws = _s5_expand(jnp.transpose(st, (1, 2, 4, 0, 3)).reshape(S5_ROWW, S5_W),
                    (cc[:, None] // S5_STATE == rr[None, :] // S5_NSTATE)
                    & (cc[:, None] % S5_STATE == rr[None, :] % S5_STATE),
                    (rr // S5_GROUP) % S5_GROUPS, (rr % S5_NSTATE) // S5_STATE)
    tt = jnp.arange(S5_CH)
    cf_r = c_re[0][None] * pr[tt + 1, 0][:, :, None, :] - c_im[0][None] * pi[tt + 1, 0][:, :, None, :]
    cf_i = c_re[0][None] * pi[tt + 1, 0][:, :, None, :] + c_im[0][None] * pr[tt + 1, 0][:, :, None, :]
    cb_r = c_re[1][None] * pr[S5_CH - tt, 1][:, :, None, :] - c_im[1][None] * pi[S5_CH - tt, 1][:, :, None, :]
    cb_i = c_re[1][None] * pi[S5_CH - tt, 1][:, :, None, :] + c_im[1][None] * pr[S5_CH - tt, 1][:, :, None, :]
    ct = jnp.stack([cf_r, -cf_i, cb_r, -cb_i], axis=0)
    wc = _s5_expand(jnp.transpose(ct, (0, 2, 4, 1, 3)).reshape(4 * S5_NSTATE, S5_W),
                    (cc[:, None] // S5_GROUP == rr[None, :] // S5_W)
                    & (cc[:, None] % S5_GROUP == rr[None, :] % S5_GROUP),
                    (rr % S5_NSTATE) // S5_STATE, (rr % S5_W) // S5_GROUP)
    lam_p = jnp.stack([jnp.stack([pr[S5_CH, 0], pi[S5_CH, 0]]), jnp.stack([pr[S5_CH, 1], pi[S5_CH, 1]])])
    return ws, e_all, wc, lam_p.reshape(2, 2, 1, S5_NSTATE)


S5_TN = 512
S5_NS = S5_ROWW // S5_W


def _s5_expand_kernel(a_ref, ex_ref, rg_ref, cg_ref, o_ref):
    v = jnp.dot(a_ref[...], ex_ref[...], preferred_element_type=F32)
    o_ref[...] = jnp.where(rg_ref[...] == cg_ref[...], v, 0.0).astype(BF16)


def _s5_expand(compact, placement, row_group, col_group):
    n = compact.shape[0]
    return pl.pallas_call(
        _s5_expand_kernel,
        grid=(S5_ROWW // S5_TN,),
        in_specs=[pl.BlockSpec((n, S5_W), lambda j: (0, 0)), pl.BlockSpec((S5_W, S5_TN), lambda j: (0, j)),
                  pl.BlockSpec((n, 1), lambda j: (0, 0)), pl.BlockSpec((1, S5_TN), lambda j: (0, j))],
        out_specs=pl.BlockSpec((n, S5_TN), lambda j: (0, j)),
        out_shape=jax.ShapeDtypeStruct((n, S5_ROWW), BF16),
        compiler_params=_cp(32 << 20, 0),
        name="s5_expand",
    )(compact.astype(BF16), placement.astype(BF16), row_group.reshape(n, 1), col_group.reshape(1, S5_ROWW))


def _s5_in_kernel(u_ref, ws_ref, e_ref, s_ref, o_ref):
    j = pl.program_id(0)

    @pl.when(j < S5_NS)
    def _():
        s_ref[...] = jnp.dot(u_ref[...], ws_ref[...], preferred_element_type=F32)

    @pl.when(j >= S5_NS)
    def _():
        t = j - S5_NS
        acc = jnp.dot(u_ref[:, 0:S5_W], e_ref[S5_CH - 1 + t], preferred_element_type=F32)
        for s in range(1, S5_CH):
            acc = acc + jnp.dot(u_ref[:, s * S5_W:(s + 1) * S5_W], e_ref[S5_CH - 1 + t - s],
                                preferred_element_type=F32)
        o_ref[...] = acc


def _s5_in(u_rows, ws, e_all, layer):
    return pl.pallas_call(
        _s5_in_kernel,
        grid=(2 * S5_NS,),
        in_specs=[pl.BlockSpec((N_CHUNK, S5_ROWW), lambda j: (0, 0)),
                  pl.BlockSpec((None, S5_ROWW, S5_W), lambda j: (layer, 0, jnp.minimum(j, S5_NS - 1))),
                  pl.BlockSpec((None, 2 * S5_CH - 1, S5_W, S5_W), lambda j: (layer, 0, 0, 0))],
        out_specs=[pl.BlockSpec((N_CHUNK, S5_W), lambda j: (0, jnp.minimum(j, S5_NS - 1))),
                   pl.BlockSpec((N_CHUNK, S5_W), lambda j: (0, jnp.maximum(j - S5_NS, 0)))],
        out_shape=[jax.ShapeDtypeStruct((N_CHUNK, 4 * S5_NSTATE), F32), jax.ShapeDtypeStruct((N_CHUNK, S5_ROWW), F32)],
        compiler_params=_cp(48 << 20, 1),
        name="s5_in",
    )(u_rows, ws, e_all)


def _s5_scan_kernel(s_ref, lam_ref, h_ref):
    lam = [[lam_ref[d, p] for p in range(2)] for d in range(2)]

    def step(b, d, chunk, h):
        row = pl.ds(chunk, 1)
        cols_r = pl.ds(d * 2 * S5_NSTATE, S5_NSTATE)
        cols_i = pl.ds(d * 2 * S5_NSTATE + S5_NSTATE, S5_NSTATE)
        h_ref[row, cols_r] = h[0]
        h_ref[row, cols_i] = h[1]
        s_r = s_ref[row, cols_r]
        s_i = s_ref[row, cols_i]
        lr, li = lam[d]
        return (lr * h[0] - li * h[1] + s_r, lr * h[1] + li * h[0] + s_i)

    def chain_order(b, d, n_ctx_done):
        ctx0 = B * LAT_CHUNKS + b * CTX_CHUNKS
        lat0 = b * LAT_CHUNKS
        if d == 0:
            return (lambda i: ctx0 + i), (lambda i: lat0 + i)
        return (lambda i: ctx0 + CTX_CHUNKS - 1 - i), (lambda i: lat0 + LAT_CHUNKS - 1 - i)

    chains = [(b, d) for b in range(B) for d in range(2)]
    zero = jnp.zeros((1, S5_NSTATE), F32)
    init = tuple((zero, zero) for _ in chains)

    def phase(n_steps, which, carry):
        def body(i, hs):
            out = []
            for (b, d), h in zip(chains, hs):
                order = chain_order(b, d, 0)[which]
                out.append(step(b, d, order(i), h))
            return tuple(out)
        return lax.fori_loop(0, n_steps, body, carry)

    carry = phase(CTX_CHUNKS, 0, init)
    phase(LAT_CHUNKS, 1, carry)


def _s5_scan(s_rows, lam_p):
    return pl.pallas_call(
        _s5_scan_kernel,
        out_shape=jax.ShapeDtypeStruct((N_CHUNK, 4 * S5_NSTATE), F32),
        compiler_params=pltpu.CompilerParams(vmem_limit_bytes=48 << 20),
        name="s5_scan",
    )(s_rows, lam_p)


S5_TM = N_CHUNK // 2
S5_TPN = S5_TN // S5_W


def _s5_out_kernel(h_ref, w_ref, y_ref, *rest):
    o_refs, hb_ref = rest[:-1], rest[-1]
    j = pl.program_id(1)

    @pl.when(j == 0)
    def _():
        hb_ref[...] = h_ref[...].astype(BF16)

    acc = jnp.dot(hb_ref[...], w_ref[...], preferred_element_type=F32) + y_ref[...]
    for tt in range(S5_TPN):
        for h, o_ref in enumerate(o_refs):
            o_ref[pl.ds(j * S5_TPN + tt, S5_TM, stride=S5_CH), :] = (
                acc[:, tt * S5_W + h * LANE:tt * S5_W + (h + 1) * LANE])


def _s5_out(h_rows, wc, y_in, layer):
    nn = S5_ROWW // S5_TN
    n_out = S5_W // LANE
    return pl.pallas_call(
        _s5_out_kernel,
        grid=(N_CHUNK // S5_TM, nn),
        in_specs=[pl.BlockSpec((S5_TM, 4 * S5_NSTATE), lambda i, j: (i, 0), pipeline_mode=pl.Buffered(1)),
                  pl.BlockSpec((None, 4 * S5_NSTATE, S5_TN), lambda i, j: (layer, 0, j)),
                  pl.BlockSpec((S5_TM, S5_TN), lambda i, j: (i, j))],
        out_specs=[pl.BlockSpec((S5_TM * S5_CH, LANE), lambda i, j: (i, 0))] * n_out,
        out_shape=[jax.ShapeDtypeStruct((T_ALL, LANE), F32)] * n_out,
        scratch_shapes=[pltpu.VMEM((S5_TM, 4 * S5_NSTATE), BF16)],
        compiler_params=_cp(52 << 20, 2),
        name="s5_out",
    )(h_rows, wc, y_in)


def _rms(x):
    return x * lax.rsqrt(jnp.mean(x * x, axis=-1, keepdims=True) + LN_EPS)


def _layer_norm(x, g, b):
    mu = jnp.mean(x, axis=-1, keepdims=True)
    xc = x - mu
    var = jnp.mean(xc * xc, axis=-1, keepdims=True)
    return xc * lax.rsqrt(var + LN_EPS) * g + b


def _merge_kernel(x_ref, attn_ref, conv_ref, hu_ref, x0_ref, s5a_ref, s5b_ref, g1_ref, sh2_ref, sc2_ref, mixg_ref,
                  hyd_ref, wglu_ref, wout_ref, lng_ref, lnb_ref, o_ref, u2_ref, u2p_ref):
    hy = (conv_ref[...] + hu_ref[...] * hyd_ref[...]) * x0_ref[...]
    g = jax.nn.gelu(jnp.concatenate([s5a_ref[...], s5b_ref[...]], axis=-1))
    s5 = g * jax.nn.sigmoid(jnp.dot(g.astype(BF16), wglu_ref[...], preferred_element_type=F32))
    mixg = mixg_ref[...]
    parts = [_rms(attn_ref[...]) * mixg[:, 0:ATTN_W],
             _rms(hy) * mixg[:, ATTN_W:ATTN_W + HY_W],
             _rms(s5) * mixg[:, ATTN_W + HY_W:MIX_W]]
    mix = jnp.concatenate(parts, axis=-1).astype(BF16)
    o = jnp.dot(mix, wout_ref[...], preferred_element_type=F32)
    x1 = _layer_norm(ALPHA * x_ref[...] + g1_ref[...] * o, lng_ref[...], lnb_ref[...])
    o_ref[...] = x1
    u2 = x1 * (1.0 + sc2_ref[...]) + sh2_ref[...]
    u2_ref[...] = u2
    u2p_ref[...] = _pack_pairs(u2)


def _merge(n_rows, xall, attn, conv, hu, x0c, s5y, g1, sh2, sc2, mix_g, hy_d, wglu_bf, wout_bf, ln_g, ln_b):
    nt = n_rows // TM
    row = lambda i: (i, 0)
    full = lambda i: (0, 0)
    return pl.pallas_call(
        _merge_kernel,
        grid=(nt,),
        in_specs=[pl.BlockSpec((TM, D), row), pl.BlockSpec((TM, ATTN_W), row), pl.BlockSpec((TM, HY_W), row),
                  pl.BlockSpec((TM, HY_W), row), pl.BlockSpec((TM, HY_W), row),
                  pl.BlockSpec((TM, LANE), row), pl.BlockSpec((TM, LANE), row),
                  _vec_spec(TM), _vec_spec(TM), _vec_spec(TM), pl.BlockSpec((1, MIX_W), full),
                  pl.BlockSpec((1, HY_W), full), pl.BlockSpec((S5_W, S5_W), full), pl.BlockSpec((MIX_W, D), full),
                  pl.BlockSpec((1, D), full), pl.BlockSpec((1, D), full)],
        out_specs=[pl.BlockSpec((TM, D), row), pl.BlockSpec((TM, D), row), pl.BlockSpec((TM, HALF_D), row)],
        out_shape=[jax.ShapeDtypeStruct((n_rows, D), F32), jax.ShapeDtypeStruct((n_rows, D), F32),
                   jax.ShapeDtypeStruct((n_rows, HALF_D), jnp.int32)],
        compiler_params=_cp(48 << 20, 1),
        name="merge",
    )(xall, attn, conv, hu, x0c, s5y[0], s5y[1], g1, sh2, sc2, mix_g.reshape(1, MIX_W), hy_d.reshape(1, HY_W), wglu_bf,
      wout_bf, ln_g.reshape(1, D), ln_b.reshape(1, D))


def _router_kernel(u_ref, wt_ref, b_ref, e_ref, g_ref):
    logits = lax.dot_general(wt_ref[...], u_ref[...], (((1,), (1,)), ((), ())), precision=HIGHEST,
                             preferred_element_type=F32)
    scores = jax.nn.sigmoid(logits)
    biased = scores + b_ref[...]
    ninf = -jnp.inf
    grow = lax.broadcasted_iota(jnp.int32, (EGROUP, TM), 0)
    groups = [biased[gi * EGROUP:(gi + 1) * EGROUP] for gi in range(N_EGROUPS)]
    gscore = []
    for vals in groups:
        m1 = jnp.max(vals, axis=0, keepdims=True)
        i1 = jnp.min(jnp.where(vals == m1, grow, EGROUP), axis=0, keepdims=True)
        m2 = jnp.max(jnp.where(grow == i1, ninf, vals), axis=0, keepdims=True)
        gscore.append(m1 + m2)
    kept = []
    for gi in range(N_EGROUPS):
        rank = jnp.zeros((1, TM), jnp.int32)
        for gj in range(N_EGROUPS):
            if gj == gi:
                continue
            ahead = (gscore[gj] > gscore[gi]) | ((gscore[gj] == gscore[gi]) & (gj < gi))
            rank = rank + ahead.astype(jnp.int32)
        kept.append(jnp.where(rank < TOPK_GROUPS, groups[gi], ninf))
    masked = jnp.concatenate(kept, axis=0)
    row = lax.broadcasted_iota(jnp.int32, (N_EXPERTS, TM), 0)
    gates = []
    gsum = jnp.zeros((1, TM), F32)
    for kk in range(TOP_K):
        m = jnp.max(masked, axis=0, keepdims=True)
        idx = jnp.min(jnp.where(masked == m, row, N_EXPERTS), axis=0, keepdims=True)
        hit = row == idx
        gate = jnp.sum(jnp.where(hit, scores, 0.0), axis=0, keepdims=True)
        masked = jnp.where(hit, ninf, masked)
        e_ref[kk:kk + 1, :] = idx
        gates.append(gate)
        gsum = gsum + gate
    for kk in range(TOP_K):
        g_ref[kk:kk + 1, :] = gates[kk] / gsum * ROUTED_SCALE


def _router(n_rows, u2, w_router_t, router_bias):
    nt = n_rows // TM
    col = lambda i: (0, i)
    return pl.pallas_call(
        _router_kernel,
        grid=(nt,),
        in_specs=[pl.BlockSpec((TM, D), lambda i: (i, 0)),
                  pl.BlockSpec((N_EXPERTS, D), lambda i: (0, 0)), pl.BlockSpec((N_EXPERTS, 1), lambda i: (0, 0))],
        out_specs=[pl.BlockSpec((TOP_K, TM), col), pl.BlockSpec((TOP_K, TM), col)],
        out_shape=[jax.ShapeDtypeStruct((TOP_K, n_rows), jnp.int32), jax.ShapeDtypeStruct((TOP_K, n_rows), F32)],
        compiler_params=_cp(32 << 20, 1),
        name="router",
    )(u2, w_router_t, router_bias.reshape(N_EXPERTS, 1))


def _dispatch(top_e):
    t = top_e.shape[1]
    tk = t * TOP_K
    nblk = tk // MOE_BLOCK
    n_steps = nblk + N_EXPERTS
    flat_e = top_e.reshape(tk)
    pos = jnp.arange(tk, dtype=jnp.int32)
    pos_bits = (tk - 1).bit_length()
    assert pos_bits + (N_EXPERTS - 1).bit_length() < 32
    order = lax.sort(flat_e * (1 << pos_bits) + pos) % (1 << pos_bits)
    experts = jnp.arange(N_EXPERTS, dtype=jnp.int32)
    counts = jnp.sum((flat_e[None, :] == experts[:, None]).astype(jnp.int32), axis=1)
    ends = jnp.cumsum(counts)
    starts = ends - counts
    fb = starts // MOE_BLOCK
    npairs = jnp.where(counts > 0, (ends - 1) // MOE_BLOCK - fb + 1, 0)
    pend = jnp.cumsum(npairs)
    poff = pend - npairs
    n_pairs = pend[-1]
    s = jnp.arange(n_steps, dtype=jnp.int32)
    sc = jnp.minimum(s, n_pairs - 1)
    pe = jnp.sum((pend[None, :] <= sc[:, None]).astype(jnp.int32), axis=1)
    pb = fb[pe] + (sc - poff[pe])
    lo = jnp.where(s < n_pairs, jnp.maximum(starts[pe] - pb * MOE_BLOCK, 0), 0)
    hi = jnp.where(s < n_pairs, jnp.minimum(ends[pe] - pb * MOE_BLOCK, MOE_BLOCK), 0)
    last_of_block = ((s == n_pairs - 1) | ((s + 1 < n_pairs) & (jnp.roll(pb, -1) != pb))).astype(jnp.int32)
    used = counts > 0
    ordinal = jnp.cumsum(used.astype(jnp.int32)) - 1
    later = jnp.where(used, experts, N_EXPERTS)
    next_used = lax.cummin(jnp.concatenate([later[1:], jnp.full((1,), N_EXPERTS, jnp.int32)]), reverse=True)
    nxt = next_used[pe]
    nxt = jnp.where(nxt < N_EXPERTS, nxt, pe)
    parity = ordinal[pe] % 2
    w_even = jnp.where(parity == 0, pe, nxt)
    w_odd = jnp.where(parity == 1, pe, nxt)
    return dict(pe=pe, pb=pb, lo=lo, hi=hi, last=last_of_block, parity=parity, w_even=w_even, w_odd=w_odd,
                tok=order % t, order=order)


X_SLOTS = 4


def _expert_kernel(pe, pb, plo, phi, plast, ppar, pwe, pwo, xs_hbm, wg0_ref, wu0_ref, wd0_ref, wg1_ref, wu1_ref,
                   wd1_ref, out_hbm, xbuf, xsem, obuf, osem, wgb, wub, wdb):
    del pwe, pwo
    s = pl.program_id(0)
    prev = jnp.maximum(s - 1, 0)
    n_blocks = out_hbm.shape[0] // MOE_BLOCK
    new_expert = (s == 0) | (pe[s] != pe[prev])

    def x_copy(b, sl):
        rows = pl.ds(pl.multiple_of(b * MOE_BLOCK, MOE_BLOCK), MOE_BLOCK)
        return pltpu.make_async_copy(xs_hbm.at[rows], xbuf.at[sl], xsem.at[sl])

    @pl.when(s == 0)
    def _():
        for b in range(X_SLOTS - 1):
            x_copy(b, b).start()
        obuf[...] = jnp.zeros(obuf.shape, obuf.dtype)

    for par, (wg_ref, wu_ref, wd_ref) in enumerate(((wg0_ref, wu0_ref, wd0_ref), (wg1_ref, wu1_ref, wd1_ref))):
        @pl.when(new_expert & (ppar[s] == par))
        def _():
            wgb[...] = wg_ref[...].astype(BF16)
            wub[...] = wu_ref[...].astype(BF16)
            wdb[...] = wd_ref[...].astype(BF16)

    lo = plo[s]
    hi = phi[s]
    blk = pb[s]
    slot = blk % 2
    first_of_block = (s == 0) | (blk != pb[prev])

    def out_copy(b, sl):
        rows = pl.ds(pl.multiple_of(b * MOE_BLOCK, MOE_BLOCK), MOE_BLOCK)
        return pltpu.make_async_copy(obuf.at[sl], out_hbm.at[rows], osem.at[sl])

    @pl.when(first_of_block)
    def _():
        ahead = blk + (X_SLOTS - 1)

        @pl.when(ahead < n_blocks)
        def _():
            x_copy(ahead, ahead % X_SLOTS).start()
        x_copy(blk, blk % X_SLOTS).wait()

    def ffn():
        x_lo, x_hi = _unpack_pairs(xbuf[blk % X_SLOTS])
        x_lo = x_lo.astype(BF16)
        x_hi = x_hi.astype(BF16)
        hg = (jnp.dot(x_lo, wgb[0:HALF_D], preferred_element_type=F32)
              + jnp.dot(x_hi, wgb[HALF_D:D], preferred_element_type=F32))
        hu = (jnp.dot(x_lo, wub[0:HALF_D], preferred_element_type=F32)
              + jnp.dot(x_hi, wub[HALF_D:D], preferred_element_type=F32))
        h = (hg * jax.nn.sigmoid(hg)) * hu
        return _pack_pairs(jnp.dot(h.astype(BF16), wdb[...], preferred_element_type=F32))

    @pl.when(first_of_block & (blk >= 2))
    def _():
        out_copy(blk - 2, slot).wait()

    @pl.when(hi > lo)
    def _():
        row = lax.broadcasted_iota(jnp.int32, (MOE_BLOCK, HALF_D), 0)
        mine = first_of_block | ((row >= lo) & (row < hi))
        obuf[slot] = jnp.where(mine, ffn(), obuf[slot])

    @pl.when(plast[s] == 1)
    def _():
        out_copy(blk, slot).start()

    @pl.when(s == pl.num_programs(0) - 1)
    def _():
        out_copy(n_blocks - 2, (n_blocks - 2) % 2).wait()
        out_copy(n_blocks - 1, (n_blocks - 1) % 2).wait()


def _experts(xs, disp, wg, wu, wd, layer):
    tk = xs.shape[0]
    n_steps = disp['pe'].shape[0]
    assert tk // MOE_BLOCK >= X_SLOTS
    weven = lambda shape: pl.BlockSpec((None, None) + shape, lambda s, *p: (layer, p[6][s], 0, 0))
    wodd = lambda shape: pl.BlockSpec((None, None) + shape, lambda s, *p: (layer, p[7][s], 0, 0))
    grid_spec = pltpu.PrefetchScalarGridSpec(
        num_scalar_prefetch=8,
        grid=(n_steps,),
        in_specs=[pl.BlockSpec(memory_space=pl.ANY),
                  weven((D, EXPERT_FF)), weven((D, EXPERT_FF)), weven((EXPERT_FF, D)),
                  wodd((D, EXPERT_FF)), wodd((D, EXPERT_FF)), wodd((EXPERT_FF, D))],
        out_specs=pl.BlockSpec(memory_space=pl.ANY),
        scratch_shapes=[pltpu.VMEM((X_SLOTS, MOE_BLOCK, HALF_D), jnp.int32), pltpu.SemaphoreType.DMA((X_SLOTS,)),
                        pltpu.VMEM((2, MOE_BLOCK, HALF_D), jnp.int32), pltpu.SemaphoreType.DMA((2,)),
                        pltpu.VMEM((D, EXPERT_FF), BF16), pltpu.VMEM((D, EXPERT_FF), BF16),
                        pltpu.VMEM((EXPERT_FF, D), BF16)],
    )
    return pl.pallas_call(
        _expert_kernel,
        grid_spec=grid_spec,
        out_shape=jax.ShapeDtypeStruct((tk, HALF_D), jnp.int32),
        compiler_params=_cp(40 << 20, 1),
        name="experts",
    )(disp['pe'], disp['pb'], disp['lo'], disp['hi'], disp['last'], disp['parity'], disp['w_even'], disp['w_odd'],
      xs, wg, wu, wd, wg, wu, wd)


SC_ROWS = 128


def _row_gather(table, idx):
    n = idx.shape[0]
    d = table.shape[1]
    mesh = plsc.VectorSubcoreMesh(core_axis_name="c", subcore_axis_name="s")
    n_workers = mesh.num_cores * mesh.num_subcores
    per_worker = n // n_workers
    assert per_worker * n_workers == n and per_worker % SC_ROWS == 0

    @functools.partial(
        pl.kernel, mesh=mesh,
        out_type=jax.ShapeDtypeStruct((n, d), table.dtype),
        scratch_types=[pltpu.VMEM((per_worker,), jnp.int32), pltpu.VMEM((SC_ROWS, d), table.dtype),
                       pltpu.SemaphoreType.DMA],
    )
    def gather(table_hbm, idx_hbm, out_hbm, idx_v, rows_v, sem):
        worker = lax.axis_index("s") * mesh.num_cores + lax.axis_index("c")
        base = pl.multiple_of(worker * per_worker, 8)
        pltpu.sync_copy(idx_hbm.at[pl.ds(base, per_worker)], idx_v)

        @pl.loop(0, per_worker // SC_ROWS)
        def _(c):
            lo = pl.multiple_of(c * SC_ROWS, 8)
            pltpu.async_copy(table_hbm.at[idx_v.at[pl.ds(lo, SC_ROWS)]], rows_v, sem).wait()
            pltpu.sync_copy(rows_v, out_hbm.at[pl.ds(base + lo, SC_ROWS)])

    return gather(table, idx)


def _row_scatter(rows, idx):
    n, d = rows.shape
    mesh = plsc.VectorSubcoreMesh(core_axis_name="c", subcore_axis_name="s")
    n_workers = mesh.num_cores * mesh.num_subcores
    per_worker = n // n_workers
    assert per_worker * n_workers == n and per_worker % SC_ROWS == 0

    @functools.partial(
        pl.kernel, mesh=mesh,
        out_type=jax.ShapeDtypeStruct((n, d), rows.dtype),
        scratch_types=[pltpu.VMEM((SC_ROWS,), jnp.int32), pltpu.VMEM((SC_ROWS, d), rows.dtype),
                       pltpu.SemaphoreType.DMA],
    )
    def scatter(rows_hbm, idx_hbm, out_hbm, idx_v, rows_v, sem):
        worker = lax.axis_index("s") * mesh.num_cores + lax.axis_index("c")
        base = worker * per_worker

        @pl.loop(0, per_worker // SC_ROWS)
        def _(c):
            off = pl.multiple_of(base + c * SC_ROWS, 8)
            pltpu.sync_copy(idx_hbm.at[pl.ds(off, SC_ROWS)], idx_v)
            pltpu.sync_copy(rows_hbm.at[pl.ds(off, SC_ROWS)], rows_v)
            pltpu.async_copy(rows_v, out_hbm.at[idx_v], sem).wait()

    return scatter(rows, idx)


FM = 128


def _shared_kernel(x_ref, u_ref, g2_ref, wsg_ref, wsu_ref, wsd_ref, o_ref):
    u_lo, u_hi = _unpack_pairs(u_ref[...])
    u_lo = u_lo.astype(BF16)
    u_hi = u_hi.astype(BF16)
    hg = (jnp.dot(u_lo, wsg_ref[0:HALF_D], preferred_element_type=F32)
          + jnp.dot(u_hi, wsg_ref[HALF_D:D], preferred_element_type=F32))
    hu = (jnp.dot(u_lo, wsu_ref[0:HALF_D], preferred_element_type=F32)
          + jnp.dot(u_hi, wsu_ref[HALF_D:D], preferred_element_type=F32))
    f = jnp.dot(((hg * jax.nn.sigmoid(hg)) * hu).astype(BF16), wsd_ref[...], preferred_element_type=F32)
    o_ref[...] = ALPHA * x_ref[...] + g2_ref[...] * f


def _shared(n_rows, x1, u2p, g2, wsg_bf, wsu_bf, wsd_bf):
    row = lambda i: (i, 0)
    full = lambda i: (0, 0)
    return pl.pallas_call(
        _shared_kernel,
        grid=(n_rows // TM,),
        in_specs=[pl.BlockSpec((TM, D), row), pl.BlockSpec((TM, HALF_D), row), _vec_spec(TM),
                  pl.BlockSpec((D, EXPERT_FF), full), pl.BlockSpec((D, EXPERT_FF), full),
                  pl.BlockSpec((EXPERT_FF, D), full)],
        out_specs=pl.BlockSpec((TM, D), row),
        out_shape=jax.ShapeDtypeStruct((n_rows, D), F32),
        compiler_params=_cp(32 << 20, 1),
        name="shared_expert",
    )(x1, u2p, g2, wsg_bf, wsu_bf, wsd_bf)


def _ffn_out_kernel(base_ref, r_ref, gate_ref, g2_ref, lng_ref, lnb_ref, o_ref):
    gates = jnp.transpose(jnp.concatenate([gate_ref[...], jnp.zeros((FM - TOP_K, FM), F32)], axis=0))
    r_lo = jnp.zeros((FM, HALF_D), F32)
    r_hi = jnp.zeros((FM, HALF_D), F32)
    for kk in range(TOP_K):
        k_lo, k_hi = _unpack_pairs(r_ref[kk])
        gk = gates[:, kk:kk + 1]
        r_lo = r_lo + k_lo * gk
        r_hi = r_hi + k_hi * gk
    routed = jnp.concatenate([r_lo, r_hi], axis=1)
    o_ref[...] = _layer_norm(base_ref[...] + g2_ref[...] * routed, lng_ref[...], lnb_ref[...])


def _ffn_out(n_rows, base, routed, gate, g2, ln_g, ln_b):
    row = lambda i: (i, 0)
    full = lambda i: (0, 0)
    return pl.pallas_call(
        _ffn_out_kernel,
        grid=(n_rows // FM,),
        in_specs=[pl.BlockSpec((FM, D), row),
                  pl.BlockSpec((TOP_K, FM, HALF_D), lambda i: (0, i, 0)), pl.BlockSpec((TOP_K, FM), lambda i: (0, i)),
                  _vec_spec(FM), pl.BlockSpec((1, D), full), pl.BlockSpec((1, D), full)],
        out_specs=pl.BlockSpec((FM, D), row),
        out_shape=jax.ShapeDtypeStruct((n_rows, D), F32),
        compiler_params=_cp(40 << 20, 1),
        name="ffn_out",
    )(base, routed, gate, g2, ln_g.reshape(1, D), ln_b.reshape(1, D))


def _rope_tables():
    t = jnp.arange(L, dtype=jnp.int32)
    row = (t // GRID_W).astype(F32)
    col = (t % GRID_W).astype(F32)
    inv_freq = ROPE_BASE ** (-jnp.arange(0, AXIS_DIM, 2, dtype=F32) / AXIS_DIM)
    half = AXIS_DIM // 2

    def axis(pos):
        ang = pos[:, None] * inv_freq[None, :]
        c = jnp.cos(ang)
        s = jnp.sin(ang)
        return jnp.concatenate([c, c], axis=1), jnp.concatenate([-s, s], axis=1)

    cr, sr = axis(row)
    cc, sc = axis(col)
    cos_h = jnp.concatenate([cr, cc], axis=1)
    sin_h = jnp.concatenate([sr, sc], axis=1)
    cos_l = jnp.tile(cos_h, (B, LANE // HEAD_DIM))
    sin_l = jnp.tile(sin_h, (B, LANE // HEAD_DIM))
    cos_t = jnp.concatenate([cos_l, jnp.ones((T_CTX, LANE), F32)], axis=0)
    sin_t = jnp.concatenate([sin_l, jnp.zeros((T_CTX, LANE), F32)], axis=0)
    del half
    return cos_t, sin_t


def kernel(x, c, ctx, c_ctx, w_ada, b_ada, w_in, w_out, sink, mix_g, hy_short_w, hy_short_b, hy_w1, hy_b1, hy_freq,
           hy_w2, hy_b2, hy_w3, hy_d, s5_a_re, s5_a_im, s5_log_dt, s5_b_re, s5_b_im, s5_c_re, s5_c_im, s5_d, s5_w_glu,
           ln1_g, ln1_b, ln2_g, ln2_b, w_router, router_bias, w_exp_gate, w_exp_up, w_exp_down, w_sh_gate, w_sh_up,
           w_sh_down):
    xall = jnp.concatenate([x.reshape(T_LAT, D), ctx.reshape(T_CTX, D)], axis=0)
    cvec = jnp.concatenate([c, c_ctx[None, :], jnp.zeros((8 - B - 1, D), F32)], axis=0)
    mod = _ada(cvec, w_ada, b_ada)[:, 0:B + 1, :].reshape(DEPTH, B + 1, 6, 1, D)

    cos_t, sin_t = _rope_tables()
    g_cos, g_sin, f_cos, f_sin = _dft_tables()
    g_cos_bf, g_sin_bf = g_cos.astype(BF16), g_sin.astype(BF16)
    i_cos_bf = jnp.swapaxes(g_cos, 1, 2).astype(BF16)
    i_sin_bf = jnp.swapaxes(g_sin, 1, 2).astype(BF16)
    f_cos_bf, f_sin_bf = f_cos.astype(BF16), f_sin.astype(BF16)
    k1 = jnp.arange(K1P)
    spec_w = jnp.where((k1 == 0) | (k1 == FFT_R // 2), 1.0, 2.0) * (k1 < K1_USED) / N_FFT
    spec_w = spec_w.astype(F32)
    kt = jnp.arange(2 * C, dtype=jnp.int32)
    ang_c = ((kt[:, None] * kt[None, :]) % (2 * C)).astype(F32) * (2.0 * math.pi / (2 * C))
    d_cos, d_sin = jnp.cos(ang_c), jnp.sin(ang_c)

    filt_params = (hy_w1, hy_b1, hy_freq, hy_w2, hy_b2, hy_w3)
    filt_lat = jax.vmap(functools.partial(_hyena_filter, L))(*filt_params)
    filt_ctx = jax.vmap(functools.partial(_hyena_filter, C))(*(p[0:DEPTH - 1] for p in filt_params))
    k_r, k_i = jax.vmap(_hyena_spectrum, in_axes=(0, None, None, None, None, None))(
        filt_lat, g_cos, g_sin, f_cos, f_sin, spec_w)
    ws, e_all, wc, lam_p = jax.vmap(_s5_matrices)(s5_a_re, s5_a_im, s5_log_dt, s5_b_re, s5_b_im, s5_c_re, s5_c_im, s5_d)

    for l in range(DEPTH):
        last = l == DEPTH - 1
        n_rows = T_LAT if last else T_ALL
        sh1, sc1, g1, sh2, sc2, g2 = (mod[l, :, j] for j in range(6))

        q, k, v, hz, s5u = _inproj(xall, sh1, sc1, w_in[l].astype(BF16), cos_t, sin_t)
        attn = _attention(sink[l], q, k, v, with_ctx=not last)

        hu, x0c = _hyena_pre(hz, hy_short_w[l], hy_short_b[l])
        conv = _hyena_fft(hu, g_cos_bf, g_sin_bf, i_cos_bf, i_sin_bf, f_cos_bf, f_sin_bf, k_r, k_i, l)
        if not last:
            conv = jnp.concatenate([conv, _hyena_ctx(hu, filt_ctx[l], d_cos, d_sin)], axis=0)

        s_rows, y_in = _s5_in(s5u, ws, e_all, l)
        s5y = _s5_out(_s5_scan(s_rows, lam_p[l]), wc, y_in, l)

        x1, u2, u2p = _merge(n_rows, xall, attn, conv, hu, x0c, s5y, g1, sh2, sc2, mix_g[l], hy_d[l],
                             s5_w_glu[l].astype(BF16), w_out[l].astype(BF16), ln1_g[l], ln1_b[l])

        base = _shared(n_rows, x1, u2p, g2, w_sh_gate[l].astype(BF16), w_sh_up[l].astype(BF16),
                       w_sh_down[l].astype(BF16))
        top_e, gate = _router(n_rows, u2, w_router[l].T, router_bias[l])
        disp = _dispatch(top_e)
        ys = _experts(_row_gather(u2p, disp['tok']), disp, w_exp_gate, w_exp_up, w_exp_down, l)
        routed = _row_scatter(ys, disp['order']).reshape(TOP_K, n_rows, HALF_D)
        xall = _ffn_out(n_rows, base, routed, gate, g2, ln2_g[l], ln2_b[l])
    return xall.reshape(B, L, D)
```
